```python
import math
import jax, jax.numpy as jnp
from jax import lax
import numpy as np

D_MODEL = 2048
BATCH = 8
SEQ = 4096
DEPTH = 4

CHUNK = 64
N_MIXERS = 2
BRANCH = D_MODEL
CONV_K = 3
SB_HEADS = 16
SB_HEAD_DIM = BRANCH // SB_HEADS
Q_BLOCK = 128
RMS_EPS = 1e-6

kernel_name = "hybrid_shortconv_stickbreaking_trunk"


def rmsnorm(x, g):
    xf = x.astype(jnp.float32)
    y = xf * lax.rsqrt(jnp.mean(xf * xf, axis=-1, keepdims=True) + RMS_EPS)
    return (y * g.astype(jnp.float32)).astype(x.dtype)


def causal_depthwise_conv(u, w):
    rhs = w[:, None, :]
    return lax.conv_general_dilated(
        u, rhs.astype(u.dtype), window_strides=(1,), padding=[(CONV_K - 1, 0)],
        dimension_numbers=("NWC", "WIO", "NWC"), feature_group_count=u.shape[-1])


def short_conv_mixer(h, w_in, conv_w, w_out):
    proj = jnp.einsum("bsd,de->bse", h, w_in)
    b_gate, c_gate, xt, z = jnp.split(proj, 4, axis=-1)
    y = b_gate * causal_depthwise_conv(c_gate * xt, conv_w)
    return jnp.einsum("bse,ed->bsd", jax.nn.silu(z) * y, w_out)


def stick_breaking_mixer(h, w_in, w_out):
    bsz, seq, _ = h.shape
    proj = jnp.einsum("bsd,de->bse", h, w_in)
    q, k, v, z = jnp.split(proj, 4, axis=-1)
    to_heads = lambda t: t.reshape(bsz, seq, SB_HEADS, SB_HEAD_DIM).transpose(0, 2, 1, 3)
    q, k, v = to_heads(q), to_heads(k), to_heads(v)
    scale = 1.0 / math.sqrt(SB_HEAD_DIM)
    outs = []
    for blk in range(seq // Q_BLOCK):
        s0 = blk * Q_BLOCK
        end = s0 + Q_BLOCK
        qb = q[:, :, s0:end]
        kb = k[:, :, :end]
        vb = v[:, :, :end]
        logits = jnp.einsum("bhqd,bhkd->bhqk", qb, kb).astype(jnp.float32) * scale
        t_idx = s0 + jnp.arange(Q_BLOCK)[:, None]
        s_idx = jnp.arange(end)[None, :]
        mask = s_idx < t_idx
        log_keep = jnp.where(mask, jax.nn.log_sigmoid(-logits), 0.0)
        tail = lax.cumsum(log_keep, axis=3, reverse=True) - log_keep
        weights = jnp.where(mask, jnp.exp(jax.nn.log_sigmoid(logits) + tail), 0.0)
        outs.append(jnp.einsum("bhqk,bhkd->bhqd", weights.astype(vb.dtype), vb))
    o = jnp.concatenate(outs, axis=2)
    o = o.transpose(0, 2, 1, 3).reshape(bsz, seq, BRANCH)
    return jnp.einsum("bse,ed->bsd", jax.nn.silu(z) * o, w_out)


def _fwd_setup_inputs(seed: int = 0) -> dict:
    key = jax.random.key(seed)
    keys = iter(jax.random.split(key, 64))
    nrm = lambda shape, s: jax.random.normal(next(keys), shape, jnp.float32) * s
    gain = lambda: 1.0 + nrm((D_MODEL,), 0.02)
    inp = {"x": nrm((BATCH, SEQ, D_MODEL), 1.0)}
    for i in range(DEPTH):
        inp[f"ln_pre_{i}"] = gain()
        if i % N_MIXERS == 0:
            inp[f"conv_w_in_{i}"] = nrm((D_MODEL, 4 * BRANCH), D_MODEL ** -0.5)
            inp[f"conv_w_{i}"] = nrm((CONV_K, BRANCH), CONV_K ** -0.5)
            inp[f"conv_w_out_{i}"] = nrm((BRANCH, D_MODEL), BRANCH ** -0.5)
        else:
            inp[f"sb_w_in_{i}"] = nrm((D_MODEL, 4 * BRANCH), D_MODEL ** -0.5)
            inp[f"sb_w_out_{i}"] = nrm((BRANCH, D_MODEL), BRANCH ** -0.5)
        inp[f"ln_post_{i}"] = gain()
    return inp


def _fwd_reference(x,
              ln_pre_0, conv_w_in_0, conv_w_0, conv_w_out_0, ln_post_0,
              ln_pre_1, sb_w_in_1, sb_w_out_1, ln_post_1,
              ln_pre_2, conv_w_in_2, conv_w_2, conv_w_out_2, ln_post_2,
              ln_pre_3, sb_w_in_3, sb_w_out_3, ln_post_3):
    layers = [
        (ln_pre_0, (conv_w_in_0, conv_w_0, conv_w_out_0), ln_post_0),
        (ln_pre_1, (sb_w_in_1, sb_w_out_1), ln_post_1),
        (ln_pre_2, (conv_w_in_2, conv_w_2, conv_w_out_2), ln_post_2),
        (ln_pre_3, (sb_w_in_3, sb_w_out_3), ln_post_3),
    ]
    h = x
    for i in range(DEPTH):
        g_pre, params, g_post = layers[i]
        u = rmsnorm(h, g_pre)
        if i % N_MIXERS == 0:
            m = short_conv_mixer(u, *params)
        else:
            m = stick_breaking_mixer(u, *params)
        h = h + rmsnorm(m, g_post)
    return h


import jax as _jax
import jax.numpy as _jnp

TWIN_FORMAT = 'train_step'
FWD_PARAMS = ['x', 'ln_pre_0', 'conv_w_in_0', 'conv_w_0', 'conv_w_out_0', 'ln_post_0', 'ln_pre_1', 'sb_w_in_1', 'sb_w_out_1', 'ln_post_1', 'ln_pre_2', 'conv_w_in_2', 'conv_w_2', 'conv_w_out_2', 'ln_post_2', 'ln_pre_3', 'sb_w_in_3', 'sb_w_out_3', 'ln_post_3']
TWIN_WEIGHTS = ['ln_pre_0', 'conv_w_in_0', 'conv_w_0', 'conv_w_out_0', 'ln_post_0', 'ln_pre_1', 'sb_w_in_1', 'sb_w_out_1', 'ln_post_1', 'ln_pre_2', 'conv_w_in_2', 'conv_w_2', 'conv_w_out_2', 'ln_post_2', 'ln_pre_3', 'sb_w_in_3', 'sb_w_out_3', 'ln_post_3']
TWIN_DIFF_INPUT = 'x'
TWIN_INPUTS = ['x', 'ln_pre_0', 'conv_w_in_0', 'conv_w_0', 'conv_w_out_0', 'ln_post_0', 'ln_pre_1', 'sb_w_in_1', 'sb_w_out_1', 'ln_post_1', 'ln_pre_2', 'conv_w_in_2', 'conv_w_2', 'conv_w_out_2', 'ln_post_2', 'ln_pre_3', 'sb_w_in_3', 'sb_w_out_3', 'ln_post_3', 'loss_target', 'm_ln_pre_0', 'm_conv_w_in_0', 'm_conv_w_0', 'm_conv_w_out_0', 'm_ln_post_0', 'm_ln_pre_1', 'm_sb_w_in_1', 'm_sb_w_out_1', 'm_ln_post_1', 'm_ln_pre_2', 'm_conv_w_in_2', 'm_conv_w_2', 'm_conv_w_out_2', 'm_ln_post_2', 'm_ln_pre_3', 'm_sb_w_in_3', 'm_sb_w_out_3', 'm_ln_post_3', 'v_ln_pre_0', 'v_conv_w_in_0', 'v_conv_w_0', 'v_conv_w_out_0', 'v_ln_post_0', 'v_ln_pre_1', 'v_sb_w_in_1', 'v_sb_w_out_1', 'v_ln_post_1', 'v_ln_pre_2', 'v_conv_w_in_2', 'v_conv_w_2', 'v_conv_w_out_2', 'v_ln_post_2', 'v_ln_pre_3', 'v_sb_w_in_3', 'v_sb_w_out_3', 'v_ln_post_3']
TWIN_OUTPUTS = ['loss', 'grad_x', 'grad_ln_pre_0', 'grad_conv_w_in_0', 'grad_conv_w_0', 'grad_conv_w_out_0', 'grad_ln_post_0', 'grad_ln_pre_1', 'grad_sb_w_in_1', 'grad_sb_w_out_1', 'grad_ln_post_1', 'grad_ln_pre_2', 'grad_conv_w_in_2', 'grad_conv_w_2', 'grad_conv_w_out_2', 'grad_ln_post_2', 'grad_ln_pre_3', 'grad_sb_w_in_3', 'grad_sb_w_out_3', 'grad_ln_post_3', 'delta_ln_pre_0', 'delta_conv_w_in_0', 'delta_conv_w_0', 'delta_conv_w_out_0', 'delta_ln_post_0', 'delta_ln_pre_1', 'delta_sb_w_in_1', 'delta_sb_w_out_1', 'delta_ln_post_1', 'delta_ln_pre_2', 'delta_conv_w_in_2', 'delta_conv_w_2', 'delta_conv_w_out_2', 'delta_ln_post_2', 'delta_ln_pre_3', 'delta_sb_w_in_3', 'delta_sb_w_out_3', 'delta_ln_post_3', 'new_m_ln_pre_0', 'new_m_conv_w_in_0', 'new_m_conv_w_0', 'new_m_conv_w_out_0', 'new_m_ln_post_0', 'new_m_ln_pre_1', 'new_m_sb_w_in_1', 'new_m_sb_w_out_1', 'new_m_ln_post_1', 'new_m_ln_pre_2', 'new_m_conv_w_in_2', 'new_m_conv_w_2', 'new_m_conv_w_out_2', 'new_m_ln_post_2', 'new_m_ln_pre_3', 'new_m_sb_w_in_3', 'new_m_sb_w_out_3', 'new_m_ln_post_3', 'new_v_ln_pre_0', 'new_v_conv_w_in_0', 'new_v_conv_w_0', 'new_v_conv_w_out_0', 'new_v_ln_post_0', 'new_v_ln_pre_1', 'new_v_sb_w_in_1', 'new_v_sb_w_out_1', 'new_v_ln_post_1', 'new_v_ln_pre_2', 'new_v_conv_w_in_2', 'new_v_conv_w_2', 'new_v_conv_w_out_2', 'new_v_ln_post_2', 'new_v_ln_pre_3', 'new_v_sb_w_in_3', 'new_v_sb_w_out_3', 'new_v_ln_post_3']
TWIN_LEAF_KINDS = {'loss': 'loss', 'grad_x': 'grad_x', 'grad_ln_pre_0': 'grad_w', 'grad_conv_w_in_0': 'grad_w', 'grad_conv_w_0': 'grad_w', 'grad_conv_w_out_0': 'grad_w', 'grad_ln_post_0': 'grad_w', 'grad_ln_pre_1': 'grad_w', 'grad_sb_w_in_1': 'grad_w', 'grad_sb_w_out_1': 'grad_w', 'grad_ln_post_1': 'grad_w', 'grad_ln_pre_2': 'grad_w', 'grad_conv_w_in_2': 'grad_w', 'grad_conv_w_2': 'grad_w', 'grad_conv_w_out_2': 'grad_w', 'grad_ln_post_2': 'grad_w', 'grad_ln_pre_3': 'grad_w', 'grad_sb_w_in_3': 'grad_w', 'grad_sb_w_out_3': 'grad_w', 'grad_ln_post_3': 'grad_w', 'delta_ln_pre_0': 'delta_w', 'delta_conv_w_in_0': 'delta_w', 'delta_conv_w_0': 'delta_w', 'delta_conv_w_out_0': 'delta_w', 'delta_ln_post_0': 'delta_w', 'delta_ln_pre_1': 'delta_w', 'delta_sb_w_in_1': 'delta_w', 'delta_sb_w_out_1': 'delta_w', 'delta_ln_post_1': 'delta_w', 'delta_ln_pre_2': 'delta_w', 'delta_conv_w_in_2': 'delta_w', 'delta_conv_w_2': 'delta_w', 'delta_conv_w_out_2': 'delta_w', 'delta_ln_post_2': 'delta_w', 'delta_ln_pre_3': 'delta_w', 'delta_sb_w_in_3': 'delta_w', 'delta_sb_w_out_3': 'delta_w', 'delta_ln_post_3': 'delta_w', 'new_m_ln_pre_0': 'new_m', 'new_m_conv_w_in_0': 'new_m', 'new_m_conv_w_0': 'new_m', 'new_m_conv_w_out_0': 'new_m', 'new_m_ln_post_0': 'new_m', 'new_m_ln_pre_1': 'new_m', 'new_m_sb_w_in_1': 'new_m', 'new_m_sb_w_out_1': 'new_m', 'new_m_ln_post_1': 'new_m', 'new_m_ln_pre_2': 'new_m', 'new_m_conv_w_in_2': 'new_m', 'new_m_conv_w_2': 'new_m', 'new_m_conv_w_out_2': 'new_m', 'new_m_ln_post_2': 'new_m', 'new_m_ln_pre_3': 'new_m', 'new_m_sb_w_in_3': 'new_m', 'new_m_sb_w_out_3': 'new_m', 'new_m_ln_post_3': 'new_m', 'new_v_ln_pre_0': 'new_v', 'new_v_conv_w_in_0': 'new_v', 'new_v_conv_w_0': 'new_v', 'new_v_conv_w_out_0': 'new_v', 'new_v_ln_post_0': 'new_v', 'new_v_ln_pre_1': 'new_v', 'new_v_sb_w_in_1': 'new_v', 'new_v_sb_w_out_1': 'new_v', 'new_v_ln_post_1': 'new_v', 'new_v_ln_pre_2': 'new_v', 'new_v_conv_w_in_2': 'new_v', 'new_v_conv_w_2': 'new_v', 'new_v_conv_w_out_2': 'new_v', 'new_v_ln_post_2': 'new_v', 'new_v_ln_pre_3': 'new_v', 'new_v_sb_w_in_3': 'new_v', 'new_v_sb_w_out_3': 'new_v', 'new_v_ln_post_3': 'new_v'}


def _forward(args):
    return _fwd_reference(*[args[k] for k in FWD_PARAMS])


def _output_shape():
    def fwd():
        inp = _fwd_setup_inputs(0)
        return _fwd_reference(*[inp[k] for k in FWD_PARAMS])
    out = _jax.eval_shape(fwd)
    return out.shape, out.dtype

N_MICROBATCH = 1
ADAM_LR = 0.001
ADAM_B1 = 0.9
ADAM_B2 = 0.999
ADAM_EPS = 1e-08
ADAM_WD = 0.01
ADAM_STEP = 10
PER_EXAMPLE_BATCH_AXIS = {'x': 0, 'loss_target': 0}
SHARED_INPUTS = []
_WEIGHT_DTYPES = {'ln_pre_0': _jnp.float32, 'conv_w_in_0': _jnp.float32, 'conv_w_0': _jnp.float32, 'conv_w_out_0': _jnp.float32, 'ln_post_0': _jnp.float32, 'ln_pre_1': _jnp.float32, 'sb_w_in_1': _jnp.float32, 'sb_w_out_1': _jnp.float32, 'ln_post_1': _jnp.float32, 'ln_pre_2': _jnp.float32, 'conv_w_in_2': _jnp.float32, 'conv_w_2': _jnp.float32, 'conv_w_out_2': _jnp.float32, 'ln_post_2': _jnp.float32, 'ln_pre_3': _jnp.float32, 'sb_w_in_3': _jnp.float32, 'sb_w_out_3': _jnp.float32, 'ln_post_3': _jnp.float32}
MOMENT_SCALE = {'ln_pre_0': 1.223152e+00, 'conv_w_in_0': 5.994837e-01, 'conv_w_0': 5.966832e-01, 'conv_w_out_0': 5.866138e-01, 'ln_post_0': 1.599537e+01, 'ln_pre_1': 6.157242e-01, 'sb_w_in_1': 3.092669e-01, 'sb_w_out_1': 3.922823e-01, 'ln_post_1': 1.600795e+01, 'ln_pre_2': 5.144669e-01, 'conv_w_in_2': 2.579140e-01, 'conv_w_2': 2.552037e-01, 'conv_w_out_2': 2.524457e-01, 'ln_post_2': 1.598788e+01, 'ln_pre_3': 3.177731e-01, 'sb_w_in_3': 1.573888e-01, 'sb_w_out_3': 1.997452e-01, 'ln_post_3': 1.597171e+01}


def _to_microbatches(a, axis):
    t = _jnp.moveaxis(a, axis, 0)
    t = t.reshape((N_MICROBATCH, t.shape[0] // N_MICROBATCH) + t.shape[1:])
    return _jnp.moveaxis(t, 1, axis + 1)


def setup_inputs(seed: int = 0) -> dict:
    inp = _fwd_setup_inputs(seed)
    key = _jax.random.fold_in(_jax.random.key(seed), 7919)
    shape, _ = _output_shape()
    out = dict(inp)
    out["loss_target"] = _jax.random.normal(_jax.random.fold_in(key, 0), shape, _jnp.float32)
    for i, name in enumerate(TWIN_WEIGHTS):
        w = inp[name].astype(_jnp.float32)
        if MOMENT_SCALE is None:
            s = _jnp.sqrt(_jnp.mean(_jnp.square(w)) + 1e-30)
        else:
            s = MOMENT_SCALE[name]
        km, kv = _jax.random.split(_jax.random.fold_in(key, i + 1))
        out[name] = w
        out["m_" + name] = s * _jax.random.normal(km, w.shape, _jnp.float32)
        out["v_" + name] = (s * s) * _jax.random.uniform(kv, w.shape, _jnp.float32, 0.5, 1.5)
    if N_MICROBATCH > 1:
        for name, axis in PER_EXAMPLE_BATCH_AXIS.items():
            out[name] = _to_microbatches(out[name], axis)
    return {'x': out['x'], 'ln_pre_0': out['ln_pre_0'], 'conv_w_in_0': out['conv_w_in_0'], 'conv_w_0': out['conv_w_0'], 'conv_w_out_0': out['conv_w_out_0'], 'ln_post_0': out['ln_post_0'], 'ln_pre_1': out['ln_pre_1'], 'sb_w_in_1': out['sb_w_in_1'], 'sb_w_out_1': out['sb_w_out_1'], 'ln_post_1': out['ln_post_1'], 'ln_pre_2': out['ln_pre_2'], 'conv_w_in_2': out['conv_w_in_2'], 'conv_w_2': out['conv_w_2'], 'conv_w_out_2': out['conv_w_out_2'], 'ln_post_2': out['ln_post_2'], 'ln_pre_3': out['ln_pre_3'], 'sb_w_in_3': out['sb_w_in_3'], 'sb_w_out_3': out['sb_w_out_3'], 'ln_post_3': out['ln_post_3'], 'loss_target': out['loss_target'], 'm_ln_pre_0': out['m_ln_pre_0'], 'm_conv_w_in_0': out['m_conv_w_in_0'], 'm_conv_w_0': out['m_conv_w_0'], 'm_conv_w_out_0': out['m_conv_w_out_0'], 'm_ln_post_0': out['m_ln_post_0'], 'm_ln_pre_1': out['m_ln_pre_1'], 'm_sb_w_in_1': out['m_sb_w_in_1'], 'm_sb_w_out_1': out['m_sb_w_out_1'], 'm_ln_post_1': out['m_ln_post_1'], 'm_ln_pre_2': out['m_ln_pre_2'], 'm_conv_w_in_2': out['m_conv_w_in_2'], 'm_conv_w_2': out['m_conv_w_2'], 'm_conv_w_out_2': out['m_conv_w_out_2'], 'm_ln_post_2': out['m_ln_post_2'], 'm_ln_pre_3': out['m_ln_pre_3'], 'm_sb_w_in_3': out['m_sb_w_in_3'], 'm_sb_w_out_3': out['m_sb_w_out_3'], 'm_ln_post_3': out['m_ln_post_3'], 'v_ln_pre_0': out['v_ln_pre_0'], 'v_conv_w_in_0': out['v_conv_w_in_0'], 'v_conv_w_0': out['v_conv_w_0'], 'v_conv_w_out_0': out['v_conv_w_out_0'], 'v_ln_post_0': out['v_ln_post_0'], 'v_ln_pre_1': out['v_ln_pre_1'], 'v_sb_w_in_1': out['v_sb_w_in_1'], 'v_sb_w_out_1': out['v_sb_w_out_1'], 'v_ln_post_1': out['v_ln_post_1'], 'v_ln_pre_2': out['v_ln_pre_2'], 'v_conv_w_in_2': out['v_conv_w_in_2'], 'v_conv_w_2': out['v_conv_w_2'], 'v_conv_w_out_2': out['v_conv_w_out_2'], 'v_ln_post_2': out['v_ln_post_2'], 'v_ln_pre_3': out['v_ln_pre_3'], 'v_sb_w_in_3': out['v_sb_w_in_3'], 'v_sb_w_out_3': out['v_sb_w_out_3'], 'v_ln_post_3': out['v_ln_post_3']}


def _loss(weights, diff, rest, loss_target):
    with _jax.named_scope("forward"):
        args = {**rest, TWIN_DIFF_INPUT: diff, **{k: w.astype(_WEIGHT_DTYPES[k]) for k, w in weights.items()}}
        y = _forward(args)
    with _jax.named_scope("loss_head"):
        err = _jnp.square(y.astype(_jnp.float32) - loss_target)
        return 0.5 * _jnp.sum(_jnp.mean(err, axis=-1)) if err.ndim else 0.5 * err


def _adamw(w, g, m, v):
    m = ADAM_B1 * m + (1.0 - ADAM_B1) * g
    v = ADAM_B2 * v + (1.0 - ADAM_B2) * _jnp.square(g)
    m_hat = m / (1.0 - ADAM_B1 ** ADAM_STEP)
    v_hat = v / (1.0 - ADAM_B2 ** ADAM_STEP)
    delta = -ADAM_LR * (m_hat / (_jnp.sqrt(v_hat) + ADAM_EPS) + ADAM_WD * w)
    return delta, m, v


def reference(x, ln_pre_0, conv_w_in_0, conv_w_0, conv_w_out_0, ln_post_0, ln_pre_1, sb_w_in_1, sb_w_out_1, ln_post_1, ln_pre_2, conv_w_in_2, conv_w_2, conv_w_out_2, ln_post_2, ln_pre_3, sb_w_in_3, sb_w_out_3, ln_post_3, loss_target, m_ln_pre_0, m_conv_w_in_0, m_conv_w_0, m_conv_w_out_0, m_ln_post_0, m_ln_pre_1, m_sb_w_in_1, m_sb_w_out_1, m_ln_post_1, m_ln_pre_2, m_conv_w_in_2, m_conv_w_2, m_conv_w_out_2, m_ln_post_2, m_ln_pre_3, m_sb_w_in_3, m_sb_w_out_3, m_ln_post_3, v_ln_pre_0, v_conv_w_in_0, v_conv_w_0, v_conv_w_out_0, v_ln_post_0, v_ln_pre_1, v_sb_w_in_1, v_sb_w_out_1, v_ln_post_1, v_ln_pre_2, v_conv_w_in_2, v_conv_w_2, v_conv_w_out_2, v_ln_post_2, v_ln_pre_3, v_sb_w_in_3, v_sb_w_out_3, v_ln_post_3):
    given = dict(x=x, ln_pre_0=ln_pre_0, conv_w_in_0=conv_w_in_0, conv_w_0=conv_w_0, conv_w_out_0=conv_w_out_0, ln_post_0=ln_post_0, ln_pre_1=ln_pre_1, sb_w_in_1=sb_w_in_1, sb_w_out_1=sb_w_out_1, ln_post_1=ln_post_1, ln_pre_2=ln_pre_2, conv_w_in_2=conv_w_in_2, conv_w_2=conv_w_2, conv_w_out_2=conv_w_out_2, ln_post_2=ln_post_2, ln_pre_3=ln_pre_3, sb_w_in_3=sb_w_in_3, sb_w_out_3=sb_w_out_3, ln_post_3=ln_post_3, loss_target=loss_target, m_ln_pre_0=m_ln_pre_0, m_conv_w_in_0=m_conv_w_in_0, m_conv_w_0=m_conv_w_0, m_conv_w_out_0=m_conv_w_out_0, m_ln_post_0=m_ln_post_0, m_ln_pre_1=m_ln_pre_1, m_sb_w_in_1=m_sb_w_in_1, m_sb_w_out_1=m_sb_w_out_1, m_ln_post_1=m_ln_post_1, m_ln_pre_2=m_ln_pre_2, m_conv_w_in_2=m_conv_w_in_2, m_conv_w_2=m_conv_w_2, m_conv_w_out_2=m_conv_w_out_2, m_ln_post_2=m_ln_post_2, m_ln_pre_3=m_ln_pre_3, m_sb_w_in_3=m_sb_w_in_3, m_sb_w_out_3=m_sb_w_out_3, m_ln_post_3=m_ln_post_3, v_ln_pre_0=v_ln_pre_0, v_conv_w_in_0=v_conv_w_in_0, v_conv_w_0=v_conv_w_0, v_conv_w_out_0=v_conv_w_out_0, v_ln_post_0=v_ln_post_0, v_ln_pre_1=v_ln_pre_1, v_sb_w_in_1=v_sb_w_in_1, v_sb_w_out_1=v_sb_w_out_1, v_ln_post_1=v_ln_post_1, v_ln_pre_2=v_ln_pre_2, v_conv_w_in_2=v_conv_w_in_2, v_conv_w_2=v_conv_w_2, v_conv_w_out_2=v_conv_w_out_2, v_ln_post_2=v_ln_post_2, v_ln_pre_3=v_ln_pre_3, v_sb_w_in_3=v_sb_w_in_3, v_sb_w_out_3=v_sb_w_out_3, v_ln_post_3=v_ln_post_3)
    weights = {n: given[n] for n in TWIN_WEIGHTS}
    shared = {n: given[n] for n in SHARED_INPUTS}
    per_example = {n: given[n] for n in ['x']}
    grad_fn = _jax.value_and_grad(_loss, argnums=(0, 1))

    def one_microbatch(ex, loss_target):
        ex = dict(ex)
        diff = ex.pop(TWIN_DIFF_INPUT)
        return grad_fn(weights, diff, {**shared, **ex}, loss_target)

    if N_MICROBATCH == 1:
        loss, (grad_w, grad_x) = one_microbatch(per_example, given["loss_target"])
    else:
        def body(carry, xs):
            loss_sum, grad_sum = carry
            l_k, (gw_k, gx_k) = one_microbatch(xs[0], xs[1])
            with _jax.named_scope("update"):
                return (loss_sum + l_k, _jax.tree.map(_jnp.add, grad_sum, gw_k)), gx_k

        init = (_jnp.zeros((), _jnp.float32), _jax.tree.map(_jnp.zeros_like, weights))
        (loss, grad_w), grad_x = _jax.lax.scan(body, init, (per_example, given["loss_target"]))
    with _jax.named_scope("update"):
        delta_w, new_m, new_v = {}, {}, {}
        for n in TWIN_WEIGHTS:
            delta_w[n], new_m[n], new_v[n] = _adamw(weights[n], grad_w[n], given["m_" + n], given["v_" + n])
    return (loss, grad_x, *[grad_w[n] for n in TWIN_WEIGHTS], *[delta_w[n] for n in TWIN_WEIGHTS],
            *[new_m[n] for n in TWIN_WEIGHTS], *[new_v[n] for n in TWIN_WEIGHTS])
```

```python
import functools
import math

import jax
import jax.numpy as jnp
from jax import lax
from jax.experimental import pallas as pl
from jax.experimental.pallas import tpu as pltpu

F32 = jnp.float32
BF16 = jnp.bfloat16

N_CHIPS = 4
N_DEV = 8
N_LAYERS = 4
HEAD_DIM = 128
RMS_EPS = 1e-6
ADAM_LR = 0.001
ADAM_B1 = 0.9
ADAM_B2 = 0.999
ADAM_EPS = 1e-08
ADAM_WD = 0.01
ADAM_STEP = 10

VMEM_LIMIT = 56 * 1024 * 1024
MESH_IDS = pl.DeviceIdType.MESH
HBM_SPEC = pl.BlockSpec(memory_space=pltpu.HBM)

NN = (((1,), (0,)), ((), ()))
NT = (((1,), (1,)), ((), ()))
TN = (((0,), (0,)), ((), ()))


def _params(n_axes):
    return pltpu.CompilerParams(dimension_semantics=("arbitrary",) * n_axes, vmem_limit_bytes=VMEM_LIMIT)


def _dot(a, b, dims):
    return lax.dot_general(a, b, dims, preferred_element_type=F32)


def _sigmoid(z):
    return 1.0 / (1.0 + jnp.exp(-z))


def _matmul(name, a, b, *, grid, a_spec, b_spec, o_spec, out_shape, dims, reduce_axis=None, acc_shape=None,
            alias_out=None):
    out_dtype = out_shape.dtype
    direct = reduce_axis is not None and out_dtype == F32
    n_red = grid[reduce_axis] if reduce_axis is not None else 1

    def body(*refs):
        if alias_out is not None:
            refs = refs[1:]
        a_ref, b_ref, o_ref = refs[:3]
        prod = _dot(a_ref[...], b_ref[...], dims)
        if reduce_axis is None:
            o_ref[...] = prod.astype(out_dtype)
            return
        acc_ref = o_ref if direct else refs[3]
        k = pl.program_id(reduce_axis)

        @pl.when(k == 0)
        def _():
            acc_ref[...] = prod

        @pl.when(k > 0)
        def _():
            acc_ref[...] += prod

        if not direct:
            @pl.when(k == n_red - 1)
            def _():
                o_ref[...] = acc_ref[...].astype(out_dtype)

    scratch = []
    if reduce_axis is not None and not direct:
        scratch = [pltpu.VMEM(acc_shape, F32)]
    in_specs = [a_spec, b_spec]
    operands = [a, b]
    aliases = {}
    if alias_out is not None:
        in_specs = [HBM_SPEC] + in_specs
        operands = [alias_out] + operands
        aliases = {0: 0}
    return pl.pallas_call(
        body, name=name, grid=grid, in_specs=in_specs, out_specs=o_spec, out_shape=out_shape,
        scratch_shapes=scratch, input_output_aliases=aliases, compiler_params=_params(len(grid)),
    )(*operands)


def _mm_proj(u, g):
    t, d = u.shape
    tm = min(512, t)
    return _matmul(
        "mm_proj", u, g, grid=(N_CHIPS, t // tm),
        a_spec=pl.BlockSpec((tm, d), lambda s, m: (m, 0)),
        b_spec=pl.BlockSpec((None, d, d), lambda s, m: (s, 0, 0)),
        o_spec=pl.BlockSpec((None, tm, d), lambda s, m: (s, m, 0)),
        out_shape=jax.ShapeDtypeStruct((N_CHIPS, t, d), BF16), dims=NN)


def _mm_out(gated, g):
    t, d = gated.shape
    dq = d // N_CHIPS
    tm = min(512, t)
    return _matmul(
        "mm_out", gated, g, grid=(t // tm, N_CHIPS),
        a_spec=pl.BlockSpec((tm, dq), lambda m, s: (m, s)),
        b_spec=pl.BlockSpec((None, dq, d), lambda m, s: (s, N_CHIPS, 0)),
        o_spec=pl.BlockSpec((tm, d), lambda m, s: (m, 0)),
        out_shape=jax.ShapeDtypeStruct((t, d), F32), dims=NN, reduce_axis=1)


def _mm_dgated(dm, g):
    t, d = dm.shape
    dq = d // N_CHIPS
    tm = min(512, t)
    return _matmul(
        "mm_dgated", dm, g, grid=(t // tm, N_CHIPS),
        a_spec=pl.BlockSpec((tm, d), lambda m, s: (m, 0)),
        b_spec=pl.BlockSpec((None, dq, d), lambda m, s: (s, N_CHIPS, 0)),
        o_spec=pl.BlockSpec((tm, dq), lambda m, s: (m, s)),
        out_shape=jax.ShapeDtypeStruct((t, d), BF16), dims=NT)


def _mm_dwout(gated, dm):
    t, d = gated.shape
    dq = d // N_CHIPS
    tk = min(512, t)
    return _matmul(
        "mm_dwout", gated, dm, grid=(N_CHIPS, t // tk),
        a_spec=pl.BlockSpec((tk, dq), lambda s, k: (k, s)),
        b_spec=pl.BlockSpec((tk, d), lambda s, k: (k, 0)),
        o_spec=pl.BlockSpec((None, dq, d), lambda s, k: (s, N_CHIPS, 0)),
        out_shape=jax.ShapeDtypeStruct((N_CHIPS, d + dq, d), BF16), dims=TN, reduce_axis=1, acc_shape=(dq, d))


def _mm_du(dproj, g):
    _, t, d = dproj.shape
    tm = min(512, t)
    return _matmul(
        "mm_du", dproj, g, grid=(t // tm, N_CHIPS),
        a_spec=pl.BlockSpec((None, tm, d), lambda m, s: (s, m, 0)),
        b_spec=pl.BlockSpec((None, d, d), lambda m, s: (s, 0, 0)),
        o_spec=pl.BlockSpec((tm, d), lambda m, s: (m, 0)),
        out_shape=jax.ShapeDtypeStruct((t, d), F32), dims=NT, reduce_axis=1)


def _mm_dwin(u, dproj, gp):
    t, d = u.shape
    tmo = min(1024, d)
    tk = min(512, t)
    return _matmul(
        "mm_dwin", u, dproj, grid=(N_CHIPS, d // tmo, t // tk),
        a_spec=pl.BlockSpec((tk, tmo), lambda s, mo, k: (k, mo)),
        b_spec=pl.BlockSpec((None, tk, d), lambda s, mo, k: (s, k, 0)),
        o_spec=pl.BlockSpec((None, tmo, d), lambda s, mo, k: (s, mo, 0)),
        out_shape=jax.ShapeDtypeStruct(gp.shape, BF16), dims=TN, reduce_axis=2, acc_shape=(tmo, d), alias_out=gp)


def _rms(v):
    r = lax.rsqrt(jnp.mean(v * v, axis=-1, keepdims=True) + RMS_EPS)
    return v * r, r


def _rms_bwd(dout, n, r, gain):
    dn = dout * gain
    return r * (dn - n * jnp.mean(dn * n, axis=-1, keepdims=True))


def _fold8(v):
    return jnp.sum(v.reshape(v.shape[0] // 8, 8, v.shape[1]), axis=0)


def _row0(total):
    rows = lax.broadcasted_iota(jnp.int32, total.shape, 0)
    return jnp.where(rows == 0, jnp.sum(total, axis=0, keepdims=True), 0.0)


def _norm_tile(t):
    return min(256, t)


def _norm_first(x, g_pre):
    t, d = x.shape
    tr = _norm_tile(t)

    def body(x_ref, g_ref, u_ref):
        n, _ = _rms(x_ref[...])
        u_ref[...] = (n * g_ref[...]).astype(BF16)

    row = pl.BlockSpec((tr, d), lambda i: (i, 0))
    vec = pl.BlockSpec((1, d), lambda i: (0, 0))
    return pl.pallas_call(
        body, name="norm_first", grid=(t // tr,), in_specs=[row, vec], out_specs=row,
        out_shape=jax.ShapeDtypeStruct((t, d), BF16), compiler_params=_params(1),
    )(x, g_pre.reshape(1, d))


def _norm_mid(h, m, g_post, g_pre_next):
    t, d = h.shape
    tr = _norm_tile(t)

    def body(h_ref, m_ref, gp_ref, gn_ref, hn_ref, u_ref):
        n, _ = _rms(m_ref[...])
        hn = h_ref[...] + n * gp_ref[...]
        hn_ref[...] = hn
        n2, _ = _rms(hn)
        u_ref[...] = (n2 * gn_ref[...]).astype(BF16)

    row = pl.BlockSpec((tr, d), lambda i: (i, 0))
    vec = pl.BlockSpec((1, d), lambda i: (0, 0))
    return pl.pallas_call(
        body, name="norm_mid", grid=(t // tr,), in_specs=[row, row, vec, vec], out_specs=[row, row],
        out_shape=[jax.ShapeDtypeStruct((t, d), F32), jax.ShapeDtypeStruct((t, d), BF16)],
        compiler_params=_params(1),
    )(h, m, g_post.reshape(1, d), g_pre_next.reshape(1, d))


def _norm_last(h, m, g_post, target):
    t, d = h.shape
    tr = _norm_tile(t)
    nsteps = t // tr

    def body(h_ref, m_ref, gp_ref, tg_ref, dy_ref, dm_ref, dgp_ref, loss_ref, acc_g, acc_l):
        i = pl.program_id(0)

        @pl.when(i == 0)
        def _():
            acc_g[...] = jnp.zeros_like(acc_g)
            acc_l[...] = jnp.zeros_like(acc_l)

        gain = gp_ref[...]
        n, r = _rms(m_ref[...])
        err = h_ref[...] + n * gain - tg_ref[...]
        dy = err / d
        dy_ref[...] = dy
        dm_ref[...] = _rms_bwd(dy, n, r, gain).astype(BF16)
        acc_g[...] += _fold8(dy * n)
        acc_l[...] += _fold8(err * err)

        @pl.when(i == nsteps - 1)
        def _():
            dgp_ref[...] = _row0(acc_g[...])
            loss_ref[...] = jnp.zeros((8, 128), F32) + (0.5 / d) * jnp.sum(acc_l[...])

    row = pl.BlockSpec((tr, d), lambda i: (i, 0))
    vec = pl.BlockSpec((1, d), lambda i: (0, 0))
    acc = pl.BlockSpec((8, d), lambda i: (0, 0))
    return pl.pallas_call(
        body, name="norm_last", grid=(nsteps,), in_specs=[row, row, vec, row],
        out_specs=[row, row, acc, pl.BlockSpec((8, 128), lambda i: (0, 0))],
        out_shape=[jax.ShapeDtypeStruct((t, d), F32), jax.ShapeDtypeStruct((t, d), BF16),
                   jax.ShapeDtypeStruct((8, d), F32), jax.ShapeDtypeStruct((8, 128), F32)],
        scratch_shapes=[pltpu.VMEM((8, d), F32), pltpu.VMEM((8, d), F32)],
        compiler_params=_params(1),
    )(h, m, g_post.reshape(1, d), target)


def _norm_bwd_mid(dh, du, h_in, g_pre, m_prev, g_post_prev):
    t, d = dh.shape
    tr = _norm_tile(t)
    nsteps = t // tr

    def body(dh_ref, du_ref, h_ref, gpre_ref, m_ref, gpost_ref, dhn_ref, dm_ref, dgpre_ref, dgpost_ref, acc_a, acc_b):
        i = pl.program_id(0)

        @pl.when(i == 0)
        def _():
            acc_a[...] = jnp.zeros_like(acc_a)
            acc_b[...] = jnp.zeros_like(acc_b)

        du_t = du_ref[...]
        n, r = _rms(h_ref[...])
        dhn = dh_ref[...] + _rms_bwd(du_t, n, r, gpre_ref[...])
        dhn_ref[...] = dhn
        acc_a[...] += _fold8(du_t * n)
        n2, r2 = _rms(m_ref[...])
        dm_ref[...] = _rms_bwd(dhn, n2, r2, gpost_ref[...]).astype(BF16)
        acc_b[...] += _fold8(dhn * n2)

        @pl.when(i == nsteps - 1)
        def _():
            dgpre_ref[...] = _row0(acc_a[...])
            dgpost_ref[...] = _row0(acc_b[...])

    row = pl.BlockSpec((tr, d), lambda i: (i, 0))
    vec = pl.BlockSpec((1, d), lambda i: (0, 0))
    acc = pl.BlockSpec((8, d), lambda i: (0, 0))
    return pl.pallas_call(
        body, name="norm_bwd_mid", grid=(nsteps,), in_specs=[row, row, row, vec, row, vec],
        out_specs=[row, row, acc, acc],
        out_shape=[jax.ShapeDtypeStruct((t, d), F32), jax.ShapeDtypeStruct((t, d), BF16),
                   jax.ShapeDtypeStruct((8, d), F32), jax.ShapeDtypeStruct((8, d), F32)],
        scratch_shapes=[pltpu.VMEM((8, d), F32), pltpu.VMEM((8, d), F32)],
        compiler_params=_params(1),
    )(dh, du, h_in, g_pre.reshape(1, d), m_prev, g_post_prev.reshape(1, d))


def _norm_bwd_first(dh, du, x, g_pre):
    t, d = dh.shape
    tr = _norm_tile(t)
    nsteps = t // tr

    def body(dh_ref, du_ref, x_ref, gpre_ref, dx_ref, dgpre_ref, acc_a):
        i = pl.program_id(0)

        @pl.when(i == 0)
        def _():
            acc_a[...] = jnp.zeros_like(acc_a)

        du_t = du_ref[...]
        n, r = _rms(x_ref[...])
        dx_ref[...] = dh_ref[...] + _rms_bwd(du_t, n, r, gpre_ref[...])
        acc_a[...] += _fold8(du_t * n)

        @pl.when(i == nsteps - 1)
        def _():
            dgpre_ref[...] = _row0(acc_a[...])

    row = pl.BlockSpec((tr, d), lambda i: (i, 0))
    vec = pl.BlockSpec((1, d), lambda i: (0, 0))
    acc = pl.BlockSpec((8, d), lambda i: (0, 0))
    return pl.pallas_call(
        body, name="norm_bwd_first", grid=(nsteps,), in_specs=[row, row, row, vec], out_specs=[row, acc],
        out_shape=[jax.ShapeDtypeStruct((t, d), F32), jax.ShapeDtypeStruct((8, d), F32)],
        scratch_shapes=[pltpu.VMEM((8, d), F32)], compiler_params=_params(1),
    )(dh, du, x, g_pre.reshape(1, d))


CONV_TC = 128
CONV_HALO = 16


def _conv_chunk(t):
    return min(512, t)


def _shift_down(v, steps, fill):
    rows = lax.broadcasted_iota(jnp.int32, v.shape, 0)
    out = pltpu.roll(v, steps, axis=0)
    for k in range(steps):
        out = jnp.where(rows == k, fill[CONV_HALO - steps + k:CONV_HALO - steps + k + 1, :], out)
    return out


def _shift_up(v, steps, fill):
    nrows = v.shape[0]
    rows = lax.broadcasted_iota(jnp.int32, v.shape, 0)
    out = pltpu.roll(v, nrows - steps, axis=0)
    for k in range(steps):
        out = jnp.where(rows == nrows - steps + k, fill[k:k + 1, :], out)
    return out


def _conv_fwd(proj, cw):
    _, t, d = proj.shape
    chunk = _conv_chunk(t)

    def body(p_ref, w_ref, o_ref):
        w = w_ref[...]
        w0, w1, w2 = w[0:1, :], w[1:2, :], w[2:3, :]
        for ci in range(t // chunk):
            t0 = ci * chunk
            rows = pl.ds(t0, chunk)
            b = p_ref[0, rows, :].astype(F32)
            cx = p_ref[1, rows, :].astype(F32) * p_ref[2, rows, :].astype(F32)
            z = p_ref[3, rows, :].astype(F32)
            if ci == 0:
                prev = jnp.zeros((CONV_HALO, CONV_TC), F32)
            else:
                halo = pl.ds(t0 - CONV_HALO, CONV_HALO)
                prev = p_ref[1, halo, :].astype(F32) * p_ref[2, halo, :].astype(F32)
            conv = w2 * cx + w1 * _shift_down(cx, 1, prev) + w0 * _shift_down(cx, 2, prev)
            o_ref[rows, :] = (z * _sigmoid(z) * b * conv).astype(BF16)

    return pl.pallas_call(
        body, name="conv_fwd", grid=(d // CONV_TC,),
        in_specs=[pl.BlockSpec((N_CHIPS, t, CONV_TC), lambda j: (0, 0, j)), pl.BlockSpec((8, CONV_TC), lambda j: (0, j))],
        out_specs=pl.BlockSpec((t, CONV_TC), lambda j: (0, j)),
        out_shape=jax.ShapeDtypeStruct((t, d), BF16), compiler_params=_params(1),
    )(proj, cw)


def _conv_bwd(proj, dgated, cw):
    _, t, d = proj.shape
    chunk = _conv_chunk(t)
    nchunks = t // chunk

    def body(p_ref, dg_ref, w_ref, dp_ref, dw_ref):
        w = w_ref[...]
        w0, w1, w2 = w[0:1, :], w[1:2, :], w[2:3, :]
        dw0 = jnp.zeros((1, CONV_TC), F32)
        dw1 = jnp.zeros((1, CONV_TC), F32)
        dw2 = jnp.zeros((1, CONV_TC), F32)
        for ci in range(nchunks):
            t0 = ci * chunk
            rows = pl.ds(t0, chunk)
            b = p_ref[0, rows, :].astype(F32)
            c = p_ref[1, rows, :].astype(F32)
            xt = p_ref[2, rows, :].astype(F32)
            z = p_ref[3, rows, :].astype(F32)
            dg = dg_ref[rows, :].astype(F32)
            cx = c * xt
            if ci == 0:
                prev = jnp.zeros((CONV_HALO, CONV_TC), F32)
            else:
                halo = pl.ds(t0 - CONV_HALO, CONV_HALO)
                prev = p_ref[1, halo, :].astype(F32) * p_ref[2, halo, :].astype(F32)
            cx1 = _shift_down(cx, 1, prev)
            cx2 = _shift_down(cx, 2, prev)
            conv = w2 * cx + w1 * cx1 + w0 * cx2
            sig = _sigmoid(z)
            dy = dg * (z * sig)
            dconv = dy * b
            if ci == nchunks - 1:
                nxt = jnp.zeros((CONV_HALO, CONV_TC), F32)
            else:
                halo = pl.ds(t0 + chunk, CONV_HALO)
                zn = p_ref[3, halo, :].astype(F32)
                nxt = dg_ref[halo, :].astype(F32) * (zn * _sigmoid(zn)) * p_ref[0, halo, :].astype(F32)
            dcx = w2 * dconv + w1 * _shift_up(dconv, 1, nxt) + w0 * _shift_up(dconv, 2, nxt)
            dp_ref[0, rows, :] = (dy * conv).astype(BF16)
            dp_ref[1, rows, :] = (dcx * xt).astype(BF16)
            dp_ref[2, rows, :] = (dcx * c).astype(BF16)
            dp_ref[3, rows, :] = (dg * (b * conv) * (sig * (1.0 + z * (1.0 - sig)))).astype(BF16)
            dw0 = dw0 + jnp.sum(dconv * cx2, axis=0, keepdims=True)
            dw1 = dw1 + jnp.sum(dconv * cx1, axis=0, keepdims=True)
            dw2 = dw2 + jnp.sum(dconv * cx, axis=0, keepdims=True)
        taps = lax.broadcasted_iota(jnp.int32, (8, CONV_TC), 0)
        dw_ref[...] = jnp.where(taps == 0, dw0, jnp.where(taps == 1, dw1, jnp.where(taps == 2, dw2, 0.0)))

    return pl.pallas_call(
        body, name="conv_bwd", grid=(d // CONV_TC,),
        in_specs=[pl.BlockSpec((N_CHIPS, t, CONV_TC), lambda j: (0, 0, j)),
                  pl.BlockSpec((t, CONV_TC), lambda j: (0, j)),
                  pl.BlockSpec((8, CONV_TC), lambda j: (0, j))],
        out_specs=[pl.BlockSpec((N_CHIPS, t, CONV_TC), lambda j: (0, 0, j)), pl.BlockSpec((8, CONV_TC), lambda j: (0, j))],
        out_shape=[jax.ShapeDtypeStruct((N_CHIPS, t, d), BF16), jax.ShapeDtypeStruct((8, d), F32)],
        compiler_params=_params(1),
    )(proj, dgated, cw)


def _sb_block(t):
    return min(256, t)


def _split_dot(v, tri):
    hi = v.astype(BF16)
    lo = (v - hi.astype(F32)).astype(BF16)
    return _dot(hi, tri, NN) + _dot(lo, tri, NN)


def _sb_scores(q, kj, q0, k0, scale):
    blk_q, blk_k = q.shape[0], kj.shape[0]
    s = _dot(q, kj, NT) * scale
    sp = jnp.maximum(s, 0.0) + jnp.log1p(jnp.exp(-jnp.abs(s)))
    rows = lax.broadcasted_iota(jnp.int32, (blk_q, blk_k), 0) + q0
    cols = lax.broadcasted_iota(jnp.int32, (blk_q, blk_k), 1) + k0
    mask = cols < rows
    keep = jnp.where(mask, -sp, 0.0)
    return keep, s - sp, sp, mask


def _sb_fwd(proj):
    _, t, d = proj.shape
    heads = d // HEAD_DIM
    blk = _sb_block(t)
    nblk = t // blk
    scale = 1.0 / math.sqrt(HEAD_DIM)

    def body(q_ref, k_ref, v_ref, z_ref, gated_ref, o_ref, car_ref, tail_ref, acc_ref):
        i = pl.program_id(1)
        q = q_ref[...]
        r_i = lax.broadcasted_iota(jnp.int32, (blk, blk), 0)
        c_i = lax.broadcasted_iota(jnp.int32, (blk, blk), 1)
        tri_after = (r_i > c_i).astype(BF16)
        lanes = lax.broadcasted_iota(jnp.int32, (blk, HEAD_DIM), 1)

        tail_ref[...] = jnp.zeros_like(tail_ref)
        acc_ref[...] = jnp.zeros_like(acc_ref)
        car_ref[...] = jnp.zeros_like(car_ref)

        def step(jj, carry):
            j = i - jj
            k0 = pl.multiple_of(j * blk, blk)
            kj = k_ref[pl.ds(k0, blk), :]
            vj = v_ref[pl.ds(k0, blk), :]
            keep, log_beta, _, mask = _sb_scores(q, kj, i * blk, k0, scale)
            within = _split_dot(keep, tri_after)
            tail_b = tail_ref[...]
            w = jnp.where(mask, jnp.exp(log_beta + tail_b[:, 0:1] + within), 0.0)
            acc_ref[...] += _dot(w.astype(BF16), vj, NN)
            car_ref[...] = jnp.where(lanes == j, tail_b, car_ref[...])
            tail_ref[...] = tail_b + jnp.sum(keep, axis=1, keepdims=True)
            return carry

        lax.fori_loop(0, i + 1, step, 0)
        z = z_ref[...].astype(F32)
        acc = acc_ref[...]
        o_ref[...] = acc.astype(BF16)
        gated_ref[...] = (z * _sigmoid(z) * acc).astype(BF16)

    qspec = lambda s: pl.BlockSpec((None, blk, HEAD_DIM), lambda h, i: (s, i, h))
    kspec = lambda s: pl.BlockSpec((None, t, HEAD_DIM), lambda h, i: (s, 0, h))
    ospec = pl.BlockSpec((blk, HEAD_DIM), lambda h, i: (i, h))
    return pl.pallas_call(
        body, name="sb_fwd", grid=(heads, nblk),
        in_specs=[qspec(0), kspec(1), kspec(2), qspec(3)],
        out_specs=[ospec, ospec, pl.BlockSpec((None, blk, HEAD_DIM), lambda h, i: (h, i, 0))],
        out_shape=[jax.ShapeDtypeStruct((t, d), BF16), jax.ShapeDtypeStruct((t, d), BF16),
                   jax.ShapeDtypeStruct((heads, t, HEAD_DIM), F32)],
        scratch_shapes=[pltpu.VMEM((blk, HEAD_DIM), F32), pltpu.VMEM((blk, HEAD_DIM), F32)],
        compiler_params=_params(2),
    )(proj, proj, proj, proj)


def _sb_bwd(proj, dgated, o, car):
    _, t, d = proj.shape
    heads = d // HEAD_DIM
    blk = _sb_block(t)
    nblk = t // blk
    scale = 1.0 / math.sqrt(HEAD_DIM)

    def body(q_ref, k_ref, v_ref, z_ref, dg_ref, o_ref, car_ref, dp_ref, dk_acc, dv_acc, gsum_ref, dq_ref):
        step_i = pl.program_id(1)
        i = nblk - 1 - step_i

        @pl.when(step_i == 0)
        def _():
            dk_acc[...] = jnp.zeros_like(dk_acc)
            dv_acc[...] = jnp.zeros_like(dv_acc)

        q = q_ref[...]
        z = z_ref[...].astype(F32)
        dg = dg_ref[...].astype(F32)
        sig = _sigmoid(z)
        d_o = (dg * (z * sig)).astype(BF16)
        car_all = car_ref[...]
        r_i = lax.broadcasted_iota(jnp.int32, (blk, blk), 0)
        c_i = lax.broadcasted_iota(jnp.int32, (blk, blk), 1)
        tri_after = (r_i > c_i).astype(BF16)
        tri_before = (r_i < c_i).astype(BF16)
        lanes = lax.broadcasted_iota(jnp.int32, (blk, HEAD_DIM), 1)

        gsum_ref[...] = jnp.zeros_like(gsum_ref)
        dq_ref[...] = jnp.zeros_like(dq_ref)

        def step(j, carry):
            k0 = pl.multiple_of(j * blk, blk)
            kj = k_ref[pl.ds(k0, blk), :]
            vj = v_ref[pl.ds(k0, blk), :]
            keep, log_beta, sp, mask = _sb_scores(q, kj, i * blk, k0, scale)
            tail = jnp.sum(jnp.where(lanes == j, car_all, 0.0), axis=1, keepdims=True)
            w = jnp.where(mask, jnp.exp(log_beta + tail + _split_dot(keep, tri_after)), 0.0)
            g = w * _dot(d_o, vj, NT)
            g_before = gsum_ref[...]
            g_cum = g_before[:, 0:1] + _split_dot(g, tri_before)
            dl = jnp.where(mask, g * jnp.exp(-sp) - g_cum * jnp.exp(log_beta), 0.0) * scale
            dl = dl.astype(BF16)
            dq_ref[...] += _dot(dl, kj, NN)
            dk_acc[pl.ds(k0, blk), :] += _dot(dl, q, TN)
            dv_acc[pl.ds(k0, blk), :] += _dot(w.astype(BF16), d_o, TN)
            gsum_ref[...] = g_before + jnp.sum(g, axis=1, keepdims=True)
            return carry

        lax.fori_loop(0, i + 1, step, 0)
        own = pl.ds(pl.multiple_of(i * blk, blk), blk)
        dp_ref[0] = dq_ref[...].astype(BF16)
        dp_ref[1] = dk_acc[own, :].astype(BF16)
        dp_ref[2] = dv_acc[own, :].astype(BF16)
        dp_ref[3] = (dg * o_ref[...].astype(F32) * (sig * (1.0 + z * (1.0 - sig)))).astype(BF16)

    qspec = lambda s: pl.BlockSpec((None, blk, HEAD_DIM), lambda h, i: (s, nblk - 1 - i, h))
    kspec = lambda s: pl.BlockSpec((None, t, HEAD_DIM), lambda h, i: (s, 0, h))
    tspec = pl.BlockSpec((blk, HEAD_DIM), lambda h, i: (nblk - 1 - i, h))
    return pl.pallas_call(
        body, name="sb_bwd", grid=(heads, nblk),
        in_specs=[qspec(0), kspec(1), kspec(2), qspec(3), tspec, tspec,
                  pl.BlockSpec((None, blk, HEAD_DIM), lambda h, i: (h, nblk - 1 - i, 0))],
        out_specs=pl.BlockSpec((N_CHIPS, blk, HEAD_DIM), lambda h, i: (0, nblk - 1 - i, h)),
        out_shape=jax.ShapeDtypeStruct((N_CHIPS, t, d), BF16),
        scratch_shapes=[pltpu.VMEM((t, HEAD_DIM), F32), pltpu.VMEM((t, HEAD_DIM), F32),
                        pltpu.VMEM((blk, HEAD_DIM), F32), pltpu.VMEM((blk, HEAD_DIM), F32)],
        compiler_params=_params(2),
    )(proj, proj, proj, proj, dgated, o, car)


def _pack_weights(w_in, w_out):
    d = w_in.shape[0]
    rb = d // 8
    n_in = d // rb
    n_out = w_out.shape[0] // rb

    def body(wi_ref, wo_ref, o_ref):
        r = pl.program_id(0)

        @pl.when(r < n_in)
        def _():
            o_ref[...] = wi_ref[...].astype(BF16)

        @pl.when(r >= n_in)
        def _():
            o_ref[...] = wo_ref[...].astype(BF16)

    return pl.pallas_call(
        body, name="pack_weights", grid=(n_in + n_out,),
        in_specs=[pl.BlockSpec((rb, d), lambda r: (jnp.minimum(r, n_in - 1), 0)),
                  pl.BlockSpec((rb, d), lambda r: (jnp.maximum(r - n_in, 0), 0))],
        out_specs=pl.BlockSpec((rb, d), lambda r: (r, 0)),
        out_shape=jax.ShapeDtypeStruct((d + w_out.shape[0], d), BF16), compiler_params=_params(1),
    )(w_in, w_out)


def _flip(v, bit):
    return 1 - v if bit else v


def _remote(src, dst, send_sem, recv_sem, target):
    return pltpu.make_async_remote_copy(src_ref=src, dst_ref=dst, send_sem=send_sem, recv_sem=recv_sem,
                                        device_id=target, device_id_type=MESH_IDS)


def _all_gather_weights(packs, cw):
    n = len(packs)
    p_rows, d = packs[0].shape
    hr = p_rows // 2

    def body(*refs):
        p = refs[:n]
        cw_ref = refs[n]
        g = refs[n + 1:2 * n + 1]
        cwg = refs[2 * n + 1]
        send, recv, fsend, frecv, lsem = refs[2 * n + 2:]
        x, y, c = lax.axis_index("x"), lax.axis_index("y"), lax.axis_index("c")
        me = 2 * x + y
        sibling = (x, y, 1 - c)
        mine = pl.ds(c * hr, hr)
        other = pl.ds((1 - c) * hr, hr)
        chips = [(_flip(x, k >> 1), _flip(y, k & 1)) for k in (1, 2, 3)]

        local = [pltpu.make_async_copy(p[l], g[l].at[me], lsem.at[l]) for l in range(n)]
        local.append(pltpu.make_async_copy(cw_ref, cwg.at[me], lsem.at[n]))
        for cp in local:
            cp.start()
        sends = []
        for k, (px, py) in enumerate(chips):
            for l in range(n):
                sends.append(_remote(p[l].at[mine], g[l].at[me, mine], send.at[k, l], recv.at[k, l], (px, py, c)))
            sends.append(_remote(cw_ref, cwg.at[me], send.at[k, n], recv.at[k, n], (px, py, c)))
        for cp in sends:
            cp.start()
        passed = []
        for k, (px, py) in enumerate(chips):
            pc = 2 * px + py
            for l in range(n):
                landed = g[l].at[pc, mine]
                _remote(landed, landed, send.at[k, l], recv.at[k, l], (px, py, c)).wait_recv()
                fwd = _remote(landed, landed, fsend.at[k, l], frecv.at[k, l], sibling)
                fwd.start()
                passed.append(fwd)
            _remote(cw_ref, cwg.at[pc], send.at[k, n], recv.at[k, n], (px, py, c)).wait_recv()
        for k, (px, py) in enumerate(chips):
            pc = 2 * px + py
            for l in range(n):
                theirs = g[l].at[pc, other]
                _remote(theirs, theirs, fsend.at[k, l], frecv.at[k, l], sibling).wait_recv()
        for cp in sends + passed:
            cp.wait_send()
        for cp in local:
            cp.wait()

    out_shape = [jax.ShapeDtypeStruct((N_CHIPS, p_rows, d), BF16) for _ in range(n)]
    out_shape.append(jax.ShapeDtypeStruct((N_CHIPS,) + cw.shape, cw.dtype))
    return pl.pallas_call(
        body, name="all_gather_weights", in_specs=[HBM_SPEC] * (n + 1), out_specs=[HBM_SPEC] * (n + 1),
        out_shape=out_shape,
        scratch_shapes=[pltpu.SemaphoreType.DMA((3, n + 1)), pltpu.SemaphoreType.DMA((3, n + 1)),
                        pltpu.SemaphoreType.DMA((3, n)), pltpu.SemaphoreType.DMA((3, n)),
                        pltpu.SemaphoreType.DMA((n + 1,))],
    )(*packs, cw)


def _exchange_grads(gps, small):
    n = len(gps)
    _, p_rows, d = gps[0].shape
    hr = p_rows // 2

    def body(*refs):
        gp = refs[:n]
        sm = refs[n]
        got = refs[n + 1:2 * n + 1]
        smg = refs[2 * n + 1]
        send, recv, lsem = refs[2 * n + 2:]
        x, y, c = lax.axis_index("x"), lax.axis_index("y"), lax.axis_index("c")
        me = 4 * x + 2 * y + c
        peers = [(_flip(x, r >> 2), _flip(y, (r >> 1) & 1), _flip(c, r & 1)) for r in range(1, N_DEV)]

        local = [pltpu.make_async_copy(gp[l].at[2 * x + y, pl.ds(c * hr, hr)], got[l].at[me], lsem.at[l])
                 for l in range(n)]
        local.append(pltpu.make_async_copy(sm, smg.at[me], lsem.at[n]))
        for cp in local:
            cp.start()
        sends = []
        for r, (tx, ty, tc) in enumerate(peers):
            for l in range(n):
                sends.append(_remote(gp[l].at[2 * tx + ty, pl.ds(tc * hr, hr)], got[l].at[me],
                                     send.at[r, l], recv.at[r, l], (tx, ty, tc)))
            sends.append(_remote(sm, smg.at[me], send.at[r, n], recv.at[r, n], (tx, ty, tc)))
        for cp in sends:
            cp.start()
        for r, (tx, ty, tc) in enumerate(peers):
            src = 4 * tx + 2 * ty + tc
            for l in range(n):
                slot = got[l].at[src]
                _remote(slot, slot, send.at[r, l], recv.at[r, l], (tx, ty, tc)).wait_recv()
            _remote(sm, smg.at[src], send.at[r, n], recv.at[r, n], (tx, ty, tc)).wait_recv()
        for cp in sends:
            cp.wait_send()
        for cp in local:
            cp.wait()

    out_shape = [jax.ShapeDtypeStruct((N_DEV, hr, d), BF16) for _ in range(n)]
    out_shape.append(jax.ShapeDtypeStruct((N_DEV,) + small.shape, small.dtype))
    return pl.pallas_call(
        body, name="exchange_grads", in_specs=[HBM_SPEC] * (n + 1), out_specs=[HBM_SPEC] * (n + 1),
        out_shape=out_shape,
        scratch_shapes=[pltpu.SemaphoreType.DMA((N_DEV - 1, n + 1)), pltpu.SemaphoreType.DMA((N_DEV - 1, n + 1)),
                        pltpu.SemaphoreType.DMA((n + 1,))],
    )(*gps, small)


def _row_tile(rows):
    return next(tr for tr in (128, 64, 32, 16, 8) if rows % tr == 0) if rows > 128 else rows


def _sum_sources(name, parts):
    nsrc, rows, cols = parts.shape
    tr = _row_tile(rows)

    def body(p_ref, o_ref):
        total = p_ref[0].astype(F32)
        for s in range(1, nsrc):
            total = total + p_ref[s].astype(F32)
        o_ref[...] = total

    return pl.pallas_call(
        body, name=name, grid=(rows // tr,),
        in_specs=[pl.BlockSpec((nsrc, tr, cols), lambda i: (0, i, 0))],
        out_specs=pl.BlockSpec((tr, cols), lambda i: (i, 0)),
        out_shape=jax.ShapeDtypeStruct((rows, cols), F32), compiler_params=_params(1),
    )(parts)


def _exchange_halves(halves):
    n = len(halves)
    hr, d = halves[0].shape

    def body(*refs):
        src = refs[:n]
        full = refs[n:2 * n]
        send, recv, lsem = refs[2 * n:]
        x, y, c = lax.axis_index("x"), lax.axis_index("y"), lax.axis_index("c")
        sibling = (x, y, 1 - c)
        local = [pltpu.make_async_copy(src[l], full[l].at[c], lsem.at[l]) for l in range(n)]
        sends = [_remote(src[l], full[l].at[c], send.at[l], recv.at[l], sibling) for l in range(n)]
        for cp in local + sends:
            cp.start()
        for l in range(n):
            theirs = full[l].at[1 - c]
            _remote(theirs, theirs, send.at[l], recv.at[l], sibling).wait_recv()
        for cp in sends:
            cp.wait_send()
        for cp in local:
            cp.wait()

    return pl.pallas_call(
        body, name="exchange_halves", in_specs=[HBM_SPEC] * n, out_specs=[HBM_SPEC] * n,
        out_shape=[jax.ShapeDtypeStruct((2, hr, d), F32) for _ in range(n)],
        scratch_shapes=[pltpu.SemaphoreType.DMA((n,)), pltpu.SemaphoreType.DMA((n,)), pltpu.SemaphoreType.DMA((n,))],
    )(*halves)


def _adamw(name, w, m, v, g, g_row0=0):
    rows, cols = w.shape
    tr = _row_tile(rows)
    off = g_row0 // tr

    def body(w_ref, m_ref, v_ref, g_ref, go_ref, d_ref, mo_ref, vo_ref):
        grad = g_ref[...]
        m_new = ADAM_B1 * m_ref[...] + (1.0 - ADAM_B1) * grad
        v_new = ADAM_B2 * v_ref[...] + (1.0 - ADAM_B2) * (grad * grad)
        m_hat = m_new / (1.0 - ADAM_B1 ** ADAM_STEP)
        v_hat = v_new / (1.0 - ADAM_B2 ** ADAM_STEP)
        go_ref[...] = grad
        d_ref[...] = -ADAM_LR * (m_hat / (jnp.sqrt(v_hat) + ADAM_EPS) + ADAM_WD * w_ref[...])
        mo_ref[...] = m_new
        vo_ref[...] = v_new

    blk = pl.BlockSpec((tr, cols), lambda i: (i, 0))
    return pl.pallas_call(
        body, name=name, grid=(rows // tr,),
        in_specs=[blk, blk, blk, pl.BlockSpec((tr, cols), lambda i: (i + off, 0))],
        out_specs=[blk, blk, blk, blk],
        out_shape=[jax.ShapeDtypeStruct((rows, cols), F32)] * 4, compiler_params=_params(1),
    )(w, m, v, g)


def _pad_rows8(a):
    return jnp.concatenate([a, jnp.zeros((8 - a.shape[0],) + a.shape[1:], a.dtype)], axis=0)


def kernel(x, ln_pre_0, conv_w_in_0, conv_w_0, conv_w_out_0, ln_post_0, ln_pre_1, sb_w_in_1, sb_w_out_1, ln_post_1, ln_pre_2, conv_w_in_2, conv_w_2, conv_w_out_2, ln_post_2, ln_pre_3, sb_w_in_3, sb_w_out_3, ln_post_3, loss_target, m_ln_pre_0, m_conv_w_in_0, m_conv_w_0, m_conv_w_out_0, m_ln_post_0, m_ln_pre_1, m_sb_w_in_1, m_sb_w_out_1, m_ln_post_1, m_ln_pre_2, m_conv_w_in_2, m_conv_w_2, m_conv_w_out_2, m_ln_post_2, m_ln_pre_3, m_sb_w_in_3, m_sb_w_out_3, m_ln_post_3, v_ln_pre_0, v_conv_w_in_0, v_conv_w_0, v_conv_w_out_0, v_ln_post_0, v_ln_pre_1, v_sb_w_in_1, v_sb_w_out_1, v_ln_post_1, v_ln_pre_2, v_conv_w_in_2, v_conv_w_2, v_conv_w_out_2, v_ln_post_2, v_ln_pre_3, v_sb_w_in_3, v_sb_w_out_3, v_ln_post_3):
    t, d = x.shape[1], x.shape[2]
    dq = d // N_CHIPS
    xs = x.reshape(t, d)
    target = loss_target.reshape(t, d)
    w_in = [conv_w_in_0, sb_w_in_1, conv_w_in_2, sb_w_in_3]
    w_out = [conv_w_out_0, sb_w_out_1, conv_w_out_2, sb_w_out_3]
    m_in = [m_conv_w_in_0, m_sb_w_in_1, m_conv_w_in_2, m_sb_w_in_3]
    m_out = [m_conv_w_out_0, m_sb_w_out_1, m_conv_w_out_2, m_sb_w_out_3]
    v_in = [v_conv_w_in_0, v_sb_w_in_1, v_conv_w_in_2, v_sb_w_in_3]
    v_out = [v_conv_w_out_0, v_sb_w_out_1, v_conv_w_out_2, v_sb_w_out_3]
    ln_pre = [ln_pre_0, ln_pre_1, ln_pre_2, ln_pre_3]
    ln_post = [ln_post_0, ln_post_1, ln_post_2, ln_post_3]
    conv_w = [conv_w_0, conv_w_2]
    m_conv = [m_conv_w_0, m_conv_w_2]
    v_conv = [v_conv_w_0, v_conv_w_2]
    chip = 2 * lax.axis_index("x") + lax.axis_index("y")

    packs = [_pack_weights(w_in[l], w_out[l]) for l in range(N_LAYERS)]
    cw_local = jnp.concatenate([_pad_rows8(conv_w[0]), _pad_rows8(conv_w[1])], axis=0)
    *gathered, cw_all = _all_gather_weights(packs, cw_local)
    cw_full = jnp.transpose(cw_all, (1, 0, 2)).reshape(16, d)
    conv_taps = {0: cw_full[0:8], 2: cw_full[8:16]}

    h_in, us, projs, gateds, ms, sb_saved = [], [], [], [], [], {}
    h = xs
    u = _norm_first(xs, ln_pre[0])
    for l in range(N_LAYERS):
        h_in.append(h)
        us.append(u)
        proj = _mm_proj(u, gathered[l])
        if l % 2 == 0:
            gated = _conv_fwd(proj, conv_taps[l])
        else:
            gated, o, car = _sb_fwd(proj)
            sb_saved[l] = (o, car)
        m = _mm_out(gated, gathered[l])
        projs.append(proj)
        gateds.append(gated)
        ms.append(m)
        if l < N_LAYERS - 1:
            h, u = _norm_mid(h, m, ln_post[l], ln_pre[l + 1])
    dh, dm, dg_post_last, loss_part = _norm_last(h, ms[-1], ln_post[-1], target)
    loss = lax.psum(loss_part[0, 0], ("x", "y", "c"))

    dg_pre = [None] * N_LAYERS
    dg_post = [None] * N_LAYERS
    dg_post[N_LAYERS - 1] = dg_post_last
    dconv = {}
    gps = [None] * N_LAYERS
    grad_x = None
    for l in reversed(range(N_LAYERS)):
        dgated = _mm_dgated(dm, gathered[l])
        gp = _mm_dwout(gateds[l], dm)
        if l % 2 == 0:
            dproj, dconv[l] = _conv_bwd(projs[l], dgated, conv_taps[l])
        else:
            o, car = sb_saved[l]
            dproj = _sb_bwd(projs[l], dgated, o, car)
        du = _mm_du(dproj, gathered[l])
        gps[l] = _mm_dwin(us[l], dproj, gp)
        if l > 0:
            dh, dm, dg_pre[l], dg_post[l - 1] = _norm_bwd_mid(dh, du, h_in[l], ln_pre[l], ms[l - 1], ln_post[l - 1])
        else:
            grad_x, dg_pre[0] = _norm_bwd_first(dh, du, h_in[0], ln_pre[0])

    small = jnp.concatenate(dg_pre + dg_post + [dconv[0], dconv[2]], axis=0)
    *got, small_all = _exchange_grads(gps, small)
    halves = [_sum_sources("sum_grad_half", got[l]) for l in range(N_LAYERS)]
    small_sum = _sum_sources("sum_small", small_all)
    fulls = [f.reshape(d + dq, d) for f in _exchange_halves(halves)]

    res_in = [_adamw("adamw_w_in", w_in[l], m_in[l], v_in[l], fulls[l], 0) for l in range(N_LAYERS)]
    res_out = [_adamw("adamw_w_out", w_out[l], m_out[l], v_out[l], fulls[l], d) for l in range(N_LAYERS)]
    ln_all = ln_pre + ln_post
    ln_m = [m_ln_pre_0, m_ln_pre_1, m_ln_pre_2, m_ln_pre_3, m_ln_post_0, m_ln_post_1, m_ln_post_2, m_ln_post_3]
    ln_v = [v_ln_pre_0, v_ln_pre_1, v_ln_pre_2, v_ln_pre_3, v_ln_post_0, v_ln_post_1, v_ln_post_2, v_ln_post_3]
    ln_g = jnp.concatenate([small_sum[8 * i:8 * i + 1] for i in range(2 * N_LAYERS)], axis=0)
    res_ln = _adamw("adamw_ln", jnp.stack(ln_all), jnp.stack(ln_m), jnp.stack(ln_v), ln_g)
    conv_g = []
    for i in range(2):
        rows = small_sum[8 * (2 * N_LAYERS + i):8 * (2 * N_LAYERS + i + 1)]
        conv_g.append(lax.dynamic_slice(rows, (0, chip * dq), (8, dq)))
    stack8 = lambda a, b: jnp.concatenate([_pad_rows8(a), _pad_rows8(b)], axis=0)
    res_conv = _adamw("adamw_conv", stack8(*conv_w), stack8(*m_conv), stack8(*v_conv), jnp.concatenate(conv_g, axis=0))

    def leaf(kind, l, which):
        if kind == "ln_pre":
            return res_ln[which][l]
        if kind == "ln_post":
            return res_ln[which][N_LAYERS + l]
        if kind == "w_in":
            return res_in[l][which]
        if kind == "w_out":
            return res_out[l][which]
        return res_conv[which][8 * (l // 2):8 * (l // 2) + 3]

    order = []
    for l in range(N_LAYERS):
        order.append(("ln_pre", l))
        order.append(("w_in", l))
        if l % 2 == 0:
            order.append(("conv", l))
        order.append(("w_out", l))
        order.append(("ln_post", l))
    outs = [loss, grad_x.reshape(1, t, d)]
    for which in range(4):
        outs.extend(leaf(kind, l, which) for kind, l in order)
    return tuple(outs)
```

```python
import functools
import math

import jax
import jax.numpy as jnp
from jax import lax
from jax.experimental import pallas as pl
from jax.experimental.pallas import tpu as pltpu

F32 = jnp.float32
BF16 = jnp.bfloat16

N_CHIPS = 4
N_DEV = 8
N_LAYERS = 4
HEAD_DIM = 128
RMS_EPS = 1e-6
ADAM_LR = 0.001
ADAM_B1 = 0.9
ADAM_B2 = 0.999
ADAM_EPS = 1e-08
ADAM_WD = 0.01
ADAM_STEP = 10

VMEM_LIMIT = 56 * 1024 * 1024
MESH_IDS = pl.DeviceIdType.MESH
HBM_SPEC = pl.BlockSpec(memory_space=pltpu.HBM)

NN = (((1,), (0,)), ((), ()))
NT = (((1,), (1,)), ((), ()))
TN = (((0,), (0,)), ((), ()))


def _params(n_axes):
    return pltpu.CompilerParams(dimension_semantics=("arbitrary",) * n_axes, vmem_limit_bytes=VMEM_LIMIT)


def _dot(a, b, dims):
    return lax.dot_general(a, b, dims, preferred_element_type=F32)


def _sigmoid(z):
    return 1.0 / (1.0 + jnp.exp(-z))


def _matmul(name, a, b, *, grid, a_spec, b_spec, o_spec, out_shape, dims, reduce_axis=None, acc_shape=None,
            alias_out=None):
    out_dtype = out_shape.dtype
    direct = reduce_axis is not None and out_dtype == F32
    n_red = grid[reduce_axis] if reduce_axis is not None else 1

    def body(*refs):
        if alias_out is not None:
            refs = refs[1:]
        a_ref, b_ref, o_ref = refs[:3]
        prod = _dot(a_ref[...], b_ref[...], dims)
        if reduce_axis is None:
            o_ref[...] = prod.astype(out_dtype)
            return
        acc_ref = o_ref if direct else refs[3]
        k = pl.program_id(reduce_axis)

        @pl.when(k == 0)
        def _():
            acc_ref[...] = prod

        @pl.when(k > 0)
        def _():
            acc_ref[...] += prod

        if not direct:
            @pl.when(k == n_red - 1)
            def _():
                o_ref[...] = acc_ref[...].astype(out_dtype)

    scratch = []
    if reduce_axis is not None and not direct:
        scratch = [pltpu.VMEM(acc_shape, F32)]
    in_specs = [a_spec, b_spec]
    operands = [a, b]
    aliases = {}
    if alias_out is not None:
        in_specs = [HBM_SPEC] + in_specs
        operands = [alias_out] + operands
        aliases = {0: 0}
    return pl.pallas_call(
        body, name=name, grid=grid, in_specs=in_specs, out_specs=o_spec, out_shape=out_shape,
        scratch_shapes=scratch, input_output_aliases=aliases, compiler_params=_params(len(grid)),
    )(*operands)


def _mm_proj(u, g):
    t, d = u.shape
    tm = min(512, t)
    return _matmul(
        "mm_proj", u, g, grid=(N_CHIPS, t // tm),
        a_spec=pl.BlockSpec((tm, d), lambda s, m: (m, 0)),
        b_spec=pl.BlockSpec((None, d, d), lambda s, m: (s, 0, 0)),
        o_spec=pl.BlockSpec((None, tm, d), lambda s, m: (s, m, 0)),
        out_shape=jax.ShapeDtypeStruct((N_CHIPS, t, d), BF16), dims=NN)


def _mm_out(gated, g):
    t, d = gated.shape
    dq = d // N_CHIPS
    tm = min(512, t)
    return _matmul(
        "mm_out", gated, g, grid=(t // tm, N_CHIPS),
        a_spec=pl.BlockSpec((tm, dq), lambda m, s: (m, s)),
        b_spec=pl.BlockSpec((None, dq, d), lambda m, s: (s, N_CHIPS, 0)),
        o_spec=pl.BlockSpec((tm, d), lambda m, s: (m, 0)),
        out_shape=jax.ShapeDtypeStruct((t, d), F32), dims=NN, reduce_axis=1)


def _mm_dgated(dm, g):
    t, d = dm.shape
    dq = d // N_CHIPS
    tm = min(512, t)
    return _matmul(
        "mm_dgated", dm, g, grid=(t // tm, N_CHIPS),
        a_spec=pl.BlockSpec((tm, d), lambda m, s: (m, 0)),
        b_spec=pl.BlockSpec((None, dq, d), lambda m, s: (s, N_CHIPS, 0)),
        o_spec=pl.BlockSpec((tm, dq), lambda m, s: (m, s)),
        out_shape=jax.ShapeDtypeStruct((t, d), BF16), dims=NT)


def _mm_dwout(gated, dm):
    t, d = gated.shape
    dq = d // N_CHIPS
    tk = min(512, t)
    return _matmul(
        "mm_dwout", gated, dm, grid=(N_CHIPS, t // tk),
        a_spec=pl.BlockSpec((tk, dq), lambda s, k: (k, s)),
        b_spec=pl.BlockSpec((tk, d), lambda s, k: (k, 0)),
        o_spec=pl.BlockSpec((None, dq, d), lambda s, k: (s, N_CHIPS, 0)),
        out_shape=jax.ShapeDtypeStruct((N_CHIPS, d + dq, d), BF16), dims=TN, reduce_axis=1, acc_shape=(dq, d))


def _mm_du(dproj, g):
    _, t, d = dproj.shape
    tm = min(512, t)
    return _matmul(
        "mm_du", dproj, g, grid=(t // tm, N_CHIPS),
        a_spec=pl.BlockSpec((None, tm, d), lambda m, s: (s, m, 0)),
        b_spec=pl.BlockSpec((None, d, d), lambda m, s: (s, 0, 0)),
        o_spec=pl.BlockSpec((tm, d), lambda m, s: (m, 0)),
        out_shape=jax.ShapeDtypeStruct((t, d), F32), dims=NT, reduce_axis=1)


def _mm_dwin(u, dproj, gp):
    t, d = u.shape
    tmo = min(1024, d)
    tk = min(512, t)
    return _matmul(
        "mm_dwin", u, dproj, grid=(N_CHIPS, d // tmo, t // tk),
        a_spec=pl.BlockSpec((tk, tmo), lambda s, mo, k: (k, mo)),
        b_spec=pl.BlockSpec((None, tk, d), lambda s, mo, k: (s, k, 0)),
        o_spec=pl.BlockSpec((None, tmo, d), lambda s, mo, k: (s, mo, 0)),
        out_shape=jax.ShapeDtypeStruct(gp.shape, BF16), dims=TN, reduce_axis=2, acc_shape=(tmo, d), alias_out=gp)


def _rms(v):
    r = lax.rsqrt(jnp.mean(v * v, axis=-1, keepdims=True) + RMS_EPS)
    return v * r, r


def _rms_bwd(dout, n, r, gain):
    dn = dout * gain
    return r * (dn - n * jnp.mean(dn * n, axis=-1, keepdims=True))


def _fold8(v):
    return jnp.sum(v.reshape(v.shape[0] // 8, 8, v.shape[1]), axis=0)


def _row0(total):
    rows = lax.broadcasted_iota(jnp.int32, total.shape, 0)
    return jnp.where(rows == 0, jnp.sum(total, axis=0, keepdims=True), 0.0)


def _norm_tile(t):
    return min(256, t)


def _norm_first(x, g_pre):
    t, d = x.shape
    tr = _norm_tile(t)

    def body(x_ref, g_ref, u_ref):
        n, _ = _rms(x_ref[...])
        u_ref[...] = (n * g_ref[...]).astype(BF16)

    row = pl.BlockSpec((tr, d), lambda i: (i, 0))
    vec = pl.BlockSpec((1, d), lambda i: (0, 0))
    return pl.pallas_call(
        body, name="norm_first", grid=(t // tr,), in_specs=[row, vec], out_specs=row,
        out_shape=jax.ShapeDtypeStruct((t, d), BF16), compiler_params=_params(1),
    )(x, g_pre.reshape(1, d))


def _norm_mid(h, m, g_post, g_pre_next):
    t, d = h.shape
    tr = _norm_tile(t)

    def body(h_ref, m_ref, gp_ref, gn_ref, hn_ref, u_ref):
        n, _ = _rms(m_ref[...])
        hn = h_ref[...] + n * gp_ref[...]
        hn_ref[...] = hn
        n2, _ = _rms(hn)
        u_ref[...] = (n2 * gn_ref[...]).astype(BF16)

    row = pl.BlockSpec((tr, d), lambda i: (i, 0))
    vec = pl.BlockSpec((1, d), lambda i: (0, 0))
    return pl.pallas_call(
        body, name="norm_mid", grid=(t // tr,), in_specs=[row, row, vec, vec], out_specs=[row, row],
        out_shape=[jax.ShapeDtypeStruct((t, d), F32), jax.ShapeDtypeStruct((t, d), BF16)],
        compiler_params=_params(1),
    )(h, m, g_post.reshape(1, d), g_pre_next.reshape(1, d))


def _norm_last(h, m, g_post, target):
    t, d = h.shape
    tr = _norm_tile(t)
    nsteps = t // tr

    def body(h_ref, m_ref, gp_ref, tg_ref, dy_ref, dm_ref, dgp_ref, loss_ref, acc_g, acc_l):
        i = pl.program_id(0)

        @pl.when(i == 0)
        def _():
            acc_g[...] = jnp.zeros_like(acc_g)
            acc_l[...] = jnp.zeros_like(acc_l)

        gain = gp_ref[...]
        n, r = _rms(m_ref[...])
        err = h_ref[...] + n * gain - tg_ref[...]
        dy = err / d
        dy_ref[...] = dy
        dm_ref[...] = _rms_bwd(dy, n, r, gain).astype(BF16)
        acc_g[...] += _fold8(dy * n)
        acc_l[...] += _fold8(err * err)

        @pl.when(i == nsteps - 1)
        def _():
            dgp_ref[...] = _row0(acc_g[...])
            loss_ref[...] = jnp.zeros((8, 128), F32) + (0.5 / d) * jnp.sum(acc_l[...])

    row = pl.BlockSpec((tr, d), lambda i: (i, 0))
    vec = pl.BlockSpec((1, d), lambda i: (0, 0))
    acc = pl.BlockSpec((8, d), lambda i: (0, 0))
    return pl.pallas_call(
        body, name="norm_last", grid=(nsteps,), in_specs=[row, row, vec, row],
        out_specs=[row, row, acc, pl.BlockSpec((8, 128), lambda i: (0, 0))],
        out_shape=[jax.ShapeDtypeStruct((t, d), F32), jax.ShapeDtypeStruct((t, d), BF16),
                   jax.ShapeDtypeStruct((8, d), F32), jax.ShapeDtypeStruct((8, 128), F32)],
        scratch_shapes=[pltpu.VMEM((8, d), F32), pltpu.VMEM((8, d), F32)],
        compiler_params=_params(1),
    )(h, m, g_post.reshape(1, d), target)


def _norm_bwd_mid(dh, du, h_in, g_pre, m_prev, g_post_prev):
    t, d = dh.shape
    tr = _norm_tile(t)
    nsteps = t // tr

    def body(dh_ref, du_ref, h_ref, gpre_ref, m_ref, gpost_ref, dhn_ref, dm_ref, dgpre_ref, dgpost_ref, acc_a, acc_b):
        i = pl.program_id(0)

        @pl.when(i == 0)
        def _():
            acc_a[...] = jnp.zeros_like(acc_a)
            acc_b[...] = jnp.zeros_like(acc_b)

        du_t = du_ref[...]
        n, r = _rms(h_ref[...])
        dhn = dh_ref[...] + _rms_bwd(du_t, n, r, gpre_ref[...])
        dhn_ref[...] = dhn
        acc_a[...] += _fold8(du_t * n)
        n2, r2 = _rms(m_ref[...])
        dm_ref[...] = _rms_bwd(dhn, n2, r2, gpost_ref[...]).astype(BF16)
        acc_b[...] += _fold8(dhn * n2)

        @pl.when(i == nsteps - 1)
        def _():
            dgpre_ref[...] = _row0(acc_a[...])
            dgpost_ref[...] = _row0(acc_b[...])

    row = pl.BlockSpec((tr, d), lambda i: (i, 0))
    vec = pl.BlockSpec((1, d), lambda i: (0, 0))
    acc = pl.BlockSpec((8, d), lambda i: (0, 0))
    return pl.pallas_call(
        body, name="norm_bwd_mid", grid=(nsteps,), in_specs=[row, row, row, vec, row, vec],
        out_specs=[row, row, acc, acc],
        out_shape=[jax.ShapeDtypeStruct((t, d), F32), jax.ShapeDtypeStruct((t, d), BF16),
                   jax.ShapeDtypeStruct((8, d), F32), jax.ShapeDtypeStruct((8, d), F32)],
        scratch_shapes=[pltpu.VMEM((8, d), F32), pltpu.VMEM((8, d), F32)],
        compiler_params=_params(1),
    )(dh, du, h_in, g_pre.reshape(1, d), m_prev, g_post_prev.reshape(1, d))


def _norm_bwd_first(dh, du, x, g_pre):
    t, d = dh.shape
    tr = _norm_tile(t)
    nsteps = t // tr

    def body(dh_ref, du_ref, x_ref, gpre_ref, dx_ref, dgpre_ref, acc_a):
        i = pl.program_id(0)

        @pl.when(i == 0)
        def _():
            acc_a[...] = jnp.zeros_like(acc_a)

        du_t = du_ref[...]
        n, r = _rms(x_ref[...])
        dx_ref[...] = dh_ref[...] + _rms_bwd(du_t, n, r, gpre_ref[...])
        acc_a[...] += _fold8(du_t * n)

        @pl.when(i == nsteps - 1)
        def _():
            dgpre_ref[...] = _row0(acc_a[...])

    row = pl.BlockSpec((tr, d), lambda i: (i, 0))
    vec = pl.BlockSpec((1, d), lambda i: (0, 0))
    acc = pl.BlockSpec((8, d), lambda i: (0, 0))
    return pl.pallas_call(
        body, name="norm_bwd_first", grid=(nsteps,), in_specs=[row, row, row, vec], out_specs=[row, acc],
        out_shape=[jax.ShapeDtypeStruct((t, d), F32), jax.ShapeDtypeStruct((8, d), F32)],
        scratch_shapes=[pltpu.VMEM((8, d), F32)], compiler_params=_params(1),
    )(dh, du, x, g_pre.reshape(1, d))


CONV_TC = 128
CONV_HALO = 16


def _conv_chunk(t):
    return min(512, t)


def _shift_down(v, steps, fill):
    rows = lax.broadcasted_iota(jnp.int32, v.shape, 0)
    out = pltpu.roll(v, steps, axis=0)
    for k in range(steps):
        out = jnp.where(rows == k, fill[CONV_HALO - steps + k:CONV_HALO - steps + k + 1, :], out)
    return out


def _shift_up(v, steps, fill):
    nrows = v.shape[0]
    rows = lax.broadcasted_iota(jnp.int32, v.shape, 0)
    out = pltpu.roll(v, nrows - steps, axis=0)
    for k in range(steps):
        out = jnp.where(rows == nrows - steps + k, fill[k:k + 1, :], out)
    return out


def _conv_fwd(proj, cw):
    _, t, d = proj.shape
    chunk = _conv_chunk(t)

    def body(p_ref, w_ref, o_ref):
        w = w_ref[...]
        w0, w1, w2 = w[0:1, :], w[1:2, :], w[2:3, :]
        for ci in range(t // chunk):
            t0 = ci * chunk
            rows = pl.ds(t0, chunk)
            b = p_ref[0, rows, :].astype(F32)
            cx = p_ref[1, rows, :].astype(F32) * p_ref[2, rows, :].astype(F32)
            z = p_ref[3, rows, :].astype(F32)
            if ci == 0:
                prev = jnp.zeros((CONV_HALO, CONV_TC), F32)
            else:
                halo = pl.ds(t0 - CONV_HALO, CONV_HALO)
                prev = p_ref[1, halo, :].astype(F32) * p_ref[2, halo, :].astype(F32)
            conv = w2 * cx + w1 * _shift_down(cx, 1, prev) + w0 * _shift_down(cx, 2, prev)
            o_ref[rows, :] = (z * _sigmoid(z) * b * conv).astype(BF16)

    return pl.pallas_call(
        body, name="conv_fwd", grid=(d // CONV_TC,),
        in_specs=[pl.BlockSpec((N_CHIPS, t, CONV_TC), lambda j: (0, 0, j)), pl.BlockSpec((8, CONV_TC), lambda j: (0, j))],
        out_specs=pl.BlockSpec((t, CONV_TC), lambda j: (0, j)),
        out_shape=jax.ShapeDtypeStruct((t, d), BF16), compiler_params=_params(1),
    )(proj, cw)


def _conv_bwd(proj, dgated, cw):
    _, t, d = proj.shape
    chunk = _conv_chunk(t)
    nchunks = t // chunk

    def body(p_ref, dg_ref, w_ref, dp_ref, dw_ref):
        w = w_ref[...]
        w0, w1, w2 = w[0:1, :], w[1:2, :], w[2:3, :]
        dw0 = jnp.zeros((1, CONV_TC), F32)
        dw1 = jnp.zeros((1, CONV_TC), F32)
        dw2 = jnp.zeros((1, CONV_TC), F32)
        for ci in range(nchunks):
            t0 = ci * chunk
            rows = pl.ds(t0, chunk)
            b = p_ref[0, rows, :].astype(F32)
            c = p_ref[1, rows, :].astype(F32)
            xt = p_ref[2, rows, :].astype(F32)
            z = p_ref[3, rows, :].astype(F32)
            dg = dg_ref[rows, :].astype(F32)
            cx = c * xt
            if ci == 0:
                prev = jnp.zeros((CONV_HALO, CONV_TC), F32)
            else:
                halo = pl.ds(t0 - CONV_HALO, CONV_HALO)
                prev = p_ref[1, halo, :].astype(F32) * p_ref[2, halo, :].astype(F32)
            cx1 = _shift_down(cx, 1, prev)
            cx2 = _shift_down(cx, 2, prev)
            conv = w2 * cx + w1 * cx1 + w0 * cx2
            sig = _sigmoid(z)
            dy = dg * (z * sig)
            dconv = dy * b
            if ci == nchunks - 1:
                nxt = jnp.zeros((CONV_HALO, CONV_TC), F32)
            else:
                halo = pl.ds(t0 + chunk, CONV_HALO)
                zn = p_ref[3, halo, :].astype(F32)
                nxt = dg_ref[halo, :].astype(F32) * (zn * _sigmoid(zn)) * p_ref[0, halo, :].astype(F32)
            dcx = w2 * dconv + w1 * _shift_up(dconv, 1, nxt) + w0 * _shift_up(dconv, 2, nxt)
            dp_ref[0, rows, :] = (dy * conv).astype(BF16)
            dp_ref[1, rows, :] = (dcx * xt).astype(BF16)
            dp_ref[2, rows, :] = (dcx * c).astype(BF16)
            dp_ref[3, rows, :] = (dg * (b * conv) * (sig * (1.0 + z * (1.0 - sig)))).astype(BF16)
            dw0 = dw0 + jnp.sum(dconv * cx2, axis=0, keepdims=True)
            dw1 = dw1 + jnp.sum(dconv * cx1, axis=0, keepdims=True)
            dw2 = dw2 + jnp.sum(dconv * cx, axis=0, keepdims=True)
        taps = lax.broadcasted_iota(jnp.int32, (8, CONV_TC), 0)
        dw_ref[...] = jnp.where(taps == 0, dw0, jnp.where(taps == 1, dw1, jnp.where(taps == 2, dw2, 0.0)))

    return pl.pallas_call(
        body, name="conv_bwd", grid=(d // CONV_TC,),
        in_specs=[pl.BlockSpec((N_CHIPS, t, CONV_TC), lambda j: (0, 0, j)),
                  pl.BlockSpec((t, CONV_TC), lambda j: (0, j)),
                  pl.BlockSpec((8, CONV_TC), lambda j: (0, j))],
        out_specs=[pl.BlockSpec((N_CHIPS, t, CONV_TC), lambda j: (0, 0, j)), pl.BlockSpec((8, CONV_TC), lambda j: (0, j))],
        out_shape=[jax.ShapeDtypeStruct((N_CHIPS, t, d), BF16), jax.ShapeDtypeStruct((8, d), F32)],
        compiler_params=_params(1),
    )(proj, dgated, cw)


SB_DEAD_TAIL = -105.0
SB_COUNT_LANE = HEAD_DIM - 1


def _sb_block(t):
    return min(256, t)


def _split_dot(v, tri):
    hi = v.astype(BF16)
    lo = (v - hi.astype(F32)).astype(BF16)
    return _dot(hi, tri, NN) + _dot(lo, tri, NN)


def _sb_scores(q, kj, q0, k0, scale):
    blk_q, blk_k = q.shape[0], kj.shape[0]
    s = _dot(q, kj, NT) * scale
    sp = jnp.maximum(s, 0.0) + jnp.log1p(jnp.exp(-jnp.abs(s)))
    rows = lax.broadcasted_iota(jnp.int32, (blk_q, blk_k), 0) + q0
    cols = lax.broadcasted_iota(jnp.int32, (blk_q, blk_k), 1) + k0
    mask = cols < rows
    keep = jnp.where(mask, -sp, 0.0)
    return keep, s - sp, sp, mask


def _sb_fwd(proj):
    _, t, d = proj.shape
    heads = d // HEAD_DIM
    blk = _sb_block(t)
    nblk = t // blk
    scale = 1.0 / math.sqrt(HEAD_DIM)

    def body(q_ref, k_ref, v_ref, z_ref, gated_ref, o_ref, car_ref, tail_ref, acc_ref):
        i = pl.program_id(1)
        q = q_ref[...]
        r_i = lax.broadcasted_iota(jnp.int32, (blk, blk), 0)
        c_i = lax.broadcasted_iota(jnp.int32, (blk, blk), 1)
        tri_after = (r_i > c_i).astype(BF16)
        lanes = lax.broadcasted_iota(jnp.int32, (blk, HEAD_DIM), 1)

        tail_ref[...] = jnp.zeros_like(tail_ref)
        acc_ref[...] = jnp.zeros_like(acc_ref)
        car_ref[...] = jnp.zeros_like(car_ref)

        def more(state):
            jj, live = state
            return jnp.logical_and(jj <= i, live)

        def step(state):
            jj, _ = state
            j = i - jj
            k0 = pl.multiple_of(j * blk, blk)
            kj = k_ref[pl.ds(k0, blk), :]
            vj = v_ref[pl.ds(k0, blk), :]
            keep, log_beta, _, mask = _sb_scores(q, kj, i * blk, k0, scale)
            within = _split_dot(keep, tri_after)
            tail_b = tail_ref[...]
            w = jnp.where(mask, jnp.exp(log_beta + tail_b[:, 0:1] + within), 0.0)
            acc_ref[...] += _dot(w.astype(BF16), vj, NN)
            car_ref[...] = jnp.where(lanes == j, tail_b, car_ref[...])
            tail_new = tail_b + jnp.sum(keep, axis=1, keepdims=True)
            tail_ref[...] = tail_new
            return jj + 1, jnp.max(tail_new) > SB_DEAD_TAIL

        visited, _ = lax.while_loop(more, step, (jnp.int32(0), True))
        car_ref[...] = jnp.where(lanes == SB_COUNT_LANE, visited.astype(F32), car_ref[...])
        z = z_ref[...].astype(F32)
        acc = acc_ref[...]
        o_ref[...] = acc.astype(BF16)
        gated_ref[...] = (z * _sigmoid(z) * acc).astype(BF16)

    qspec = lambda s: pl.BlockSpec((None, blk, HEAD_DIM), lambda h, i: (s, i, h))
    kspec = lambda s: pl.BlockSpec((None, t, HEAD_DIM), lambda h, i: (s, 0, h))
    ospec = pl.BlockSpec((blk, HEAD_DIM), lambda h, i: (i, h))
    return pl.pallas_call(
        body, name="sb_fwd", grid=(heads, nblk),
        in_specs=[qspec(0), kspec(1), kspec(2), qspec(3)],
        out_specs=[ospec, ospec, pl.BlockSpec((None, blk, HEAD_DIM), lambda h, i: (h, i, 0))],
        out_shape=[jax.ShapeDtypeStruct((t, d), BF16), jax.ShapeDtypeStruct((t, d), BF16),
                   jax.ShapeDtypeStruct((heads, t, HEAD_DIM), F32)],
        scratch_shapes=[pltpu.VMEM((blk, HEAD_DIM), F32), pltpu.VMEM((blk, HEAD_DIM), F32)],
        compiler_params=_params(2),
    )(proj, proj, proj, proj)


def _sb_bwd(proj, dgated, o, car):
    _, t, d = proj.shape
    heads = d // HEAD_DIM
    blk = _sb_block(t)
    nblk = t // blk
    scale = 1.0 / math.sqrt(HEAD_DIM)

    def body(q_ref, k_ref, v_ref, z_ref, dg_ref, o_ref, car_ref, dp_ref, dk_acc, dv_acc, gsum_ref, dq_ref):
        step_i = pl.program_id(1)
        i = nblk - 1 - step_i

        @pl.when(step_i == 0)
        def _():
            dk_acc[...] = jnp.zeros_like(dk_acc)
            dv_acc[...] = jnp.zeros_like(dv_acc)

        q = q_ref[...]
        z = z_ref[...].astype(F32)
        dg = dg_ref[...].astype(F32)
        sig = _sigmoid(z)
        d_o = (dg * (z * sig)).astype(BF16)
        car_all = car_ref[...]
        r_i = lax.broadcasted_iota(jnp.int32, (blk, blk), 0)
        c_i = lax.broadcasted_iota(jnp.int32, (blk, blk), 1)
        tri_after = (r_i > c_i).astype(BF16)
        tri_before = (r_i < c_i).astype(BF16)
        lanes = lax.broadcasted_iota(jnp.int32, (blk, HEAD_DIM), 1)

        gsum_ref[...] = jnp.zeros_like(gsum_ref)
        dq_ref[...] = jnp.zeros_like(dq_ref)

        def step(j, carry):
            k0 = pl.multiple_of(j * blk, blk)
            kj = k_ref[pl.ds(k0, blk), :]
            vj = v_ref[pl.ds(k0, blk), :]
            keep, log_beta, sp, mask = _sb_scores(q, kj, i * blk, k0, scale)
            tail = jnp.sum(jnp.where(lanes == j, car_all, 0.0), axis=1, keepdims=True)
            w = jnp.where(mask, jnp.exp(log_beta + tail + _split_dot(keep, tri_after)), 0.0)
            g = w * _dot(d_o, vj, NT)
            g_before = gsum_ref[...]
            g_cum = g_before[:, 0:1] + _split_dot(g, tri_before)
            dl = jnp.where(mask, g * jnp.exp(-sp) - g_cum * jnp.exp(log_beta), 0.0) * scale
            dl = dl.astype(BF16)
            dq_ref[...] += _dot(dl, kj, NN)
            dk_acc[pl.ds(k0, blk), :] += _dot(dl, q, TN)
            dv_acc[pl.ds(k0, blk), :] += _dot(w.astype(BF16), d_o, TN)
            gsum_ref[...] = g_before + jnp.sum(g, axis=1, keepdims=True)
            return carry

        visited = jnp.max(jnp.where(lanes == SB_COUNT_LANE, car_all, 0.0)).astype(jnp.int32)
        lax.fori_loop(i + 1 - visited, i + 1, step, 0)
        own = pl.ds(pl.multiple_of(i * blk, blk), blk)
        dp_ref[0] = dq_ref[...].astype(BF16)
        dp_ref[1] = dk_acc[own, :].astype(BF16)
        dp_ref[2] = dv_acc[own, :].astype(BF16)
        dp_ref[3] = (dg * o_ref[...].astype(F32) * (sig * (1.0 + z * (1.0 - sig)))).astype(BF16)

    qspec = lambda s: pl.BlockSpec((None, blk, HEAD_DIM), lambda h, i: (s, nblk - 1 - i, h))
    kspec = lambda s: pl.BlockSpec((None, t, HEAD_DIM), lambda h, i: (s, 0, h))
    tspec = pl.BlockSpec((blk, HEAD_DIM), lambda h, i: (nblk - 1 - i, h))
    return pl.pallas_call(
        body, name="sb_bwd", grid=(heads, nblk),
        in_specs=[qspec(0), kspec(1), kspec(2), qspec(3), tspec, tspec,
                  pl.BlockSpec((None, blk, HEAD_DIM), lambda h, i: (h, nblk - 1 - i, 0))],
        out_specs=pl.BlockSpec((N_CHIPS, blk, HEAD_DIM), lambda h, i: (0, nblk - 1 - i, h)),
        out_shape=jax.ShapeDtypeStruct((N_CHIPS, t, d), BF16),
        scratch_shapes=[pltpu.VMEM((t, HEAD_DIM), F32), pltpu.VMEM((t, HEAD_DIM), F32),
                        pltpu.VMEM((blk, HEAD_DIM), F32), pltpu.VMEM((blk, HEAD_DIM), F32)],
        compiler_params=_params(2),
    )(proj, proj, proj, proj, dgated, o, car)


def _pack_weights(w_in, w_out, chip):
    d = w_in.shape[0]
    rb = d // 8
    n_in = d // rb
    n_out = w_out.shape[0] // rb

    def body(chip_ref, wi_ref, wo_ref, o_ref):
        r = pl.program_id(0)

        @pl.when(r < n_in)
        def _():
            o_ref[...] = wi_ref[...].astype(BF16)

        @pl.when(r >= n_in)
        def _():
            o_ref[...] = wo_ref[...].astype(BF16)

    grid_spec = pltpu.PrefetchScalarGridSpec(
        num_scalar_prefetch=1, grid=(n_in + n_out,),
        in_specs=[pl.BlockSpec((rb, d), lambda r, me: (jnp.minimum(r, n_in - 1), 0)),
                  pl.BlockSpec((rb, d), lambda r, me: (jnp.maximum(r - n_in, 0), 0))],
        out_specs=pl.BlockSpec((None, rb, d), lambda r, me: (me[0], r, 0)))
    return pl.pallas_call(
        body, name="pack_weights", grid_spec=grid_spec,
        out_shape=jax.ShapeDtypeStruct((N_CHIPS, d + w_out.shape[0], d), BF16), compiler_params=_params(1),
    )(chip, w_in, w_out)


def _flip(v, bit):
    return 1 - v if bit else v


def _remote(src, dst, send_sem, recv_sem, target):
    return pltpu.make_async_remote_copy(src_ref=src, dst_ref=dst, send_sem=send_sem, recv_sem=recv_sem,
                                        device_id=target, device_id_type=MESH_IDS)


AG_CHUNKS = 4
HALF_CHUNKS = 8


def _all_gather_weights(packs, cw):
    n = len(packs)
    _, p_rows, d = packs[0].shape
    hr = p_rows // 2
    cr = hr // AG_CHUNKS
    nc = n * AG_CHUNKS

    def body(*refs):
        cw_ref = refs[n]
        g = refs[n + 1:2 * n + 1]
        cwg = refs[2 * n + 1]
        send, recv, fsend, frecv, csend, crecv, lsem = refs[2 * n + 2:]
        x, y, c = lax.axis_index("x"), lax.axis_index("y"), lax.axis_index("c")
        me = 2 * x + y
        sibling = (x, y, 1 - c)
        chips = [(_flip(x, k >> 1), _flip(y, k & 1)) for k in (1, 2, 3)]
        pieces = [(l, q) for l in range(n) for q in range(AG_CHUNKS)]

        def rows(half, q):
            return pl.ds(half * hr + q * cr, cr)

        local = pltpu.make_async_copy(cw_ref, cwg.at[me], lsem)
        local.start()
        sends = []
        for k, (px, py) in enumerate(chips):
            for i, (l, q) in enumerate(pieces):
                piece = g[l].at[me, rows(c, q)]
                sends.append(_remote(piece, piece, send.at[k, i], recv.at[k, i], (px, py, c)))
            sends.append(_remote(cw_ref, cwg.at[me], csend.at[k], crecv.at[k], (px, py, c)))
        for cp in sends:
            cp.start()
        passed = []
        for k, (px, py) in enumerate(chips):
            pc = 2 * px + py
            for i, (l, q) in enumerate(pieces):
                landed = g[l].at[pc, rows(c, q)]
                _remote(landed, landed, send.at[k, i], recv.at[k, i], (px, py, c)).wait_recv()
                fwd = _remote(landed, landed, fsend.at[k, i], frecv.at[k, i], sibling)
                fwd.start()
                passed.append(fwd)
            _remote(cw_ref, cwg.at[pc], csend.at[k], crecv.at[k], (px, py, c)).wait_recv()
        for k, (px, py) in enumerate(chips):
            pc = 2 * px + py
            for i, (l, q) in enumerate(pieces):
                theirs = g[l].at[pc, rows(1 - c, q)]
                _remote(theirs, theirs, fsend.at[k, i], frecv.at[k, i], sibling).wait_recv()
        for cp in sends + passed:
            cp.wait_send()
        local.wait()

    out_shape = [jax.ShapeDtypeStruct(p.shape, BF16) for p in packs]
    out_shape.append(jax.ShapeDtypeStruct((N_CHIPS,) + cw.shape, cw.dtype))
    return pl.pallas_call(
        body, name="all_gather_weights", in_specs=[HBM_SPEC] * (n + 1), out_specs=[HBM_SPEC] * (n + 1),
        out_shape=out_shape, input_output_aliases={l: l for l in range(n)},
        scratch_shapes=[pltpu.SemaphoreType.DMA((3, nc)), pltpu.SemaphoreType.DMA((3, nc)),
                        pltpu.SemaphoreType.DMA((3, nc)), pltpu.SemaphoreType.DMA((3, nc)),
                        pltpu.SemaphoreType.DMA((3,)), pltpu.SemaphoreType.DMA((3,)), pltpu.SemaphoreType.DMA],
    )(*packs, cw)


def _exchange_grads(gps, small):
    n = len(gps)
    _, p_rows, d = gps[0].shape
    hr = p_rows // 2

    def body(*refs):
        gp = refs[:n]
        sm = refs[n]
        got = refs[n + 1:2 * n + 1]
        smg = refs[2 * n + 1]
        send, recv, lsem = refs[2 * n + 2:]
        x, y, c = lax.axis_index("x"), lax.axis_index("y"), lax.axis_index("c")
        me = 4 * x + 2 * y + c
        peers = [(_flip(x, r >> 2), _flip(y, (r >> 1) & 1), _flip(c, r & 1)) for r in range(1, N_DEV)]

        local = pltpu.make_async_copy(sm, smg.at[me], lsem)
        local.start()
        sends = []
        for r, (tx, ty, tc) in enumerate(peers):
            for l in range(n):
                sends.append(_remote(gp[l].at[2 * tx + ty, pl.ds(tc * hr, hr)], got[l].at[r],
                                     send.at[r, l], recv.at[r, l], (tx, ty, tc)))
            sends.append(_remote(sm, smg.at[me], send.at[r, n], recv.at[r, n], (tx, ty, tc)))
        for cp in sends:
            cp.start()
        for r, (tx, ty, tc) in enumerate(peers):
            for l in range(n):
                slot = got[l].at[r]
                _remote(slot, slot, send.at[r, l], recv.at[r, l], (tx, ty, tc)).wait_recv()
            _remote(sm, smg.at[4 * tx + 2 * ty + tc], send.at[r, n], recv.at[r, n], (tx, ty, tc)).wait_recv()
        for cp in sends:
            cp.wait_send()
        local.wait()

    out_shape = [jax.ShapeDtypeStruct((N_DEV - 1, hr, d), BF16) for _ in range(n)]
    out_shape.append(jax.ShapeDtypeStruct((N_DEV,) + small.shape, small.dtype))
    return pl.pallas_call(
        body, name="exchange_grads", in_specs=[HBM_SPEC] * (n + 1), out_specs=[HBM_SPEC] * (n + 1),
        out_shape=out_shape,
        scratch_shapes=[pltpu.SemaphoreType.DMA((N_DEV - 1, n + 1)), pltpu.SemaphoreType.DMA((N_DEV - 1, n + 1)),
                        pltpu.SemaphoreType.DMA],
    )(*gps, small)


def _row_tile(rows):
    return next(tr for tr in (128, 64, 32, 16, 8) if rows % tr == 0) if rows > 128 else rows


def _sum_sources(name, parts):
    nsrc, rows, cols = parts.shape
    tr = _row_tile(rows)

    def body(p_ref, o_ref):
        total = p_ref[0].astype(F32)
        for s in range(1, nsrc):
            total = total + p_ref[s].astype(F32)
        o_ref[...] = total

    return pl.pallas_call(
        body, name=name, grid=(rows // tr,),
        in_specs=[pl.BlockSpec((nsrc, tr, cols), lambda i: (0, i, 0))],
        out_specs=pl.BlockSpec((tr, cols), lambda i: (i, 0)),
        out_shape=jax.ShapeDtypeStruct((rows, cols), F32), compiler_params=_params(1),
    )(parts)


def _sum_grad_half(got, gp, place):
    nsrc, hr, d = got.shape
    tr = _row_tile(hr)
    steps = hr // tr

    def body(place_ref, got_ref, own_ref, o_ref):
        total = own_ref[...].astype(F32)
        for s in range(nsrc):
            total = total + got_ref[s].astype(F32)
        o_ref[...] = total

    grid_spec = pltpu.PrefetchScalarGridSpec(
        num_scalar_prefetch=1, grid=(steps,),
        in_specs=[pl.BlockSpec((nsrc, tr, d), lambda i, pc: (0, i, 0)),
                  pl.BlockSpec((None, tr, d), lambda i, pc: (pc[0], pc[1] * steps + i, 0))],
        out_specs=pl.BlockSpec((None, tr, d), lambda i, pc: (pc[1], i, 0)))
    return pl.pallas_call(
        body, name="sum_grad_half", grid_spec=grid_spec,
        out_shape=jax.ShapeDtypeStruct((2, hr, d), F32), compiler_params=_params(1),
    )(place, got, gp)


def _exchange_halves(fulls):
    n = len(fulls)
    _, hr, d = fulls[0].shape
    cr = hr // HALF_CHUNKS
    nc = n * HALF_CHUNKS

    def body(*refs):
        full = refs[n:2 * n]
        send, recv = refs[2 * n:]
        x, y, c = lax.axis_index("x"), lax.axis_index("y"), lax.axis_index("c")
        sibling = (x, y, 1 - c)
        pieces = [(l, q) for l in range(n) for q in range(HALF_CHUNKS)]
        sends = []
        for i, (l, q) in enumerate(pieces):
            piece = full[l].at[c, pl.ds(q * cr, cr)]
            sends.append(_remote(piece, piece, send.at[i], recv.at[i], sibling))
        for cp in sends:
            cp.start()
        for i, (l, q) in enumerate(pieces):
            theirs = full[l].at[1 - c, pl.ds(q * cr, cr)]
            _remote(theirs, theirs, send.at[i], recv.at[i], sibling).wait_recv()
        for cp in sends:
            cp.wait_send()

    return pl.pallas_call(
        body, name="exchange_halves", in_specs=[HBM_SPEC] * n, out_specs=[HBM_SPEC] * n,
        out_shape=[jax.ShapeDtypeStruct(f.shape, F32) for f in fulls], input_output_aliases={l: l for l in range(n)},
        scratch_shapes=[pltpu.SemaphoreType.DMA((nc,)), pltpu.SemaphoreType.DMA((nc,))],
    )(*fulls)


def _adamw(name, w, m, v, g, g_row0=0):
    rows, cols = w.shape
    tr = _row_tile(rows)
    off = g_row0 // tr

    def body(w_ref, m_ref, v_ref, g_ref, go_ref, d_ref, mo_ref, vo_ref):
        grad = g_ref[...]
        m_new = ADAM_B1 * m_ref[...] + (1.0 - ADAM_B1) * grad
        v_new = ADAM_B2 * v_ref[...] + (1.0 - ADAM_B2) * (grad * grad)
        m_hat = m_new / (1.0 - ADAM_B1 ** ADAM_STEP)
        v_hat = v_new / (1.0 - ADAM_B2 ** ADAM_STEP)
        go_ref[...] = grad
        d_ref[...] = -ADAM_LR * (m_hat / (jnp.sqrt(v_hat) + ADAM_EPS) + ADAM_WD * w_ref[...])
        mo_ref[...] = m_new
        vo_ref[...] = v_new

    blk = pl.BlockSpec((tr, cols), lambda i: (i, 0))
    return pl.pallas_call(
        body, name=name, grid=(rows // tr,),
        in_specs=[blk, blk, blk, pl.BlockSpec((tr, cols), lambda i: (i + off, 0))],
        out_specs=[blk, blk, blk, blk],
        out_shape=[jax.ShapeDtypeStruct((rows, cols), F32)] * 4, compiler_params=_params(1),
    )(w, m, v, g)


def _pad_rows8(a):
    return jnp.concatenate([a, jnp.zeros((8 - a.shape[0],) + a.shape[1:], a.dtype)], axis=0)


def kernel(x, ln_pre_0, conv_w_in_0, conv_w_0, conv_w_out_0, ln_post_0, ln_pre_1, sb_w_in_1, sb_w_out_1, ln_post_1, ln_pre_2, conv_w_in_2, conv_w_2, conv_w_out_2, ln_post_2, ln_pre_3, sb_w_in_3, sb_w_out_3, ln_post_3, loss_target, m_ln_pre_0, m_conv_w_in_0, m_conv_w_0, m_conv_w_out_0, m_ln_post_0, m_ln_pre_1, m_sb_w_in_1, m_sb_w_out_1, m_ln_post_1, m_ln_pre_2, m_conv_w_in_2, m_conv_w_2, m_conv_w_out_2, m_ln_post_2, m_ln_pre_3, m_sb_w_in_3, m_sb_w_out_3, m_ln_post_3, v_ln_pre_0, v_conv_w_in_0, v_conv_w_0, v_conv_w_out_0, v_ln_post_0, v_ln_pre_1, v_sb_w_in_1, v_sb_w_out_1, v_ln_post_1, v_ln_pre_2, v_conv_w_in_2, v_conv_w_2, v_conv_w_out_2, v_ln_post_2, v_ln_pre_3, v_sb_w_in_3, v_sb_w_out_3, v_ln_post_3):
    t, d = x.shape[1], x.shape[2]
    dq = d // N_CHIPS
    xs = x.reshape(t, d)
    target = loss_target.reshape(t, d)
    w_in = [conv_w_in_0, sb_w_in_1, conv_w_in_2, sb_w_in_3]
    w_out = [conv_w_out_0, sb_w_out_1, conv_w_out_2, sb_w_out_3]
    m_in = [m_conv_w_in_0, m_sb_w_in_1, m_conv_w_in_2, m_sb_w_in_3]
    m_out = [m_conv_w_out_0, m_sb_w_out_1, m_conv_w_out_2, m_sb_w_out_3]
    v_in = [v_conv_w_in_0, v_sb_w_in_1, v_conv_w_in_2, v_sb_w_in_3]
    v_out = [v_conv_w_out_0, v_sb_w_out_1, v_conv_w_out_2, v_sb_w_out_3]
    ln_pre = [ln_pre_0, ln_pre_1, ln_pre_2, ln_pre_3]
    ln_post = [ln_post_0, ln_post_1, ln_post_2, ln_post_3]
    conv_w = [conv_w_0, conv_w_2]
    m_conv = [m_conv_w_0, m_conv_w_2]
    v_conv = [v_conv_w_0, v_conv_w_2]
    chip = 2 * lax.axis_index("x") + lax.axis_index("y")
    chip_arr = jnp.reshape(chip, (1,)).astype(jnp.int32)
    place = jnp.stack([chip, lax.axis_index("c")]).astype(jnp.int32)

    packs = [_pack_weights(w_in[l], w_out[l], chip_arr) for l in range(N_LAYERS)]
    cw_local = jnp.concatenate([_pad_rows8(conv_w[0]), _pad_rows8(conv_w[1])], axis=0)
    *gathered, cw_all = _all_gather_weights(packs, cw_local)
    cw_full = jnp.transpose(cw_all, (1, 0, 2)).reshape(16, d)
    conv_taps = {0: cw_full[0:8], 2: cw_full[8:16]}

    h_in, us, projs, gateds, ms, sb_saved = [], [], [], [], [], {}
    h = xs
    u = _norm_first(xs, ln_pre[0])
    for l in range(N_LAYERS):
        h_in.append(h)
        us.append(u)
        proj = _mm_proj(u, gathered[l])
        if l % 2 == 0:
            gated = _conv_fwd(proj, conv_taps[l])
        else:
            gated, o, car = _sb_fwd(proj)
            sb_saved[l] = (o, car)
        m = _mm_out(gated, gathered[l])
        projs.append(proj)
        gateds.append(gated)
        ms.append(m)
        if l < N_LAYERS - 1:
            h, u = _norm_mid(h, m, ln_post[l], ln_pre[l + 1])
    dh, dm, dg_post_last, loss_part = _norm_last(h, ms[-1], ln_post[-1], target)
    loss = lax.psum(loss_part[0, 0], ("x", "y", "c"))

    dg_pre = [None] * N_LAYERS
    dg_post = [None] * N_LAYERS
    dg_post[N_LAYERS - 1] = dg_post_last
    dconv = {}
    gps = [None] * N_LAYERS
    grad_x = None
    for l in reversed(range(N_LAYERS)):
        dgated = _mm_dgated(dm, gathered[l])
        gp = _mm_dwout(gateds[l], dm)
        if l % 2 == 0:
            dproj, dconv[l] = _conv_bwd(projs[l], dgated, conv_taps[l])
        else:
            o, car = sb_saved[l]
            dproj = _sb_bwd(projs[l], dgated, o, car)
        du = _mm_du(dproj, gathered[l])
        gps[l] = _mm_dwin(us[l], dproj, gp)
        if l > 0:
            dh, dm, dg_pre[l], dg_post[l - 1] = _norm_bwd_mid(dh, du, h_in[l], ln_pre[l], ms[l - 1], ln_post[l - 1])
        else:
            grad_x, dg_pre[0] = _norm_bwd_first(dh, du, h_in[0], ln_pre[0])

    small = jnp.concatenate(dg_pre + dg_post + [dconv[0], dconv[2]], axis=0)
    *got, small_all = _exchange_grads(gps, small)
    halves = [_sum_grad_half(got[l], gps[l], place) for l in range(N_LAYERS)]
    small_sum = _sum_sources("sum_small", small_all)
    fulls = [f.reshape(d + dq, d) for f in _exchange_halves(halves)]

    res_in = [_adamw("adamw_w_in", w_in[l], m_in[l], v_in[l], fulls[l], 0) for l in range(N_LAYERS)]
    res_out = [_adamw("adamw_w_out", w_out[l], m_out[l], v_out[l], fulls[l], d) for l in range(N_LAYERS)]
    ln_all = ln_pre + ln_post
    ln_m = [m_ln_pre_0, m_ln_pre_1, m_ln_pre_2, m_ln_pre_3, m_ln_post_0, m_ln_post_1, m_ln_post_2, m_ln_post_3]
    ln_v = [v_ln_pre_0, v_ln_pre_1, v_ln_pre_2, v_ln_pre_3, v_ln_post_0, v_ln_post_1, v_ln_post_2, v_ln_post_3]
    ln_g = jnp.concatenate([small_sum[8 * i:8 * i + 1] for i in range(2 * N_LAYERS)], axis=0)
    res_ln = _adamw("adamw_ln", jnp.stack(ln_all), jnp.stack(ln_m), jnp.stack(ln_v), ln_g)
    conv_g = []
    for i in range(2):
        rows = small_sum[8 * (2 * N_LAYERS + i):8 * (2 * N_LAYERS + i + 1)]
        conv_g.append(lax.dynamic_slice(rows, (0, chip * dq), (8, dq)))
    stack8 = lambda a, b: jnp.concatenate([_pad_rows8(a), _pad_rows8(b)], axis=0)
    res_conv = _adamw("adamw_conv", stack8(*conv_w), stack8(*m_conv), stack8(*v_conv), jnp.concatenate(conv_g, axis=0))

    def leaf(kind, l, which):
        if kind == "ln_pre":
            return res_ln[which][l]
        if kind == "ln_post":
            return res_ln[which][N_LAYERS + l]
        if kind == "w_in":
            return res_in[l][which]
        if kind == "w_out":
            return res_out[l][which]
        return res_conv[which][8 * (l // 2):8 * (l // 2) + 3]

    order = []
    for l in range(N_LAYERS):
        order.append(("ln_pre", l))
        order.append(("w_in", l))
        if l % 2 == 0:
            order.append(("conv", l))
        order.append(("w_out", l))
        order.append(("ln_post", l))
    outs = [loss, grad_x.reshape(1, t, d)]
    for which in range(4):
        outs.extend(leaf(kind, l, which) for kind, l in order)
    return tuple(outs)
```

```python
import functools
import math
from typing import Any, Callable, Mapping, NamedTuple, Sequence

import jax
import jax.numpy as jnp
from jax import lax
from jax.experimental import pallas as pl
from jax.experimental.pallas import tpu as pltpu

F32 = jnp.float32
BF16 = jnp.bfloat16

N_CHIPS = 4
N_DEV = 8
N_LAYERS = 4
HEAD_DIM = 128
RMS_EPS = 1e-6
ADAM_LR = 0.001
ADAM_B1 = 0.9
ADAM_B2 = 0.999
ADAM_EPS = 1e-08
ADAM_WD = 0.01
ADAM_STEP = 10

VMEM_LIMIT = 56 * 1024 * 1024
MESH_IDS = pl.DeviceIdType.MESH
HBM_SPEC = pl.BlockSpec(memory_space=pltpu.HBM)

NN = (((1,), (0,)), ((), ()))
NT = (((1,), (1,)), ((), ()))
TN = (((0,), (0,)), ((), ()))


def _params(n_axes):
    return pltpu.CompilerParams(dimension_semantics=("arbitrary",) * n_axes, vmem_limit_bytes=VMEM_LIMIT)


def _dot(a, b, dims):
    return lax.dot_general(a, b, dims, preferred_element_type=F32)


def _sigmoid(z):
    return 1.0 / (1.0 + jnp.exp(-z))


class _Rider(NamedTuple):
    operands: Sequence[Any]
    out_shapes: Sequence[Any]
    aliases: Mapping[int, int]
    sems: Sequence[Any]
    start: Callable
    finish: Callable


def _compute_call(body, name, *, grid, in_specs, operands, out_specs, out_shape, scratch=(), aliases=None, rider=None):
    in_specs, operands = list(in_specs), list(operands)
    out_specs, out_shape, scratch = list(out_specs), list(out_shape), list(scratch)
    aliases = dict(aliases or {})
    n_in, n_out, n_scratch = len(operands), len(out_shape), len(scratch)
    hosted = body
    if rider is not None:
        r_in, r_out = len(rider.operands), len(rider.out_shapes)
        aliases.update({n_in + i: n_out + o for i, o in rider.aliases.items()})

        def hosted(*refs):
            ins, refs = refs[:n_in], refs[n_in:]
            rider_ins, refs = refs[:r_in], refs[r_in:]
            outs, refs = refs[:n_out], refs[n_out:]
            rider_outs, refs = refs[:r_out], refs[r_out:]
            own_scratch, rider_sems = refs[:n_scratch], refs[n_scratch:]
            ids = [pl.program_id(axis) for axis in range(len(grid))]
            first = functools.reduce(jnp.logical_and, [i == 0 for i in ids])
            last = functools.reduce(jnp.logical_and, [i == g - 1 for i, g in zip(ids, grid)])

            @pl.when(first)
            def _():
                rider.start(rider_ins, rider_outs, rider_sems)

            body(*ins, *outs, *own_scratch)

            @pl.when(last)
            def _():
                rider.finish(rider_ins, rider_outs, rider_sems)

        in_specs += [HBM_SPEC] * r_in
        operands += list(rider.operands)
        out_specs += [HBM_SPEC] * r_out
        out_shape += list(rider.out_shapes)
        scratch += list(rider.sems)
    return pl.pallas_call(
        hosted, name=name, grid=grid, in_specs=in_specs, out_specs=out_specs, out_shape=out_shape,
        scratch_shapes=scratch, input_output_aliases=aliases, compiler_params=_params(len(grid)),
    )(*operands)


def _matmul(name, a, b, *, grid, a_spec, b_spec, o_spec, out_shape, dims, reduce_axis=None, acc_shape=None,
            alias_out=None, rider=None):
    out_dtype = out_shape.dtype
    direct = reduce_axis is not None and out_dtype == F32
    n_red = grid[reduce_axis] if reduce_axis is not None else 1

    def body(*refs):
        if alias_out is not None:
            refs = refs[1:]
        a_ref, b_ref, o_ref = refs[:3]
        prod = _dot(a_ref[...], b_ref[...], dims)
        if reduce_axis is None:
            o_ref[...] = prod.astype(out_dtype)
            return
        acc_ref = o_ref if direct else refs[3]
        k = pl.program_id(reduce_axis)

        @pl.when(k == 0)
        def _():
            acc_ref[...] = prod

        @pl.when(k > 0)
        def _():
            acc_ref[...] += prod

        if not direct:
            @pl.when(k == n_red - 1)
            def _():
                o_ref[...] = acc_ref[...].astype(out_dtype)

    scratch = []
    if reduce_axis is not None and not direct:
        scratch = [pltpu.VMEM(acc_shape, F32)]
    in_specs = [a_spec, b_spec]
    operands = [a, b]
    aliases = {}
    if alias_out is not None:
        in_specs = [HBM_SPEC] + in_specs
        operands = [alias_out] + operands
        aliases = {0: 0}
    return _compute_call(body, name, grid=grid, in_specs=in_specs, operands=operands, out_specs=[o_spec],
                         out_shape=[out_shape], scratch=scratch, aliases=aliases, rider=rider)


def _mm_proj(u, g, rider=None):
    t, d = u.shape
    tm = min(512, t)
    return _matmul(
        "mm_proj", u, g, grid=(N_CHIPS, t // tm),
        a_spec=pl.BlockSpec((tm, d), lambda s, m: (m, 0)),
        b_spec=pl.BlockSpec((None, d, d), lambda s, m: (s, 0, 0)),
        o_spec=pl.BlockSpec((None, tm, d), lambda s, m: (s, m, 0)),
        out_shape=jax.ShapeDtypeStruct((N_CHIPS, t, d), BF16), dims=NN, rider=rider)


def _mm_out(gated, g):
    t, d = gated.shape
    dq = d // N_CHIPS
    tm = min(512, t)
    return _matmul(
        "mm_out", gated, g, grid=(t // tm, N_CHIPS),
        a_spec=pl.BlockSpec((tm, dq), lambda m, s: (m, s)),
        b_spec=pl.BlockSpec((None, dq, d), lambda m, s: (s, N_CHIPS, 0)),
        o_spec=pl.BlockSpec((tm, d), lambda m, s: (m, 0)),
        out_shape=jax.ShapeDtypeStruct((t, d), F32), dims=NN, reduce_axis=1)[0]


def _mm_dgated(dm, g):
    t, d = dm.shape
    dq = d // N_CHIPS
    tm = min(512, t)
    return _matmul(
        "mm_dgated", dm, g, grid=(t // tm, N_CHIPS),
        a_spec=pl.BlockSpec((tm, d), lambda m, s: (m, 0)),
        b_spec=pl.BlockSpec((None, dq, d), lambda m, s: (s, N_CHIPS, 0)),
        o_spec=pl.BlockSpec((tm, dq), lambda m, s: (m, s)),
        out_shape=jax.ShapeDtypeStruct((t, d), BF16), dims=NT)[0]


def _mm_dwout(gated, dm):
    t, d = gated.shape
    dq = d // N_CHIPS
    tk = min(512, t)
    return _matmul(
        "mm_dwout", gated, dm, grid=(N_CHIPS, t // tk),
        a_spec=pl.BlockSpec((tk, dq), lambda s, k: (k, s)),
        b_spec=pl.BlockSpec((tk, d), lambda s, k: (k, 0)),
        o_spec=pl.BlockSpec((None, dq, d), lambda s, k: (s, N_CHIPS, 0)),
        out_shape=jax.ShapeDtypeStruct((N_CHIPS, d + dq, d), BF16), dims=TN, reduce_axis=1, acc_shape=(dq, d))[0]


def _mm_du(dproj, g, rider=None):
    _, t, d = dproj.shape
    tm = min(512, t)
    return _matmul(
        "mm_du", dproj, g, grid=(t // tm, N_CHIPS),
        a_spec=pl.BlockSpec((None, tm, d), lambda m, s: (s, m, 0)),
        b_spec=pl.BlockSpec((None, d, d), lambda m, s: (s, 0, 0)),
        o_spec=pl.BlockSpec((tm, d), lambda m, s: (m, 0)),
        out_shape=jax.ShapeDtypeStruct((t, d), F32), dims=NT, reduce_axis=1, rider=rider)


def _mm_dwin(u, dproj, gp, rider=None):
    t, d = u.shape
    tmo = min(1024, d)
    tk = min(512, t)
    return _matmul(
        "mm_dwin", u, dproj, grid=(N_CHIPS, d // tmo, t // tk),
        a_spec=pl.BlockSpec((tk, tmo), lambda s, mo, k: (k, mo)),
        b_spec=pl.BlockSpec((None, tk, d), lambda s, mo, k: (s, k, 0)),
        o_spec=pl.BlockSpec((None, tmo, d), lambda s, mo, k: (s, mo, 0)),
        out_shape=jax.ShapeDtypeStruct(gp.shape, BF16), dims=TN, reduce_axis=2, acc_shape=(tmo, d), alias_out=gp,
        rider=rider)


def _rms(v):
    r = lax.rsqrt(jnp.mean(v * v, axis=-1, keepdims=True) + RMS_EPS)
    return v * r, r


def _rms_bwd(dout, n, r, gain):
    dn = dout * gain
    return r * (dn - n * jnp.mean(dn * n, axis=-1, keepdims=True))


def _fold8(v):
    return jnp.sum(v.reshape(v.shape[0] // 8, 8, v.shape[1]), axis=0)


def _row0(total):
    rows = lax.broadcasted_iota(jnp.int32, total.shape, 0)
    return jnp.where(rows == 0, jnp.sum(total, axis=0, keepdims=True), 0.0)


def _norm_tile(t):
    return min(256, t)


def _norm_first(x, g_pre):
    t, d = x.shape
    tr = _norm_tile(t)

    def body(x_ref, g_ref, u_ref):
        n, _ = _rms(x_ref[...])
        u_ref[...] = (n * g_ref[...]).astype(BF16)

    row = pl.BlockSpec((tr, d), lambda i: (i, 0))
    vec = pl.BlockSpec((1, d), lambda i: (0, 0))
    return pl.pallas_call(
        body, name="norm_first", grid=(t // tr,), in_specs=[row, vec], out_specs=row,
        out_shape=jax.ShapeDtypeStruct((t, d), BF16), compiler_params=_params(1),
    )(x, g_pre.reshape(1, d))


def _norm_mid(h, m, g_post, g_pre_next):
    t, d = h.shape
    tr = _norm_tile(t)

    def body(h_ref, m_ref, gp_ref, gn_ref, hn_ref, u_ref):
        n, _ = _rms(m_ref[...])
        hn = h_ref[...] + n * gp_ref[...]
        hn_ref[...] = hn
        n2, _ = _rms(hn)
        u_ref[...] = (n2 * gn_ref[...]).astype(BF16)

    row = pl.BlockSpec((tr, d), lambda i: (i, 0))
    vec = pl.BlockSpec((1, d), lambda i: (0, 0))
    return pl.pallas_call(
        body, name="norm_mid", grid=(t // tr,), in_specs=[row, row, vec, vec], out_specs=[row, row],
        out_shape=[jax.ShapeDtypeStruct((t, d), F32), jax.ShapeDtypeStruct((t, d), BF16)],
        compiler_params=_params(1),
    )(h, m, g_post.reshape(1, d), g_pre_next.reshape(1, d))


def _norm_last(h, m, g_post, target):
    t, d = h.shape
    tr = _norm_tile(t)
    nsteps = t // tr

    def body(h_ref, m_ref, gp_ref, tg_ref, dy_ref, dm_ref, dgp_ref, loss_ref, acc_g, acc_l):
        i = pl.program_id(0)

        @pl.when(i == 0)
        def _():
            acc_g[...] = jnp.zeros_like(acc_g)
            acc_l[...] = jnp.zeros_like(acc_l)

        gain = gp_ref[...]
        n, r = _rms(m_ref[...])
        err = h_ref[...] + n * gain - tg_ref[...]
        dy = err / d
        dy_ref[...] = dy
        dm_ref[...] = _rms_bwd(dy, n, r, gain).astype(BF16)
        acc_g[...] += _fold8(dy * n)
        acc_l[...] += _fold8(err * err)

        @pl.when(i == nsteps - 1)
        def _():
            dgp_ref[...] = _row0(acc_g[...])
            loss_ref[...] = jnp.zeros((8, 128), F32) + (0.5 / d) * jnp.sum(acc_l[...])

    row = pl.BlockSpec((tr, d), lambda i: (i, 0))
    vec = pl.BlockSpec((1, d), lambda i: (0, 0))
    acc = pl.BlockSpec((8, d), lambda i: (0, 0))
    return pl.pallas_call(
        body, name="norm_last", grid=(nsteps,), in_specs=[row, row, vec, row],
        out_specs=[row, row, acc, pl.BlockSpec((8, 128), lambda i: (0, 0))],
        out_shape=[jax.ShapeDtypeStruct((t, d), F32), jax.ShapeDtypeStruct((t, d), BF16),
                   jax.ShapeDtypeStruct((8, d), F32), jax.ShapeDtypeStruct((8, 128), F32)],
        scratch_shapes=[pltpu.VMEM((8, d), F32), pltpu.VMEM((8, d), F32)],
        compiler_params=_params(1),
    )(h, m, g_post.reshape(1, d), target)


def _norm_bwd_mid(dh, du, h_in, g_pre, m_prev, g_post_prev):
    t, d = dh.shape
    tr = _norm_tile(t)
    nsteps = t // tr

    def body(dh_ref, du_ref, h_ref, gpre_ref, m_ref, gpost_ref, dhn_ref, dm_ref, dgpre_ref, dgpost_ref, acc_a, acc_b):
        i = pl.program_id(0)

        @pl.when(i == 0)
        def _():
            acc_a[...] = jnp.zeros_like(acc_a)
            acc_b[...] = jnp.zeros_like(acc_b)

        du_t = du_ref[...]
        n, r = _rms(h_ref[...])
        dhn = dh_ref[...] + _rms_bwd(du_t, n, r, gpre_ref[...])
        dhn_ref[...] = dhn
        acc_a[...] += _fold8(du_t * n)
        n2, r2 = _rms(m_ref[...])
        dm_ref[...] = _rms_bwd(dhn, n2, r2, gpost_ref[...]).astype(BF16)
        acc_b[...] += _fold8(dhn * n2)

        @pl.when(i == nsteps - 1)
        def _():
            dgpre_ref[...] = _row0(acc_a[...])
            dgpost_ref[...] = _row0(acc_b[...])

    row = pl.BlockSpec((tr, d), lambda i: (i, 0))
    vec = pl.BlockSpec((1, d), lambda i: (0, 0))
    acc = pl.BlockSpec((8, d), lambda i: (0, 0))
    return pl.pallas_call(
        body, name="norm_bwd_mid", grid=(nsteps,), in_specs=[row, row, row, vec, row, vec],
        out_specs=[row, row, acc, acc],
        out_shape=[jax.ShapeDtypeStruct((t, d), F32), jax.ShapeDtypeStruct((t, d), BF16),
                   jax.ShapeDtypeStruct((8, d), F32), jax.ShapeDtypeStruct((8, d), F32)],
        scratch_shapes=[pltpu.VMEM((8, d), F32), pltpu.VMEM((8, d), F32)],
        compiler_params=_params(1),
    )(dh, du, h_in, g_pre.reshape(1, d), m_prev, g_post_prev.reshape(1, d))


def _norm_bwd_first(dh, du, x, g_pre):
    t, d = dh.shape
    tr = _norm_tile(t)
    nsteps = t // tr

    def body(dh_ref, du_ref, x_ref, gpre_ref, dx_ref, dgpre_ref, acc_a):
        i = pl.program_id(0)

        @pl.when(i == 0)
        def _():
            acc_a[...] = jnp.zeros_like(acc_a)

        du_t = du_ref[...]
        n, r = _rms(x_ref[...])
        dx_ref[...] = dh_ref[...] + _rms_bwd(du_t, n, r, gpre_ref[...])
        acc_a[...] += _fold8(du_t * n)

        @pl.when(i == nsteps - 1)
        def _():
            dgpre_ref[...] = _row0(acc_a[...])

    row = pl.BlockSpec((tr, d), lambda i: (i, 0))
    vec = pl.BlockSpec((1, d), lambda i: (0, 0))
    acc = pl.BlockSpec((8, d), lambda i: (0, 0))
    return pl.pallas_call(
        body, name="norm_bwd_first", grid=(nsteps,), in_specs=[row, row, row, vec], out_specs=[row, acc],
        out_shape=[jax.ShapeDtypeStruct((t, d), F32), jax.ShapeDtypeStruct((8, d), F32)],
        scratch_shapes=[pltpu.VMEM((8, d), F32)], compiler_params=_params(1),
    )(dh, du, x, g_pre.reshape(1, d))


CONV_TC = 128
CONV_HALO = 16


def _conv_chunk(t):
    return min(512, t)


def _shift_down(v, steps, fill):
    rows = lax.broadcasted_iota(jnp.int32, v.shape, 0)
    out = pltpu.roll(v, steps, axis=0)
    for k in range(steps):
        out = jnp.where(rows == k, fill[CONV_HALO - steps + k:CONV_HALO - steps + k + 1, :], out)
    return out


def _shift_up(v, steps, fill):
    nrows = v.shape[0]
    rows = lax.broadcasted_iota(jnp.int32, v.shape, 0)
    out = pltpu.roll(v, nrows - steps, axis=0)
    for k in range(steps):
        out = jnp.where(rows == nrows - steps + k, fill[k:k + 1, :], out)
    return out


def _conv_fwd(proj, cw):
    _, t, d = proj.shape
    chunk = _conv_chunk(t)

    def body(p_ref, w_ref, o_ref):
        w = w_ref[...]
        w0, w1, w2 = w[0:1, :], w[1:2, :], w[2:3, :]
        for ci in range(t // chunk):
            t0 = ci * chunk
            rows = pl.ds(t0, chunk)
            b = p_ref[0, rows, :].astype(F32)
            cx = p_ref[1, rows, :].astype(F32) * p_ref[2, rows, :].astype(F32)
            z = p_ref[3, rows, :].astype(F32)
            if ci == 0:
                prev = jnp.zeros((CONV_HALO, CONV_TC), F32)
            else:
                halo = pl.ds(t0 - CONV_HALO, CONV_HALO)
                prev = p_ref[1, halo, :].astype(F32) * p_ref[2, halo, :].astype(F32)
            conv = w2 * cx + w1 * _shift_down(cx, 1, prev) + w0 * _shift_down(cx, 2, prev)
            o_ref[rows, :] = (z * _sigmoid(z) * b * conv).astype(BF16)

    return pl.pallas_call(
        body, name="conv_fwd", grid=(d // CONV_TC,),
        in_specs=[pl.BlockSpec((N_CHIPS, t, CONV_TC), lambda j: (0, 0, j)), pl.BlockSpec((8, CONV_TC), lambda j: (0, j))],
        out_specs=pl.BlockSpec((t, CONV_TC), lambda j: (0, j)),
        out_shape=jax.ShapeDtypeStruct((t, d), BF16), compiler_params=_params(1),
    )(proj, cw)


def _conv_bwd(proj, dgated, cw):
    _, t, d = proj.shape
    chunk = _conv_chunk(t)
    nchunks = t // chunk

    def body(p_ref, dg_ref, w_ref, dp_ref, dw_ref):
        w = w_ref[...]
        w0, w1, w2 = w[0:1, :], w[1:2, :], w[2:3, :]
        dw0 = jnp.zeros((1, CONV_TC), F32)
        dw1 = jnp.zeros((1, CONV_TC), F32)
        dw2 = jnp.zeros((1, CONV_TC), F32)
        for ci in range(nchunks):
            t0 = ci * chunk
            rows = pl.ds(t0, chunk)
            b = p_ref[0, rows, :].astype(F32)
            c = p_ref[1, rows, :].astype(F32)
            xt = p_ref[2, rows, :].astype(F32)
            z = p_ref[3, rows, :].astype(F32)
            dg = dg_ref[rows, :].astype(F32)
            cx = c * xt
            if ci == 0:
                prev = jnp.zeros((CONV_HALO, CONV_TC), F32)
            else:
                halo = pl.ds(t0 - CONV_HALO, CONV_HALO)
                prev = p_ref[1, halo, :].astype(F32) * p_ref[2, halo, :].astype(F32)
            cx1 = _shift_down(cx, 1, prev)
            cx2 = _shift_down(cx, 2, prev)
            conv = w2 * cx + w1 * cx1 + w0 * cx2
            sig = _sigmoid(z)
            dy = dg * (z * sig)
            dconv = dy * b
            if ci == nchunks - 1:
                nxt = jnp.zeros((CONV_HALO, CONV_TC), F32)
            else:
                halo = pl.ds(t0 + chunk, CONV_HALO)
                zn = p_ref[3, halo, :].astype(F32)
                nxt = dg_ref[halo, :].astype(F32) * (zn * _sigmoid(zn)) * p_ref[0, halo, :].astype(F32)
            dcx = w2 * dconv + w1 * _shift_up(dconv, 1, nxt) + w0 * _shift_up(dconv, 2, nxt)
            dp_ref[0, rows, :] = (dy * conv).astype(BF16)
            dp_ref[1, rows, :] = (dcx * xt).astype(BF16)
            dp_ref[2, rows, :] = (dcx * c).astype(BF16)
            dp_ref[3, rows, :] = (dg * (b * conv) * (sig * (1.0 + z * (1.0 - sig)))).astype(BF16)
            dw0 = dw0 + jnp.sum(dconv * cx2, axis=0, keepdims=True)
            dw1 = dw1 + jnp.sum(dconv * cx1, axis=0, keepdims=True)
            dw2 = dw2 + jnp.sum(dconv * cx, axis=0, keepdims=True)
        taps = lax.broadcasted_iota(jnp.int32, (8, CONV_TC), 0)
        dw_ref[...] = jnp.where(taps == 0, dw0, jnp.where(taps == 1, dw1, jnp.where(taps == 2, dw2, 0.0)))

    return pl.pallas_call(
        body, name="conv_bwd", grid=(d // CONV_TC,),
        in_specs=[pl.BlockSpec((N_CHIPS, t, CONV_TC), lambda j: (0, 0, j)),
                  pl.BlockSpec((t, CONV_TC), lambda j: (0, j)),
                  pl.BlockSpec((8, CONV_TC), lambda j: (0, j))],
        out_specs=[pl.BlockSpec((N_CHIPS, t, CONV_TC), lambda j: (0, 0, j)), pl.BlockSpec((8, CONV_TC), lambda j: (0, j))],
        out_shape=[jax.ShapeDtypeStruct((N_CHIPS, t, d), BF16), jax.ShapeDtypeStruct((8, d), F32)],
        compiler_params=_params(1),
    )(proj, dgated, cw)


SB_DEAD_TAIL = -105.0
SB_COUNT_LANE = HEAD_DIM - 1


def _sb_block(t):
    return min(256, t)


def _split_dot(v, tri):
    hi = v.astype(BF16)
    lo = (v - hi.astype(F32)).astype(BF16)
    return _dot(hi, tri, NN) + _dot(lo, tri, NN)


def _sb_scores(q, kj, q0, k0, scale):
    blk_q, blk_k = q.shape[0], kj.shape[0]
    s = _dot(q, kj, NT) * scale
    sp = jnp.maximum(s, 0.0) + jnp.log1p(jnp.exp(-jnp.abs(s)))
    rows = lax.broadcasted_iota(jnp.int32, (blk_q, blk_k), 0) + q0
    cols = lax.broadcasted_iota(jnp.int32, (blk_q, blk_k), 1) + k0
    mask = cols < rows
    keep = jnp.where(mask, -sp, 0.0)
    return keep, s - sp, sp, mask


def _sb_fwd(proj, rider=None):
    _, t, d = proj.shape
    heads = d // HEAD_DIM
    blk = _sb_block(t)
    nblk = t // blk
    scale = 1.0 / math.sqrt(HEAD_DIM)

    def body(q_ref, k_ref, v_ref, z_ref, gated_ref, o_ref, car_ref, tail_ref, acc_ref):
        i = pl.program_id(1)
        q = q_ref[...]
        r_i = lax.broadcasted_iota(jnp.int32, (blk, blk), 0)
        c_i = lax.broadcasted_iota(jnp.int32, (blk, blk), 1)
        tri_after = (r_i > c_i).astype(BF16)
        lanes = lax.broadcasted_iota(jnp.int32, (blk, HEAD_DIM), 1)

        tail_ref[...] = jnp.zeros_like(tail_ref)
        acc_ref[...] = jnp.zeros_like(acc_ref)
        car_ref[...] = jnp.zeros_like(car_ref)

        def more(state):
            jj, live = state
            return jnp.logical_and(jj <= i, live)

        def step(state):
            jj, _ = state
            j = i - jj
            k0 = pl.multiple_of(j * blk, blk)
            kj = k_ref[pl.ds(k0, blk), :]
            vj = v_ref[pl.ds(k0, blk), :]
            keep, log_beta, _, mask = _sb_scores(q, kj, i * blk, k0, scale)
            within = _split_dot(keep, tri_after)
            tail_b = tail_ref[...]
            w = jnp.where(mask, jnp.exp(log_beta + tail_b[:, 0:1] + within), 0.0)
            acc_ref[...] += _dot(w.astype(BF16), vj, NN)
            car_ref[...] = jnp.where(lanes == j, tail_b, car_ref[...])
            tail_new = tail_b + jnp.sum(keep, axis=1, keepdims=True)
            tail_ref[...] = tail_new
            return jj + 1, jnp.max(tail_new) > SB_DEAD_TAIL

        visited, _ = lax.while_loop(more, step, (jnp.int32(0), True))
        car_ref[...] = jnp.where(lanes == SB_COUNT_LANE, visited.astype(F32), car_ref[...])
        z = z_ref[...].astype(F32)
        acc = acc_ref[...]
        o_ref[...] = acc.astype(BF16)
        gated_ref[...] = (z * _sigmoid(z) * acc).astype(BF16)

    qspec = lambda s: pl.BlockSpec((None, blk, HEAD_DIM), lambda h, i: (s, i, h))
    kspec = lambda s: pl.BlockSpec((None, t, HEAD_DIM), lambda h, i: (s, 0, h))
    ospec = pl.BlockSpec((blk, HEAD_DIM), lambda h, i: (i, h))
    return _compute_call(
        body, "sb_fwd", grid=(heads, nblk),
        in_specs=[qspec(0), kspec(1), kspec(2), qspec(3)], operands=[proj, proj, proj, proj],
        out_specs=[ospec, ospec, pl.BlockSpec((None, blk, HEAD_DIM), lambda h, i: (h, i, 0))],
        out_shape=[jax.ShapeDtypeStruct((t, d), BF16), jax.ShapeDtypeStruct((t, d), BF16),
                   jax.ShapeDtypeStruct((heads, t, HEAD_DIM), F32)],
        scratch=[pltpu.VMEM((blk, HEAD_DIM), F32), pltpu.VMEM((blk, HEAD_DIM), F32)], rider=rider)


def _sb_bwd(proj, dgated, o, car):
    _, t, d = proj.shape
    heads = d // HEAD_DIM
    blk = _sb_block(t)
    nblk = t // blk
    scale = 1.0 / math.sqrt(HEAD_DIM)

    def body(q_ref, k_ref, v_ref, z_ref, dg_ref, o_ref, car_ref, dp_ref, dk_acc, dv_acc, gsum_ref, dq_ref):
        step_i = pl.program_id(1)
        i = nblk - 1 - step_i

        @pl.when(step_i == 0)
        def _():
            dk_acc[...] = jnp.zeros_like(dk_acc)
            dv_acc[...] = jnp.zeros_like(dv_acc)

        q = q_ref[...]
        z = z_ref[...].astype(F32)
        dg = dg_ref[...].astype(F32)
        sig = _sigmoid(z)
        d_o = (dg * (z * sig)).astype(BF16)
        car_all = car_ref[...]
        r_i = lax.broadcasted_iota(jnp.int32, (blk, blk), 0)
        c_i = lax.broadcasted_iota(jnp.int32, (blk, blk), 1)
        tri_after = (r_i > c_i).astype(BF16)
        tri_before = (r_i < c_i).astype(BF16)
        lanes = lax.broadcasted_iota(jnp.int32, (blk, HEAD_DIM), 1)

        gsum_ref[...] = jnp.zeros_like(gsum_ref)
        dq_ref[...] = jnp.zeros_like(dq_ref)

        def step(j, carry):
            k0 = pl.multiple_of(j * blk, blk)
            kj = k_ref[pl.ds(k0, blk), :]
            vj = v_ref[pl.ds(k0, blk), :]
            keep, log_beta, sp, mask = _sb_scores(q, kj, i * blk, k0, scale)
            tail = jnp.sum(jnp.where(lanes == j, car_all, 0.0), axis=1, keepdims=True)
            w = jnp.where(mask, jnp.exp(log_beta + tail + _split_dot(keep, tri_after)), 0.0)
            g = w * _dot(d_o, vj, NT)
            g_before = gsum_ref[...]
            g_cum = g_before[:, 0:1] + _split_dot(g, tri_before)
            dl = jnp.where(mask, g * jnp.exp(-sp) - g_cum * jnp.exp(log_beta), 0.0) * scale
            dl = dl.astype(BF16)
            dq_ref[...] += _dot(dl, kj, NN)
            dk_acc[pl.ds(k0, blk), :] += _dot(dl, q, TN)
            dv_acc[pl.ds(k0, blk), :] += _dot(w.astype(BF16), d_o, TN)
            gsum_ref[...] = g_before + jnp.sum(g, axis=1, keepdims=True)
            return carry

        visited = jnp.max(jnp.where(lanes == SB_COUNT_LANE, car_all, 0.0)).astype(jnp.int32)
        lax.fori_loop(i + 1 - visited, i + 1, step, 0)
        own = pl.ds(pl.multiple_of(i * blk, blk), blk)
        dp_ref[0] = dq_ref[...].astype(BF16)
        dp_ref[1] = dk_acc[own, :].astype(BF16)
        dp_ref[2] = dv_acc[own, :].astype(BF16)
        dp_ref[3] = (dg * o_ref[...].astype(F32) * (sig * (1.0 + z * (1.0 - sig)))).astype(BF16)

    qspec = lambda s: pl.BlockSpec((None, blk, HEAD_DIM), lambda h, i: (s, nblk - 1 - i, h))
    kspec = lambda s: pl.BlockSpec((None, t, HEAD_DIM), lambda h, i: (s, 0, h))
    tspec = pl.BlockSpec((blk, HEAD_DIM), lambda h, i: (nblk - 1 - i, h))
    return pl.pallas_call(
        body, name="sb_bwd", grid=(heads, nblk),
        in_specs=[qspec(0), kspec(1), kspec(2), qspec(3), tspec, tspec,
                  pl.BlockSpec((None, blk, HEAD_DIM), lambda h, i: (h, nblk - 1 - i, 0))],
        out_specs=pl.BlockSpec((N_CHIPS, blk, HEAD_DIM), lambda h, i: (0, nblk - 1 - i, h)),
        out_shape=jax.ShapeDtypeStruct((N_CHIPS, t, d), BF16),
        scratch_shapes=[pltpu.VMEM((t, HEAD_DIM), F32), pltpu.VMEM((t, HEAD_DIM), F32),
                        pltpu.VMEM((blk, HEAD_DIM), F32), pltpu.VMEM((blk, HEAD_DIM), F32)],
        compiler_params=_params(2),
    )(proj, proj, proj, proj, dgated, o, car)


def _pack_weights(w_in, w_out, chip):
    d = w_in.shape[0]
    rb = d // 8
    n_in = d // rb
    n_out = w_out.shape[0] // rb

    def body(chip_ref, wi_ref, wo_ref, o_ref):
        r = pl.program_id(0)

        @pl.when(r < n_in)
        def _():
            o_ref[...] = wi_ref[...].astype(BF16)

        @pl.when(r >= n_in)
        def _():
            o_ref[...] = wo_ref[...].astype(BF16)

    grid_spec = pltpu.PrefetchScalarGridSpec(
        num_scalar_prefetch=1, grid=(n_in + n_out,),
        in_specs=[pl.BlockSpec((rb, d), lambda r, me: (jnp.minimum(r, n_in - 1), 0)),
                  pl.BlockSpec((rb, d), lambda r, me: (jnp.maximum(r - n_in, 0), 0))],
        out_specs=pl.BlockSpec((None, rb, d), lambda r, me: (me[0], r, 0)))
    return pl.pallas_call(
        body, name="pack_weights", grid_spec=grid_spec,
        out_shape=jax.ShapeDtypeStruct((N_CHIPS, d + w_out.shape[0], d), BF16), compiler_params=_params(1),
    )(chip, w_in, w_out)


def _flip(v, bit):
    return 1 - v if bit else v


def _remote(src, dst, send_sem, recv_sem, target):
    return pltpu.make_async_remote_copy(src_ref=src, dst_ref=dst, send_sem=send_sem, recv_sem=recv_sem,
                                        device_id=target, device_id_type=MESH_IDS)


AG_CHUNKS = 4
HALF_CHUNKS = 8


SWAP_CHUNKS = 2


def _other_chips(x, y):
    return [(_flip(x, k >> 1), _flip(y, k & 1)) for k in (1, 2, 3)]


def _gather_sems(n):
    return [pltpu.SemaphoreType.DMA((3, n * AG_CHUNKS)) for _ in range(4)]


def _gather_pieces(g):
    hr = g[0].shape[1] // 2
    cr = hr // AG_CHUNKS
    return hr, [(l * AG_CHUNKS + q, g[l], q * cr, cr) for l in range(len(g)) for q in range(AG_CHUNKS)]


def _gather_start(g, send, recv):
    x, y, c = lax.axis_index("x"), lax.axis_index("y"), lax.axis_index("c")
    hr, pieces = _gather_pieces(g)
    for k, (px, py) in enumerate(_other_chips(x, y)):
        for i, ref, r0, cr in pieces:
            piece = ref.at[2 * x + y, pl.ds(c * hr + r0, cr)]
            _remote(piece, piece, send.at[k, i], recv.at[k, i], (px, py, c)).start()


def _gather_finish(g, send, recv, fsend, frecv):
    x, y, c = lax.axis_index("x"), lax.axis_index("y"), lax.axis_index("c")
    sibling = (x, y, 1 - c)
    hr, pieces = _gather_pieces(g)
    chips = _other_chips(x, y)
    for k, (px, py) in enumerate(chips):
        for i, ref, r0, cr in pieces:
            landed = ref.at[2 * px + py, pl.ds(c * hr + r0, cr)]
            _remote(landed, landed, send.at[k, i], recv.at[k, i], (px, py, c)).wait_recv()
            _remote(landed, landed, fsend.at[k, i], frecv.at[k, i], sibling).start()
    for k, (px, py) in enumerate(chips):
        for i, ref, r0, cr in pieces:
            theirs = ref.at[2 * px + py, pl.ds((1 - c) * hr + r0, cr)]
            _remote(theirs, theirs, fsend.at[k, i], frecv.at[k, i], sibling).wait_recv()
    for k, (px, py) in enumerate(chips):
        for i, ref, r0, cr in pieces:
            mine = ref.at[2 * x + y, pl.ds(c * hr + r0, cr)]
            _remote(mine, mine, send.at[k, i], recv.at[k, i], (px, py, c)).wait_send()
            landed = ref.at[2 * px + py, pl.ds(c * hr + r0, cr)]
            _remote(landed, landed, fsend.at[k, i], frecv.at[k, i], sibling).wait_send()


def _all_gather_weights(packs, cw):
    n = len(packs)

    def body(*refs):
        cw_ref = refs[n]
        g = refs[n + 1:2 * n + 1]
        cwg = refs[2 * n + 1]
        send, recv, fsend, frecv, csend, crecv, lsem = refs[2 * n + 2:]
        x, y, c = lax.axis_index("x"), lax.axis_index("y"), lax.axis_index("c")
        me = 2 * x + y
        local = pltpu.make_async_copy(cw_ref, cwg.at[me], lsem)
        local.start()
        taps = [_remote(cw_ref, cwg.at[me], csend.at[k], crecv.at[k], (px, py, c))
                for k, (px, py) in enumerate(_other_chips(x, y))]
        for cp in taps:
            cp.start()
        _gather_start(g, send, recv)
        _gather_finish(g, send, recv, fsend, frecv)
        for k, (px, py) in enumerate(_other_chips(x, y)):
            _remote(cw_ref, cwg.at[2 * px + py], csend.at[k], crecv.at[k], (px, py, c)).wait_recv()
        for cp in taps:
            cp.wait_send()
        local.wait()

    out_shape = [jax.ShapeDtypeStruct(p.shape, BF16) for p in packs]
    out_shape.append(jax.ShapeDtypeStruct((N_CHIPS,) + cw.shape, cw.dtype))
    return pl.pallas_call(
        body, name="all_gather_weights", in_specs=[HBM_SPEC] * (n + 1), out_specs=[HBM_SPEC] * (n + 1),
        out_shape=out_shape, input_output_aliases={l: l for l in range(n)},
        scratch_shapes=_gather_sems(n) + [pltpu.SemaphoreType.DMA((3,)), pltpu.SemaphoreType.DMA((3,)),
                                          pltpu.SemaphoreType.DMA],
    )(*packs, cw)


def _gather_rider(packs):
    return _Rider(
        operands=packs, out_shapes=[jax.ShapeDtypeStruct(p.shape, BF16) for p in packs],
        aliases={l: l for l in range(len(packs))}, sems=_gather_sems(len(packs)),
        start=lambda ins, outs, sems: _gather_start(outs, sems[0], sems[1]),
        finish=lambda ins, outs, sems: _gather_finish(outs, *sems))


def _exchange_last(sums, small):
    _, hr, d = sums.shape

    def body(s_ref, sm, got, smg, send, recv, ssend, srecv, lsem):
        x, y, c = lax.axis_index("x"), lax.axis_index("y"), lax.axis_index("c")
        me = 4 * x + 2 * y + c
        peers = [(_flip(x, r >> 2), _flip(y, (r >> 1) & 1), _flip(c, r & 1)) for r in range(1, N_DEV)]
        local = pltpu.make_async_copy(sm, smg.at[me], lsem)
        local.start()
        _send_sums_start(s_ref, got, send, recv)
        smalls = [_remote(sm, smg.at[me], ssend.at[r], srecv.at[r], peer) for r, peer in enumerate(peers)]
        for cp in smalls:
            cp.start()
        for r, (tx, ty, tc) in enumerate(peers):
            _remote(sm, smg.at[4 * tx + 2 * ty + tc], ssend.at[r], srecv.at[r], (tx, ty, tc)).wait_recv()
        _send_sums_finish(s_ref, got, send, recv)
        for cp in smalls:
            cp.wait_send()
        local.wait()

    return pl.pallas_call(
        body, name="exchange_last", in_specs=[HBM_SPEC] * 2, out_specs=[HBM_SPEC] * 2,
        out_shape=[jax.ShapeDtypeStruct((N_CHIPS - 1, hr, d), BF16),
                   jax.ShapeDtypeStruct((N_DEV,) + small.shape, small.dtype)],
        scratch_shapes=[pltpu.SemaphoreType.DMA((3,)), pltpu.SemaphoreType.DMA((3,)),
                        pltpu.SemaphoreType.DMA((N_DEV - 1,)), pltpu.SemaphoreType.DMA((N_DEV - 1,)),
                        pltpu.SemaphoreType.DMA],
    )(sums, small)


def _send_sums_start(s_ref, got, send, recv):
    x, y, c = lax.axis_index("x"), lax.axis_index("y"), lax.axis_index("c")
    for k, (px, py) in enumerate(_other_chips(x, y)):
        _remote(s_ref.at[2 * px + py], got.at[k], send.at[k], recv.at[k], (px, py, c)).start()


def _send_sums_finish(s_ref, got, send, recv):
    x, y, c = lax.axis_index("x"), lax.axis_index("y"), lax.axis_index("c")
    for k, (px, py) in enumerate(_other_chips(x, y)):
        _remote(got.at[k], got.at[k], send.at[k], recv.at[k], (px, py, c)).wait_recv()
    for k, (px, py) in enumerate(_other_chips(x, y)):
        _remote(s_ref.at[2 * px + py], got.at[k], send.at[k], recv.at[k], (px, py, c)).wait_send()


def _send_sums_rider(sums):
    _, hr, d = sums.shape
    return _Rider(
        operands=[sums], out_shapes=[jax.ShapeDtypeStruct((N_CHIPS - 1, hr, d), BF16)], aliases={},
        sems=[pltpu.SemaphoreType.DMA((3,)), pltpu.SemaphoreType.DMA((3,))],
        start=lambda ins, outs, sems: _send_sums_start(ins[0], outs[0], *sems),
        finish=lambda ins, outs, sems: _send_sums_finish(ins[0], outs[0], *sems))


def _swap_pieces(gp_ref, x_ref, c):
    hr = x_ref.shape[1]
    cr = hr // SWAP_CHUNKS
    return [(a * SWAP_CHUNKS + q, gp_ref.at[a, pl.ds((1 - c) * hr + q * cr, cr)], x_ref.at[a, pl.ds(q * cr, cr)])
            for a in range(N_CHIPS) for q in range(SWAP_CHUNKS)]


def _swap_rider(gp):
    _, p_rows, d = gp.shape

    def start(ins, outs, sems):
        x, y, c = lax.axis_index("x"), lax.axis_index("y"), lax.axis_index("c")
        for i, src, dst in _swap_pieces(ins[0], outs[0], c):
            _remote(src, dst, sems[0].at[i], sems[1].at[i], (x, y, 1 - c)).start()

    def finish(ins, outs, sems):
        x, y, c = lax.axis_index("x"), lax.axis_index("y"), lax.axis_index("c")
        pieces = _swap_pieces(ins[0], outs[0], c)
        for i, src, dst in pieces:
            _remote(dst, dst, sems[0].at[i], sems[1].at[i], (x, y, 1 - c)).wait_recv()
        for i, src, dst in pieces:
            _remote(src, dst, sems[0].at[i], sems[1].at[i], (x, y, 1 - c)).wait_send()

    nsem = N_CHIPS * SWAP_CHUNKS
    return _Rider(
        operands=[gp], out_shapes=[jax.ShapeDtypeStruct((N_CHIPS, p_rows // 2, d), BF16)], aliases={},
        sems=[pltpu.SemaphoreType.DMA((nsem,)), pltpu.SemaphoreType.DMA((nsem,))], start=start, finish=finish)


def _presum(gp, theirs, core):
    _, hr, d = theirs.shape
    tr = _row_tile(hr)
    steps = hr // tr

    def body(core_ref, mine_ref, theirs_ref, o_ref):
        o_ref[...] = (mine_ref[...].astype(F32) + theirs_ref[...].astype(F32)).astype(BF16)

    grid_spec = pltpu.PrefetchScalarGridSpec(
        num_scalar_prefetch=1, grid=(N_CHIPS, steps),
        in_specs=[pl.BlockSpec((None, tr, d), lambda a, i, cr: (a, cr[0] * steps + i, 0)),
                  pl.BlockSpec((None, tr, d), lambda a, i, cr: (a, i, 0))],
        out_specs=pl.BlockSpec((None, tr, d), lambda a, i, cr: (a, i, 0)))
    return pl.pallas_call(
        body, name="presum", grid_spec=grid_spec,
        out_shape=jax.ShapeDtypeStruct((N_CHIPS, hr, d), BF16), compiler_params=_params(2),
    )(core, gp, theirs)


def _row_tile(rows):
    return next(tr for tr in (128, 64, 32, 16, 8) if rows % tr == 0) if rows > 128 else rows


def _sum_sources(name, parts):
    nsrc, rows, cols = parts.shape
    tr = _row_tile(rows)

    def body(p_ref, o_ref):
        total = p_ref[0].astype(F32)
        for s in range(1, nsrc):
            total = total + p_ref[s].astype(F32)
        o_ref[...] = total

    return pl.pallas_call(
        body, name=name, grid=(rows // tr,),
        in_specs=[pl.BlockSpec((nsrc, tr, cols), lambda i: (0, i, 0))],
        out_specs=pl.BlockSpec((tr, cols), lambda i: (i, 0)),
        out_shape=jax.ShapeDtypeStruct((rows, cols), F32), compiler_params=_params(1),
    )(parts)


def _sum_grad_half(got, sums, place):
    nsrc, hr, d = got.shape
    tr = _row_tile(hr)

    def body(place_ref, got_ref, own_ref, o_ref):
        total = own_ref[...].astype(F32)
        for s in range(nsrc):
            total = total + got_ref[s].astype(F32)
        o_ref[...] = total

    grid_spec = pltpu.PrefetchScalarGridSpec(
        num_scalar_prefetch=1, grid=(hr // tr,),
        in_specs=[pl.BlockSpec((nsrc, tr, d), lambda i, pc: (0, i, 0)),
                  pl.BlockSpec((None, tr, d), lambda i, pc: (pc[0], i, 0))],
        out_specs=pl.BlockSpec((None, tr, d), lambda i, pc: (pc[1], i, 0)))
    return pl.pallas_call(
        body, name="sum_grad_half", grid_spec=grid_spec,
        out_shape=jax.ShapeDtypeStruct((2, hr, d), F32), compiler_params=_params(1),
    )(place, got, sums)


def _exchange_halves(fulls):
    n = len(fulls)
    _, hr, d = fulls[0].shape
    cr = hr // HALF_CHUNKS
    nc = n * HALF_CHUNKS

    def body(*refs):
        full = refs[n:2 * n]
        send, recv = refs[2 * n:]
        x, y, c = lax.axis_index("x"), lax.axis_index("y"), lax.axis_index("c")
        sibling = (x, y, 1 - c)
        pieces = [(l, q) for l in range(n) for q in range(HALF_CHUNKS)]
        sends = []
        for i, (l, q) in enumerate(pieces):
            piece = full[l].at[c, pl.ds(q * cr, cr)]
            sends.append(_remote(piece, piece, send.at[i], recv.at[i], sibling))
        for cp in sends:
            cp.start()
        for i, (l, q) in enumerate(pieces):
            theirs = full[l].at[1 - c, pl.ds(q * cr, cr)]
            _remote(theirs, theirs, send.at[i], recv.at[i], sibling).wait_recv()
        for cp in sends:
            cp.wait_send()

    return pl.pallas_call(
        body, name="exchange_halves", in_specs=[HBM_SPEC] * n, out_specs=[HBM_SPEC] * n,
        out_shape=[jax.ShapeDtypeStruct(f.shape, F32) for f in fulls], input_output_aliases={l: l for l in range(n)},
        scratch_shapes=[pltpu.SemaphoreType.DMA((nc,)), pltpu.SemaphoreType.DMA((nc,))],
    )(*fulls)


def _adamw(name, w, m, v, g, g_row0=0):
    rows, cols = w.shape
    tr = _row_tile(rows)
    off = g_row0 // tr

    def body(w_ref, m_ref, v_ref, g_ref, go_ref, d_ref, mo_ref, vo_ref):
        grad = g_ref[...]
        m_new = ADAM_B1 * m_ref[...] + (1.0 - ADAM_B1) * grad
        v_new = ADAM_B2 * v_ref[...] + (1.0 - ADAM_B2) * (grad * grad)
        m_hat = m_new / (1.0 - ADAM_B1 ** ADAM_STEP)
        v_hat = v_new / (1.0 - ADAM_B2 ** ADAM_STEP)
        go_ref[...] = grad
        d_ref[...] = -ADAM_LR * (m_hat / (jnp.sqrt(v_hat) + ADAM_EPS) + ADAM_WD * w_ref[...])
        mo_ref[...] = m_new
        vo_ref[...] = v_new

    blk = pl.BlockSpec((tr, cols), lambda i: (i, 0))
    return pl.pallas_call(
        body, name=name, grid=(rows // tr,),
        in_specs=[blk, blk, blk, pl.BlockSpec((tr, cols), lambda i: (i + off, 0))],
        out_specs=[blk, blk, blk, blk],
        out_shape=[jax.ShapeDtypeStruct((rows, cols), F32)] * 4, compiler_params=_params(1),
    )(w, m, v, g)


def _pad_rows8(a):
    return jnp.concatenate([a, jnp.zeros((8 - a.shape[0],) + a.shape[1:], a.dtype)], axis=0)


def kernel(x, ln_pre_0, conv_w_in_0, conv_w_0, conv_w_out_0, ln_post_0, ln_pre_1, sb_w_in_1, sb_w_out_1, ln_post_1, ln_pre_2, conv_w_in_2, conv_w_2, conv_w_out_2, ln_post_2, ln_pre_3, sb_w_in_3, sb_w_out_3, ln_post_3, loss_target, m_ln_pre_0, m_conv_w_in_0, m_conv_w_0, m_conv_w_out_0, m_ln_post_0, m_ln_pre_1, m_sb_w_in_1, m_sb_w_out_1, m_ln_post_1, m_ln_pre_2, m_conv_w_in_2, m_conv_w_2, m_conv_w_out_2, m_ln_post_2, m_ln_pre_3, m_sb_w_in_3, m_sb_w_out_3, m_ln_post_3, v_ln_pre_0, v_conv_w_in_0, v_conv_w_0, v_conv_w_out_0, v_ln_post_0, v_ln_pre_1, v_sb_w_in_1, v_sb_w_out_1, v_ln_post_1, v_ln_pre_2, v_conv_w_in_2, v_conv_w_2, v_conv_w_out_2, v_ln_post_2, v_ln_pre_3, v_sb_w_in_3, v_sb_w_out_3, v_ln_post_3):
    t, d = x.shape[1], x.shape[2]
    dq = d // N_CHIPS
    xs = x.reshape(t, d)
    target = loss_target.reshape(t, d)
    w_in = [conv_w_in_0, sb_w_in_1, conv_w_in_2, sb_w_in_3]
    w_out = [conv_w_out_0, sb_w_out_1, conv_w_out_2, sb_w_out_3]
    m_in = [m_conv_w_in_0, m_sb_w_in_1, m_conv_w_in_2, m_sb_w_in_3]
    m_out = [m_conv_w_out_0, m_sb_w_out_1, m_conv_w_out_2, m_sb_w_out_3]
    v_in = [v_conv_w_in_0, v_sb_w_in_1, v_conv_w_in_2, v_sb_w_in_3]
    v_out = [v_conv_w_out_0, v_sb_w_out_1, v_conv_w_out_2, v_sb_w_out_3]
    ln_pre = [ln_pre_0, ln_pre_1, ln_pre_2, ln_pre_3]
    ln_post = [ln_post_0, ln_post_1, ln_post_2, ln_post_3]
    conv_w = [conv_w_0, conv_w_2]
    m_conv = [m_conv_w_0, m_conv_w_2]
    v_conv = [v_conv_w_0, v_conv_w_2]
    chip = 2 * lax.axis_index("x") + lax.axis_index("y")
    chip_arr = jnp.reshape(chip, (1,)).astype(jnp.int32)
    place = jnp.stack([chip, lax.axis_index("c")]).astype(jnp.int32)
    core_arr = jnp.reshape(lax.axis_index("c"), (1,)).astype(jnp.int32)

    packs = [_pack_weights(w_in[l], w_out[l], chip_arr) for l in range(N_LAYERS)]
    cw_local = jnp.concatenate([_pad_rows8(conv_w[0]), _pad_rows8(conv_w[1])], axis=0)
    gathered = list(packs)
    gathered[0], cw_all = _all_gather_weights(packs[:1], cw_local)
    cw_full = jnp.transpose(cw_all, (1, 0, 2)).reshape(16, d)
    conv_taps = {0: cw_full[0:8], 2: cw_full[8:16]}

    h_in, us, projs, gateds, ms, sb_saved = [], [], [], [], [], {}
    h = xs
    u = _norm_first(xs, ln_pre[0])
    for l in range(N_LAYERS):
        h_in.append(h)
        us.append(u)
        if l == 0:
            proj, gathered[1] = _mm_proj(u, gathered[0], rider=_gather_rider(gathered[1:2]))
        else:
            proj, = _mm_proj(u, gathered[l])
        if l % 2 == 0:
            gated = _conv_fwd(proj, conv_taps[l])
        elif l == 1:
            gated, o, car, gathered[2], gathered[3] = _sb_fwd(proj, rider=_gather_rider(gathered[2:4]))
            sb_saved[l] = (o, car)
        else:
            gated, o, car = _sb_fwd(proj)
            sb_saved[l] = (o, car)
        m = _mm_out(gated, gathered[l])
        projs.append(proj)
        gateds.append(gated)
        ms.append(m)
        if l < N_LAYERS - 1:
            h, u = _norm_mid(h, m, ln_post[l], ln_pre[l + 1])
    dh, dm, dg_post_last, loss_part = _norm_last(h, ms[-1], ln_post[-1], target)
    loss = lax.psum(loss_part[0, 0], ("x", "y", "c"))

    dg_pre = [None] * N_LAYERS
    dg_post = [None] * N_LAYERS
    dg_post[N_LAYERS - 1] = dg_post_last
    dconv = {}
    sums = [None] * N_LAYERS
    got = [None] * N_LAYERS
    grad_x = None
    for l in reversed(range(N_LAYERS)):
        dgated = _mm_dgated(dm, gathered[l])
        gp = _mm_dwout(gateds[l], dm)
        if l % 2 == 0:
            dproj, dconv[l] = _conv_bwd(projs[l], dgated, conv_taps[l])
        else:
            o, car = sb_saved[l]
            dproj = _sb_bwd(projs[l], dgated, o, car)
        if l < N_LAYERS - 1:
            gp, got[l + 1] = _mm_dwin(us[l], dproj, gp, rider=_send_sums_rider(sums[l + 1]))
        else:
            gp, = _mm_dwin(us[l], dproj, gp)
        du, theirs = _mm_du(dproj, gathered[l], rider=_swap_rider(gp))
        sums[l] = _presum(gp, theirs, core_arr)
        if l > 0:
            dh, dm, dg_pre[l], dg_post[l - 1] = _norm_bwd_mid(dh, du, h_in[l], ln_pre[l], ms[l - 1], ln_post[l - 1])
        else:
            grad_x, dg_pre[0] = _norm_bwd_first(dh, du, h_in[0], ln_pre[0])

    small = jnp.concatenate(dg_pre + dg_post + [dconv[0], dconv[2]], axis=0)
    got[0], small_all = _exchange_last(sums[0], small)
    halves = [_sum_grad_half(got[l], sums[l], place) for l in range(N_LAYERS)]
    small_sum = _sum_sources("sum_small", small_all)
    fulls = [f.reshape(d + dq, d) for f in _exchange_halves(halves)]

    res_in = [_adamw("adamw_w_in", w_in[l], m_in[l], v_in[l], fulls[l], 0) for l in range(N_LAYERS)]
    res_out = [_adamw("adamw_w_out", w_out[l], m_out[l], v_out[l], fulls[l], d) for l in range(N_LAYERS)]
    ln_all = ln_pre + ln_post
    ln_m = [m_ln_pre_0, m_ln_pre_1, m_ln_pre_2, m_ln_pre_3, m_ln_post_0, m_ln_post_1, m_ln_post_2, m_ln_post_3]
    ln_v = [v_ln_pre_0, v_ln_pre_1, v_ln_pre_2, v_ln_pre_3, v_ln_post_0, v_ln_post_1, v_ln_post_2, v_ln_post_3]
    ln_g = jnp.concatenate([small_sum[8 * i:8 * i + 1] for i in range(2 * N_LAYERS)], axis=0)
    res_ln = _adamw("adamw_ln", jnp.stack(ln_all), jnp.stack(ln_m), jnp.stack(ln_v), ln_g)
    conv_g = []
    for i in range(2):
        rows = small_sum[8 * (2 * N_LAYERS + i):8 * (2 * N_LAYERS + i + 1)]
        conv_g.append(lax.dynamic_slice(rows, (0, chip * dq), (8, dq)))
    stack8 = lambda a, b: jnp.concatenate([_pad_rows8(a), _pad_rows8(b)], axis=0)
    res_conv = _adamw("adamw_conv", stack8(*conv_w), stack8(*m_conv), stack8(*v_conv), jnp.concatenate(conv_g, axis=0))

    def leaf(kind, l, which):
        if kind == "ln_pre":
            return res_ln[which][l]
        if kind == "ln_post":
            return res_ln[which][N_LAYERS + l]
        if kind == "w_in":
            return res_in[l][which]
        if kind == "w_out":
            return res_out[l][which]
        return res_conv[which][8 * (l // 2):8 * (l // 2) + 3]

    order = []
    for l in range(N_LAYERS):
        order.append(("ln_pre", l))
        order.append(("w_in", l))
        if l % 2 == 0:
            order.append(("conv", l))
        order.append(("w_out", l))
        order.append(("ln_post", l))
    outs = [loss, grad_x.reshape(1, t, d)]
    for which in range(4):
        outs.extend(leaf(kind, l, which) for kind, l in order)
    return tuple(outs)
```

```python
import functools
import math
from typing import Any, Callable, Mapping, NamedTuple, Sequence

import jax
import jax.numpy as jnp
from jax import lax
from jax.experimental import pallas as pl
from jax.experimental.pallas import tpu as pltpu

F32 = jnp.float32
BF16 = jnp.bfloat16

N_CHIPS = 4
N_DEV = 8
N_LAYERS = 4
HEAD_DIM = 128
RMS_EPS = 1e-6
ADAM_LR = 0.001
ADAM_B1 = 0.9
ADAM_B2 = 0.999
ADAM_EPS = 1e-08
ADAM_WD = 0.01
ADAM_STEP = 10

VMEM_LIMIT = 56 * 1024 * 1024
MESH_IDS = pl.DeviceIdType.MESH
HBM_SPEC = pl.BlockSpec(memory_space=pltpu.HBM)

NN = (((1,), (0,)), ((), ()))
NT = (((1,), (1,)), ((), ()))
TN = (((0,), (0,)), ((), ()))


def _params(n_axes):
    return pltpu.CompilerParams(dimension_semantics=("arbitrary",) * n_axes, vmem_limit_bytes=VMEM_LIMIT)


def _dot(a, b, dims):
    return lax.dot_general(a, b, dims, preferred_element_type=F32)


def _sigmoid(z):
    return 1.0 / (1.0 + jnp.exp(-z))


class _Rider(NamedTuple):
    operands: Sequence[Any]
    out_shapes: Sequence[Any]
    aliases: Mapping[int, int]
    sems: Sequence[Any]
    start: Callable
    finish: Callable


def _compute_call(body, name, *, grid, in_specs, operands, out_specs, out_shape, scratch=(), aliases=None, rider=None):
    in_specs, operands = list(in_specs), list(operands)
    out_specs, out_shape, scratch = list(out_specs), list(out_shape), list(scratch)
    aliases = dict(aliases or {})
    n_in, n_out, n_scratch = len(operands), len(out_shape), len(scratch)
    hosted = body
    if rider is not None:
        r_in, r_out = len(rider.operands), len(rider.out_shapes)
        aliases.update({n_in + i: n_out + o for i, o in rider.aliases.items()})

        def hosted(*refs):
            ins, refs = refs[:n_in], refs[n_in:]
            rider_ins, refs = refs[:r_in], refs[r_in:]
            outs, refs = refs[:n_out], refs[n_out:]
            rider_outs, refs = refs[:r_out], refs[r_out:]
            own_scratch, rider_sems = refs[:n_scratch], refs[n_scratch:]
            ids = [pl.program_id(axis) for axis in range(len(grid))]
            first = functools.reduce(jnp.logical_and, [i == 0 for i in ids])
            last = functools.reduce(jnp.logical_and, [i == g - 1 for i, g in zip(ids, grid)])

            @pl.when(first)
            def _():
                rider.start(rider_ins, rider_outs, rider_sems)

            body(*ins, *outs, *own_scratch)

            @pl.when(last)
            def _():
                rider.finish(rider_ins, rider_outs, rider_sems)

        in_specs += [HBM_SPEC] * r_in
        operands += list(rider.operands)
        out_specs += [HBM_SPEC] * r_out
        out_shape += list(rider.out_shapes)
        scratch += list(rider.sems)
    return pl.pallas_call(
        hosted, name=name, grid=grid, in_specs=in_specs, out_specs=out_specs, out_shape=out_shape,
        scratch_shapes=scratch, input_output_aliases=aliases, compiler_params=_params(len(grid)),
    )(*operands)


def _matmul(name, a, b, *, grid, a_spec, b_spec, o_spec, out_shape, dims, reduce_axis=None, acc_shape=None,
            alias_out=None, rider=None):
    out_dtype = out_shape.dtype
    direct = reduce_axis is not None and out_dtype == F32
    n_red = grid[reduce_axis] if reduce_axis is not None else 1

    def body(*refs):
        if alias_out is not None:
            refs = refs[1:]
        a_ref, b_ref, o_ref = refs[:3]
        if reduce_axis is None:
            o_ref[...] = _dot(a_ref[...], b_ref[...], dims).astype(out_dtype)
            return
        acc_ref = o_ref if direct else refs[3]
        k = pl.program_id(reduce_axis)

        @pl.when(k == 0)
        def _():
            acc_ref[...] = jnp.zeros_like(acc_ref)

        acc_ref[...] += _dot(a_ref[...], b_ref[...], dims)

        if not direct:
            @pl.when(k == n_red - 1)
            def _():
                o_ref[...] = acc_ref[...].astype(out_dtype)

    scratch = []
    if reduce_axis is not None and not direct:
        scratch = [pltpu.VMEM(acc_shape, F32)]
    in_specs = [a_spec, b_spec]
    operands = [a, b]
    aliases = {}
    if alias_out is not None:
        in_specs = [HBM_SPEC] + in_specs
        operands = [alias_out] + operands
        aliases = {0: 0}
    return _compute_call(body, name, grid=grid, in_specs=in_specs, operands=operands, out_specs=[o_spec],
                         out_shape=[out_shape], scratch=scratch, aliases=aliases, rider=rider)


def _mm_proj(u, g, rider=None):
    t, d = u.shape
    tm = min(512, t)
    return _matmul(
        "mm_proj", u, g, grid=(N_CHIPS, t // tm),
        a_spec=pl.BlockSpec((tm, d), lambda s, m: (m, 0)),
        b_spec=pl.BlockSpec((None, d, d), lambda s, m: (s, 0, 0)),
        o_spec=pl.BlockSpec((None, tm, d), lambda s, m: (s, m, 0)),
        out_shape=jax.ShapeDtypeStruct((N_CHIPS, t, d), BF16), dims=NN, rider=rider)


def _mm_out(gated, g):
    t, d = gated.shape
    dq = d // N_CHIPS
    tm = min(512, t)

    def body(a_ref, b_ref, o_ref):
        acc = _dot(a_ref[:, 0:dq], b_ref[0], NN)
        for s in range(1, N_CHIPS):
            acc = acc + _dot(a_ref[:, s * dq:(s + 1) * dq], b_ref[s], NN)
        o_ref[...] = acc

    return _compute_call(
        body, "mm_out", grid=(t // tm,),
        in_specs=[pl.BlockSpec((tm, d), lambda m: (m, 0)), pl.BlockSpec((N_CHIPS, dq, d), lambda m: (0, N_CHIPS, 0))],
        operands=[gated, g], out_specs=[pl.BlockSpec((tm, d), lambda m: (m, 0))],
        out_shape=[jax.ShapeDtypeStruct((t, d), F32)])[0]


def _mm_dgated(dm, g):
    t, d = dm.shape
    dq = d // N_CHIPS
    tm = min(512, t)

    def body(a_ref, b_ref, o_ref):
        a = a_ref[...]
        for s in range(N_CHIPS):
            o_ref[:, s * dq:(s + 1) * dq] = _dot(a, b_ref[s], NT).astype(BF16)

    return _compute_call(
        body, "mm_dgated", grid=(t // tm,),
        in_specs=[pl.BlockSpec((tm, d), lambda m: (m, 0)), pl.BlockSpec((N_CHIPS, dq, d), lambda m: (0, N_CHIPS, 0))],
        operands=[dm, g], out_specs=[pl.BlockSpec((tm, d), lambda m: (m, 0))],
        out_shape=[jax.ShapeDtypeStruct((t, d), BF16)])[0]


def _mm_dwout(gated, dm):
    t, d = gated.shape
    dq = d // N_CHIPS
    tk = min(1024, t)
    return _matmul(
        "mm_dwout", gated, dm, grid=(N_CHIPS, t // tk),
        a_spec=pl.BlockSpec((tk, dq), lambda s, k: (k, s)),
        b_spec=pl.BlockSpec((tk, d), lambda s, k: (k, 0)),
        o_spec=pl.BlockSpec((None, dq, d), lambda s, k: (s, N_CHIPS, 0)),
        out_shape=jax.ShapeDtypeStruct((N_CHIPS, d + dq, d), BF16), dims=TN, reduce_axis=1, acc_shape=(dq, d))[0]


def _mm_du(dproj, g, rider=None):
    _, t, d = dproj.shape
    tm = min(512, t)
    return _matmul(
        "mm_du", dproj, g, grid=(t // tm, N_CHIPS),
        a_spec=pl.BlockSpec((None, tm, d), lambda m, s: (s, m, 0)),
        b_spec=pl.BlockSpec((None, d, d), lambda m, s: (s, 0, 0)),
        o_spec=pl.BlockSpec((tm, d), lambda m, s: (m, 0)),
        out_shape=jax.ShapeDtypeStruct((t, d), F32), dims=NT, reduce_axis=1, rider=rider)


def _mm_dwin(u, dproj, gp, rider=None):
    t, d = u.shape
    tmo = min(1024, d)
    tk = min(1024, t)
    return _matmul(
        "mm_dwin", u, dproj, grid=(N_CHIPS, d // tmo, t // tk),
        a_spec=pl.BlockSpec((tk, tmo), lambda s, mo, k: (k, mo)),
        b_spec=pl.BlockSpec((None, tk, d), lambda s, mo, k: (s, k, 0)),
        o_spec=pl.BlockSpec((None, tmo, d), lambda s, mo, k: (s, mo, 0)),
        out_shape=jax.ShapeDtypeStruct(gp.shape, BF16), dims=TN, reduce_axis=2, acc_shape=(tmo, d), alias_out=gp,
        rider=rider)


def _rms(v):
    r = lax.rsqrt(jnp.mean(v * v, axis=-1, keepdims=True) + RMS_EPS)
    return v * r, r


def _rms_bwd(dout, n, r, gain):
    dn = dout * gain
    return r * (dn - n * jnp.mean(dn * n, axis=-1, keepdims=True))


def _fold8(v):
    return jnp.sum(v.reshape(v.shape[0] // 8, 8, v.shape[1]), axis=0)


def _row0(total):
    rows = lax.broadcasted_iota(jnp.int32, total.shape, 0)
    return jnp.where(rows == 0, jnp.sum(total, axis=0, keepdims=True), 0.0)


def _norm_tile(t):
    return min(256, t)


def _norm_first(x, g_pre):
    t, d = x.shape
    tr = _norm_tile(t)

    def body(x_ref, g_ref, u_ref):
        n, _ = _rms(x_ref[...])
        u_ref[...] = (n * g_ref[...]).astype(BF16)

    row = pl.BlockSpec((tr, d), lambda i: (i, 0))
    vec = pl.BlockSpec((1, d), lambda i: (0, 0))
    return pl.pallas_call(
        body, name="norm_first", grid=(t // tr,), in_specs=[row, vec], out_specs=row,
        out_shape=jax.ShapeDtypeStruct((t, d), BF16), compiler_params=_params(1),
    )(x, g_pre.reshape(1, d))


def _norm_mid(h, m, g_post, g_pre_next):
    t, d = h.shape
    tr = _norm_tile(t)

    def body(h_ref, m_ref, gp_ref, gn_ref, hn_ref, u_ref):
        n, _ = _rms(m_ref[...])
        hn = h_ref[...] + n * gp_ref[...]
        hn_ref[...] = hn
        n2, _ = _rms(hn)
        u_ref[...] = (n2 * gn_ref[...]).astype(BF16)

    row = pl.BlockSpec((tr, d), lambda i: (i, 0))
    vec = pl.BlockSpec((1, d), lambda i: (0, 0))
    return pl.pallas_call(
        body, name="norm_mid", grid=(t // tr,), in_specs=[row, row, vec, vec], out_specs=[row, row],
        out_shape=[jax.ShapeDtypeStruct((t, d), F32), jax.ShapeDtypeStruct((t, d), BF16)],
        compiler_params=_params(1),
    )(h, m, g_post.reshape(1, d), g_pre_next.reshape(1, d))


def _norm_last(h, m, g_post, target):
    t, d = h.shape
    tr = _norm_tile(t)
    nsteps = t // tr

    def body(h_ref, m_ref, gp_ref, tg_ref, dy_ref, dm_ref, dgp_ref, loss_ref, acc_g, acc_l):
        i = pl.program_id(0)

        @pl.when(i == 0)
        def _():
            acc_g[...] = jnp.zeros_like(acc_g)
            acc_l[...] = jnp.zeros_like(acc_l)

        gain = gp_ref[...]
        n, r = _rms(m_ref[...])
        err = h_ref[...] + n * gain - tg_ref[...]
        dy = err / d
        dy_ref[...] = dy
        dm_ref[...] = _rms_bwd(dy, n, r, gain).astype(BF16)
        acc_g[...] += _fold8(dy * n)
        acc_l[...] += _fold8(err * err)

        @pl.when(i == nsteps - 1)
        def _():
            dgp_ref[...] = _row0(acc_g[...])
            loss_ref[...] = jnp.zeros((8, 128), F32) + (0.5 / d) * jnp.sum(acc_l[...])

    row = pl.BlockSpec((tr, d), lambda i: (i, 0))
    vec = pl.BlockSpec((1, d), lambda i: (0, 0))
    acc = pl.BlockSpec((8, d), lambda i: (0, 0))
    return pl.pallas_call(
        body, name="norm_last", grid=(nsteps,), in_specs=[row, row, vec, row],
        out_specs=[row, row, acc, pl.BlockSpec((8, 128), lambda i: (0, 0))],
        out_shape=[jax.ShapeDtypeStruct((t, d), F32), jax.ShapeDtypeStruct((t, d), BF16),
                   jax.ShapeDtypeStruct((8, d), F32), jax.ShapeDtypeStruct((8, 128), F32)],
        scratch_shapes=[pltpu.VMEM((8, d), F32), pltpu.VMEM((8, d), F32)],
        compiler_params=_params(1),
    )(h, m, g_post.reshape(1, d), target)


def _norm_bwd_mid(dh, du, h_in, g_pre, m_prev, g_post_prev):
    t, d = dh.shape
    tr = _norm_tile(t)
    nsteps = t // tr

    def body(dh_ref, du_ref, h_ref, gpre_ref, m_ref, gpost_ref, dhn_ref, dm_ref, dgpre_ref, dgpost_ref, acc_a, acc_b):
        i = pl.program_id(0)

        @pl.when(i == 0)
        def _():
            acc_a[...] = jnp.zeros_like(acc_a)
            acc_b[...] = jnp.zeros_like(acc_b)

        du_t = du_ref[...]
        n, r = _rms(h_ref[...])
        dhn = dh_ref[...] + _rms_bwd(du_t, n, r, gpre_ref[...])
        dhn_ref[...] = dhn
        acc_a[...] += _fold8(du_t * n)
        n2, r2 = _rms(m_ref[...])
        dm_ref[...] = _rms_bwd(dhn, n2, r2, gpost_ref[...]).astype(BF16)
        acc_b[...] += _fold8(dhn * n2)

        @pl.when(i == nsteps - 1)
        def _():
            dgpre_ref[...] = _row0(acc_a[...])
            dgpost_ref[...] = _row0(acc_b[...])

    row = pl.BlockSpec((tr, d), lambda i: (i, 0))
    vec = pl.BlockSpec((1, d), lambda i: (0, 0))
    acc = pl.BlockSpec((8, d), lambda i: (0, 0))
    return pl.pallas_call(
        body, name="norm_bwd_mid", grid=(nsteps,), in_specs=[row, row, row, vec, row, vec],
        out_specs=[row, row, acc, acc],
        out_shape=[jax.ShapeDtypeStruct((t, d), F32), jax.ShapeDtypeStruct((t, d), BF16),
                   jax.ShapeDtypeStruct((8, d), F32), jax.ShapeDtypeStruct((8, d), F32)],
        scratch_shapes=[pltpu.VMEM((8, d), F32), pltpu.VMEM((8, d), F32)],
        compiler_params=_params(1),
    )(dh, du, h_in, g_pre.reshape(1, d), m_prev, g_post_prev.reshape(1, d))


def _norm_bwd_first(dh, du, x, g_pre):
    t, d = dh.shape
    tr = _norm_tile(t)
    nsteps = t // tr

    def body(dh_ref, du_ref, x_ref, gpre_ref, dx_ref, dgpre_ref, acc_a):
        i = pl.program_id(0)

        @pl.when(i == 0)
        def _():
            acc_a[...] = jnp.zeros_like(acc_a)

        du_t = du_ref[...]
        n, r = _rms(x_ref[...])
        dx_ref[...] = dh_ref[...] + _rms_bwd(du_t, n, r, gpre_ref[...])
        acc_a[...] += _fold8(du_t * n)

        @pl.when(i == nsteps - 1)
        def _():
            dgpre_ref[...] = _row0(acc_a[...])

    row = pl.BlockSpec((tr, d), lambda i: (i, 0))
    vec = pl.BlockSpec((1, d), lambda i: (0, 0))
    acc = pl.BlockSpec((8, d), lambda i: (0, 0))
    return pl.pallas_call(
        body, name="norm_bwd_first", grid=(nsteps,), in_specs=[row, row, row, vec], out_specs=[row, acc],
        out_shape=[jax.ShapeDtypeStruct((t, d), F32), jax.ShapeDtypeStruct((8, d), F32)],
        scratch_shapes=[pltpu.VMEM((8, d), F32)], compiler_params=_params(1),
    )(dh, du, x, g_pre.reshape(1, d))


CONV_TC = 128
CONV_HALO = 16


def _conv_chunk(t):
    return min(512, t)


def _shift_down(v, steps, fill):
    rows = lax.broadcasted_iota(jnp.int32, v.shape, 0)
    out = pltpu.roll(v, steps, axis=0)
    for k in range(steps):
        out = jnp.where(rows == k, fill[CONV_HALO - steps + k:CONV_HALO - steps + k + 1, :], out)
    return out


def _shift_up(v, steps, fill):
    nrows = v.shape[0]
    rows = lax.broadcasted_iota(jnp.int32, v.shape, 0)
    out = pltpu.roll(v, nrows - steps, axis=0)
    for k in range(steps):
        out = jnp.where(rows == nrows - steps + k, fill[k:k + 1, :], out)
    return out


def _conv_fwd(proj, cw):
    _, t, d = proj.shape
    chunk = _conv_chunk(t)

    def body(p_ref, w_ref, o_ref):
        w = w_ref[...]
        w0, w1, w2 = w[0:1, :], w[1:2, :], w[2:3, :]
        for ci in range(t // chunk):
            t0 = ci * chunk
            rows = pl.ds(t0, chunk)
            b = p_ref[0, rows, :].astype(F32)
            cx = p_ref[1, rows, :].astype(F32) * p_ref[2, rows, :].astype(F32)
            z = p_ref[3, rows, :].astype(F32)
            if ci == 0:
                prev = jnp.zeros((CONV_HALO, CONV_TC), F32)
            else:
                halo = pl.ds(t0 - CONV_HALO, CONV_HALO)
                prev = p_ref[1, halo, :].astype(F32) * p_ref[2, halo, :].astype(F32)
            conv = w2 * cx + w1 * _shift_down(cx, 1, prev) + w0 * _shift_down(cx, 2, prev)
            o_ref[rows, :] = (z * _sigmoid(z) * b * conv).astype(BF16)

    return pl.pallas_call(
        body, name="conv_fwd", grid=(d // CONV_TC,),
        in_specs=[pl.BlockSpec((N_CHIPS, t, CONV_TC), lambda j: (0, 0, j)), pl.BlockSpec((8, CONV_TC), lambda j: (0, j))],
        out_specs=pl.BlockSpec((t, CONV_TC), lambda j: (0, j)),
        out_shape=jax.ShapeDtypeStruct((t, d), BF16), compiler_params=_params(1),
    )(proj, cw)


def _conv_bwd(proj, dgated, cw):
    _, t, d = proj.shape
    chunk = _conv_chunk(t)
    nchunks = t // chunk

    def body(p_ref, dg_ref, w_ref, dp_ref, dw_ref):
        w = w_ref[...]
        w0, w1, w2 = w[0:1, :], w[1:2, :], w[2:3, :]
        dw0 = jnp.zeros((1, CONV_TC), F32)
        dw1 = jnp.zeros((1, CONV_TC), F32)
        dw2 = jnp.zeros((1, CONV_TC), F32)
        for ci in range(nchunks):
            t0 = ci * chunk
            rows = pl.ds(t0, chunk)
            b = p_ref[0, rows, :].astype(F32)
            c = p_ref[1, rows, :].astype(F32)
            xt = p_ref[2, rows, :].astype(F32)
            z = p_ref[3, rows, :].astype(F32)
            dg = dg_ref[rows, :].astype(F32)
            cx = c * xt
            if ci == 0:
                prev = jnp.zeros((CONV_HALO, CONV_TC), F32)
            else:
                halo = pl.ds(t0 - CONV_HALO, CONV_HALO)
                prev = p_ref[1, halo, :].astype(F32) * p_ref[2, halo, :].astype(F32)
            cx1 = _shift_down(cx, 1, prev)
            cx2 = _shift_down(cx, 2, prev)
            conv = w2 * cx + w1 * cx1 + w0 * cx2
            sig = _sigmoid(z)
            dy = dg * (z * sig)
            dconv = dy * b
            if ci == nchunks - 1:
                nxt = jnp.zeros((CONV_HALO, CONV_TC), F32)
            else:
                halo = pl.ds(t0 + chunk, CONV_HALO)
                zn = p_ref[3, halo, :].astype(F32)
                nxt = dg_ref[halo, :].astype(F32) * (zn * _sigmoid(zn)) * p_ref[0, halo, :].astype(F32)
            dcx = w2 * dconv + w1 * _shift_up(dconv, 1, nxt) + w0 * _shift_up(dconv, 2, nxt)
            dp_ref[0, rows, :] = (dy * conv).astype(BF16)
            dp_ref[1, rows, :] = (dcx * xt).astype(BF16)
            dp_ref[2, rows, :] = (dcx * c).astype(BF16)
            dp_ref[3, rows, :] = (dg * (b * conv) * (sig * (1.0 + z * (1.0 - sig)))).astype(BF16)
            dw0 = dw0 + jnp.sum(dconv * cx2, axis=0, keepdims=True)
            dw1 = dw1 + jnp.sum(dconv * cx1, axis=0, keepdims=True)
            dw2 = dw2 + jnp.sum(dconv * cx, axis=0, keepdims=True)
        taps = lax.broadcasted_iota(jnp.int32, (8, CONV_TC), 0)
        dw_ref[...] = jnp.where(taps == 0, dw0, jnp.where(taps == 1, dw1, jnp.where(taps == 2, dw2, 0.0)))

    return pl.pallas_call(
        body, name="conv_bwd", grid=(d // CONV_TC,),
        in_specs=[pl.BlockSpec((N_CHIPS, t, CONV_TC), lambda j: (0, 0, j)),
                  pl.BlockSpec((t, CONV_TC), lambda j: (0, j)),
                  pl.BlockSpec((8, CONV_TC), lambda j: (0, j))],
        out_specs=[pl.BlockSpec((N_CHIPS, t, CONV_TC), lambda j: (0, 0, j)), pl.BlockSpec((8, CONV_TC), lambda j: (0, j))],
        out_shape=[jax.ShapeDtypeStruct((N_CHIPS, t, d), BF16), jax.ShapeDtypeStruct((8, d), F32)],
        compiler_params=_params(1),
    )(proj, dgated, cw)


SB_DEAD_TAIL = -105.0
SB_COUNT_LANE = HEAD_DIM - 1


def _sb_block(t):
    return min(256, t)


def _split_dot(v, tri):
    hi = v.astype(BF16)
    lo = (v - hi.astype(F32)).astype(BF16)
    return _dot(hi, tri, NN) + _dot(lo, tri, NN)


SB_HEADS_PER_STEP = 2


def _sb_terms(s, diagonal):
    sp = jnp.maximum(s, 0.0) + jnp.log1p(jnp.exp(-jnp.abs(s)))
    if not diagonal:
        return -sp, s - sp, sp, None
    mask = lax.broadcasted_iota(jnp.int32, s.shape, 1) < lax.broadcasted_iota(jnp.int32, s.shape, 0)
    return jnp.where(mask, -sp, 0.0), s - sp, sp, mask


def _masked(mask, v):
    return v if mask is None else jnp.where(mask, v, 0.0)


def _sb_fwd(proj, rider=None):
    _, t, d = proj.shape
    heads = d // HEAD_DIM
    blk = _sb_block(t)
    nblk = t // blk
    scale = 1.0 / math.sqrt(HEAD_DIM)

    hps = SB_HEADS_PER_STEP
    width = hps * HEAD_DIM

    def body(q_ref, k_ref, v_ref, z_ref, gated_ref, o_ref, car_ref, tail_ref, acc_ref):
        i = pl.program_id(1)
        r_i = lax.broadcasted_iota(jnp.int32, (blk, blk), 0)
        c_i = lax.broadcasted_iota(jnp.int32, (blk, blk), 1)
        tri_after = (r_i > c_i).astype(BF16)
        lanes = lax.broadcasted_iota(jnp.int32, (blk, HEAD_DIM), 1)

        tail_ref[...] = jnp.zeros_like(tail_ref)
        acc_ref[...] = jnp.zeros_like(acc_ref)
        car_ref[...] = jnp.zeros_like(car_ref)

        def visit(j, diagonal):
            krows = pl.ds(pl.multiple_of(j * blk, blk), blk)
            hcols = [pl.ds(hh * HEAD_DIM, HEAD_DIM) for hh in range(hps)]
            logits = [_dot(q_ref[:, c], k_ref[krows, c], NT) for c in hcols]
            terms = [_sb_terms(s * scale, diagonal) for s in logits]
            within = [_split_dot(keep, tri_after) for keep, _, _, _ in terms]
            top = None
            for hh, (keep, log_beta, _, mask) in enumerate(terms):
                tail_b = tail_ref[hh]
                w = _masked(mask, jnp.exp(log_beta + tail_b[:, 0:1] + within[hh]))
                acc_ref[hh] += _dot(w.astype(BF16), v_ref[krows, hcols[hh]], NN)
                car_ref[hh] = jnp.where(lanes == j, tail_b, car_ref[hh])
                tail_new = tail_b + jnp.sum(keep, axis=1, keepdims=True)
                tail_ref[hh] = tail_new
                top = jnp.max(tail_new) if top is None else jnp.maximum(top, jnp.max(tail_new))
            return top > SB_DEAD_TAIL

        def more(state):
            jj, live = state
            return jnp.logical_and(jj <= i, live)

        def step(state):
            jj, _ = state
            return jj + 1, visit(i - jj, False)

        visited, _ = lax.while_loop(more, step, (jnp.int32(1), visit(i, True)))
        for hh in range(hps):
            cols = pl.ds(hh * HEAD_DIM, HEAD_DIM)
            car_ref[hh] = jnp.where(lanes == SB_COUNT_LANE, visited.astype(F32), car_ref[hh])
            z = z_ref[:, cols].astype(F32)
            acc = acc_ref[hh]
            o_ref[:, cols] = acc.astype(BF16)
            gated_ref[:, cols] = (z * _sigmoid(z) * acc).astype(BF16)

    qspec = lambda s: pl.BlockSpec((None, blk, width), lambda h, i: (s, i, h))
    kspec = lambda s: pl.BlockSpec((None, t, width), lambda h, i: (s, 0, h))
    ospec = pl.BlockSpec((blk, width), lambda h, i: (i, h))
    return _compute_call(
        body, "sb_fwd", grid=(heads // hps, nblk),
        in_specs=[qspec(0), kspec(1), kspec(2), qspec(3)], operands=[proj, proj, proj, proj],
        out_specs=[ospec, ospec, pl.BlockSpec((hps, blk, HEAD_DIM), lambda h, i: (h, i, 0))],
        out_shape=[jax.ShapeDtypeStruct((t, d), BF16), jax.ShapeDtypeStruct((t, d), BF16),
                   jax.ShapeDtypeStruct((heads, t, HEAD_DIM), F32)],
        scratch=[pltpu.VMEM((hps, blk, HEAD_DIM), F32), pltpu.VMEM((hps, blk, HEAD_DIM), F32)], rider=rider)


def _sb_bwd(proj, dgated, o, car):
    _, t, d = proj.shape
    heads = d // HEAD_DIM
    blk = _sb_block(t)
    nblk = t // blk
    scale = 1.0 / math.sqrt(HEAD_DIM)

    hps = SB_HEADS_PER_STEP
    width = hps * HEAD_DIM

    def body(q_ref, k_ref, v_ref, z_ref, dg_ref, o_ref, car_ref, dp_ref, dk_acc, dv_acc, gsum_ref, dq_ref, do_ref):
        step_i = pl.program_id(1)
        i = nblk - 1 - step_i

        @pl.when(step_i == 0)
        def _():
            dk_acc[...] = jnp.zeros_like(dk_acc)
            dv_acc[...] = jnp.zeros_like(dv_acc)

        r_i = lax.broadcasted_iota(jnp.int32, (blk, blk), 0)
        c_i = lax.broadcasted_iota(jnp.int32, (blk, blk), 1)
        tri_after = (r_i > c_i).astype(BF16)
        tri_before = (r_i < c_i).astype(BF16)
        lanes = lax.broadcasted_iota(jnp.int32, (blk, HEAD_DIM), 1)

        gsum_ref[...] = jnp.zeros_like(gsum_ref)
        dq_ref[...] = jnp.zeros_like(dq_ref)
        for hh in range(hps):
            cols = pl.ds(hh * HEAD_DIM, HEAD_DIM)
            z = z_ref[:, cols].astype(F32)
            dg = dg_ref[:, cols].astype(F32)
            sig = _sigmoid(z)
            do_ref[hh] = (dg * (z * sig)).astype(BF16)
            dp_ref[3, :, cols] = (dg * o_ref[:, cols].astype(F32) * (sig * (1.0 + z * (1.0 - sig)))).astype(BF16)

        def visit(j, diagonal):
            krows = pl.ds(pl.multiple_of(j * blk, blk), blk)
            hcols = [pl.ds(hh * HEAD_DIM, HEAD_DIM) for hh in range(hps)]
            logits = [_dot(q_ref[:, c], k_ref[krows, c], NT) for c in hcols]
            dws = [_dot(do_ref[hh], v_ref[krows, c], NT) for hh, c in enumerate(hcols)]
            terms = [_sb_terms(s * scale, diagonal) for s in logits]
            within = [_split_dot(keep, tri_after) for keep, _, _, _ in terms]
            ws, gs = [], []
            for hh, (keep, log_beta, sp, mask) in enumerate(terms):
                tail = jnp.sum(jnp.where(lanes == j, car_ref[hh], 0.0), axis=1, keepdims=True)
                w = _masked(mask, jnp.exp(log_beta + tail + within[hh]))
                ws.append(w.astype(BF16))
                gs.append(w * dws[hh])
            g_within = [_split_dot(g, tri_before) for g in gs]
            for hh, (keep, log_beta, sp, mask) in enumerate(terms):
                c = hcols[hh]
                g_before = gsum_ref[hh]
                g_cum = g_before[:, 0:1] + g_within[hh]
                dl = (_masked(mask, gs[hh] * jnp.exp(-sp) - g_cum * jnp.exp(log_beta)) * scale).astype(BF16)
                dq_ref[hh] += _dot(dl, k_ref[krows, c], NN)
                dk_acc[krows, c] += _dot(dl, q_ref[:, c], TN)
                dv_acc[krows, c] += _dot(ws[hh], do_ref[hh], TN)
                gsum_ref[hh] = g_before + jnp.sum(gs[hh], axis=1, keepdims=True)

        def step(j, carry):
            visit(j, False)
            return carry

        visited = jnp.max(jnp.where(lanes == SB_COUNT_LANE, car_ref[0], 0.0)).astype(jnp.int32)
        lax.fori_loop(i + 1 - visited, i, step, 0)
        visit(i, True)
        own = pl.ds(pl.multiple_of(i * blk, blk), blk)
        for hh in range(hps):
            cols = pl.ds(hh * HEAD_DIM, HEAD_DIM)
            dp_ref[0, :, cols] = dq_ref[hh].astype(BF16)
        dp_ref[1] = dk_acc[own, :].astype(BF16)
        dp_ref[2] = dv_acc[own, :].astype(BF16)

    qspec = lambda s: pl.BlockSpec((None, blk, width), lambda h, i: (s, nblk - 1 - i, h))
    kspec = lambda s: pl.BlockSpec((None, t, width), lambda h, i: (s, 0, h))
    tspec = pl.BlockSpec((blk, width), lambda h, i: (nblk - 1 - i, h))
    return pl.pallas_call(
        body, name="sb_bwd", grid=(heads // hps, nblk),
        in_specs=[qspec(0), kspec(1), kspec(2), qspec(3), tspec, tspec,
                  pl.BlockSpec((hps, blk, HEAD_DIM), lambda h, i: (h, nblk - 1 - i, 0))],
        out_specs=pl.BlockSpec((N_CHIPS, blk, width), lambda h, i: (0, nblk - 1 - i, h)),
        out_shape=jax.ShapeDtypeStruct((N_CHIPS, t, d), BF16),
        scratch_shapes=[pltpu.VMEM((t, width), F32), pltpu.VMEM((t, width), F32),
                        pltpu.VMEM((hps, blk, HEAD_DIM), F32), pltpu.VMEM((hps, blk, HEAD_DIM), F32),
                        pltpu.VMEM((hps, blk, HEAD_DIM), BF16)],
        compiler_params=_params(2),
    )(proj, proj, proj, proj, dgated, o, car)


def _pack_weights(w_in, w_out, chip):
    d = w_in.shape[0]
    rb = d // 8
    n_in = d // rb
    n_out = w_out.shape[0] // rb

    def body(chip_ref, wi_ref, wo_ref, o_ref):
        r = pl.program_id(0)

        @pl.when(r < n_in)
        def _():
            o_ref[...] = wi_ref[...].astype(BF16)

        @pl.when(r >= n_in)
        def _():
            o_ref[...] = wo_ref[...].astype(BF16)

    grid_spec = pltpu.PrefetchScalarGridSpec(
        num_scalar_prefetch=1, grid=(n_in + n_out,),
        in_specs=[pl.BlockSpec((rb, d), lambda r, me: (jnp.minimum(r, n_in - 1), 0)),
                  pl.BlockSpec((rb, d), lambda r, me: (jnp.maximum(r - n_in, 0), 0))],
        out_specs=pl.BlockSpec((None, rb, d), lambda r, me: (me[0], r, 0)))
    return pl.pallas_call(
        body, name="pack_weights", grid_spec=grid_spec,
        out_shape=jax.ShapeDtypeStruct((N_CHIPS, d + w_out.shape[0], d), BF16), compiler_params=_params(1),
    )(chip, w_in, w_out)


def _flip(v, bit):
    return 1 - v if bit else v


def _remote(src, dst, send_sem, recv_sem, target):
    return pltpu.make_async_remote_copy(src_ref=src, dst_ref=dst, send_sem=send_sem, recv_sem=recv_sem,
                                        device_id=target, device_id_type=MESH_IDS)


AG_CHUNKS = 4
HALF_CHUNKS = 8


SWAP_CHUNKS = 2


def _other_chips(x, y):
    return [(_flip(x, k >> 1), _flip(y, k & 1)) for k in (1, 2, 3)]


def _gather_sems(n):
    return [pltpu.SemaphoreType.DMA((3, n * AG_CHUNKS)) for _ in range(4)]


def _gather_pieces(g):
    hr = g[0].shape[1] // 2
    cr = hr // AG_CHUNKS
    return hr, [(l * AG_CHUNKS + q, g[l], q * cr, cr) for l in range(len(g)) for q in range(AG_CHUNKS)]


def _gather_start(g, send, recv):
    x, y, c = lax.axis_index("x"), lax.axis_index("y"), lax.axis_index("c")
    hr, pieces = _gather_pieces(g)
    for k, (px, py) in enumerate(_other_chips(x, y)):
        for i, ref, r0, cr in pieces:
            piece = ref.at[2 * x + y, pl.ds(c * hr + r0, cr)]
            _remote(piece, piece, send.at[k, i], recv.at[k, i], (px, py, c)).start()


def _gather_finish(g, send, recv, fsend, frecv):
    x, y, c = lax.axis_index("x"), lax.axis_index("y"), lax.axis_index("c")
    sibling = (x, y, 1 - c)
    hr, pieces = _gather_pieces(g)
    chips = _other_chips(x, y)
    for k, (px, py) in enumerate(chips):
        for i, ref, r0, cr in pieces:
            landed = ref.at[2 * px + py, pl.ds(c * hr + r0, cr)]
            _remote(landed, landed, send.at[k, i], recv.at[k, i], (px, py, c)).wait_recv()
            _remote(landed, landed, fsend.at[k, i], frecv.at[k, i], sibling).start()
    for k, (px, py) in enumerate(chips):
        for i, ref, r0, cr in pieces:
            theirs = ref.at[2 * px + py, pl.ds((1 - c) * hr + r0, cr)]
            _remote(theirs, theirs, fsend.at[k, i], frecv.at[k, i], sibling).wait_recv()
    for k, (px, py) in enumerate(chips):
        for i, ref, r0, cr in pieces:
            mine = ref.at[2 * x + y, pl.ds(c * hr + r0, cr)]
            _remote(mine, mine, send.at[k, i], recv.at[k, i], (px, py, c)).wait_send()
            landed = ref.at[2 * px + py, pl.ds(c * hr + r0, cr)]
            _remote(landed, landed, fsend.at[k, i], frecv.at[k, i], sibling).wait_send()


def _all_gather_weights(packs, cw):
    n = len(packs)

    def body(*refs):
        cw_ref = refs[n]
        g = refs[n + 1:2 * n + 1]
        cwg = refs[2 * n + 1]
        send, recv, fsend, frecv, csend, crecv, lsem = refs[2 * n + 2:]
        x, y, c = lax.axis_index("x"), lax.axis_index("y"), lax.axis_index("c")
        me = 2 * x + y
        local = pltpu.make_async_copy(cw_ref, cwg.at[me], lsem)
        local.start()
        taps = [_remote(cw_ref, cwg.at[me], csend.at[k], crecv.at[k], (px, py, c))
                for k, (px, py) in enumerate(_other_chips(x, y))]
        for cp in taps:
            cp.start()
        _gather_start(g, send, recv)
        _gather_finish(g, send, recv, fsend, frecv)
        for k, (px, py) in enumerate(_other_chips(x, y)):
            _remote(cw_ref, cwg.at[2 * px + py], csend.at[k], crecv.at[k], (px, py, c)).wait_recv()
        for cp in taps:
            cp.wait_send()
        local.wait()

    out_shape = [jax.ShapeDtypeStruct(p.shape, BF16) for p in packs]
    out_shape.append(jax.ShapeDtypeStruct((N_CHIPS,) + cw.shape, cw.dtype))
    return pl.pallas_call(
        body, name="all_gather_weights", in_specs=[HBM_SPEC] * (n + 1), out_specs=[HBM_SPEC] * (n + 1),
        out_shape=out_shape, input_output_aliases={l: l for l in range(n)},
        scratch_shapes=_gather_sems(n) + [pltpu.SemaphoreType.DMA((3,)), pltpu.SemaphoreType.DMA((3,)),
                                          pltpu.SemaphoreType.DMA],
    )(*packs, cw)


def _gather_rider(packs):
    return _Rider(
        operands=packs, out_shapes=[jax.ShapeDtypeStruct(p.shape, BF16) for p in packs],
        aliases={l: l for l in range(len(packs))}, sems=_gather_sems(len(packs)),
        start=lambda ins, outs, sems: _gather_start(outs, sems[0], sems[1]),
        finish=lambda ins, outs, sems: _gather_finish(outs, *sems))


def _exchange_last(sums, small):
    _, hr, d = sums.shape

    def body(s_ref, sm, got, smg, send, recv, ssend, srecv, lsem):
        x, y, c = lax.axis_index("x"), lax.axis_index("y"), lax.axis_index("c")
        me = 4 * x + 2 * y + c
        peers = [(_flip(x, r >> 2), _flip(y, (r >> 1) & 1), _flip(c, r & 1)) for r in range(1, N_DEV)]
        local = pltpu.make_async_copy(sm, smg.at[me], lsem)
        local.start()
        _send_sums_start(s_ref, got, send, recv)
        smalls = [_remote(sm, smg.at[me], ssend.at[r], srecv.at[r], peer) for r, peer in enumerate(peers)]
        for cp in smalls:
            cp.start()
        for r, (tx, ty, tc) in enumerate(peers):
            _remote(sm, smg.at[4 * tx + 2 * ty + tc], ssend.at[r], srecv.at[r], (tx, ty, tc)).wait_recv()
        _send_sums_finish(s_ref, got, send, recv)
        for cp in smalls:
            cp.wait_send()
        local.wait()

    return pl.pallas_call(
        body, name="exchange_last", in_specs=[HBM_SPEC] * 2, out_specs=[HBM_SPEC] * 2,
        out_shape=[jax.ShapeDtypeStruct((N_CHIPS - 1, hr, d), BF16),
                   jax.ShapeDtypeStruct((N_DEV,) + small.shape, small.dtype)],
        scratch_shapes=[pltpu.SemaphoreType.DMA((3,)), pltpu.SemaphoreType.DMA((3,)),
                        pltpu.SemaphoreType.DMA((N_DEV - 1,)), pltpu.SemaphoreType.DMA((N_DEV - 1,)),
                        pltpu.SemaphoreType.DMA],
    )(sums, small)


def _send_sums_start(s_ref, got, send, recv):
    x, y, c = lax.axis_index("x"), lax.axis_index("y"), lax.axis_index("c")
    for k, (px, py) in enumerate(_other_chips(x, y)):
        _remote(s_ref.at[2 * px + py], got.at[k], send.at[k], recv.at[k], (px, py, c)).start()


def _send_sums_finish(s_ref, got, send, recv):
    x, y, c = lax.axis_index("x"), lax.axis_index("y"), lax.axis_index("c")
    for k, (px, py) in enumerate(_other_chips(x, y)):
        _remote(got.at[k], got.at[k], send.at[k], recv.at[k], (px, py, c)).wait_recv()
    for k, (px, py) in enumerate(_other_chips(x, y)):
        _remote(s_ref.at[2 * px + py], got.at[k], send.at[k], recv.at[k], (px, py, c)).wait_send()


def _send_sums_rider(sums):
    _, hr, d = sums.shape
    return _Rider(
        operands=[sums], out_shapes=[jax.ShapeDtypeStruct((N_CHIPS - 1, hr, d), BF16)], aliases={},
        sems=[pltpu.SemaphoreType.DMA((3,)), pltpu.SemaphoreType.DMA((3,))],
        start=lambda ins, outs, sems: _send_sums_start(ins[0], outs[0], *sems),
        finish=lambda ins, outs, sems: _send_sums_finish(ins[0], outs[0], *sems))


def _swap_pieces(gp_ref, x_ref, c):
    hr = x_ref.shape[1]
    cr = hr // SWAP_CHUNKS
    return [(a * SWAP_CHUNKS + q, gp_ref.at[a, pl.ds((1 - c) * hr + q * cr, cr)], x_ref.at[a, pl.ds(q * cr, cr)])
            for a in range(N_CHIPS) for q in range(SWAP_CHUNKS)]


def _swap_rider(gp):
    _, p_rows, d = gp.shape

    def start(ins, outs, sems):
        x, y, c = lax.axis_index("x"), lax.axis_index("y"), lax.axis_index("c")
        for i, src, dst in _swap_pieces(ins[0], outs[0], c):
            _remote(src, dst, sems[0].at[i], sems[1].at[i], (x, y, 1 - c)).start()

    def finish(ins, outs, sems):
        x, y, c = lax.axis_index("x"), lax.axis_index("y"), lax.axis_index("c")
        pieces = _swap_pieces(ins[0], outs[0], c)
        for i, src, dst in pieces:
            _remote(dst, dst, sems[0].at[i], sems[1].at[i], (x, y, 1 - c)).wait_recv()
        for i, src, dst in pieces:
            _remote(src, dst, sems[0].at[i], sems[1].at[i], (x, y, 1 - c)).wait_send()

    nsem = N_CHIPS * SWAP_CHUNKS
    return _Rider(
        operands=[gp], out_shapes=[jax.ShapeDtypeStruct((N_CHIPS, p_rows // 2, d), BF16)], aliases={},
        sems=[pltpu.SemaphoreType.DMA((nsem,)), pltpu.SemaphoreType.DMA((nsem,))], start=start, finish=finish)


def _presum(gp, theirs, core):
    _, hr, d = theirs.shape
    tr = _row_tile(hr, 640)
    steps = hr // tr

    def body(core_ref, mine_ref, theirs_ref, o_ref):
        o_ref[...] = (mine_ref[...].astype(F32) + theirs_ref[...].astype(F32)).astype(BF16)

    grid_spec = pltpu.PrefetchScalarGridSpec(
        num_scalar_prefetch=1, grid=(N_CHIPS, steps),
        in_specs=[pl.BlockSpec((None, tr, d), lambda a, i, cr: (a, cr[0] * steps + i, 0)),
                  pl.BlockSpec((None, tr, d), lambda a, i, cr: (a, i, 0))],
        out_specs=pl.BlockSpec((None, tr, d), lambda a, i, cr: (a, i, 0)))
    return pl.pallas_call(
        body, name="presum", grid_spec=grid_spec,
        out_shape=jax.ShapeDtypeStruct((N_CHIPS, hr, d), BF16), compiler_params=_params(2),
    )(core, gp, theirs)


def _row_tile(rows, cap=128):
    if rows <= cap:
        return rows
    return next(tr for tr in range(cap, 0, -16) if rows % tr == 0)


def _sum_sources(name, parts):
    nsrc, rows, cols = parts.shape
    tr = _row_tile(rows)

    def body(p_ref, o_ref):
        total = p_ref[0].astype(F32)
        for s in range(1, nsrc):
            total = total + p_ref[s].astype(F32)
        o_ref[...] = total

    return pl.pallas_call(
        body, name=name, grid=(rows // tr,),
        in_specs=[pl.BlockSpec((nsrc, tr, cols), lambda i: (0, i, 0))],
        out_specs=pl.BlockSpec((tr, cols), lambda i: (i, 0)),
        out_shape=jax.ShapeDtypeStruct((rows, cols), F32), compiler_params=_params(1),
    )(parts)


def _sum_grad_half(got, sums, place):
    nsrc, hr, d = got.shape
    tr = _row_tile(hr, 256)

    def body(place_ref, got_ref, own_ref, o_ref):
        total = own_ref[...].astype(F32)
        for s in range(nsrc):
            total = total + got_ref[s].astype(F32)
        o_ref[...] = total

    grid_spec = pltpu.PrefetchScalarGridSpec(
        num_scalar_prefetch=1, grid=(hr // tr,),
        in_specs=[pl.BlockSpec((nsrc, tr, d), lambda i, pc: (0, i, 0)),
                  pl.BlockSpec((None, tr, d), lambda i, pc: (pc[0], i, 0))],
        out_specs=pl.BlockSpec((None, tr, d), lambda i, pc: (pc[1], i, 0)))
    return pl.pallas_call(
        body, name="sum_grad_half", grid_spec=grid_spec,
        out_shape=jax.ShapeDtypeStruct((2, hr, d), F32), compiler_params=_params(1),
    )(place, got, sums)


def _exchange_halves(fulls):
    n = len(fulls)
    _, hr, d = fulls[0].shape
    cr = hr // HALF_CHUNKS
    nc = n * HALF_CHUNKS

    def body(*refs):
        full = refs[n:2 * n]
        send, recv = refs[2 * n:]
        x, y, c = lax.axis_index("x"), lax.axis_index("y"), lax.axis_index("c")
        sibling = (x, y, 1 - c)
        pieces = [(l, q) for l in range(n) for q in range(HALF_CHUNKS)]
        sends = []
        for i, (l, q) in enumerate(pieces):
            piece = full[l].at[c, pl.ds(q * cr, cr)]
            sends.append(_remote(piece, piece, send.at[i], recv.at[i], sibling))
        for cp in sends:
            cp.start()
        for i, (l, q) in enumerate(pieces):
            theirs = full[l].at[1 - c, pl.ds(q * cr, cr)]
            _remote(theirs, theirs, send.at[i], recv.at[i], sibling).wait_recv()
        for cp in sends:
            cp.wait_send()

    return pl.pallas_call(
        body, name="exchange_halves", in_specs=[HBM_SPEC] * n, out_specs=[HBM_SPEC] * n,
        out_shape=[jax.ShapeDtypeStruct(f.shape, F32) for f in fulls], input_output_aliases={l: l for l in range(n)},
        scratch_shapes=[pltpu.SemaphoreType.DMA((nc,)), pltpu.SemaphoreType.DMA((nc,))],
    )(*fulls)


def _adamw(name, w, m, v, g, g_row0=0):
    rows, cols = w.shape
    tr = _row_tile(rows, 256)
    off = g_row0 // tr

    def body(w_ref, m_ref, v_ref, g_ref, go_ref, d_ref, mo_ref, vo_ref):
        grad = g_ref[...]
        m_new = ADAM_B1 * m_ref[...] + (1.0 - ADAM_B1) * grad
        v_new = ADAM_B2 * v_ref[...] + (1.0 - ADAM_B2) * (grad * grad)
        m_hat = m_new / (1.0 - ADAM_B1 ** ADAM_STEP)
        v_hat = v_new / (1.0 - ADAM_B2 ** ADAM_STEP)
        go_ref[...] = grad
        d_ref[...] = -ADAM_LR * (m_hat / (jnp.sqrt(v_hat) + ADAM_EPS) + ADAM_WD * w_ref[...])
        mo_ref[...] = m_new
        vo_ref[...] = v_new

    blk = pl.BlockSpec((tr, cols), lambda i: (i, 0))
    return pl.pallas_call(
        body, name=name, grid=(rows // tr,),
        in_specs=[blk, blk, blk, pl.BlockSpec((tr, cols), lambda i: (i + off, 0))],
        out_specs=[blk, blk, blk, blk],
        out_shape=[jax.ShapeDtypeStruct((rows, cols), F32)] * 4, compiler_params=_params(1),
    )(w, m, v, g)


def _pad_rows8(a):
    return jnp.concatenate([a, jnp.zeros((8 - a.shape[0],) + a.shape[1:], a.dtype)], axis=0)


def kernel(x, ln_pre_0, conv_w_in_0, conv_w_0, conv_w_out_0, ln_post_0, ln_pre_1, sb_w_in_1, sb_w_out_1, ln_post_1, ln_pre_2, conv_w_in_2, conv_w_2, conv_w_out_2, ln_post_2, ln_pre_3, sb_w_in_3, sb_w_out_3, ln_post_3, loss_target, m_ln_pre_0, m_conv_w_in_0, m_conv_w_0, m_conv_w_out_0, m_ln_post_0, m_ln_pre_1, m_sb_w_in_1, m_sb_w_out_1, m_ln_post_1, m_ln_pre_2, m_conv_w_in_2, m_conv_w_2, m_conv_w_out_2, m_ln_post_2, m_ln_pre_3, m_sb_w_in_3, m_sb_w_out_3, m_ln_post_3, v_ln_pre_0, v_conv_w_in_0, v_conv_w_0, v_conv_w_out_0, v_ln_post_0, v_ln_pre_1, v_sb_w_in_1, v_sb_w_out_1, v_ln_post_1, v_ln_pre_2, v_conv_w_in_2, v_conv_w_2, v_conv_w_out_2, v_ln_post_2, v_ln_pre_3, v_sb_w_in_3, v_sb_w_out_3, v_ln_post_3):
    t, d = x.shape[1], x.shape[2]
    dq = d // N_CHIPS
    xs = x.reshape(t, d)
    target = loss_target.reshape(t, d)
    w_in = [conv_w_in_0, sb_w_in_1, conv_w_in_2, sb_w_in_3]
    w_out = [conv_w_out_0, sb_w_out_1, conv_w_out_2, sb_w_out_3]
    m_in = [m_conv_w_in_0, m_sb_w_in_1, m_conv_w_in_2, m_sb_w_in_3]
    m_out = [m_conv_w_out_0, m_sb_w_out_1, m_conv_w_out_2, m_sb_w_out_3]
    v_in = [v_conv_w_in_0, v_sb_w_in_1, v_conv_w_in_2, v_sb_w_in_3]
    v_out = [v_conv_w_out_0, v_sb_w_out_1, v_conv_w_out_2, v_sb_w_out_3]
    ln_pre = [ln_pre_0, ln_pre_1, ln_pre_2, ln_pre_3]
    ln_post = [ln_post_0, ln_post_1, ln_post_2, ln_post_3]
    conv_w = [conv_w_0, conv_w_2]
    m_conv = [m_conv_w_0, m_conv_w_2]
    v_conv = [v_conv_w_0, v_conv_w_2]
    chip = 2 * lax.axis_index("x") + lax.axis_index("y")
    chip_arr = jnp.reshape(chip, (1,)).astype(jnp.int32)
    place = jnp.stack([chip, lax.axis_index("c")]).astype(jnp.int32)
    core_arr = jnp.reshape(lax.axis_index("c"), (1,)).astype(jnp.int32)

    packs = [_pack_weights(w_in[l], w_out[l], chip_arr) for l in range(N_LAYERS)]
    cw_local = jnp.concatenate([_pad_rows8(conv_w[0]), _pad_rows8(conv_w[1])], axis=0)
    gathered = list(packs)
    gathered[0], cw_all = _all_gather_weights(packs[:1], cw_local)
    cw_full = jnp.transpose(cw_all, (1, 0, 2)).reshape(16, d)
    conv_taps = {0: cw_full[0:8], 2: cw_full[8:16]}

    h_in, us, projs, gateds, ms, sb_saved = [], [], [], [], [], {}
    h = xs
    u = _norm_first(xs, ln_pre[0])
    for l in range(N_LAYERS):
        h_in.append(h)
        us.append(u)
        if l == 0:
            proj, gathered[1] = _mm_proj(u, gathered[0], rider=_gather_rider(gathered[1:2]))
        else:
            proj, = _mm_proj(u, gathered[l])
        if l % 2 == 0:
            gated = _conv_fwd(proj, conv_taps[l])
        elif l == 1:
            gated, o, car, gathered[2], gathered[3] = _sb_fwd(proj, rider=_gather_rider(gathered[2:4]))
            sb_saved[l] = (o, car)
        else:
            gated, o, car = _sb_fwd(proj)
            sb_saved[l] = (o, car)
        m = _mm_out(gated, gathered[l])
        projs.append(proj)
        gateds.append(gated)
        ms.append(m)
        if l < N_LAYERS - 1:
            h, u = _norm_mid(h, m, ln_post[l], ln_pre[l + 1])
    dh, dm, dg_post_last, loss_part = _norm_last(h, ms[-1], ln_post[-1], target)
    loss = lax.psum(loss_part[0, 0], ("x", "y", "c"))

    dg_pre = [None] * N_LAYERS
    dg_post = [None] * N_LAYERS
    dg_post[N_LAYERS - 1] = dg_post_last
    dconv = {}
    sums = [None] * N_LAYERS
    got = [None] * N_LAYERS
    grad_x = None
    for l in reversed(range(N_LAYERS)):
        dgated = _mm_dgated(dm, gathered[l])
        gp = _mm_dwout(gateds[l], dm)
        if l % 2 == 0:
            dproj, dconv[l] = _conv_bwd(projs[l], dgated, conv_taps[l])
        else:
            o, car = sb_saved[l]
            dproj = _sb_bwd(projs[l], dgated, o, car)
        if l < N_LAYERS - 1:
            gp, got[l + 1] = _mm_dwin(us[l], dproj, gp, rider=_send_sums_rider(sums[l + 1]))
        else:
            gp, = _mm_dwin(us[l], dproj, gp)
        du, theirs = _mm_du(dproj, gathered[l], rider=_swap_rider(gp))
        sums[l] = _presum(gp, theirs, core_arr)
        if l > 0:
            dh, dm, dg_pre[l], dg_post[l - 1] = _norm_bwd_mid(dh, du, h_in[l], ln_pre[l], ms[l - 1], ln_post[l - 1])
        else:
            grad_x, dg_pre[0] = _norm_bwd_first(dh, du, h_in[0], ln_pre[0])

    small = jnp.concatenate(dg_pre + dg_post + [dconv[0], dconv[2]], axis=0)
    got[0], small_all = _exchange_last(sums[0], small)
    halves = [_sum_grad_half(got[l], sums[l], place) for l in range(N_LAYERS)]
    small_sum = _sum_sources("sum_small", small_all)
    fulls = [f.reshape(d + dq, d) for f in _exchange_halves(halves)]

    res_in = [_adamw("adamw_w_in", w_in[l], m_in[l], v_in[l], fulls[l], 0) for l in range(N_LAYERS)]
    res_out = [_adamw("adamw_w_out", w_out[l], m_out[l], v_out[l], fulls[l], d) for l in range(N_LAYERS)]
    ln_all = ln_pre + ln_post
    ln_m = [m_ln_pre_0, m_ln_pre_1, m_ln_pre_2, m_ln_pre_3, m_ln_post_0, m_ln_post_1, m_ln_post_2, m_ln_post_3]
    ln_v = [v_ln_pre_0, v_ln_pre_1, v_ln_pre_2, v_ln_pre_3, v_ln_post_0, v_ln_post_1, v_ln_post_2, v_ln_post_3]
    ln_g = jnp.concatenate([small_sum[8 * i:8 * i + 1] for i in range(2 * N_LAYERS)], axis=0)
    res_ln = _adamw("adamw_ln", jnp.stack(ln_all), jnp.stack(ln_m), jnp.stack(ln_v), ln_g)
    conv_g = []
    for i in range(2):
        rows = small_sum[8 * (2 * N_LAYERS + i):8 * (2 * N_LAYERS + i + 1)]
        conv_g.append(lax.dynamic_slice(rows, (0, chip * dq), (8, dq)))
    stack8 = lambda a, b: jnp.concatenate([_pad_rows8(a), _pad_rows8(b)], axis=0)
    res_conv = _adamw("adamw_conv", stack8(*conv_w), stack8(*m_conv), stack8(*v_conv), jnp.concatenate(conv_g, axis=0))

    def leaf(kind, l, which):
        if kind == "ln_pre":
            return res_ln[which][l]
        if kind == "ln_post":
            return res_ln[which][N_LAYERS + l]
        if kind == "w_in":
            return res_in[l][which]
        if kind == "w_out":
            return res_out[l][which]
        return res_conv[which][8 * (l // 2):8 * (l // 2) + 3]

    order = []
    for l in range(N_LAYERS):
        order.append(("ln_pre", l))
        order.append(("w_in", l))
        if l % 2 == 0:
            order.append(("conv", l))
        order.append(("w_out", l))
        order.append(("ln_post", l))
    outs = [loss, grad_x.reshape(1, t, d)]
    for which in range(4):
        outs.extend(leaf(kind, l, which) for kind, l in order)
    return tuple(outs)
```

```python
import functools
import math
from typing import Any, Callable, Mapping, NamedTuple, Sequence

import jax
import jax.numpy as jnp
from jax import lax
from jax.experimental import pallas as pl
from jax.experimental.pallas import tpu as pltpu

F32 = jnp.float32
BF16 = jnp.bfloat16

N_CHIPS = 4
N_DEV = 8
N_LAYERS = 4
HEAD_DIM = 128
RMS_EPS = 1e-6
ADAM_LR = 0.001
ADAM_B1 = 0.9
ADAM_B2 = 0.999
ADAM_EPS = 1e-08
ADAM_WD = 0.01
ADAM_STEP = 10

VMEM_LIMIT = 56 * 1024 * 1024
MESH_IDS = pl.DeviceIdType.MESH
HBM_SPEC = pl.BlockSpec(memory_space=pltpu.HBM)

NN = (((1,), (0,)), ((), ()))
NT = (((1,), (1,)), ((), ()))
TN = (((0,), (0,)), ((), ()))


def _params(n_axes):
    return pltpu.CompilerParams(dimension_semantics=("arbitrary",) * n_axes, vmem_limit_bytes=VMEM_LIMIT)


def _dot(a, b, dims):
    return lax.dot_general(a, b, dims, preferred_element_type=F32)


def _sigmoid(z):
    return 1.0 / (1.0 + jnp.exp(-z))


class _Rider(NamedTuple):
    operands: Sequence[Any]
    out_shapes: Sequence[Any]
    aliases: Mapping[int, int]
    sems: Sequence[Any]
    start: Callable
    finish: Callable


def _compute_call(body, name, *, grid, in_specs, operands, out_specs, out_shape, scratch=(), aliases=None, rider=None):
    in_specs, operands = list(in_specs), list(operands)
    out_specs, out_shape, scratch = list(out_specs), list(out_shape), list(scratch)
    aliases = dict(aliases or {})
    n_in, n_out, n_scratch = len(operands), len(out_shape), len(scratch)
    hosted = body
    if rider is not None:
        r_in, r_out = len(rider.operands), len(rider.out_shapes)
        aliases.update({n_in + i: n_out + o for i, o in rider.aliases.items()})

        def hosted(*refs):
            ins, refs = refs[:n_in], refs[n_in:]
            rider_ins, refs = refs[:r_in], refs[r_in:]
            outs, refs = refs[:n_out], refs[n_out:]
            rider_outs, refs = refs[:r_out], refs[r_out:]
            own_scratch, rider_sems = refs[:n_scratch], refs[n_scratch:]
            ids = [pl.program_id(axis) for axis in range(len(grid))]
            first = functools.reduce(jnp.logical_and, [i == 0 for i in ids])
            last = functools.reduce(jnp.logical_and, [i == g - 1 for i, g in zip(ids, grid)])

            @pl.when(first)
            def _():
                rider.start(rider_ins, rider_outs, rider_sems)

            body(*ins, *outs, *own_scratch)

            @pl.when(last)
            def _():
                rider.finish(rider_ins, rider_outs, rider_sems)

        in_specs += [HBM_SPEC] * r_in
        operands += list(rider.operands)
        out_specs += [HBM_SPEC] * r_out
        out_shape += list(rider.out_shapes)
        scratch += list(rider.sems)
    return pl.pallas_call(
        hosted, name=name, grid=grid, in_specs=in_specs, out_specs=out_specs, out_shape=out_shape,
        scratch_shapes=scratch, input_output_aliases=aliases, compiler_params=_params(len(grid)),
    )(*operands)


def _matmul(name, a, b, *, grid, a_spec, b_spec, o_spec, out_shape, dims, reduce_axis=None, acc_shape=None,
            alias_out=None, rider=None):
    out_dtype = out_shape.dtype
    direct = reduce_axis is not None and out_dtype == F32
    n_red = grid[reduce_axis] if reduce_axis is not None else 1

    def body(*refs):
        if alias_out is not None:
            refs = refs[1:]
        a_ref, b_ref, o_ref = refs[:3]
        if reduce_axis is None:
            o_ref[...] = _dot(a_ref[...], b_ref[...], dims).astype(out_dtype)
            return
        acc_ref = o_ref if direct else refs[3]
        k = pl.program_id(reduce_axis)

        @pl.when(k == 0)
        def _():
            acc_ref[...] = jnp.zeros_like(acc_ref)

        acc_ref[...] += _dot(a_ref[...], b_ref[...], dims)

        if not direct:
            @pl.when(k == n_red - 1)
            def _():
                o_ref[...] = acc_ref[...].astype(out_dtype)

    scratch = []
    if reduce_axis is not None and not direct:
        scratch = [pltpu.VMEM(acc_shape, F32)]
    in_specs = [a_spec, b_spec]
    operands = [a, b]
    aliases = {}
    if alias_out is not None:
        in_specs = [HBM_SPEC] + in_specs
        operands = [alias_out] + operands
        aliases = {0: 0}
    return _compute_call(body, name, grid=grid, in_specs=in_specs, operands=operands, out_specs=[o_spec],
                         out_shape=[out_shape], scratch=scratch, aliases=aliases, rider=rider)


def _mm_proj(u, g, rider=None):
    t, d = u.shape
    tm = min(512, t)
    return _matmul(
        "mm_proj", u, g, grid=(N_CHIPS, t // tm),
        a_spec=pl.BlockSpec((tm, d), lambda s, m: (m, 0)),
        b_spec=pl.BlockSpec((None, d, d), lambda s, m: (s, 0, 0)),
        o_spec=pl.BlockSpec((None, tm, d), lambda s, m: (s, m, 0)),
        out_shape=jax.ShapeDtypeStruct((N_CHIPS, t, d), BF16), dims=NN, rider=rider)


def _mm_out(gated, g, rider=None):
    t, d = gated.shape
    dq = d // N_CHIPS
    tm = min(512, t)

    def body(a_ref, b_ref, o_ref):
        acc = _dot(a_ref[:, 0:dq], b_ref[0], NN)
        for s in range(1, N_CHIPS):
            acc = acc + _dot(a_ref[:, s * dq:(s + 1) * dq], b_ref[s], NN)
        o_ref[...] = acc

    return _compute_call(
        body, "mm_out", grid=(t // tm,),
        in_specs=[pl.BlockSpec((tm, d), lambda m: (m, 0)), pl.BlockSpec((N_CHIPS, dq, d), lambda m: (0, N_CHIPS, 0))],
        operands=[gated, g], out_specs=[pl.BlockSpec((tm, d), lambda m: (m, 0))],
        out_shape=[jax.ShapeDtypeStruct((t, d), F32)], rider=rider)


def _mm_dgated(dm, g):
    t, d = dm.shape
    dq = d // N_CHIPS
    tm = min(512, t)

    def body(a_ref, b_ref, o_ref):
        a = a_ref[...]
        for s in range(N_CHIPS):
            o_ref[:, s * dq:(s + 1) * dq] = _dot(a, b_ref[s], NT).astype(BF16)

    return _compute_call(
        body, "mm_dgated", grid=(t // tm,),
        in_specs=[pl.BlockSpec((tm, d), lambda m: (m, 0)), pl.BlockSpec((N_CHIPS, dq, d), lambda m: (0, N_CHIPS, 0))],
        operands=[dm, g], out_specs=[pl.BlockSpec((tm, d), lambda m: (m, 0))],
        out_shape=[jax.ShapeDtypeStruct((t, d), BF16)])[0]


def _mm_dwout(gated, dm):
    t, d = gated.shape
    dq = d // N_CHIPS
    tk = min(1024, t)
    return _matmul(
        "mm_dwout", gated, dm, grid=(N_CHIPS, t // tk),
        a_spec=pl.BlockSpec((tk, dq), lambda s, k: (k, s)),
        b_spec=pl.BlockSpec((tk, d), lambda s, k: (k, 0)),
        o_spec=pl.BlockSpec((None, dq, d), lambda s, k: (s, N_CHIPS, 0)),
        out_shape=jax.ShapeDtypeStruct((N_CHIPS, d + dq, d), BF16), dims=TN, reduce_axis=1, acc_shape=(dq, d))[0]


def _mm_du(dproj, g, rider=None):
    _, t, d = dproj.shape
    tm = min(512, t)
    return _matmul(
        "mm_du", dproj, g, grid=(t // tm, N_CHIPS),
        a_spec=pl.BlockSpec((None, tm, d), lambda m, s: (s, m, 0)),
        b_spec=pl.BlockSpec((None, d, d), lambda m, s: (s, 0, 0)),
        o_spec=pl.BlockSpec((tm, d), lambda m, s: (m, 0)),
        out_shape=jax.ShapeDtypeStruct((t, d), F32), dims=NT, reduce_axis=1, rider=rider)


def _mm_dwin(u, dproj, gp, rider=None):
    t, d = u.shape
    tmo = min(1024, d)
    tk = min(1024, t)
    return _matmul(
        "mm_dwin", u, dproj, grid=(N_CHIPS, d // tmo, t // tk),
        a_spec=pl.BlockSpec((tk, tmo), lambda s, mo, k: (k, mo)),
        b_spec=pl.BlockSpec((None, tk, d), lambda s, mo, k: (s, k, 0)),
        o_spec=pl.BlockSpec((None, tmo, d), lambda s, mo, k: (s, mo, 0)),
        out_shape=jax.ShapeDtypeStruct(gp.shape, BF16), dims=TN, reduce_axis=2, acc_shape=(tmo, d), alias_out=gp,
        rider=rider)


def _rms(v):
    r = lax.rsqrt(jnp.mean(v * v, axis=-1, keepdims=True) + RMS_EPS)
    return v * r, r


def _rms_bwd(dout, n, r, gain):
    dn = dout * gain
    return r * (dn - n * jnp.mean(dn * n, axis=-1, keepdims=True))


def _fold8(v):
    return jnp.sum(v.reshape(v.shape[0] // 8, 8, v.shape[1]), axis=0)


def _row0(total):
    rows = lax.broadcasted_iota(jnp.int32, total.shape, 0)
    return jnp.where(rows == 0, jnp.sum(total, axis=0, keepdims=True), 0.0)


def _norm_tile(t):
    return min(256, t)


def _norm_first(x, g_pre):
    t, d = x.shape
    tr = _norm_tile(t)

    def body(x_ref, g_ref, u_ref):
        n, _ = _rms(x_ref[...])
        u_ref[...] = (n * g_ref[...]).astype(BF16)

    row = pl.BlockSpec((tr, d), lambda i: (i, 0))
    vec = pl.BlockSpec((1, d), lambda i: (0, 0))
    return pl.pallas_call(
        body, name="norm_first", grid=(t // tr,), in_specs=[row, vec], out_specs=row,
        out_shape=jax.ShapeDtypeStruct((t, d), BF16), compiler_params=_params(1),
    )(x, g_pre.reshape(1, d))


def _norm_mid(h, m, g_post, g_pre_next):
    t, d = h.shape
    tr = _norm_tile(t)

    def body(h_ref, m_ref, gp_ref, gn_ref, hn_ref, u_ref):
        n, _ = _rms(m_ref[...])
        hn = h_ref[...] + n * gp_ref[...]
        hn_ref[...] = hn
        n2, _ = _rms(hn)
        u_ref[...] = (n2 * gn_ref[...]).astype(BF16)

    row = pl.BlockSpec((tr, d), lambda i: (i, 0))
    vec = pl.BlockSpec((1, d), lambda i: (0, 0))
    return pl.pallas_call(
        body, name="norm_mid", grid=(t // tr,), in_specs=[row, row, vec, vec], out_specs=[row, row],
        out_shape=[jax.ShapeDtypeStruct((t, d), F32), jax.ShapeDtypeStruct((t, d), BF16)],
        compiler_params=_params(1),
    )(h, m, g_post.reshape(1, d), g_pre_next.reshape(1, d))


def _norm_last(h, m, g_post, target):
    t, d = h.shape
    tr = _norm_tile(t)
    nsteps = t // tr

    def body(h_ref, m_ref, gp_ref, tg_ref, dy_ref, dm_ref, dgp_ref, loss_ref, acc_g, acc_l):
        i = pl.program_id(0)

        @pl.when(i == 0)
        def _():
            acc_g[...] = jnp.zeros_like(acc_g)
            acc_l[...] = jnp.zeros_like(acc_l)

        gain = gp_ref[...]
        n, r = _rms(m_ref[...])
        err = h_ref[...] + n * gain - tg_ref[...]
        dy = err / d
        dy_ref[...] = dy
        dm_ref[...] = _rms_bwd(dy, n, r, gain).astype(BF16)
        acc_g[...] += _fold8(dy * n)
        acc_l[...] += _fold8(err * err)

        @pl.when(i == nsteps - 1)
        def _():
            dgp_ref[...] = _row0(acc_g[...])
            loss_ref[...] = jnp.zeros((8, 128), F32) + (0.5 / d) * jnp.sum(acc_l[...])

    row = pl.BlockSpec((tr, d), lambda i: (i, 0))
    vec = pl.BlockSpec((1, d), lambda i: (0, 0))
    acc = pl.BlockSpec((8, d), lambda i: (0, 0))
    return pl.pallas_call(
        body, name="norm_last", grid=(nsteps,), in_specs=[row, row, vec, row],
        out_specs=[row, row, acc, pl.BlockSpec((8, 128), lambda i: (0, 0))],
        out_shape=[jax.ShapeDtypeStruct((t, d), F32), jax.ShapeDtypeStruct((t, d), BF16),
                   jax.ShapeDtypeStruct((8, d), F32), jax.ShapeDtypeStruct((8, 128), F32)],
        scratch_shapes=[pltpu.VMEM((8, d), F32), pltpu.VMEM((8, d), F32)],
        compiler_params=_params(1),
    )(h, m, g_post.reshape(1, d), target)


def _norm_bwd_mid(dh, du, h_in, g_pre, m_prev, g_post_prev):
    t, d = dh.shape
    tr = _norm_tile(t)
    nsteps = t // tr

    def body(dh_ref, du_ref, h_ref, gpre_ref, m_ref, gpost_ref, dhn_ref, dm_ref, dgpre_ref, dgpost_ref, acc_a, acc_b):
        i = pl.program_id(0)

        @pl.when(i == 0)
        def _():
            acc_a[...] = jnp.zeros_like(acc_a)
            acc_b[...] = jnp.zeros_like(acc_b)

        du_t = du_ref[...]
        n, r = _rms(h_ref[...])
        dhn = dh_ref[...] + _rms_bwd(du_t, n, r, gpre_ref[...])
        dhn_ref[...] = dhn
        acc_a[...] += _fold8(du_t * n)
        n2, r2 = _rms(m_ref[...])
        dm_ref[...] = _rms_bwd(dhn, n2, r2, gpost_ref[...]).astype(BF16)
        acc_b[...] += _fold8(dhn * n2)

        @pl.when(i == nsteps - 1)
        def _():
            dgpre_ref[...] = _row0(acc_a[...])
            dgpost_ref[...] = _row0(acc_b[...])

    row = pl.BlockSpec((tr, d), lambda i: (i, 0))
    vec = pl.BlockSpec((1, d), lambda i: (0, 0))
    acc = pl.BlockSpec((8, d), lambda i: (0, 0))
    return pl.pallas_call(
        body, name="norm_bwd_mid", grid=(nsteps,), in_specs=[row, row, row, vec, row, vec],
        out_specs=[row, row, acc, acc],
        out_shape=[jax.ShapeDtypeStruct((t, d), F32), jax.ShapeDtypeStruct((t, d), BF16),
                   jax.ShapeDtypeStruct((8, d), F32), jax.ShapeDtypeStruct((8, d), F32)],
        scratch_shapes=[pltpu.VMEM((8, d), F32), pltpu.VMEM((8, d), F32)],
        compiler_params=_params(1),
    )(dh, du, h_in, g_pre.reshape(1, d), m_prev, g_post_prev.reshape(1, d))


def _norm_bwd_first(dh, du, x, g_pre, rider=None):
    t, d = dh.shape
    tr = _norm_tile(t)
    nsteps = t // tr

    def body(dh_ref, du_ref, x_ref, gpre_ref, dx_ref, dgpre_ref, acc_a):
        i = pl.program_id(0)

        @pl.when(i == 0)
        def _():
            acc_a[...] = jnp.zeros_like(acc_a)

        du_t = du_ref[...]
        n, r = _rms(x_ref[...])
        dx_ref[...] = dh_ref[...] + _rms_bwd(du_t, n, r, gpre_ref[...])
        acc_a[...] += _fold8(du_t * n)

        @pl.when(i == nsteps - 1)
        def _():
            dgpre_ref[...] = _row0(acc_a[...])

    row = pl.BlockSpec((tr, d), lambda i: (i, 0))
    vec = pl.BlockSpec((1, d), lambda i: (0, 0))
    acc = pl.BlockSpec((8, d), lambda i: (0, 0))
    return _compute_call(
        body, "norm_bwd_first", grid=(nsteps,), in_specs=[row, row, row, vec],
        operands=[dh, du, x, g_pre.reshape(1, d)], out_specs=[row, acc],
        out_shape=[jax.ShapeDtypeStruct((t, d), F32), jax.ShapeDtypeStruct((8, d), F32)],
        scratch=[pltpu.VMEM((8, d), F32)], rider=rider)


CONV_TC = 128
CONV_HALO = 16


def _conv_chunk(t):
    return min(512, t)


def _shift_down(v, steps, fill):
    rows = lax.broadcasted_iota(jnp.int32, v.shape, 0)
    out = pltpu.roll(v, steps, axis=0)
    for k in range(steps):
        out = jnp.where(rows == k, fill[CONV_HALO - steps + k:CONV_HALO - steps + k + 1, :], out)
    return out


def _shift_up(v, steps, fill):
    nrows = v.shape[0]
    rows = lax.broadcasted_iota(jnp.int32, v.shape, 0)
    out = pltpu.roll(v, nrows - steps, axis=0)
    for k in range(steps):
        out = jnp.where(rows == nrows - steps + k, fill[k:k + 1, :], out)
    return out


def _conv_fwd(proj, cw):
    _, t, d = proj.shape
    chunk = _conv_chunk(t)

    def body(p_ref, w_ref, o_ref):
        w = w_ref[...]
        w0, w1, w2 = w[0:1, :], w[1:2, :], w[2:3, :]
        for ci in range(t // chunk):
            t0 = ci * chunk
            rows = pl.ds(t0, chunk)
            b = p_ref[0, rows, :].astype(F32)
            cx = p_ref[1, rows, :].astype(F32) * p_ref[2, rows, :].astype(F32)
            z = p_ref[3, rows, :].astype(F32)
            if ci == 0:
                prev = jnp.zeros((CONV_HALO, CONV_TC), F32)
            else:
                halo = pl.ds(t0 - CONV_HALO, CONV_HALO)
                prev = p_ref[1, halo, :].astype(F32) * p_ref[2, halo, :].astype(F32)
            conv = w2 * cx + w1 * _shift_down(cx, 1, prev) + w0 * _shift_down(cx, 2, prev)
            o_ref[rows, :] = (z * _sigmoid(z) * b * conv).astype(BF16)

    return pl.pallas_call(
        body, name="conv_fwd", grid=(d // CONV_TC,),
        in_specs=[pl.BlockSpec((N_CHIPS, t, CONV_TC), lambda j: (0, 0, j)), pl.BlockSpec((8, CONV_TC), lambda j: (0, j))],
        out_specs=pl.BlockSpec((t, CONV_TC), lambda j: (0, j)),
        out_shape=jax.ShapeDtypeStruct((t, d), BF16), compiler_params=_params(1),
    )(proj, cw)


def _conv_bwd(proj, dgated, cw):
    _, t, d = proj.shape
    chunk = _conv_chunk(t)
    nchunks = t // chunk

    def body(p_ref, dg_ref, w_ref, dp_ref, dw_ref):
        w = w_ref[...]
        w0, w1, w2 = w[0:1, :], w[1:2, :], w[2:3, :]
        dw0 = jnp.zeros((1, CONV_TC), F32)
        dw1 = jnp.zeros((1, CONV_TC), F32)
        dw2 = jnp.zeros((1, CONV_TC), F32)
        for ci in range(nchunks):
            t0 = ci * chunk
            rows = pl.ds(t0, chunk)
            b = p_ref[0, rows, :].astype(F32)
            c = p_ref[1, rows, :].astype(F32)
            xt = p_ref[2, rows, :].astype(F32)
            z = p_ref[3, rows, :].astype(F32)
            dg = dg_ref[rows, :].astype(F32)
            cx = c * xt
            if ci == 0:
                prev = jnp.zeros((CONV_HALO, CONV_TC), F32)
            else:
                halo = pl.ds(t0 - CONV_HALO, CONV_HALO)
                prev = p_ref[1, halo, :].astype(F32) * p_ref[2, halo, :].astype(F32)
            cx1 = _shift_down(cx, 1, prev)
            cx2 = _shift_down(cx, 2, prev)
            conv = w2 * cx + w1 * cx1 + w0 * cx2
            sig = _sigmoid(z)
            dy = dg * (z * sig)
            dconv = dy * b
            if ci == nchunks - 1:
                nxt = jnp.zeros((CONV_HALO, CONV_TC), F32)
            else:
                halo = pl.ds(t0 + chunk, CONV_HALO)
                zn = p_ref[3, halo, :].astype(F32)
                nxt = dg_ref[halo, :].astype(F32) * (zn * _sigmoid(zn)) * p_ref[0, halo, :].astype(F32)
            dcx = w2 * dconv + w1 * _shift_up(dconv, 1, nxt) + w0 * _shift_up(dconv, 2, nxt)
            dp_ref[0, rows, :] = (dy * conv).astype(BF16)
            dp_ref[1, rows, :] = (dcx * xt).astype(BF16)
            dp_ref[2, rows, :] = (dcx * c).astype(BF16)
            dp_ref[3, rows, :] = (dg * (b * conv) * (sig * (1.0 + z * (1.0 - sig)))).astype(BF16)
            dw0 = dw0 + jnp.sum(dconv * cx2, axis=0, keepdims=True)
            dw1 = dw1 + jnp.sum(dconv * cx1, axis=0, keepdims=True)
            dw2 = dw2 + jnp.sum(dconv * cx, axis=0, keepdims=True)
        taps = lax.broadcasted_iota(jnp.int32, (8, CONV_TC), 0)
        dw_ref[...] = jnp.where(taps == 0, dw0, jnp.where(taps == 1, dw1, jnp.where(taps == 2, dw2, 0.0)))

    return pl.pallas_call(
        body, name="conv_bwd", grid=(d // CONV_TC,),
        in_specs=[pl.BlockSpec((N_CHIPS, t, CONV_TC), lambda j: (0, 0, j)),
                  pl.BlockSpec((t, CONV_TC), lambda j: (0, j)),
                  pl.BlockSpec((8, CONV_TC), lambda j: (0, j))],
        out_specs=[pl.BlockSpec((N_CHIPS, t, CONV_TC), lambda j: (0, 0, j)), pl.BlockSpec((8, CONV_TC), lambda j: (0, j))],
        out_shape=[jax.ShapeDtypeStruct((N_CHIPS, t, d), BF16), jax.ShapeDtypeStruct((8, d), F32)],
        compiler_params=_params(1),
    )(proj, dgated, cw)


SB_DEAD_TAIL = -105.0
SB_COUNT_LANE = HEAD_DIM - 1


def _sb_block(t):
    return min(256, t)


def _split_dot(v, tri):
    hi = v.astype(BF16)
    lo = (v - hi.astype(F32)).astype(BF16)
    return _dot(hi, tri, NN) + _dot(lo, tri, NN)


SB_HEADS_PER_STEP = 2


def _sb_terms(s, diagonal):
    sp = jnp.maximum(s, 0.0) + jnp.log1p(jnp.exp(-jnp.abs(s)))
    if not diagonal:
        return -sp, s - sp, sp, None
    mask = lax.broadcasted_iota(jnp.int32, s.shape, 1) < lax.broadcasted_iota(jnp.int32, s.shape, 0)
    return jnp.where(mask, -sp, 0.0), s - sp, sp, mask


def _masked(mask, v):
    return v if mask is None else jnp.where(mask, v, 0.0)


def _sb_fwd(proj, rider=None):
    _, t, d = proj.shape
    heads = d // HEAD_DIM
    blk = _sb_block(t)
    nblk = t // blk
    scale = 1.0 / math.sqrt(HEAD_DIM)

    hps = SB_HEADS_PER_STEP
    width = hps * HEAD_DIM

    def body(q_ref, k_ref, v_ref, z_ref, gated_ref, o_ref, car_ref, tail_ref, acc_ref):
        i = pl.program_id(1)
        r_i = lax.broadcasted_iota(jnp.int32, (blk, blk), 0)
        c_i = lax.broadcasted_iota(jnp.int32, (blk, blk), 1)
        tri_after = (r_i > c_i).astype(BF16)
        lanes = lax.broadcasted_iota(jnp.int32, (blk, HEAD_DIM), 1)

        tail_ref[...] = jnp.zeros_like(tail_ref)
        acc_ref[...] = jnp.zeros_like(acc_ref)
        car_ref[...] = jnp.zeros_like(car_ref)

        def visit(j, diagonal):
            krows = pl.ds(pl.multiple_of(j * blk, blk), blk)
            hcols = [pl.ds(hh * HEAD_DIM, HEAD_DIM) for hh in range(hps)]
            logits = [_dot(q_ref[:, c], k_ref[krows, c], NT) for c in hcols]
            terms = [_sb_terms(s * scale, diagonal) for s in logits]
            within = [_split_dot(keep, tri_after) for keep, _, _, _ in terms]
            top = None
            for hh, (keep, log_beta, _, mask) in enumerate(terms):
                tail_b = tail_ref[hh]
                w = _masked(mask, jnp.exp(log_beta + tail_b[:, 0:1] + within[hh]))
                acc_ref[hh] += _dot(w.astype(BF16), v_ref[krows, hcols[hh]], NN)
                car_ref[hh] = jnp.where(lanes == j, tail_b, car_ref[hh])
                tail_new = tail_b + jnp.sum(keep, axis=1, keepdims=True)
                tail_ref[hh] = tail_new
                top = jnp.max(tail_new) if top is None else jnp.maximum(top, jnp.max(tail_new))
            return top > SB_DEAD_TAIL

        def more(state):
            jj, live = state
            return jnp.logical_and(jj <= i, live)

        def step(state):
            jj, _ = state
            return jj + 1, visit(i - jj, False)

        visited, _ = lax.while_loop(more, step, (jnp.int32(1), visit(i, True)))
        for hh in range(hps):
            cols = pl.ds(hh * HEAD_DIM, HEAD_DIM)
            car_ref[hh] = jnp.where(lanes == SB_COUNT_LANE, visited.astype(F32), car_ref[hh])
            z = z_ref[:, cols].astype(F32)
            acc = acc_ref[hh]
            o_ref[:, cols] = acc.astype(BF16)
            gated_ref[:, cols] = (z * _sigmoid(z) * acc).astype(BF16)

    qspec = lambda s: pl.BlockSpec((None, blk, width), lambda h, i: (s, i, h))
    kspec = lambda s: pl.BlockSpec((None, t, width), lambda h, i: (s, 0, h))
    ospec = pl.BlockSpec((blk, width), lambda h, i: (i, h))
    return _compute_call(
        body, "sb_fwd", grid=(heads // hps, nblk),
        in_specs=[qspec(0), kspec(1), kspec(2), qspec(3)], operands=[proj, proj, proj, proj],
        out_specs=[ospec, ospec, pl.BlockSpec((hps, blk, HEAD_DIM), lambda h, i: (h, i, 0))],
        out_shape=[jax.ShapeDtypeStruct((t, d), BF16), jax.ShapeDtypeStruct((t, d), BF16),
                   jax.ShapeDtypeStruct((heads, t, HEAD_DIM), F32)],
        scratch=[pltpu.VMEM((hps, blk, HEAD_DIM), F32), pltpu.VMEM((hps, blk, HEAD_DIM), F32)], rider=rider)


def _sb_bwd(proj, dgated, o, car, rider=None):
    _, t, d = proj.shape
    heads = d // HEAD_DIM
    blk = _sb_block(t)
    nblk = t // blk
    scale = 1.0 / math.sqrt(HEAD_DIM)

    hps = SB_HEADS_PER_STEP
    width = hps * HEAD_DIM

    def body(q_ref, k_ref, v_ref, z_ref, dg_ref, o_ref, car_ref, dp_ref, dk_acc, dv_acc, gsum_ref, dq_ref, do_ref):
        step_i = pl.program_id(1)
        i = nblk - 1 - step_i

        @pl.when(step_i == 0)
        def _():
            dk_acc[...] = jnp.zeros_like(dk_acc)
            dv_acc[...] = jnp.zeros_like(dv_acc)

        r_i = lax.broadcasted_iota(jnp.int32, (blk, blk), 0)
        c_i = lax.broadcasted_iota(jnp.int32, (blk, blk), 1)
        tri_after = (r_i > c_i).astype(BF16)
        tri_before = (r_i < c_i).astype(BF16)
        lanes = lax.broadcasted_iota(jnp.int32, (blk, HEAD_DIM), 1)

        gsum_ref[...] = jnp.zeros_like(gsum_ref)
        dq_ref[...] = jnp.zeros_like(dq_ref)
        for hh in range(hps):
            cols = pl.ds(hh * HEAD_DIM, HEAD_DIM)
            z = z_ref[:, cols].astype(F32)
            dg = dg_ref[:, cols].astype(F32)
            sig = _sigmoid(z)
            do_ref[hh] = (dg * (z * sig)).astype(BF16)
            dp_ref[3, :, cols] = (dg * o_ref[:, cols].astype(F32) * (sig * (1.0 + z * (1.0 - sig)))).astype(BF16)

        def visit(j, diagonal):
            krows = pl.ds(pl.multiple_of(j * blk, blk), blk)
            hcols = [pl.ds(hh * HEAD_DIM, HEAD_DIM) for hh in range(hps)]
            logits = [_dot(q_ref[:, c], k_ref[krows, c], NT) for c in hcols]
            dws = [_dot(do_ref[hh], v_ref[krows, c], NT) for hh, c in enumerate(hcols)]
            terms = [_sb_terms(s * scale, diagonal) for s in logits]
            within = [_split_dot(keep, tri_after) for keep, _, _, _ in terms]
            ws, gs = [], []
            for hh, (keep, log_beta, sp, mask) in enumerate(terms):
                tail = jnp.sum(jnp.where(lanes == j, car_ref[hh], 0.0), axis=1, keepdims=True)
                w = _masked(mask, jnp.exp(log_beta + tail + within[hh]))
                ws.append(w.astype(BF16))
                gs.append(w * dws[hh])
            g_within = [_split_dot(g, tri_before) for g in gs]
            for hh, (keep, log_beta, sp, mask) in enumerate(terms):
                c = hcols[hh]
                g_before = gsum_ref[hh]
                g_cum = g_before[:, 0:1] + g_within[hh]
                dl = (_masked(mask, gs[hh] * jnp.exp(-sp) - g_cum * jnp.exp(log_beta)) * scale).astype(BF16)
                dq_ref[hh] += _dot(dl, k_ref[krows, c], NN)
                dk_acc[krows, c] += _dot(dl, q_ref[:, c], TN)
                dv_acc[krows, c] += _dot(ws[hh], do_ref[hh], TN)
                gsum_ref[hh] = g_before + jnp.sum(gs[hh], axis=1, keepdims=True)

        def step(j, carry):
            visit(j, False)
            return carry

        visited = jnp.max(jnp.where(lanes == SB_COUNT_LANE, car_ref[0], 0.0)).astype(jnp.int32)
        lax.fori_loop(i + 1 - visited, i, step, 0)
        visit(i, True)
        own = pl.ds(pl.multiple_of(i * blk, blk), blk)
        for hh in range(hps):
            cols = pl.ds(hh * HEAD_DIM, HEAD_DIM)
            dp_ref[0, :, cols] = dq_ref[hh].astype(BF16)
        dp_ref[1] = dk_acc[own, :].astype(BF16)
        dp_ref[2] = dv_acc[own, :].astype(BF16)

    qspec = lambda s: pl.BlockSpec((None, blk, width), lambda h, i: (s, nblk - 1 - i, h))
    kspec = lambda s: pl.BlockSpec((None, t, width), lambda h, i: (s, 0, h))
    tspec = pl.BlockSpec((blk, width), lambda h, i: (nblk - 1 - i, h))
    return _compute_call(
        body, "sb_bwd", grid=(heads // hps, nblk),
        in_specs=[qspec(0), kspec(1), kspec(2), qspec(3), tspec, tspec,
                  pl.BlockSpec((hps, blk, HEAD_DIM), lambda h, i: (h, nblk - 1 - i, 0))],
        operands=[proj, proj, proj, proj, dgated, o, car],
        out_specs=[pl.BlockSpec((N_CHIPS, blk, width), lambda h, i: (0, nblk - 1 - i, h))],
        out_shape=[jax.ShapeDtypeStruct((N_CHIPS, t, d), BF16)],
        scratch=[pltpu.VMEM((t, width), F32), pltpu.VMEM((t, width), F32),
                 pltpu.VMEM((hps, blk, HEAD_DIM), F32), pltpu.VMEM((hps, blk, HEAD_DIM), F32),
                 pltpu.VMEM((hps, blk, HEAD_DIM), BF16)], rider=rider)


def _pack_weights(w_in, w_out, chip):
    d = w_in.shape[0]
    rb = d // 8
    n_in = d // rb
    n_out = w_out.shape[0] // rb

    def body(chip_ref, wi_ref, wo_ref, o_ref):
        r = pl.program_id(0)

        @pl.when(r < n_in)
        def _():
            o_ref[...] = wi_ref[...].astype(BF16)

        @pl.when(r >= n_in)
        def _():
            o_ref[...] = wo_ref[...].astype(BF16)

    grid_spec = pltpu.PrefetchScalarGridSpec(
        num_scalar_prefetch=1, grid=(n_in + n_out,),
        in_specs=[pl.BlockSpec((rb, d), lambda r, me: (jnp.minimum(r, n_in - 1), 0)),
                  pl.BlockSpec((rb, d), lambda r, me: (jnp.maximum(r - n_in, 0), 0))],
        out_specs=pl.BlockSpec((None, rb, d), lambda r, me: (me[0], r, 0)))
    return pl.pallas_call(
        body, name="pack_weights", grid_spec=grid_spec,
        out_shape=jax.ShapeDtypeStruct((N_CHIPS, d + w_out.shape[0], d), BF16), compiler_params=_params(1),
    )(chip, w_in, w_out)


def _flip(v, bit):
    return 1 - v if bit else v


def _remote(src, dst, send_sem, recv_sem, target):
    return pltpu.make_async_remote_copy(src_ref=src, dst_ref=dst, send_sem=send_sem, recv_sem=recv_sem,
                                        device_id=target, device_id_type=MESH_IDS)


AG_CHUNKS = 4
AG_EARLY = range(0, 3)
AG_LATE = range(3, AG_CHUNKS)
HALF_CHUNKS = 8


SWAP_CHUNKS = 2


def _other_chips(x, y):
    return [(_flip(x, k >> 1), _flip(y, k & 1)) for k in (1, 2, 3)]


def _gather_sems(n):
    return [pltpu.SemaphoreType.DMA((3, n * AG_CHUNKS)) for _ in range(4)]


def _gather_pieces(g, chunks):
    hr = g[0].shape[1] // 2
    cr = hr // AG_CHUNKS
    return hr, [(l * AG_CHUNKS + q, g[l], q * cr, cr) for l in range(len(g)) for q in chunks]


def _gather_start(g, send, recv, chunks=range(AG_CHUNKS)):
    x, y, c = lax.axis_index("x"), lax.axis_index("y"), lax.axis_index("c")
    hr, pieces = _gather_pieces(g, chunks)
    for k, (px, py) in enumerate(_other_chips(x, y)):
        for i, ref, r0, cr in pieces:
            piece = ref.at[2 * x + y, pl.ds(c * hr + r0, cr)]
            _remote(piece, piece, send.at[k, i], recv.at[k, i], (px, py, c)).start()


def _gather_finish(g, send, recv, fsend, frecv, chunks=range(AG_CHUNKS)):
    x, y, c = lax.axis_index("x"), lax.axis_index("y"), lax.axis_index("c")
    sibling = (x, y, 1 - c)
    hr, pieces = _gather_pieces(g, chunks)
    chips = _other_chips(x, y)
    for k, (px, py) in enumerate(chips):
        for i, ref, r0, cr in pieces:
            landed = ref.at[2 * px + py, pl.ds(c * hr + r0, cr)]
            _remote(landed, landed, send.at[k, i], recv.at[k, i], (px, py, c)).wait_recv()
            _remote(landed, landed, fsend.at[k, i], frecv.at[k, i], sibling).start()
    for k, (px, py) in enumerate(chips):
        for i, ref, r0, cr in pieces:
            theirs = ref.at[2 * px + py, pl.ds((1 - c) * hr + r0, cr)]
            _remote(theirs, theirs, fsend.at[k, i], frecv.at[k, i], sibling).wait_recv()
    for k, (px, py) in enumerate(chips):
        for i, ref, r0, cr in pieces:
            mine = ref.at[2 * x + y, pl.ds(c * hr + r0, cr)]
            _remote(mine, mine, send.at[k, i], recv.at[k, i], (px, py, c)).wait_send()
            landed = ref.at[2 * px + py, pl.ds(c * hr + r0, cr)]
            _remote(landed, landed, fsend.at[k, i], frecv.at[k, i], sibling).wait_send()


def _all_gather_weights(packs, cw):
    n = len(packs)

    def body(*refs):
        cw_ref = refs[n]
        g = refs[n + 1:2 * n + 1]
        cwg = refs[2 * n + 1]
        send, recv, fsend, frecv, csend, crecv, lsem = refs[2 * n + 2:]
        x, y, c = lax.axis_index("x"), lax.axis_index("y"), lax.axis_index("c")
        me = 2 * x + y
        local = pltpu.make_async_copy(cw_ref, cwg.at[me], lsem)
        local.start()
        taps = [_remote(cw_ref, cwg.at[me], csend.at[k], crecv.at[k], (px, py, c))
                for k, (px, py) in enumerate(_other_chips(x, y))]
        for cp in taps:
            cp.start()
        _gather_start(g, send, recv)
        _gather_finish(g, send, recv, fsend, frecv)
        for k, (px, py) in enumerate(_other_chips(x, y)):
            _remote(cw_ref, cwg.at[2 * px + py], csend.at[k], crecv.at[k], (px, py, c)).wait_recv()
        for cp in taps:
            cp.wait_send()
        local.wait()

    out_shape = [jax.ShapeDtypeStruct(p.shape, BF16) for p in packs]
    out_shape.append(jax.ShapeDtypeStruct((N_CHIPS,) + cw.shape, cw.dtype))
    return pl.pallas_call(
        body, name="all_gather_weights", in_specs=[HBM_SPEC] * (n + 1), out_specs=[HBM_SPEC] * (n + 1),
        out_shape=out_shape, input_output_aliases={l: l for l in range(n)},
        scratch_shapes=_gather_sems(n) + [pltpu.SemaphoreType.DMA((3,)), pltpu.SemaphoreType.DMA((3,)),
                                          pltpu.SemaphoreType.DMA],
    )(*packs, cw)


def _gather_rider(packs, chunks=range(AG_CHUNKS)):
    return _Rider(
        operands=packs, out_shapes=[jax.ShapeDtypeStruct(p.shape, BF16) for p in packs],
        aliases={l: l for l in range(len(packs))}, sems=_gather_sems(len(packs)),
        start=lambda ins, outs, sems: _gather_start(outs, sems[0], sems[1], chunks),
        finish=lambda ins, outs, sems: _gather_finish(outs, *sems, chunks))


def _join_riders(a, b):
    na, oa, sa = len(a.operands), len(a.out_shapes), len(a.sems)
    aliases = dict(a.aliases)
    aliases.update({na + i: oa + o for i, o in b.aliases.items()})
    return _Rider(
        operands=list(a.operands) + list(b.operands), out_shapes=list(a.out_shapes) + list(b.out_shapes),
        aliases=aliases, sems=list(a.sems) + list(b.sems),
        start=lambda ins, outs, sems: (a.start(ins[:na], outs[:oa], sems[:sa]),
                                       b.start(ins[na:], outs[oa:], sems[sa:])),
        finish=lambda ins, outs, sems: (a.finish(ins[:na], outs[:oa], sems[:sa]),
                                        b.finish(ins[na:], outs[oa:], sems[sa:])))


def _exchange_small(small):
    def body(sm, smg, ssend, srecv, lsem):
        x, y, c = lax.axis_index("x"), lax.axis_index("y"), lax.axis_index("c")
        me = 4 * x + 2 * y + c
        peers = [(_flip(x, r >> 2), _flip(y, (r >> 1) & 1), _flip(c, r & 1)) for r in range(1, N_DEV)]
        local = pltpu.make_async_copy(sm, smg.at[me], lsem)
        local.start()
        smalls = [_remote(sm, smg.at[me], ssend.at[r], srecv.at[r], peer) for r, peer in enumerate(peers)]
        for cp in smalls:
            cp.start()
        for r, (tx, ty, tc) in enumerate(peers):
            _remote(sm, smg.at[4 * tx + 2 * ty + tc], ssend.at[r], srecv.at[r], (tx, ty, tc)).wait_recv()
        for cp in smalls:
            cp.wait_send()
        local.wait()

    return pl.pallas_call(
        body, name="exchange_small", in_specs=[HBM_SPEC], out_specs=HBM_SPEC,
        out_shape=jax.ShapeDtypeStruct((N_DEV,) + small.shape, small.dtype),
        scratch_shapes=[pltpu.SemaphoreType.DMA((N_DEV - 1,)), pltpu.SemaphoreType.DMA((N_DEV - 1,)),
                        pltpu.SemaphoreType.DMA],
    )(small)


def _send_sums_start(s_ref, got, send, recv, part=0, parts=1):
    x, y, c = lax.axis_index("x"), lax.axis_index("y"), lax.axis_index("c")
    nrows = s_ref.shape[1] // parts
    rows = pl.ds(part * nrows, nrows)
    for k, (px, py) in enumerate(_other_chips(x, y)):
        _remote(s_ref.at[2 * px + py, rows], got.at[k, rows], send.at[k], recv.at[k], (px, py, c)).start()


def _send_sums_finish(s_ref, got, send, recv, part=0, parts=1):
    x, y, c = lax.axis_index("x"), lax.axis_index("y"), lax.axis_index("c")
    nrows = s_ref.shape[1] // parts
    rows = pl.ds(part * nrows, nrows)
    for k, (px, py) in enumerate(_other_chips(x, y)):
        _remote(got.at[k, rows], got.at[k, rows], send.at[k], recv.at[k], (px, py, c)).wait_recv()
    for k, (px, py) in enumerate(_other_chips(x, y)):
        _remote(s_ref.at[2 * px + py, rows], got.at[k, rows], send.at[k], recv.at[k], (px, py, c)).wait_send()


def _send_sums_rider(sums, got=None, part=0, parts=1):
    _, hr, d = sums.shape
    return _Rider(
        operands=[sums] if got is None else [sums, got],
        out_shapes=[jax.ShapeDtypeStruct((N_CHIPS - 1, hr, d), BF16)], aliases={} if got is None else {1: 0},
        sems=[pltpu.SemaphoreType.DMA((3,)), pltpu.SemaphoreType.DMA((3,))],
        start=lambda ins, outs, sems: _send_sums_start(ins[0], outs[0], *sems, part, parts),
        finish=lambda ins, outs, sems: _send_sums_finish(ins[0], outs[0], *sems, part, parts))


def _swap_pieces(gp_ref, x_ref, c):
    hr = x_ref.shape[1]
    cr = hr // SWAP_CHUNKS
    return [(a * SWAP_CHUNKS + q, gp_ref.at[a, pl.ds((1 - c) * hr + q * cr, cr)], x_ref.at[a, pl.ds(q * cr, cr)])
            for a in range(N_CHIPS) for q in range(SWAP_CHUNKS)]


def _swap_rider(gp):
    _, p_rows, d = gp.shape

    def start(ins, outs, sems):
        x, y, c = lax.axis_index("x"), lax.axis_index("y"), lax.axis_index("c")
        for i, src, dst in _swap_pieces(ins[0], outs[0], c):
            _remote(src, dst, sems[0].at[i], sems[1].at[i], (x, y, 1 - c)).start()

    def finish(ins, outs, sems):
        x, y, c = lax.axis_index("x"), lax.axis_index("y"), lax.axis_index("c")
        pieces = _swap_pieces(ins[0], outs[0], c)
        for i, src, dst in pieces:
            _remote(dst, dst, sems[0].at[i], sems[1].at[i], (x, y, 1 - c)).wait_recv()
        for i, src, dst in pieces:
            _remote(src, dst, sems[0].at[i], sems[1].at[i], (x, y, 1 - c)).wait_send()

    nsem = N_CHIPS * SWAP_CHUNKS
    return _Rider(
        operands=[gp], out_shapes=[jax.ShapeDtypeStruct((N_CHIPS, p_rows // 2, d), BF16)], aliases={},
        sems=[pltpu.SemaphoreType.DMA((nsem,)), pltpu.SemaphoreType.DMA((nsem,))], start=start, finish=finish)


def _presum(gp, theirs, core):
    _, hr, d = theirs.shape
    tr = _row_tile(hr, 640)
    steps = hr // tr

    def body(core_ref, mine_ref, theirs_ref, o_ref):
        o_ref[...] = (mine_ref[...].astype(F32) + theirs_ref[...].astype(F32)).astype(BF16)

    grid_spec = pltpu.PrefetchScalarGridSpec(
        num_scalar_prefetch=1, grid=(N_CHIPS, steps),
        in_specs=[pl.BlockSpec((None, tr, d), lambda a, i, cr: (a, cr[0] * steps + i, 0)),
                  pl.BlockSpec((None, tr, d), lambda a, i, cr: (a, i, 0))],
        out_specs=pl.BlockSpec((None, tr, d), lambda a, i, cr: (a, i, 0)))
    return pl.pallas_call(
        body, name="presum", grid_spec=grid_spec,
        out_shape=jax.ShapeDtypeStruct((N_CHIPS, hr, d), BF16), compiler_params=_params(2),
    )(core, gp, theirs)


def _row_tile(rows, cap=128):
    if rows <= cap:
        return rows
    return next(tr for tr in range(cap, 0, -16) if rows % tr == 0)


def _sum_sources(name, parts):
    nsrc, rows, cols = parts.shape
    tr = _row_tile(rows)

    def body(p_ref, o_ref):
        total = p_ref[0].astype(F32)
        for s in range(1, nsrc):
            total = total + p_ref[s].astype(F32)
        o_ref[...] = total

    return pl.pallas_call(
        body, name=name, grid=(rows // tr,),
        in_specs=[pl.BlockSpec((nsrc, tr, cols), lambda i: (0, i, 0))],
        out_specs=pl.BlockSpec((tr, cols), lambda i: (i, 0)),
        out_shape=jax.ShapeDtypeStruct((rows, cols), F32), compiler_params=_params(1),
    )(parts)


def _sum_grad_half(got, sums, place):
    nsrc, hr, d = got.shape
    tr = _row_tile(hr, 256)

    def body(place_ref, got_ref, own_ref, o_ref):
        total = own_ref[...].astype(F32)
        for s in range(nsrc):
            total = total + got_ref[s].astype(F32)
        o_ref[...] = total

    grid_spec = pltpu.PrefetchScalarGridSpec(
        num_scalar_prefetch=1, grid=(hr // tr,),
        in_specs=[pl.BlockSpec((nsrc, tr, d), lambda i, pc: (0, i, 0)),
                  pl.BlockSpec((None, tr, d), lambda i, pc: (pc[0], i, 0))],
        out_specs=pl.BlockSpec((None, tr, d), lambda i, pc: (pc[1], i, 0)))
    return pl.pallas_call(
        body, name="sum_grad_half", grid_spec=grid_spec,
        out_shape=jax.ShapeDtypeStruct((2, hr, d), F32), compiler_params=_params(1),
    )(place, got, sums)


def _exchange_halves(fulls):
    n = len(fulls)
    _, hr, d = fulls[0].shape
    cr = hr // HALF_CHUNKS
    nc = n * HALF_CHUNKS

    def body(*refs):
        full = refs[n:2 * n]
        send, recv = refs[2 * n:]
        x, y, c = lax.axis_index("x"), lax.axis_index("y"), lax.axis_index("c")
        sibling = (x, y, 1 - c)
        pieces = [(l, q) for l in range(n) for q in range(HALF_CHUNKS)]
        sends = []
        for i, (l, q) in enumerate(pieces):
            piece = full[l].at[c, pl.ds(q * cr, cr)]
            sends.append(_remote(piece, piece, send.at[i], recv.at[i], sibling))
        for cp in sends:
            cp.start()
        for i, (l, q) in enumerate(pieces):
            theirs = full[l].at[1 - c, pl.ds(q * cr, cr)]
            _remote(theirs, theirs, send.at[i], recv.at[i], sibling).wait_recv()
        for cp in sends:
            cp.wait_send()

    return pl.pallas_call(
        body, name="exchange_halves", in_specs=[HBM_SPEC] * n, out_specs=[HBM_SPEC] * n,
        out_shape=[jax.ShapeDtypeStruct(f.shape, F32) for f in fulls], input_output_aliases={l: l for l in range(n)},
        scratch_shapes=[pltpu.SemaphoreType.DMA((nc,)), pltpu.SemaphoreType.DMA((nc,))],
    )(*fulls)


def _adamw(name, w, m, v, g, g_row0=0, rider=None):
    rows, cols = w.shape
    tr = _row_tile(rows, 256)
    off = g_row0 // tr

    def body(w_ref, m_ref, v_ref, g_ref, go_ref, d_ref, mo_ref, vo_ref):
        grad = g_ref[...]
        m_new = ADAM_B1 * m_ref[...] + (1.0 - ADAM_B1) * grad
        v_new = ADAM_B2 * v_ref[...] + (1.0 - ADAM_B2) * (grad * grad)
        m_hat = m_new / (1.0 - ADAM_B1 ** ADAM_STEP)
        v_hat = v_new / (1.0 - ADAM_B2 ** ADAM_STEP)
        go_ref[...] = grad
        d_ref[...] = -ADAM_LR * (m_hat / (jnp.sqrt(v_hat) + ADAM_EPS) + ADAM_WD * w_ref[...])
        mo_ref[...] = m_new
        vo_ref[...] = v_new

    blk = pl.BlockSpec((tr, cols), lambda i: (i, 0))
    return _compute_call(
        body, name, grid=(rows // tr,),
        in_specs=[blk, blk, blk, pl.BlockSpec((tr, cols), lambda i: (i + off, 0))], operands=[w, m, v, g],
        out_specs=[blk, blk, blk, blk], out_shape=[jax.ShapeDtypeStruct((rows, cols), F32)] * 4, rider=rider)


def _pad_rows8(a):
    return jnp.concatenate([a, jnp.zeros((8 - a.shape[0],) + a.shape[1:], a.dtype)], axis=0)


def kernel(x, ln_pre_0, conv_w_in_0, conv_w_0, conv_w_out_0, ln_post_0, ln_pre_1, sb_w_in_1, sb_w_out_1, ln_post_1, ln_pre_2, conv_w_in_2, conv_w_2, conv_w_out_2, ln_post_2, ln_pre_3, sb_w_in_3, sb_w_out_3, ln_post_3, loss_target, m_ln_pre_0, m_conv_w_in_0, m_conv_w_0, m_conv_w_out_0, m_ln_post_0, m_ln_pre_1, m_sb_w_in_1, m_sb_w_out_1, m_ln_post_1, m_ln_pre_2, m_conv_w_in_2, m_conv_w_2, m_conv_w_out_2, m_ln_post_2, m_ln_pre_3, m_sb_w_in_3, m_sb_w_out_3, m_ln_post_3, v_ln_pre_0, v_conv_w_in_0, v_conv_w_0, v_conv_w_out_0, v_ln_post_0, v_ln_pre_1, v_sb_w_in_1, v_sb_w_out_1, v_ln_post_1, v_ln_pre_2, v_conv_w_in_2, v_conv_w_2, v_conv_w_out_2, v_ln_post_2, v_ln_pre_3, v_sb_w_in_3, v_sb_w_out_3, v_ln_post_3):
    t, d = x.shape[1], x.shape[2]
    dq = d // N_CHIPS
    xs = x.reshape(t, d)
    target = loss_target.reshape(t, d)
    w_in = [conv_w_in_0, sb_w_in_1, conv_w_in_2, sb_w_in_3]
    w_out = [conv_w_out_0, sb_w_out_1, conv_w_out_2, sb_w_out_3]
    m_in = [m_conv_w_in_0, m_sb_w_in_1, m_conv_w_in_2, m_sb_w_in_3]
    m_out = [m_conv_w_out_0, m_sb_w_out_1, m_conv_w_out_2, m_sb_w_out_3]
    v_in = [v_conv_w_in_0, v_sb_w_in_1, v_conv_w_in_2, v_sb_w_in_3]
    v_out = [v_conv_w_out_0, v_sb_w_out_1, v_conv_w_out_2, v_sb_w_out_3]
    ln_pre = [ln_pre_0, ln_pre_1, ln_pre_2, ln_pre_3]
    ln_post = [ln_post_0, ln_post_1, ln_post_2, ln_post_3]
    conv_w = [conv_w_0, conv_w_2]
    m_conv = [m_conv_w_0, m_conv_w_2]
    v_conv = [v_conv_w_0, v_conv_w_2]
    chip = 2 * lax.axis_index("x") + lax.axis_index("y")
    chip_arr = jnp.reshape(chip, (1,)).astype(jnp.int32)
    place = jnp.stack([chip, lax.axis_index("c")]).astype(jnp.int32)
    core_arr = jnp.reshape(lax.axis_index("c"), (1,)).astype(jnp.int32)

    packs = [_pack_weights(w_in[l], w_out[l], chip_arr) for l in range(N_LAYERS)]
    cw_local = jnp.concatenate([_pad_rows8(conv_w[0]), _pad_rows8(conv_w[1])], axis=0)
    gathered = list(packs)
    gathered[0], cw_all = _all_gather_weights(packs[:1], cw_local)
    cw_full = jnp.transpose(cw_all, (1, 0, 2)).reshape(16, d)
    conv_taps = {0: cw_full[0:8], 2: cw_full[8:16]}

    h_in, us, projs, gateds, ms, sb_saved = [], [], [], [], [], {}
    h = xs
    u = _norm_first(xs, ln_pre[0])
    for l in range(N_LAYERS):
        h_in.append(h)
        us.append(u)
        nxt = l + 1
        if l % 2 == 0 and nxt < N_LAYERS:
            proj, gathered[nxt] = _mm_proj(u, gathered[l], rider=_gather_rider(gathered[nxt:nxt + 1], AG_EARLY))
        else:
            proj, = _mm_proj(u, gathered[l])
        if l % 2 == 0:
            gated = _conv_fwd(proj, conv_taps[l])
        elif nxt < N_LAYERS:
            gated, o, car, gathered[nxt] = _sb_fwd(proj, rider=_gather_rider(gathered[nxt:nxt + 1]))
            sb_saved[l] = (o, car)
        else:
            gated, o, car = _sb_fwd(proj)
            sb_saved[l] = (o, car)
        if l % 2 == 0 and nxt < N_LAYERS:
            m, gathered[nxt] = _mm_out(gated, gathered[l], rider=_gather_rider(gathered[nxt:nxt + 1], AG_LATE))
        else:
            m, = _mm_out(gated, gathered[l])
        projs.append(proj)
        gateds.append(gated)
        ms.append(m)
        if l < N_LAYERS - 1:
            h, u = _norm_mid(h, m, ln_post[l], ln_pre[l + 1])
    dh, dm, dg_post_last, loss_part = _norm_last(h, ms[-1], ln_post[-1], target)

    dg_pre = [None] * N_LAYERS
    dg_post = [None] * N_LAYERS
    dg_post[N_LAYERS - 1] = dg_post_last
    dconv = {}
    sums = [None] * N_LAYERS
    got = [None] * N_LAYERS
    grad_x = None
    for l in reversed(range(N_LAYERS)):
        above = l + 1 if l + 1 < N_LAYERS else None
        dgated = _mm_dgated(dm, gathered[l])
        gp = _mm_dwout(gateds[l], dm)
        if l % 2 == 0:
            dproj, dconv[l] = _conv_bwd(projs[l], dgated, conv_taps[l])
        elif above is not None:
            o, car = sb_saved[l]
            dproj, got[above] = _sb_bwd(projs[l], dgated, o, car, rider=_send_sums_rider(sums[above]))
        else:
            o, car = sb_saved[l]
            dproj, = _sb_bwd(projs[l], dgated, o, car)
        if l % 2 == 0 and above is not None:
            gp, got[above] = _mm_dwin(us[l], dproj, gp, rider=_send_sums_rider(sums[above], None, 0, 2))
            du, theirs, got[above] = _mm_du(
                dproj, gathered[l], rider=_join_riders(_swap_rider(gp), _send_sums_rider(sums[above], got[above], 1, 2)))
        else:
            gp, = _mm_dwin(us[l], dproj, gp)
            du, theirs = _mm_du(dproj, gathered[l], rider=_swap_rider(gp))
        sums[l] = _presum(gp, theirs, core_arr)
        if l > 0:
            dh, dm, dg_pre[l], dg_post[l - 1] = _norm_bwd_mid(dh, du, h_in[l], ln_pre[l], ms[l - 1], ln_post[l - 1])

    upper = list(range(1, N_LAYERS))
    fulls = [None] * N_LAYERS
    halves = [_sum_grad_half(got[l], sums[l], place) for l in upper]
    for l, f in zip(upper, _exchange_halves(halves)):
        fulls[l] = f.reshape(d + dq, d)
    res_in = [None] * N_LAYERS
    parts = len(upper) + 1
    for part, l in enumerate(reversed(upper)):
        *res_in[l], got[0] = _adamw("adamw_w_in", w_in[l], m_in[l], v_in[l], fulls[l], 0,
                                    rider=_send_sums_rider(sums[0], got[0], part, parts))
    grad_x, dg_pre[0], got[0] = _norm_bwd_first(dh, du, h_in[0], ln_pre[0],
                                                rider=_send_sums_rider(sums[0], got[0], parts - 1, parts))
    fulls[0] = _exchange_halves([_sum_grad_half(got[0], sums[0], place)])[0].reshape(d + dq, d)
    res_in[0] = _adamw("adamw_w_in", w_in[0], m_in[0], v_in[0], fulls[0], 0)
    res_out = [_adamw("adamw_w_out", w_out[l], m_out[l], v_out[l], fulls[l], d) for l in range(N_LAYERS)]

    loss_rows = jnp.pad(loss_part, ((0, 0), (0, d - loss_part.shape[1])))
    small = jnp.concatenate(dg_pre + dg_post + [dconv[0], dconv[2], loss_rows], axis=0)
    small_sum = _sum_sources("sum_small", _exchange_small(small))
    ln_all = ln_pre + ln_post
    ln_m = [m_ln_pre_0, m_ln_pre_1, m_ln_pre_2, m_ln_pre_3, m_ln_post_0, m_ln_post_1, m_ln_post_2, m_ln_post_3]
    ln_v = [v_ln_pre_0, v_ln_pre_1, v_ln_pre_2, v_ln_pre_3, v_ln_post_0, v_ln_post_1, v_ln_post_2, v_ln_post_3]
    ln_g = jnp.concatenate([small_sum[8 * i:8 * i + 1] for i in range(2 * N_LAYERS)], axis=0)
    res_ln = _adamw("adamw_ln", jnp.stack(ln_all), jnp.stack(ln_m), jnp.stack(ln_v), ln_g)
    conv_g = []
    for i in range(2):
        rows = small_sum[8 * (2 * N_LAYERS + i):8 * (2 * N_LAYERS + i + 1)]
        conv_g.append(lax.dynamic_slice(rows, (0, chip * dq), (8, dq)))
    stack8 = lambda a, b: jnp.concatenate([_pad_rows8(a), _pad_rows8(b)], axis=0)
    res_conv = _adamw("adamw_conv", stack8(*conv_w), stack8(*m_conv), stack8(*v_conv), jnp.concatenate(conv_g, axis=0))

    def leaf(kind, l, which):
        if kind == "ln_pre":
            return res_ln[which][l]
        if kind == "ln_post":
            return res_ln[which][N_LAYERS + l]
        if kind == "w_in":
            return res_in[l][which]
        if kind == "w_out":
            return res_out[l][which]
        return res_conv[which][8 * (l // 2):8 * (l // 2) + 3]

    order = []
    for l in range(N_LAYERS):
        order.append(("ln_pre", l))
        order.append(("w_in", l))
        if l % 2 == 0:
            order.append(("conv", l))
        order.append(("w_out", l))
        order.append(("ln_post", l))
    loss = small_sum[8 * (2 * N_LAYERS + 2), 0]
    outs = [loss, grad_x.reshape(1, t, d)]
    for which in range(4):
        outs.extend(leaf(kind, l, which) for kind, l in order)
    return tuple(outs)
```

```python
import functools
import math
from typing import Any, Callable, Mapping, NamedTuple, Sequence

import jax
import jax.numpy as jnp
from jax import lax
from jax.experimental import pallas as pl
from jax.experimental.pallas import tpu as pltpu

F32 = jnp.float32
BF16 = jnp.bfloat16

N_CHIPS = 4
N_DEV = 8
N_LAYERS = 4
HEAD_DIM = 128
RMS_EPS = 1e-6
ADAM_LR = 0.001
ADAM_B1 = 0.9
ADAM_B2 = 0.999
ADAM_EPS = 1e-08
ADAM_WD = 0.01
ADAM_STEP = 10

VMEM_LIMIT = 56 * 1024 * 1024
MESH_IDS = pl.DeviceIdType.MESH
HBM_SPEC = pl.BlockSpec(memory_space=pltpu.HBM)

NN = (((1,), (0,)), ((), ()))
NT = (((1,), (1,)), ((), ()))
TN = (((0,), (0,)), ((), ()))


def _params(n_axes):
    return pltpu.CompilerParams(dimension_semantics=("arbitrary",) * n_axes, vmem_limit_bytes=VMEM_LIMIT)


def _dot(a, b, dims):
    return lax.dot_general(a, b, dims, preferred_element_type=F32)


def _sigmoid(z):
    return 1.0 / (1.0 + jnp.exp(-z))


class _Rider(NamedTuple):
    operands: Sequence[Any]
    out_shapes: Sequence[Any]
    aliases: Mapping[int, int]
    sems: Sequence[Any]
    start: Callable
    finish: Callable


def _compute_call(body, name, *, grid, in_specs, operands, out_specs, out_shape, scratch=(), aliases=None, rider=None):
    in_specs, operands = list(in_specs), list(operands)
    out_specs, out_shape, scratch = list(out_specs), list(out_shape), list(scratch)
    aliases = dict(aliases or {})
    n_in, n_out, n_scratch = len(operands), len(out_shape), len(scratch)
    hosted = body
    if rider is not None:
        r_in, r_out = len(rider.operands), len(rider.out_shapes)
        aliases.update({n_in + i: n_out + o for i, o in rider.aliases.items()})

        def hosted(*refs):
            ins, refs = refs[:n_in], refs[n_in:]
            rider_ins, refs = refs[:r_in], refs[r_in:]
            outs, refs = refs[:n_out], refs[n_out:]
            rider_outs, refs = refs[:r_out], refs[r_out:]
            own_scratch, rider_sems = refs[:n_scratch], refs[n_scratch:]
            ids = [pl.program_id(axis) for axis in range(len(grid))]
            first = functools.reduce(jnp.logical_and, [i == 0 for i in ids])
            last = functools.reduce(jnp.logical_and, [i == g - 1 for i, g in zip(ids, grid)])

            @pl.when(first)
            def _():
                rider.start(rider_ins, rider_outs, rider_sems)

            body(*ins, *outs, *own_scratch)

            @pl.when(last)
            def _():
                rider.finish(rider_ins, rider_outs, rider_sems)

        in_specs += [HBM_SPEC] * r_in
        operands += list(rider.operands)
        out_specs += [HBM_SPEC] * r_out
        out_shape += list(rider.out_shapes)
        scratch += list(rider.sems)
    return pl.pallas_call(
        hosted, name=name, grid=grid, in_specs=in_specs, out_specs=out_specs, out_shape=out_shape,
        scratch_shapes=scratch, input_output_aliases=aliases, compiler_params=_params(len(grid)),
    )(*operands)


def _matmul(name, a, b, *, grid, a_spec, b_spec, o_spec, out_shape, dims, reduce_axis=None, acc_shape=None,
            alias_out=None, rider=None):
    out_dtype = out_shape.dtype
    direct = reduce_axis is not None and out_dtype == F32
    n_red = grid[reduce_axis] if reduce_axis is not None else 1

    def body(*refs):
        if alias_out is not None:
            refs = refs[1:]
        a_ref, b_ref, o_ref = refs[:3]
        if reduce_axis is None:
            o_ref[...] = _dot(a_ref[...], b_ref[...], dims).astype(out_dtype)
            return
        acc_ref = o_ref if direct else refs[3]
        k = pl.program_id(reduce_axis)

        @pl.when(k == 0)
        def _():
            acc_ref[...] = jnp.zeros_like(acc_ref)

        acc_ref[...] += _dot(a_ref[...], b_ref[...], dims)

        if not direct:
            @pl.when(k == n_red - 1)
            def _():
                o_ref[...] = acc_ref[...].astype(out_dtype)

    scratch = []
    if reduce_axis is not None and not direct:
        scratch = [pltpu.VMEM(acc_shape, F32)]
    in_specs = [a_spec, b_spec]
    operands = [a, b]
    aliases = {}
    if alias_out is not None:
        in_specs = [HBM_SPEC] + in_specs
        operands = [alias_out] + operands
        aliases = {0: 0}
    return _compute_call(body, name, grid=grid, in_specs=in_specs, operands=operands, out_specs=[o_spec],
                         out_shape=[out_shape], scratch=scratch, aliases=aliases, rider=rider)


def _mm_proj(u, g, rider=None):
    t, d = u.shape
    tm = min(512, t)
    return _matmul(
        "mm_proj", u, g, grid=(N_CHIPS, t // tm),
        a_spec=pl.BlockSpec((tm, d), lambda s, m: (m, 0)),
        b_spec=pl.BlockSpec((None, d, d), lambda s, m: (s, 0, 0)),
        o_spec=pl.BlockSpec((None, tm, d), lambda s, m: (s, m, 0)),
        out_shape=jax.ShapeDtypeStruct((N_CHIPS, t, d), BF16), dims=NN, rider=rider)


def _mm_out(gated, g, rider=None):
    t, d = gated.shape
    dq = d // N_CHIPS
    tm = min(512, t)

    def body(a_ref, b_ref, o_ref):
        acc = _dot(a_ref[:, 0:dq], b_ref[0], NN)
        for s in range(1, N_CHIPS):
            acc = acc + _dot(a_ref[:, s * dq:(s + 1) * dq], b_ref[s], NN)
        o_ref[...] = acc

    return _compute_call(
        body, "mm_out", grid=(t // tm,),
        in_specs=[pl.BlockSpec((tm, d), lambda m: (m, 0)), pl.BlockSpec((N_CHIPS, dq, d), lambda m: (0, N_CHIPS, 0))],
        operands=[gated, g], out_specs=[pl.BlockSpec((tm, d), lambda m: (m, 0))],
        out_shape=[jax.ShapeDtypeStruct((t, d), F32)], rider=rider)


def _mm_dgated(dm, g):
    t, d = dm.shape
    dq = d // N_CHIPS
    tm = min(512, t)

    def body(a_ref, b_ref, o_ref):
        a = a_ref[...]
        for s in range(N_CHIPS):
            o_ref[:, s * dq:(s + 1) * dq] = _dot(a, b_ref[s], NT).astype(BF16)

    return _compute_call(
        body, "mm_dgated", grid=(t // tm,),
        in_specs=[pl.BlockSpec((tm, d), lambda m: (m, 0)), pl.BlockSpec((N_CHIPS, dq, d), lambda m: (0, N_CHIPS, 0))],
        operands=[dm, g], out_specs=[pl.BlockSpec((tm, d), lambda m: (m, 0))],
        out_shape=[jax.ShapeDtypeStruct((t, d), BF16)])[0]


def _mm_dwout(gated, dm):
    t, d = gated.shape
    dq = d // N_CHIPS
    tk = min(1024, t)
    return _matmul(
        "mm_dwout", gated, dm, grid=(N_CHIPS, t // tk),
        a_spec=pl.BlockSpec((tk, dq), lambda s, k: (k, s)),
        b_spec=pl.BlockSpec((tk, d), lambda s, k: (k, 0)),
        o_spec=pl.BlockSpec((None, dq, d), lambda s, k: (s, N_CHIPS, 0)),
        out_shape=jax.ShapeDtypeStruct((N_CHIPS, d + dq, d), BF16), dims=TN, reduce_axis=1, acc_shape=(dq, d))[0]


def _mm_du(dproj, g, rider=None):
    _, t, d = dproj.shape
    tm = min(512, t)
    return _matmul(
        "mm_du", dproj, g, grid=(t // tm, N_CHIPS),
        a_spec=pl.BlockSpec((None, tm, d), lambda m, s: (s, m, 0)),
        b_spec=pl.BlockSpec((None, d, d), lambda m, s: (s, 0, 0)),
        o_spec=pl.BlockSpec((tm, d), lambda m, s: (m, 0)),
        out_shape=jax.ShapeDtypeStruct((t, d), F32), dims=NT, reduce_axis=1, rider=rider)


def _mm_dwin(u, dproj, gp, rider=None):
    t, d = u.shape
    tmo = min(1024, d)
    tk = min(1024, t)
    return _matmul(
        "mm_dwin", u, dproj, grid=(N_CHIPS, d // tmo, t // tk),
        a_spec=pl.BlockSpec((tk, tmo), lambda s, mo, k: (k, mo)),
        b_spec=pl.BlockSpec((None, tk, d), lambda s, mo, k: (s, k, 0)),
        o_spec=pl.BlockSpec((None, tmo, d), lambda s, mo, k: (s, mo, 0)),
        out_shape=jax.ShapeDtypeStruct(gp.shape, BF16), dims=TN, reduce_axis=2, acc_shape=(tmo, d), alias_out=gp,
        rider=rider)


def _rms(v):
    r = lax.rsqrt(jnp.mean(v * v, axis=-1, keepdims=True) + RMS_EPS)
    return v * r, r


def _rms_bwd(dout, n, r, gain):
    dn = dout * gain
    return r * (dn - n * jnp.mean(dn * n, axis=-1, keepdims=True))


def _fold8(v):
    return jnp.sum(v.reshape(v.shape[0] // 8, 8, v.shape[1]), axis=0)


def _row0(total):
    rows = lax.broadcasted_iota(jnp.int32, total.shape, 0)
    return jnp.where(rows == 0, jnp.sum(total, axis=0, keepdims=True), 0.0)


def _norm_tile(t):
    return min(256, t)


def _norm_first(x, g_pre):
    t, d = x.shape
    tr = _norm_tile(t)

    def body(x_ref, g_ref, u_ref):
        n, _ = _rms(x_ref[...])
        u_ref[...] = (n * g_ref[...]).astype(BF16)

    row = pl.BlockSpec((tr, d), lambda i: (i, 0))
    vec = pl.BlockSpec((1, d), lambda i: (0, 0))
    return pl.pallas_call(
        body, name="norm_first", grid=(t // tr,), in_specs=[row, vec], out_specs=row,
        out_shape=jax.ShapeDtypeStruct((t, d), BF16), compiler_params=_params(1),
    )(x, g_pre.reshape(1, d))


def _norm_mid(h, m, g_post, g_pre_next, rider=None):
    t, d = h.shape
    tr = _norm_tile(t)

    def body(h_ref, m_ref, gp_ref, gn_ref, hn_ref, u_ref):
        n, _ = _rms(m_ref[...])
        hn = h_ref[...] + n * gp_ref[...]
        hn_ref[...] = hn
        n2, _ = _rms(hn)
        u_ref[...] = (n2 * gn_ref[...]).astype(BF16)

    row = pl.BlockSpec((tr, d), lambda i: (i, 0))
    vec = pl.BlockSpec((1, d), lambda i: (0, 0))
    return _compute_call(
        body, "norm_mid", grid=(t // tr,), in_specs=[row, row, vec, vec],
        operands=[h, m, g_post.reshape(1, d), g_pre_next.reshape(1, d)], out_specs=[row, row],
        out_shape=[jax.ShapeDtypeStruct((t, d), F32), jax.ShapeDtypeStruct((t, d), BF16)], rider=rider)


def _norm_last(h, m, g_post, target):
    t, d = h.shape
    tr = _norm_tile(t)
    nsteps = t // tr

    def body(h_ref, m_ref, gp_ref, tg_ref, dy_ref, dm_ref, dgp_ref, loss_ref, acc_g, acc_l):
        i = pl.program_id(0)

        @pl.when(i == 0)
        def _():
            acc_g[...] = jnp.zeros_like(acc_g)
            acc_l[...] = jnp.zeros_like(acc_l)

        gain = gp_ref[...]
        n, r = _rms(m_ref[...])
        err = h_ref[...] + n * gain - tg_ref[...]
        dy = err / d
        dy_ref[...] = dy
        dm_ref[...] = _rms_bwd(dy, n, r, gain).astype(BF16)
        acc_g[...] += _fold8(dy * n)
        acc_l[...] += _fold8(err * err)

        @pl.when(i == nsteps - 1)
        def _():
            dgp_ref[...] = _row0(acc_g[...])
            loss_ref[...] = jnp.zeros((8, 128), F32) + (0.5 / d) * jnp.sum(acc_l[...])

    row = pl.BlockSpec((tr, d), lambda i: (i, 0))
    vec = pl.BlockSpec((1, d), lambda i: (0, 0))
    acc = pl.BlockSpec((8, d), lambda i: (0, 0))
    return pl.pallas_call(
        body, name="norm_last", grid=(nsteps,), in_specs=[row, row, vec, row],
        out_specs=[row, row, acc, pl.BlockSpec((8, 128), lambda i: (0, 0))],
        out_shape=[jax.ShapeDtypeStruct((t, d), F32), jax.ShapeDtypeStruct((t, d), BF16),
                   jax.ShapeDtypeStruct((8, d), F32), jax.ShapeDtypeStruct((8, 128), F32)],
        scratch_shapes=[pltpu.VMEM((8, d), F32), pltpu.VMEM((8, d), F32)],
        compiler_params=_params(1),
    )(h, m, g_post.reshape(1, d), target)


def _norm_bwd_mid(dh, du, h_in, g_pre, m_prev, g_post_prev):
    t, d = dh.shape
    tr = _norm_tile(t)
    nsteps = t // tr

    def body(dh_ref, du_ref, h_ref, gpre_ref, m_ref, gpost_ref, dhn_ref, dm_ref, dgpre_ref, dgpost_ref, acc_a, acc_b):
        i = pl.program_id(0)

        @pl.when(i == 0)
        def _():
            acc_a[...] = jnp.zeros_like(acc_a)
            acc_b[...] = jnp.zeros_like(acc_b)

        du_t = du_ref[...]
        n, r = _rms(h_ref[...])
        dhn = dh_ref[...] + _rms_bwd(du_t, n, r, gpre_ref[...])
        dhn_ref[...] = dhn
        acc_a[...] += _fold8(du_t * n)
        n2, r2 = _rms(m_ref[...])
        dm_ref[...] = _rms_bwd(dhn, n2, r2, gpost_ref[...]).astype(BF16)
        acc_b[...] += _fold8(dhn * n2)

        @pl.when(i == nsteps - 1)
        def _():
            dgpre_ref[...] = _row0(acc_a[...])
            dgpost_ref[...] = _row0(acc_b[...])

    row = pl.BlockSpec((tr, d), lambda i: (i, 0))
    vec = pl.BlockSpec((1, d), lambda i: (0, 0))
    acc = pl.BlockSpec((8, d), lambda i: (0, 0))
    return pl.pallas_call(
        body, name="norm_bwd_mid", grid=(nsteps,), in_specs=[row, row, row, vec, row, vec],
        out_specs=[row, row, acc, acc],
        out_shape=[jax.ShapeDtypeStruct((t, d), F32), jax.ShapeDtypeStruct((t, d), BF16),
                   jax.ShapeDtypeStruct((8, d), F32), jax.ShapeDtypeStruct((8, d), F32)],
        scratch_shapes=[pltpu.VMEM((8, d), F32), pltpu.VMEM((8, d), F32)],
        compiler_params=_params(1),
    )(dh, du, h_in, g_pre.reshape(1, d), m_prev, g_post_prev.reshape(1, d))


def _norm_bwd_first(dh, du, x, g_pre, rider=None):
    t, d = dh.shape
    tr = _norm_tile(t)
    nsteps = t // tr

    def body(dh_ref, du_ref, x_ref, gpre_ref, dx_ref, dgpre_ref, acc_a):
        i = pl.program_id(0)

        @pl.when(i == 0)
        def _():
            acc_a[...] = jnp.zeros_like(acc_a)

        du_t = du_ref[...]
        n, r = _rms(x_ref[...])
        dx_ref[...] = dh_ref[...] + _rms_bwd(du_t, n, r, gpre_ref[...])
        acc_a[...] += _fold8(du_t * n)

        @pl.when(i == nsteps - 1)
        def _():
            dgpre_ref[...] = _row0(acc_a[...])

    row = pl.BlockSpec((tr, d), lambda i: (i, 0))
    vec = pl.BlockSpec((1, d), lambda i: (0, 0))
    acc = pl.BlockSpec((8, d), lambda i: (0, 0))
    return _compute_call(
        body, "norm_bwd_first", grid=(nsteps,), in_specs=[row, row, row, vec],
        operands=[dh, du, x, g_pre.reshape(1, d)], out_specs=[row, acc],
        out_shape=[jax.ShapeDtypeStruct((t, d), F32), jax.ShapeDtypeStruct((8, d), F32)],
        scratch=[pltpu.VMEM((8, d), F32)], rider=rider)


CONV_TC = 128
CONV_HALO = 16


def _conv_chunk(t):
    return min(512, t)


def _shift_down(v, steps, fill):
    rows = lax.broadcasted_iota(jnp.int32, v.shape, 0)
    out = pltpu.roll(v, steps, axis=0)
    for k in range(steps):
        out = jnp.where(rows == k, fill[CONV_HALO - steps + k:CONV_HALO - steps + k + 1, :], out)
    return out


def _shift_up(v, steps, fill):
    nrows = v.shape[0]
    rows = lax.broadcasted_iota(jnp.int32, v.shape, 0)
    out = pltpu.roll(v, nrows - steps, axis=0)
    for k in range(steps):
        out = jnp.where(rows == nrows - steps + k, fill[k:k + 1, :], out)
    return out


def _conv_fwd(proj, cw, rider=None):
    _, t, d = proj.shape
    chunk = _conv_chunk(t)

    def body(p_ref, w_ref, o_ref):
        w = w_ref[...]
        w0, w1, w2 = w[0:1, :], w[1:2, :], w[2:3, :]
        for ci in range(t // chunk):
            t0 = ci * chunk
            rows = pl.ds(t0, chunk)
            b = p_ref[0, rows, :].astype(F32)
            cx = p_ref[1, rows, :].astype(F32) * p_ref[2, rows, :].astype(F32)
            z = p_ref[3, rows, :].astype(F32)
            if ci == 0:
                prev = jnp.zeros((CONV_HALO, CONV_TC), F32)
            else:
                halo = pl.ds(t0 - CONV_HALO, CONV_HALO)
                prev = p_ref[1, halo, :].astype(F32) * p_ref[2, halo, :].astype(F32)
            conv = w2 * cx + w1 * _shift_down(cx, 1, prev) + w0 * _shift_down(cx, 2, prev)
            o_ref[rows, :] = (z * _sigmoid(z) * b * conv).astype(BF16)

    return _compute_call(
        body, "conv_fwd", grid=(d // CONV_TC,),
        in_specs=[pl.BlockSpec((N_CHIPS, t, CONV_TC), lambda j: (0, 0, j)), pl.BlockSpec((8, CONV_TC), lambda j: (0, j))],
        operands=[proj, cw], out_specs=[pl.BlockSpec((t, CONV_TC), lambda j: (0, j))],
        out_shape=[jax.ShapeDtypeStruct((t, d), BF16)], rider=rider)


def _conv_bwd(proj, dgated, cw):
    _, t, d = proj.shape
    chunk = _conv_chunk(t)
    nchunks = t // chunk

    def body(p_ref, dg_ref, w_ref, dp_ref, dw_ref):
        w = w_ref[...]
        w0, w1, w2 = w[0:1, :], w[1:2, :], w[2:3, :]
        dw0 = jnp.zeros((1, CONV_TC), F32)
        dw1 = jnp.zeros((1, CONV_TC), F32)
        dw2 = jnp.zeros((1, CONV_TC), F32)
        for ci in range(nchunks):
            t0 = ci * chunk
            rows = pl.ds(t0, chunk)
            b = p_ref[0, rows, :].astype(F32)
            c = p_ref[1, rows, :].astype(F32)
            xt = p_ref[2, rows, :].astype(F32)
            z = p_ref[3, rows, :].astype(F32)
            dg = dg_ref[rows, :].astype(F32)
            cx = c * xt
            if ci == 0:
                prev = jnp.zeros((CONV_HALO, CONV_TC), F32)
            else:
                halo = pl.ds(t0 - CONV_HALO, CONV_HALO)
                prev = p_ref[1, halo, :].astype(F32) * p_ref[2, halo, :].astype(F32)
            cx1 = _shift_down(cx, 1, prev)
            cx2 = _shift_down(cx, 2, prev)
            conv = w2 * cx + w1 * cx1 + w0 * cx2
            sig = _sigmoid(z)
            dy = dg * (z * sig)
            dconv = dy * b
            if ci == nchunks - 1:
                nxt = jnp.zeros((CONV_HALO, CONV_TC), F32)
            else:
                halo = pl.ds(t0 + chunk, CONV_HALO)
                zn = p_ref[3, halo, :].astype(F32)
                nxt = dg_ref[halo, :].astype(F32) * (zn * _sigmoid(zn)) * p_ref[0, halo, :].astype(F32)
            dcx = w2 * dconv + w1 * _shift_up(dconv, 1, nxt) + w0 * _shift_up(dconv, 2, nxt)
            dp_ref[0, rows, :] = (dy * conv).astype(BF16)
            dp_ref[1, rows, :] = (dcx * xt).astype(BF16)
            dp_ref[2, rows, :] = (dcx * c).astype(BF16)
            dp_ref[3, rows, :] = (dg * (b * conv) * (sig * (1.0 + z * (1.0 - sig)))).astype(BF16)
            dw0 = dw0 + jnp.sum(dconv * cx2, axis=0, keepdims=True)
            dw1 = dw1 + jnp.sum(dconv * cx1, axis=0, keepdims=True)
            dw2 = dw2 + jnp.sum(dconv * cx, axis=0, keepdims=True)
        taps = lax.broadcasted_iota(jnp.int32, (8, CONV_TC), 0)
        dw_ref[...] = jnp.where(taps == 0, dw0, jnp.where(taps == 1, dw1, jnp.where(taps == 2, dw2, 0.0)))

    return pl.pallas_call(
        body, name="conv_bwd", grid=(d // CONV_TC,),
        in_specs=[pl.BlockSpec((N_CHIPS, t, CONV_TC), lambda j: (0, 0, j)),
                  pl.BlockSpec((t, CONV_TC), lambda j: (0, j)),
                  pl.BlockSpec((8, CONV_TC), lambda j: (0, j))],
        out_specs=[pl.BlockSpec((N_CHIPS, t, CONV_TC), lambda j: (0, 0, j)), pl.BlockSpec((8, CONV_TC), lambda j: (0, j))],
        out_shape=[jax.ShapeDtypeStruct((N_CHIPS, t, d), BF16), jax.ShapeDtypeStruct((8, d), F32)],
        compiler_params=_params(1),
    )(proj, dgated, cw)


SB_DEAD_TAIL = -105.0
SB_COUNT_LANE = HEAD_DIM - 1


def _sb_block(t):
    return min(256, t)


def _split_dot(v, tri):
    hi = v.astype(BF16)
    lo = (v - hi.astype(F32)).astype(BF16)
    return _dot(hi, tri, NN) + _dot(lo, tri, NN)


SB_HEADS_PER_STEP = 4


def _sb_terms(s, diagonal):
    sp = jnp.maximum(s, 0.0) + jnp.log1p(jnp.exp(-jnp.abs(s)))
    if not diagonal:
        return -sp, s - sp, sp, None
    mask = lax.broadcasted_iota(jnp.int32, s.shape, 1) < lax.broadcasted_iota(jnp.int32, s.shape, 0)
    return jnp.where(mask, -sp, 0.0), s - sp, sp, mask


def _masked(mask, v):
    return v if mask is None else jnp.where(mask, v, 0.0)


def _sb_fwd(proj, rider=None):
    _, t, d = proj.shape
    heads = d // HEAD_DIM
    blk = _sb_block(t)
    nblk = t // blk
    scale = 1.0 / math.sqrt(HEAD_DIM)

    hps = SB_HEADS_PER_STEP
    width = hps * HEAD_DIM

    def body(q_ref, k_ref, v_ref, z_ref, gated_ref, o_ref, car_ref, tail_ref, acc_ref):
        i = pl.program_id(1)
        r_i = lax.broadcasted_iota(jnp.int32, (blk, blk), 0)
        c_i = lax.broadcasted_iota(jnp.int32, (blk, blk), 1)
        tri_after = (r_i > c_i).astype(BF16)
        lanes = lax.broadcasted_iota(jnp.int32, (blk, HEAD_DIM), 1)

        tail_ref[...] = jnp.zeros_like(tail_ref)
        acc_ref[...] = jnp.zeros_like(acc_ref)
        car_ref[...] = jnp.zeros_like(car_ref)

        def visit(j, diagonal):
            krows = pl.ds(pl.multiple_of(j * blk, blk), blk)
            hcols = [pl.ds(hh * HEAD_DIM, HEAD_DIM) for hh in range(hps)]
            logits = [_dot(q_ref[:, c], k_ref[krows, c], NT) for c in hcols]
            terms = [_sb_terms(s * scale, diagonal) for s in logits]
            within = [_split_dot(keep, tri_after) for keep, _, _, _ in terms]
            top = None
            for hh, (keep, log_beta, _, mask) in enumerate(terms):
                tail_b = tail_ref[hh]
                w = _masked(mask, jnp.exp(log_beta + tail_b[:, 0:1] + within[hh]))
                acc_ref[hh] += _dot(w.astype(BF16), v_ref[krows, hcols[hh]], NN)
                car_ref[hh] = jnp.where(lanes == j, tail_b, car_ref[hh])
                tail_new = tail_b + jnp.sum(keep, axis=1, keepdims=True)
                tail_ref[hh] = tail_new
                top = jnp.max(tail_new) if top is None else jnp.maximum(top, jnp.max(tail_new))
            return top > SB_DEAD_TAIL

        def more(state):
            jj, live = state
            return jnp.logical_and(jj <= i, live)

        def step(state):
            jj, _ = state
            return jj + 1, visit(i - jj, False)

        visited, _ = lax.while_loop(more, step, (jnp.int32(1), visit(i, True)))
        for hh in range(hps):
            cols = pl.ds(hh * HEAD_DIM, HEAD_DIM)
            car_ref[hh] = jnp.where(lanes == SB_COUNT_LANE, visited.astype(F32), car_ref[hh])
            z = z_ref[:, cols].astype(F32)
            acc = acc_ref[hh]
            o_ref[:, cols] = acc.astype(BF16)
            gated_ref[:, cols] = (z * _sigmoid(z) * acc).astype(BF16)

    qspec = lambda s: pl.BlockSpec((None, blk, width), lambda h, i: (s, i, h))
    kspec = lambda s: pl.BlockSpec((None, t, width), lambda h, i: (s, 0, h))
    ospec = pl.BlockSpec((blk, width), lambda h, i: (i, h))
    return _compute_call(
        body, "sb_fwd", grid=(heads // hps, nblk),
        in_specs=[qspec(0), kspec(1), kspec(2), qspec(3)], operands=[proj, proj, proj, proj],
        out_specs=[ospec, ospec, pl.BlockSpec((hps, blk, HEAD_DIM), lambda h, i: (h, i, 0))],
        out_shape=[jax.ShapeDtypeStruct((t, d), BF16), jax.ShapeDtypeStruct((t, d), BF16),
                   jax.ShapeDtypeStruct((heads, t, HEAD_DIM), F32)],
        scratch=[pltpu.VMEM((hps, blk, HEAD_DIM), F32), pltpu.VMEM((hps, blk, HEAD_DIM), F32)], rider=rider)


def _sb_bwd(proj, dgated, o, car, rider=None):
    _, t, d = proj.shape
    heads = d // HEAD_DIM
    blk = _sb_block(t)
    nblk = t // blk
    scale = 1.0 / math.sqrt(HEAD_DIM)

    hps = SB_HEADS_PER_STEP
    width = hps * HEAD_DIM

    def body(q_ref, k_ref, v_ref, z_ref, dg_ref, o_ref, car_ref, dp_ref, dk_acc, dv_acc, gsum_ref, dq_ref, do_ref):
        step_i = pl.program_id(1)
        i = nblk - 1 - step_i

        @pl.when(step_i == 0)
        def _():
            dk_acc[...] = jnp.zeros_like(dk_acc)
            dv_acc[...] = jnp.zeros_like(dv_acc)

        r_i = lax.broadcasted_iota(jnp.int32, (blk, blk), 0)
        c_i = lax.broadcasted_iota(jnp.int32, (blk, blk), 1)
        tri_after = (r_i > c_i).astype(BF16)
        tri_before = (r_i < c_i).astype(BF16)
        lanes = lax.broadcasted_iota(jnp.int32, (blk, HEAD_DIM), 1)

        gsum_ref[...] = jnp.zeros_like(gsum_ref)
        dq_ref[...] = jnp.zeros_like(dq_ref)
        for hh in range(hps):
            cols = pl.ds(hh * HEAD_DIM, HEAD_DIM)
            z = z_ref[:, cols].astype(F32)
            dg = dg_ref[:, cols].astype(F32)
            sig = _sigmoid(z)
            do_ref[hh] = (dg * (z * sig)).astype(BF16)
            dp_ref[3, :, cols] = (dg * o_ref[:, cols].astype(F32) * (sig * (1.0 + z * (1.0 - sig)))).astype(BF16)

        def visit(j, diagonal):
            krows = pl.ds(pl.multiple_of(j * blk, blk), blk)
            hcols = [pl.ds(hh * HEAD_DIM, HEAD_DIM) for hh in range(hps)]
            logits = [_dot(q_ref[:, c], k_ref[krows, c], NT) for c in hcols]
            dws = [_dot(do_ref[hh], v_ref[krows, c], NT) for hh, c in enumerate(hcols)]
            terms = [_sb_terms(s * scale, diagonal) for s in logits]
            within = [_split_dot(keep, tri_after) for keep, _, _, _ in terms]
            ws, gs = [], []
            for hh, (keep, log_beta, sp, mask) in enumerate(terms):
                tail = jnp.sum(jnp.where(lanes == j, car_ref[hh], 0.0), axis=1, keepdims=True)
                w = _masked(mask, jnp.exp(log_beta + tail + within[hh]))
                ws.append(w.astype(BF16))
                gs.append(w * dws[hh])
            g_within = [_split_dot(g, tri_before) for g in gs]
            for hh, (keep, log_beta, sp, mask) in enumerate(terms):
                c = hcols[hh]
                g_before = gsum_ref[hh]
                g_cum = g_before[:, 0:1] + g_within[hh]
                dl = (_masked(mask, gs[hh] * jnp.exp(-sp) - g_cum * jnp.exp(log_beta)) * scale).astype(BF16)
                dq_ref[hh] += _dot(dl, k_ref[krows, c], NN)
                dk_acc[krows, c] += _dot(dl, q_ref[:, c], TN)
                dv_acc[krows, c] += _dot(ws[hh], do_ref[hh], TN)
                gsum_ref[hh] = g_before + jnp.sum(gs[hh], axis=1, keepdims=True)

        def step(j, carry):
            visit(j, False)
            return carry

        visited = jnp.max(jnp.where(lanes == SB_COUNT_LANE, car_ref[0], 0.0)).astype(jnp.int32)
        lax.fori_loop(i + 1 - visited, i, step, 0)
        visit(i, True)
        own = pl.ds(pl.multiple_of(i * blk, blk), blk)
        for hh in range(hps):
            cols = pl.ds(hh * HEAD_DIM, HEAD_DIM)
            dp_ref[0, :, cols] = dq_ref[hh].astype(BF16)
        dp_ref[1] = dk_acc[own, :].astype(BF16)
        dp_ref[2] = dv_acc[own, :].astype(BF16)

    qspec = lambda s: pl.BlockSpec((None, blk, width), lambda h, i: (s, nblk - 1 - i, h))
    kspec = lambda s: pl.BlockSpec((None, t, width), lambda h, i: (s, 0, h))
    tspec = pl.BlockSpec((blk, width), lambda h, i: (nblk - 1 - i, h))
    return _compute_call(
        body, "sb_bwd", grid=(heads // hps, nblk),
        in_specs=[qspec(0), kspec(1), kspec(2), qspec(3), tspec, tspec,
                  pl.BlockSpec((hps, blk, HEAD_DIM), lambda h, i: (h, nblk - 1 - i, 0))],
        operands=[proj, proj, proj, proj, dgated, o, car],
        out_specs=[pl.BlockSpec((N_CHIPS, blk, width), lambda h, i: (0, nblk - 1 - i, h))],
        out_shape=[jax.ShapeDtypeStruct((N_CHIPS, t, d), BF16)],
        scratch=[pltpu.VMEM((t, width), F32), pltpu.VMEM((t, width), F32),
                 pltpu.VMEM((hps, blk, HEAD_DIM), F32), pltpu.VMEM((hps, blk, HEAD_DIM), F32),
                 pltpu.VMEM((hps, blk, HEAD_DIM), BF16)], rider=rider)


def _pack_weights(w_in, w_out, chip):
    d = w_in.shape[0]
    rb = d // 8
    n_in = d // rb
    n_out = w_out.shape[0] // rb

    def body(chip_ref, wi_ref, wo_ref, o_ref):
        r = pl.program_id(0)

        @pl.when(r < n_in)
        def _():
            o_ref[...] = wi_ref[...].astype(BF16)

        @pl.when(r >= n_in)
        def _():
            o_ref[...] = wo_ref[...].astype(BF16)

    grid_spec = pltpu.PrefetchScalarGridSpec(
        num_scalar_prefetch=1, grid=(n_in + n_out,),
        in_specs=[pl.BlockSpec((rb, d), lambda r, me: (jnp.minimum(r, n_in - 1), 0)),
                  pl.BlockSpec((rb, d), lambda r, me: (jnp.maximum(r - n_in, 0), 0))],
        out_specs=pl.BlockSpec((None, rb, d), lambda r, me: (me[0], r, 0)))
    return pl.pallas_call(
        body, name="pack_weights", grid_spec=grid_spec,
        out_shape=jax.ShapeDtypeStruct((N_CHIPS, d + w_out.shape[0], d), BF16), compiler_params=_params(1),
    )(chip, w_in, w_out)


def _flip(v, bit):
    return 1 - v if bit else v


def _remote(src, dst, send_sem, recv_sem, target):
    return pltpu.make_async_remote_copy(src_ref=src, dst_ref=dst, send_sem=send_sem, recv_sem=recv_sem,
                                        device_id=target, device_id_type=MESH_IDS)


AG_CHUNKS = 8
AG_PLAN_CONV = {"mm_proj": range(0, 5), "conv_fwd": range(5, 6), "mm_out": range(6, 7), "norm_mid": range(7, 8)}
AG_PLAN_SB = {"mm_proj": range(0, 4), "sb_fwd": range(4, 8)}
HALF_CHUNKS = 8


SWAP_CHUNKS = 2


def _other_chips(x, y):
    return [(_flip(x, k >> 1), _flip(y, k & 1)) for k in (1, 2, 3)]


def _gather_sems(n):
    return [pltpu.SemaphoreType.DMA((3, n * AG_CHUNKS)) for _ in range(4)]


def _gather_pieces(g, chunks):
    hr = g[0].shape[1] // 2
    cr = hr // AG_CHUNKS
    return hr, [(l * AG_CHUNKS + q, g[l], q * cr, cr) for l in range(len(g)) for q in chunks]


def _gather_start(g, send, recv, chunks=range(AG_CHUNKS)):
    x, y, c = lax.axis_index("x"), lax.axis_index("y"), lax.axis_index("c")
    hr, pieces = _gather_pieces(g, chunks)
    for k, (px, py) in enumerate(_other_chips(x, y)):
        for i, ref, r0, cr in pieces:
            piece = ref.at[2 * x + y, pl.ds(c * hr + r0, cr)]
            _remote(piece, piece, send.at[k, i], recv.at[k, i], (px, py, c)).start()


def _gather_finish(g, send, recv, fsend, frecv, chunks=range(AG_CHUNKS)):
    x, y, c = lax.axis_index("x"), lax.axis_index("y"), lax.axis_index("c")
    sibling = (x, y, 1 - c)
    hr, pieces = _gather_pieces(g, chunks)
    chips = _other_chips(x, y)
    for k, (px, py) in enumerate(chips):
        for i, ref, r0, cr in pieces:
            landed = ref.at[2 * px + py, pl.ds(c * hr + r0, cr)]
            _remote(landed, landed, send.at[k, i], recv.at[k, i], (px, py, c)).wait_recv()
            _remote(landed, landed, fsend.at[k, i], frecv.at[k, i], sibling).start()
    for k, (px, py) in enumerate(chips):
        for i, ref, r0, cr in pieces:
            theirs = ref.at[2 * px + py, pl.ds((1 - c) * hr + r0, cr)]
            _remote(theirs, theirs, fsend.at[k, i], frecv.at[k, i], sibling).wait_recv()
    for k, (px, py) in enumerate(chips):
        for i, ref, r0, cr in pieces:
            mine = ref.at[2 * x + y, pl.ds(c * hr + r0, cr)]
            _remote(mine, mine, send.at[k, i], recv.at[k, i], (px, py, c)).wait_send()
            landed = ref.at[2 * px + py, pl.ds(c * hr + r0, cr)]
            _remote(landed, landed, fsend.at[k, i], frecv.at[k, i], sibling).wait_send()


def _all_gather_weights(packs, cw):
    n = len(packs)

    def body(*refs):
        cw_ref = refs[n]
        g = refs[n + 1:2 * n + 1]
        cwg = refs[2 * n + 1]
        send, recv, fsend, frecv, csend, crecv, lsem = refs[2 * n + 2:]
        x, y, c = lax.axis_index("x"), lax.axis_index("y"), lax.axis_index("c")
        me = 2 * x + y
        local = pltpu.make_async_copy(cw_ref, cwg.at[me], lsem)
        local.start()
        taps = [_remote(cw_ref, cwg.at[me], csend.at[k], crecv.at[k], (px, py, c))
                for k, (px, py) in enumerate(_other_chips(x, y))]
        for cp in taps:
            cp.start()
        _gather_start(g, send, recv)
        _gather_finish(g, send, recv, fsend, frecv)
        for k, (px, py) in enumerate(_other_chips(x, y)):
            _remote(cw_ref, cwg.at[2 * px + py], csend.at[k], crecv.at[k], (px, py, c)).wait_recv()
        for cp in taps:
            cp.wait_send()
        local.wait()

    out_shape = [jax.ShapeDtypeStruct(p.shape, BF16) for p in packs]
    out_shape.append(jax.ShapeDtypeStruct((N_CHIPS,) + cw.shape, cw.dtype))
    return pl.pallas_call(
        body, name="all_gather_weights", in_specs=[HBM_SPEC] * (n + 1), out_specs=[HBM_SPEC] * (n + 1),
        out_shape=out_shape, input_output_aliases={l: l for l in range(n)},
        scratch_shapes=_gather_sems(n) + [pltpu.SemaphoreType.DMA((3,)), pltpu.SemaphoreType.DMA((3,)),
                                          pltpu.SemaphoreType.DMA],
    )(*packs, cw)


def _gather_rider(packs, chunks=range(AG_CHUNKS)):
    return _Rider(
        operands=packs, out_shapes=[jax.ShapeDtypeStruct(p.shape, BF16) for p in packs],
        aliases={l: l for l in range(len(packs))}, sems=_gather_sems(len(packs)),
        start=lambda ins, outs, sems: _gather_start(outs, sems[0], sems[1], chunks),
        finish=lambda ins, outs, sems: _gather_finish(outs, *sems, chunks))


def _join_riders(a, b):
    na, oa, sa = len(a.operands), len(a.out_shapes), len(a.sems)
    aliases = dict(a.aliases)
    aliases.update({na + i: oa + o for i, o in b.aliases.items()})
    return _Rider(
        operands=list(a.operands) + list(b.operands), out_shapes=list(a.out_shapes) + list(b.out_shapes),
        aliases=aliases, sems=list(a.sems) + list(b.sems),
        start=lambda ins, outs, sems: (a.start(ins[:na], outs[:oa], sems[:sa]),
                                       b.start(ins[na:], outs[oa:], sems[sa:])),
        finish=lambda ins, outs, sems: (a.finish(ins[:na], outs[:oa], sems[:sa]),
                                        b.finish(ins[na:], outs[oa:], sems[sa:])))


def _comm_call(name, rider):
    n_in, n_out = len(rider.operands), len(rider.out_shapes)

    def body(*refs):
        ins, outs, sems = refs[:n_in], refs[n_in:n_in + n_out], refs[n_in + n_out:]
        rider.start(ins, outs, sems)
        rider.finish(ins, outs, sems)

    return pl.pallas_call(
        body, name=name, in_specs=[HBM_SPEC] * n_in, out_specs=[HBM_SPEC] * n_out, out_shape=list(rider.out_shapes),
        scratch_shapes=list(rider.sems), input_output_aliases=dict(rider.aliases),
    )(*rider.operands)


def _small_rider(small):
    def peers():
        x, y, c = lax.axis_index("x"), lax.axis_index("y"), lax.axis_index("c")
        return 4 * x + 2 * y + c, [(_flip(x, r >> 2), _flip(y, (r >> 1) & 1), _flip(c, r & 1)) for r in range(1, N_DEV)]

    def start(ins, outs, sems):
        me, others = peers()
        pltpu.make_async_copy(ins[0], outs[0].at[me], sems[2]).start()
        for r, peer in enumerate(others):
            _remote(ins[0], outs[0].at[me], sems[0].at[r], sems[1].at[r], peer).start()

    def finish(ins, outs, sems):
        me, others = peers()
        for r, (tx, ty, tc) in enumerate(others):
            _remote(ins[0], outs[0].at[4 * tx + 2 * ty + tc], sems[0].at[r], sems[1].at[r], (tx, ty, tc)).wait_recv()
        for r, peer in enumerate(others):
            _remote(ins[0], outs[0].at[me], sems[0].at[r], sems[1].at[r], peer).wait_send()
        pltpu.make_async_copy(ins[0], outs[0].at[me], sems[2]).wait()

    return _Rider(
        operands=[small], out_shapes=[jax.ShapeDtypeStruct((N_DEV,) + small.shape, small.dtype)], aliases={},
        sems=[pltpu.SemaphoreType.DMA((N_DEV - 1,)), pltpu.SemaphoreType.DMA((N_DEV - 1,)), pltpu.SemaphoreType.DMA],
        start=start, finish=finish)


SMALL_ROWS = 16


def _pack_small(ln_parts, conv_parts, loss_rows):
    d = ln_parts[0].shape[1]
    n_ln, n_conv = len(ln_parts), len(conv_parts)

    def body(*refs):
        o_ref = refs[-1]
        rows = lax.broadcasted_iota(jnp.int32, (SMALL_ROWS, d), 0)
        acc = jnp.zeros((SMALL_ROWS, d), F32)
        for i in range(n_ln):
            acc = jnp.where(rows == i, refs[i][0:1, :], acc)
        for j in range(n_conv):
            for k in range(3):
                acc = jnp.where(rows == n_ln + 3 * j + k, refs[n_ln + j][k:k + 1, :], acc)
        acc = jnp.where(rows == n_ln + 3 * n_conv, refs[n_ln + n_conv][0:1, :], acc)
        o_ref[...] = acc

    n = n_ln + n_conv + 1
    return pl.pallas_call(
        body, name="pack_small", out_shape=jax.ShapeDtypeStruct((SMALL_ROWS, d), F32),
        in_specs=[pl.BlockSpec(memory_space=pltpu.VMEM)] * n, out_specs=pl.BlockSpec(memory_space=pltpu.VMEM),
    )(*ln_parts, *conv_parts, loss_rows)


def _send_sums_start(s_ref, got, send, recv, part=0, parts=1, span=1):
    x, y, c = lax.axis_index("x"), lax.axis_index("y"), lax.axis_index("c")
    nrows = s_ref.shape[1] // parts
    rows = pl.ds(part * nrows, span * nrows)
    for k, (px, py) in enumerate(_other_chips(x, y)):
        _remote(s_ref.at[2 * px + py, rows], got.at[k, rows], send.at[k], recv.at[k], (px, py, c)).start()


def _send_sums_finish(s_ref, got, send, recv, part=0, parts=1, span=1):
    x, y, c = lax.axis_index("x"), lax.axis_index("y"), lax.axis_index("c")
    nrows = s_ref.shape[1] // parts
    rows = pl.ds(part * nrows, span * nrows)
    for k, (px, py) in enumerate(_other_chips(x, y)):
        _remote(got.at[k, rows], got.at[k, rows], send.at[k], recv.at[k], (px, py, c)).wait_recv()
    for k, (px, py) in enumerate(_other_chips(x, y)):
        _remote(s_ref.at[2 * px + py, rows], got.at[k, rows], send.at[k], recv.at[k], (px, py, c)).wait_send()


def _send_sums_rider(sums, got=None, part=0, parts=1, span=1):
    _, hr, d = sums.shape
    return _Rider(
        operands=[sums] if got is None else [sums, got],
        out_shapes=[jax.ShapeDtypeStruct((N_CHIPS - 1, hr, d), BF16)], aliases={} if got is None else {1: 0},
        sems=[pltpu.SemaphoreType.DMA((3,)), pltpu.SemaphoreType.DMA((3,))],
        start=lambda ins, outs, sems: _send_sums_start(ins[0], outs[0], *sems, part, parts, span),
        finish=lambda ins, outs, sems: _send_sums_finish(ins[0], outs[0], *sems, part, parts, span))


def _swap_pieces(gp_ref, x_ref, c):
    hr = x_ref.shape[1]
    cr = hr // SWAP_CHUNKS
    return [(a * SWAP_CHUNKS + q, gp_ref.at[a, pl.ds((1 - c) * hr + q * cr, cr)], x_ref.at[a, pl.ds(q * cr, cr)])
            for a in range(N_CHIPS) for q in range(SWAP_CHUNKS)]


def _swap_rider(gp):
    _, p_rows, d = gp.shape

    def start(ins, outs, sems):
        x, y, c = lax.axis_index("x"), lax.axis_index("y"), lax.axis_index("c")
        for i, src, dst in _swap_pieces(ins[0], outs[0], c):
            _remote(src, dst, sems[0].at[i], sems[1].at[i], (x, y, 1 - c)).start()

    def finish(ins, outs, sems):
        x, y, c = lax.axis_index("x"), lax.axis_index("y"), lax.axis_index("c")
        pieces = _swap_pieces(ins[0], outs[0], c)
        for i, src, dst in pieces:
            _remote(dst, dst, sems[0].at[i], sems[1].at[i], (x, y, 1 - c)).wait_recv()
        for i, src, dst in pieces:
            _remote(src, dst, sems[0].at[i], sems[1].at[i], (x, y, 1 - c)).wait_send()

    nsem = N_CHIPS * SWAP_CHUNKS
    return _Rider(
        operands=[gp], out_shapes=[jax.ShapeDtypeStruct((N_CHIPS, p_rows // 2, d), BF16)], aliases={},
        sems=[pltpu.SemaphoreType.DMA((nsem,)), pltpu.SemaphoreType.DMA((nsem,))], start=start, finish=finish)


def _presum(gp, theirs, core):
    _, hr, d = theirs.shape
    tr = _row_tile(hr, 640)
    steps = hr // tr

    def body(core_ref, mine_ref, theirs_ref, o_ref):
        o_ref[...] = (mine_ref[...].astype(F32) + theirs_ref[...].astype(F32)).astype(BF16)

    grid_spec = pltpu.PrefetchScalarGridSpec(
        num_scalar_prefetch=1, grid=(N_CHIPS, steps),
        in_specs=[pl.BlockSpec((None, tr, d), lambda a, i, cr: (a, cr[0] * steps + i, 0)),
                  pl.BlockSpec((None, tr, d), lambda a, i, cr: (a, i, 0))],
        out_specs=pl.BlockSpec((None, tr, d), lambda a, i, cr: (a, i, 0)))
    return pl.pallas_call(
        body, name="presum", grid_spec=grid_spec,
        out_shape=jax.ShapeDtypeStruct((N_CHIPS, hr, d), BF16), compiler_params=_params(2),
    )(core, gp, theirs)


def _row_tile(rows, cap=128):
    if rows <= cap:
        return rows
    return next(tr for tr in range(cap, 0, -16) if rows % tr == 0)


def _sum_sources(name, parts):
    nsrc, rows, cols = parts.shape
    tr = _row_tile(rows)

    def body(p_ref, o_ref):
        total = p_ref[0].astype(F32)
        for s in range(1, nsrc):
            total = total + p_ref[s].astype(F32)
        o_ref[...] = total

    return pl.pallas_call(
        body, name=name, grid=(rows // tr,),
        in_specs=[pl.BlockSpec((nsrc, tr, cols), lambda i: (0, i, 0))],
        out_specs=pl.BlockSpec((tr, cols), lambda i: (i, 0)),
        out_shape=jax.ShapeDtypeStruct((rows, cols), F32), compiler_params=_params(1),
    )(parts)


def _sum_grad_half(got, sums, place):
    nsrc, hr, d = got.shape
    tr = _row_tile(hr, 256)

    def body(place_ref, got_ref, own_ref, o_ref):
        total = own_ref[...].astype(F32)
        for s in range(nsrc):
            total = total + got_ref[s].astype(F32)
        o_ref[...] = total

    grid_spec = pltpu.PrefetchScalarGridSpec(
        num_scalar_prefetch=1, grid=(hr // tr,),
        in_specs=[pl.BlockSpec((nsrc, tr, d), lambda i, pc: (0, i, 0)),
                  pl.BlockSpec((None, tr, d), lambda i, pc: (pc[0], i, 0))],
        out_specs=pl.BlockSpec((None, tr, d), lambda i, pc: (pc[1], i, 0)))
    return pl.pallas_call(
        body, name="sum_grad_half", grid_spec=grid_spec,
        out_shape=jax.ShapeDtypeStruct((2, hr, d), F32), compiler_params=_params(1),
    )(place, got, sums)


def _exchange_halves(fulls):
    n = len(fulls)
    _, hr, d = fulls[0].shape
    cr = hr // HALF_CHUNKS
    nc = n * HALF_CHUNKS

    def body(*refs):
        full = refs[n:2 * n]
        send, recv = refs[2 * n:]
        x, y, c = lax.axis_index("x"), lax.axis_index("y"), lax.axis_index("c")
        sibling = (x, y, 1 - c)
        pieces = [(l, q) for l in range(n) for q in range(HALF_CHUNKS)]
        sends = []
        for i, (l, q) in enumerate(pieces):
            piece = full[l].at[c, pl.ds(q * cr, cr)]
            sends.append(_remote(piece, piece, send.at[i], recv.at[i], sibling))
        for cp in sends:
            cp.start()
        for i, (l, q) in enumerate(pieces):
            theirs = full[l].at[1 - c, pl.ds(q * cr, cr)]
            _remote(theirs, theirs, send.at[i], recv.at[i], sibling).wait_recv()
        for cp in sends:
            cp.wait_send()

    return pl.pallas_call(
        body, name="exchange_halves", in_specs=[HBM_SPEC] * n, out_specs=[HBM_SPEC] * n,
        out_shape=[jax.ShapeDtypeStruct(f.shape, F32) for f in fulls], input_output_aliases={l: l for l in range(n)},
        scratch_shapes=[pltpu.SemaphoreType.DMA((nc,)), pltpu.SemaphoreType.DMA((nc,))],
    )(*fulls)


def _adamw(name, w, m, v, g, g_row0=0, rider=None):
    rows, cols = w.shape
    tr = _row_tile(rows, 256)
    off = g_row0 // tr

    def body(w_ref, m_ref, v_ref, g_ref, go_ref, d_ref, mo_ref, vo_ref):
        grad = g_ref[...]
        m_new = ADAM_B1 * m_ref[...] + (1.0 - ADAM_B1) * grad
        v_new = ADAM_B2 * v_ref[...] + (1.0 - ADAM_B2) * (grad * grad)
        m_hat = m_new / (1.0 - ADAM_B1 ** ADAM_STEP)
        v_hat = v_new / (1.0 - ADAM_B2 ** ADAM_STEP)
        go_ref[...] = grad
        d_ref[...] = -ADAM_LR * (m_hat / (jnp.sqrt(v_hat) + ADAM_EPS) + ADAM_WD * w_ref[...])
        mo_ref[...] = m_new
        vo_ref[...] = v_new

    blk = pl.BlockSpec((tr, cols), lambda i: (i, 0))
    return _compute_call(
        body, name, grid=(rows // tr,),
        in_specs=[blk, blk, blk, pl.BlockSpec((tr, cols), lambda i: (i + off, 0))], operands=[w, m, v, g],
        out_specs=[blk, blk, blk, blk], out_shape=[jax.ShapeDtypeStruct((rows, cols), F32)] * 4, rider=rider)


def _pad_rows8(a):
    return jnp.concatenate([a, jnp.zeros((8 - a.shape[0],) + a.shape[1:], a.dtype)], axis=0)


def kernel(x, ln_pre_0, conv_w_in_0, conv_w_0, conv_w_out_0, ln_post_0, ln_pre_1, sb_w_in_1, sb_w_out_1, ln_post_1, ln_pre_2, conv_w_in_2, conv_w_2, conv_w_out_2, ln_post_2, ln_pre_3, sb_w_in_3, sb_w_out_3, ln_post_3, loss_target, m_ln_pre_0, m_conv_w_in_0, m_conv_w_0, m_conv_w_out_0, m_ln_post_0, m_ln_pre_1, m_sb_w_in_1, m_sb_w_out_1, m_ln_post_1, m_ln_pre_2, m_conv_w_in_2, m_conv_w_2, m_conv_w_out_2, m_ln_post_2, m_ln_pre_3, m_sb_w_in_3, m_sb_w_out_3, m_ln_post_3, v_ln_pre_0, v_conv_w_in_0, v_conv_w_0, v_conv_w_out_0, v_ln_post_0, v_ln_pre_1, v_sb_w_in_1, v_sb_w_out_1, v_ln_post_1, v_ln_pre_2, v_conv_w_in_2, v_conv_w_2, v_conv_w_out_2, v_ln_post_2, v_ln_pre_3, v_sb_w_in_3, v_sb_w_out_3, v_ln_post_3):
    t, d = x.shape[1], x.shape[2]
    dq = d // N_CHIPS
    xs = x.reshape(t, d)
    target = loss_target.reshape(t, d)
    w_in = [conv_w_in_0, sb_w_in_1, conv_w_in_2, sb_w_in_3]
    w_out = [conv_w_out_0, sb_w_out_1, conv_w_out_2, sb_w_out_3]
    m_in = [m_conv_w_in_0, m_sb_w_in_1, m_conv_w_in_2, m_sb_w_in_3]
    m_out = [m_conv_w_out_0, m_sb_w_out_1, m_conv_w_out_2, m_sb_w_out_3]
    v_in = [v_conv_w_in_0, v_sb_w_in_1, v_conv_w_in_2, v_sb_w_in_3]
    v_out = [v_conv_w_out_0, v_sb_w_out_1, v_conv_w_out_2, v_sb_w_out_3]
    ln_pre = [ln_pre_0, ln_pre_1, ln_pre_2, ln_pre_3]
    ln_post = [ln_post_0, ln_post_1, ln_post_2, ln_post_3]
    conv_w = [conv_w_0, conv_w_2]
    m_conv = [m_conv_w_0, m_conv_w_2]
    v_conv = [v_conv_w_0, v_conv_w_2]
    chip = 2 * lax.axis_index("x") + lax.axis_index("y")
    chip_arr = jnp.reshape(chip, (1,)).astype(jnp.int32)
    place = jnp.stack([chip, lax.axis_index("c")]).astype(jnp.int32)
    core_arr = jnp.reshape(lax.axis_index("c"), (1,)).astype(jnp.int32)

    packs = [_pack_weights(w_in[l], w_out[l], chip_arr) for l in range(N_LAYERS)]
    cw_local = jnp.concatenate([_pad_rows8(conv_w[0]), _pad_rows8(conv_w[1])], axis=0)
    gathered = list(packs)
    gathered[0], cw_all = _all_gather_weights(packs[:1], cw_local)
    cw_full = jnp.transpose(cw_all, (1, 0, 2)).reshape(16, d)
    conv_taps = {0: cw_full[0:8], 2: cw_full[8:16]}

    h_in, us, projs, gateds, ms, sb_saved = [], [], [], [], [], {}
    h = xs
    u = _norm_first(xs, ln_pre[0])
    for l in range(N_LAYERS):
        h_in.append(h)
        us.append(u)
        nxt = l + 1
        plan = {} if nxt == N_LAYERS else (AG_PLAN_CONV if l % 2 == 0 else AG_PLAN_SB)

        def rider_for(name):
            return _gather_rider(gathered[nxt:nxt + 1], plan[name]) if name in plan else None

        def take(results, name):
            if name not in plan:
                return results
            gathered[nxt] = results[-1]
            return results[:-1]

        proj, = take(_mm_proj(u, gathered[l], rider=rider_for("mm_proj")), "mm_proj")
        if l % 2 == 0:
            gated, = take(_conv_fwd(proj, conv_taps[l], rider=rider_for("conv_fwd")), "conv_fwd")
        else:
            gated, o, car = take(_sb_fwd(proj, rider=rider_for("sb_fwd")), "sb_fwd")
            sb_saved[l] = (o, car)
        m, = take(_mm_out(gated, gathered[l], rider=rider_for("mm_out")), "mm_out")
        projs.append(proj)
        gateds.append(gated)
        ms.append(m)
        if l < N_LAYERS - 1:
            h, u = take(_norm_mid(h, m, ln_post[l], ln_pre[nxt], rider=rider_for("norm_mid")), "norm_mid")
    dh, dm, dg_post_last, loss_part = _norm_last(h, ms[-1], ln_post[-1], target)

    dg_pre = [None] * N_LAYERS
    dg_post = [None] * N_LAYERS
    dg_post[N_LAYERS - 1] = dg_post_last
    dconv = {}
    sums = [None] * N_LAYERS
    got = [None] * N_LAYERS
    for l in reversed(range(N_LAYERS)):
        above = l + 1 if l + 1 < N_LAYERS else None
        dgated = _mm_dgated(dm, gathered[l])
        gp = _mm_dwout(gateds[l], dm)
        if l % 2 == 0:
            dproj, dconv[l] = _conv_bwd(projs[l], dgated, conv_taps[l])
        elif above is not None:
            o, car = sb_saved[l]
            dproj, got[above] = _sb_bwd(projs[l], dgated, o, car, rider=_send_sums_rider(sums[above]))
        else:
            o, car = sb_saved[l]
            dproj, = _sb_bwd(projs[l], dgated, o, car)
        if l == 0:
            gp, got[1] = _mm_dwin(us[0], dproj, gp, rider=_send_sums_rider(sums[1], None, 0, 4, 3))
            theirs, got[1] = _comm_call(
                "swap_last", _join_riders(_swap_rider(gp), _send_sums_rider(sums[1], got[1], 3, 4, 1)))
            sums[0] = _presum(gp, theirs, core_arr)
            du, got[0] = _mm_du(dproj, gathered[0], rider=_send_sums_rider(sums[0], None, 0, 4, 3))
        elif l % 2 == 0:
            gp, got[above] = _mm_dwin(us[l], dproj, gp, rider=_send_sums_rider(sums[above], None, 0, 2))
            du, theirs, got[above] = _mm_du(
                dproj, gathered[l], rider=_join_riders(_swap_rider(gp), _send_sums_rider(sums[above], got[above], 1, 2)))
            sums[l] = _presum(gp, theirs, core_arr)
        else:
            gp, = _mm_dwin(us[l], dproj, gp)
            du, theirs = _mm_du(dproj, gathered[l], rider=_swap_rider(gp))
            sums[l] = _presum(gp, theirs, core_arr)
        if l > 0:
            dh, dm, dg_pre[l], dg_post[l - 1] = _norm_bwd_mid(dh, du, h_in[l], ln_pre[l], ms[l - 1], ln_post[l - 1])
    grad_x, dg_pre[0] = _norm_bwd_first(dh, du, h_in[0], ln_pre[0])

    loss_rows = jnp.pad(loss_part, ((0, 0), (0, d - loss_part.shape[1])))
    small = _pack_small(dg_pre + dg_post, [dconv[0], dconv[2]], loss_rows)
    got[0], small_all = _comm_call(
        "exchange_last", _join_riders(_send_sums_rider(sums[0], got[0], 3, 4, 1), _small_rider(small)))
    small_sum = _sum_sources("sum_small", small_all)

    halves = [_sum_grad_half(got[l], sums[l], place) for l in range(N_LAYERS)]
    fulls = [f.reshape(d + dq, d) for f in _exchange_halves(halves)]
    res_in = [_adamw("adamw_w_in", w_in[l], m_in[l], v_in[l], fulls[l], 0) for l in range(N_LAYERS)]
    res_out = [_adamw("adamw_w_out", w_out[l], m_out[l], v_out[l], fulls[l], d) for l in range(N_LAYERS)]
    ln_all = ln_pre + ln_post
    ln_m = [m_ln_pre_0, m_ln_pre_1, m_ln_pre_2, m_ln_pre_3, m_ln_post_0, m_ln_post_1, m_ln_post_2, m_ln_post_3]
    ln_v = [v_ln_pre_0, v_ln_pre_1, v_ln_pre_2, v_ln_pre_3, v_ln_post_0, v_ln_post_1, v_ln_post_2, v_ln_post_3]
    res_ln = _adamw("adamw_ln", jnp.stack(ln_all), jnp.stack(ln_m), jnp.stack(ln_v), small_sum[0:2 * N_LAYERS])
    conv_g = [_pad_rows8(lax.dynamic_slice(small_sum, (2 * N_LAYERS + 3 * i, chip * dq), (3, dq))) for i in range(2)]
    stack8 = lambda a, b: jnp.concatenate([_pad_rows8(a), _pad_rows8(b)], axis=0)
    res_conv = _adamw("adamw_conv", stack8(*conv_w), stack8(*m_conv), stack8(*v_conv), jnp.concatenate(conv_g, axis=0))

    def leaf(kind, l, which):
        if kind == "ln_pre":
            return res_ln[which][l]
        if kind == "ln_post":
            return res_ln[which][N_LAYERS + l]
        if kind == "w_in":
            return res_in[l][which]
        if kind == "w_out":
            return res_out[l][which]
        return res_conv[which][8 * (l // 2):8 * (l // 2) + 3]

    order = []
    for l in range(N_LAYERS):
        order.append(("ln_pre", l))
        order.append(("w_in", l))
        if l % 2 == 0:
            order.append(("conv", l))
        order.append(("w_out", l))
        order.append(("ln_post", l))
    loss = small_sum[2 * N_LAYERS + 3 * 2, 0]
    outs = [loss, grad_x.reshape(1, t, d)]
    for which in range(4):
        outs.extend(leaf(kind, l, which) for kind, l in order)
    return tuple(outs)
```

```python
import functools
import math
from typing import Any, Callable, Mapping, NamedTuple, Sequence

import jax
import jax.numpy as jnp
from jax import lax
from jax.experimental import pallas as pl
from jax.experimental.pallas import tpu as pltpu

F32 = jnp.float32
BF16 = jnp.bfloat16

N_CHIPS = 4
N_DEV = 8
N_LAYERS = 4
HEAD_DIM = 128
RMS_EPS = 1e-6
ADAM_LR = 0.001
ADAM_B1 = 0.9
ADAM_B2 = 0.999
ADAM_EPS = 1e-08
ADAM_WD = 0.01
ADAM_STEP = 10

VMEM_LIMIT = 56 * 1024 * 1024
MESH_IDS = pl.DeviceIdType.MESH
HBM_SPEC = pl.BlockSpec(memory_space=pltpu.HBM)

NN = (((1,), (0,)), ((), ()))
NT = (((1,), (1,)), ((), ()))
TN = (((0,), (0,)), ((), ()))


def _params(n_axes):
    return pltpu.CompilerParams(dimension_semantics=("arbitrary",) * n_axes, vmem_limit_bytes=VMEM_LIMIT)


def _dot(a, b, dims):
    return lax.dot_general(a, b, dims, preferred_element_type=F32)


def _sigmoid(z):
    return 1.0 / (1.0 + jnp.exp(-z))


class _Rider(NamedTuple):
    operands: Sequence[Any]
    out_shapes: Sequence[Any]
    aliases: Mapping[int, int]
    sems: Sequence[Any]
    start: Callable
    finish: Callable


def _compute_call(body, name, *, grid, in_specs, operands, out_specs, out_shape, scratch=(), aliases=None, rider=None):
    in_specs, operands = list(in_specs), list(operands)
    out_specs, out_shape, scratch = list(out_specs), list(out_shape), list(scratch)
    aliases = dict(aliases or {})
    n_in, n_out, n_scratch = len(operands), len(out_shape), len(scratch)
    hosted = body
    if rider is not None:
        r_in, r_out = len(rider.operands), len(rider.out_shapes)
        aliases.update({n_in + i: n_out + o for i, o in rider.aliases.items()})

        def hosted(*refs):
            ins, refs = refs[:n_in], refs[n_in:]
            rider_ins, refs = refs[:r_in], refs[r_in:]
            outs, refs = refs[:n_out], refs[n_out:]
            rider_outs, refs = refs[:r_out], refs[r_out:]
            own_scratch, rider_sems = refs[:n_scratch], refs[n_scratch:]
            ids = [pl.program_id(axis) for axis in range(len(grid))]
            first = functools.reduce(jnp.logical_and, [i == 0 for i in ids])
            last = functools.reduce(jnp.logical_and, [i == g - 1 for i, g in zip(ids, grid)])

            @pl.when(first)
            def _():
                rider.start(rider_ins, rider_outs, rider_sems)

            body(*ins, *outs, *own_scratch)

            @pl.when(last)
            def _():
                rider.finish(rider_ins, rider_outs, rider_sems)

        in_specs += [HBM_SPEC] * r_in
        operands += list(rider.operands)
        out_specs += [HBM_SPEC] * r_out
        out_shape += list(rider.out_shapes)
        scratch += list(rider.sems)
    return pl.pallas_call(
        hosted, name=name, grid=grid, in_specs=in_specs, out_specs=out_specs, out_shape=out_shape,
        scratch_shapes=scratch, input_output_aliases=aliases, compiler_params=_params(len(grid)),
    )(*operands)


def _matmul(name, a, b, *, grid, a_spec, b_spec, o_spec, out_shape, dims, reduce_axis=None, acc_shape=None,
            alias_out=None, rider=None):
    out_dtype = out_shape.dtype
    direct = reduce_axis is not None and out_dtype == F32
    n_red = grid[reduce_axis] if reduce_axis is not None else 1

    def body(*refs):
        if alias_out is not None:
            refs = refs[1:]
        a_ref, b_ref, o_ref = refs[:3]
        if reduce_axis is None:
            o_ref[...] = _dot(a_ref[...], b_ref[...], dims).astype(out_dtype)
            return
        acc_ref = o_ref if direct else refs[3]
        k = pl.program_id(reduce_axis)

        @pl.when(k == 0)
        def _():
            acc_ref[...] = jnp.zeros_like(acc_ref)

        acc_ref[...] += _dot(a_ref[...], b_ref[...], dims)

        if not direct:
            @pl.when(k == n_red - 1)
            def _():
                o_ref[...] = acc_ref[...].astype(out_dtype)

    scratch = []
    if reduce_axis is not None and not direct:
        scratch = [pltpu.VMEM(acc_shape, F32)]
    in_specs = [a_spec, b_spec]
    operands = [a, b]
    aliases = {}
    if alias_out is not None:
        in_specs = [HBM_SPEC] + in_specs
        operands = [alias_out] + operands
        aliases = {0: 0}
    return _compute_call(body, name, grid=grid, in_specs=in_specs, operands=operands, out_specs=[o_spec],
                         out_shape=[out_shape], scratch=scratch, aliases=aliases, rider=rider)


def _mm_proj(u, g, rider=None):
    t, d = u.shape
    tm = min(512, t)
    return _matmul(
        "mm_proj", u, g, grid=(N_CHIPS, t // tm),
        a_spec=pl.BlockSpec((tm, d), lambda s, m: (m, 0)),
        b_spec=pl.BlockSpec((None, d, d), lambda s, m: (s, 0, 0)),
        o_spec=pl.BlockSpec((None, tm, d), lambda s, m: (s, m, 0)),
        out_shape=jax.ShapeDtypeStruct((N_CHIPS, t, d), BF16), dims=NN, rider=rider)


def _mm_out(gated, g, rider=None):
    t, d = gated.shape
    dq = d // N_CHIPS
    tm = min(512, t)

    def body(a_ref, b_ref, o_ref):
        acc = _dot(a_ref[:, 0:dq], b_ref[0], NN)
        for s in range(1, N_CHIPS):
            acc = acc + _dot(a_ref[:, s * dq:(s + 1) * dq], b_ref[s], NN)
        o_ref[...] = acc

    return _compute_call(
        body, "mm_out", grid=(t // tm,),
        in_specs=[pl.BlockSpec((tm, d), lambda m: (m, 0)), pl.BlockSpec((N_CHIPS, dq, d), lambda m: (0, N_CHIPS, 0))],
        operands=[gated, g], out_specs=[pl.BlockSpec((tm, d), lambda m: (m, 0))],
        out_shape=[jax.ShapeDtypeStruct((t, d), F32)], rider=rider)


def _mm_dgated(dm, g):
    t, d = dm.shape
    dq = d // N_CHIPS
    tm = min(512, t)

    def body(a_ref, b_ref, o_ref):
        a = a_ref[...]
        for s in range(N_CHIPS):
            o_ref[:, s * dq:(s + 1) * dq] = _dot(a, b_ref[s], NT).astype(BF16)

    return _compute_call(
        body, "mm_dgated", grid=(t // tm,),
        in_specs=[pl.BlockSpec((tm, d), lambda m: (m, 0)), pl.BlockSpec((N_CHIPS, dq, d), lambda m: (0, N_CHIPS, 0))],
        operands=[dm, g], out_specs=[pl.BlockSpec((tm, d), lambda m: (m, 0))],
        out_shape=[jax.ShapeDtypeStruct((t, d), BF16)])[0]


def _mm_dwout(gated, dm):
    t, d = gated.shape
    dq = d // N_CHIPS
    tk = min(1024, t)
    return _matmul(
        "mm_dwout", gated, dm, grid=(N_CHIPS, t // tk),
        a_spec=pl.BlockSpec((tk, dq), lambda s, k: (k, s)),
        b_spec=pl.BlockSpec((tk, d), lambda s, k: (k, 0)),
        o_spec=pl.BlockSpec((None, dq, d), lambda s, k: (s, N_CHIPS, 0)),
        out_shape=jax.ShapeDtypeStruct((N_CHIPS, d + dq, d), BF16), dims=TN, reduce_axis=1, acc_shape=(dq, d))[0]


def _mm_du(dproj, g, rider=None):
    _, t, d = dproj.shape
    tm = min(512, t)
    return _matmul(
        "mm_du", dproj, g, grid=(t // tm, N_CHIPS),
        a_spec=pl.BlockSpec((None, tm, d), lambda m, s: (s, m, 0)),
        b_spec=pl.BlockSpec((None, d, d), lambda m, s: (s, 0, 0)),
        o_spec=pl.BlockSpec((tm, d), lambda m, s: (m, 0)),
        out_shape=jax.ShapeDtypeStruct((t, d), F32), dims=NT, reduce_axis=1, rider=rider)


def _mm_dwin(u, dproj, gp, rider=None):
    t, d = u.shape
    tmo = min(1024, d)
    tk = min(1024, t)
    return _matmul(
        "mm_dwin", u, dproj, grid=(N_CHIPS, d // tmo, t // tk),
        a_spec=pl.BlockSpec((tk, tmo), lambda s, mo, k: (k, mo)),
        b_spec=pl.BlockSpec((None, tk, d), lambda s, mo, k: (s, k, 0)),
        o_spec=pl.BlockSpec((None, tmo, d), lambda s, mo, k: (s, mo, 0)),
        out_shape=jax.ShapeDtypeStruct(gp.shape, BF16), dims=TN, reduce_axis=2, acc_shape=(tmo, d), alias_out=gp,
        rider=rider)


def _rms(v):
    r = lax.rsqrt(jnp.mean(v * v, axis=-1, keepdims=True) + RMS_EPS)
    return v * r, r


def _rms_bwd(dout, n, r, gain):
    dn = dout * gain
    return r * (dn - n * jnp.mean(dn * n, axis=-1, keepdims=True))


def _fold8(v):
    return jnp.sum(v.reshape(v.shape[0] // 8, 8, v.shape[1]), axis=0)


def _row0(total):
    rows = lax.broadcasted_iota(jnp.int32, total.shape, 0)
    return jnp.where(rows == 0, jnp.sum(total, axis=0, keepdims=True), 0.0)


def _norm_tile(t):
    return min(256, t)


def _norm_first(x, g_pre):
    t, d = x.shape
    tr = _norm_tile(t)

    def body(x_ref, g_ref, u_ref):
        n, _ = _rms(x_ref[...])
        u_ref[...] = (n * g_ref[...]).astype(BF16)

    row = pl.BlockSpec((tr, d), lambda i: (i, 0))
    vec = pl.BlockSpec((1, d), lambda i: (0, 0))
    return pl.pallas_call(
        body, name="norm_first", grid=(t // tr,), in_specs=[row, vec], out_specs=row,
        out_shape=jax.ShapeDtypeStruct((t, d), BF16), compiler_params=_params(1),
    )(x, g_pre.reshape(1, d))


def _norm_mid(h, m, g_post, g_pre_next, rider=None):
    t, d = h.shape
    tr = _norm_tile(t)

    def body(h_ref, m_ref, gp_ref, gn_ref, hn_ref, u_ref):
        n, _ = _rms(m_ref[...])
        hn = h_ref[...] + n * gp_ref[...]
        hn_ref[...] = hn
        n2, _ = _rms(hn)
        u_ref[...] = (n2 * gn_ref[...]).astype(BF16)

    row = pl.BlockSpec((tr, d), lambda i: (i, 0))
    vec = pl.BlockSpec((1, d), lambda i: (0, 0))
    return _compute_call(
        body, "norm_mid", grid=(t // tr,), in_specs=[row, row, vec, vec],
        operands=[h, m, g_post.reshape(1, d), g_pre_next.reshape(1, d)], out_specs=[row, row],
        out_shape=[jax.ShapeDtypeStruct((t, d), F32), jax.ShapeDtypeStruct((t, d), BF16)], rider=rider)


def _norm_last(h, m, g_post, target):
    t, d = h.shape
    tr = _norm_tile(t)
    nsteps = t // tr

    def body(h_ref, m_ref, gp_ref, tg_ref, dy_ref, dm_ref, dgp_ref, loss_ref, acc_g, acc_l):
        i = pl.program_id(0)

        @pl.when(i == 0)
        def _():
            acc_g[...] = jnp.zeros_like(acc_g)
            acc_l[...] = jnp.zeros_like(acc_l)

        gain = gp_ref[...]
        n, r = _rms(m_ref[...])
        err = h_ref[...] + n * gain - tg_ref[...]
        dy = err / d
        dy_ref[...] = dy
        dm_ref[...] = _rms_bwd(dy, n, r, gain).astype(BF16)
        acc_g[...] += _fold8(dy * n)
        acc_l[...] += _fold8(err * err)

        @pl.when(i == nsteps - 1)
        def _():
            dgp_ref[...] = _row0(acc_g[...])
            loss_ref[...] = jnp.zeros((8, 128), F32) + (0.5 / d) * jnp.sum(acc_l[...])

    row = pl.BlockSpec((tr, d), lambda i: (i, 0))
    vec = pl.BlockSpec((1, d), lambda i: (0, 0))
    acc = pl.BlockSpec((8, d), lambda i: (0, 0))
    return pl.pallas_call(
        body, name="norm_last", grid=(nsteps,), in_specs=[row, row, vec, row],
        out_specs=[row, row, acc, pl.BlockSpec((8, 128), lambda i: (0, 0))],
        out_shape=[jax.ShapeDtypeStruct((t, d), F32), jax.ShapeDtypeStruct((t, d), BF16),
                   jax.ShapeDtypeStruct((8, d), F32), jax.ShapeDtypeStruct((8, 128), F32)],
        scratch_shapes=[pltpu.VMEM((8, d), F32), pltpu.VMEM((8, d), F32)],
        compiler_params=_params(1),
    )(h, m, g_post.reshape(1, d), target)


def _norm_bwd_mid(dh, du, h_in, g_pre, m_prev, g_post_prev):
    t, d = dh.shape
    tr = _norm_tile(t)
    nsteps = t // tr

    def body(dh_ref, du_ref, h_ref, gpre_ref, m_ref, gpost_ref, dhn_ref, dm_ref, dgpre_ref, dgpost_ref, acc_a, acc_b):
        i = pl.program_id(0)

        @pl.when(i == 0)
        def _():
            acc_a[...] = jnp.zeros_like(acc_a)
            acc_b[...] = jnp.zeros_like(acc_b)

        du_t = du_ref[...]
        n, r = _rms(h_ref[...])
        dhn = dh_ref[...] + _rms_bwd(du_t, n, r, gpre_ref[...])
        dhn_ref[...] = dhn
        acc_a[...] += _fold8(du_t * n)
        n2, r2 = _rms(m_ref[...])
        dm_ref[...] = _rms_bwd(dhn, n2, r2, gpost_ref[...]).astype(BF16)
        acc_b[...] += _fold8(dhn * n2)

        @pl.when(i == nsteps - 1)
        def _():
            dgpre_ref[...] = _row0(acc_a[...])
            dgpost_ref[...] = _row0(acc_b[...])

    row = pl.BlockSpec((tr, d), lambda i: (i, 0))
    vec = pl.BlockSpec((1, d), lambda i: (0, 0))
    acc = pl.BlockSpec((8, d), lambda i: (0, 0))
    return pl.pallas_call(
        body, name="norm_bwd_mid", grid=(nsteps,), in_specs=[row, row, row, vec, row, vec],
        out_specs=[row, row, acc, acc],
        out_shape=[jax.ShapeDtypeStruct((t, d), F32), jax.ShapeDtypeStruct((t, d), BF16),
                   jax.ShapeDtypeStruct((8, d), F32), jax.ShapeDtypeStruct((8, d), F32)],
        scratch_shapes=[pltpu.VMEM((8, d), F32), pltpu.VMEM((8, d), F32)],
        compiler_params=_params(1),
    )(dh, du, h_in, g_pre.reshape(1, d), m_prev, g_post_prev.reshape(1, d))


def _norm_bwd_first(dh, du, x, g_pre, rider=None):
    t, d = dh.shape
    tr = _norm_tile(t)
    nsteps = t // tr

    def body(dh_ref, du_ref, x_ref, gpre_ref, dx_ref, dgpre_ref, acc_a):
        i = pl.program_id(0)

        @pl.when(i == 0)
        def _():
            acc_a[...] = jnp.zeros_like(acc_a)

        du_t = du_ref[...]
        n, r = _rms(x_ref[...])
        dx_ref[...] = dh_ref[...] + _rms_bwd(du_t, n, r, gpre_ref[...])
        acc_a[...] += _fold8(du_t * n)

        @pl.when(i == nsteps - 1)
        def _():
            dgpre_ref[...] = _row0(acc_a[...])

    row = pl.BlockSpec((tr, d), lambda i: (i, 0))
    vec = pl.BlockSpec((1, d), lambda i: (0, 0))
    acc = pl.BlockSpec((8, d), lambda i: (0, 0))
    return _compute_call(
        body, "norm_bwd_first", grid=(nsteps,), in_specs=[row, row, row, vec],
        operands=[dh, du, x, g_pre.reshape(1, d)], out_specs=[row, acc],
        out_shape=[jax.ShapeDtypeStruct((t, d), F32), jax.ShapeDtypeStruct((8, d), F32)],
        scratch=[pltpu.VMEM((8, d), F32)], rider=rider)


CONV_TC = 128
CONV_HALO = 16


def _conv_chunk(t):
    return min(512, t)


def _shift_down(v, steps, fill):
    rows = lax.broadcasted_iota(jnp.int32, v.shape, 0)
    out = pltpu.roll(v, steps, axis=0)
    for k in range(steps):
        out = jnp.where(rows == k, fill[CONV_HALO - steps + k:CONV_HALO - steps + k + 1, :], out)
    return out


def _shift_up(v, steps, fill):
    nrows = v.shape[0]
    rows = lax.broadcasted_iota(jnp.int32, v.shape, 0)
    out = pltpu.roll(v, nrows - steps, axis=0)
    for k in range(steps):
        out = jnp.where(rows == nrows - steps + k, fill[k:k + 1, :], out)
    return out


def _conv_fwd(proj, cw, rider=None):
    _, t, d = proj.shape
    chunk = _conv_chunk(t)

    def body(p_ref, w_ref, o_ref):
        w = w_ref[...]
        w0, w1, w2 = w[0:1, :], w[1:2, :], w[2:3, :]
        for ci in range(t // chunk):
            t0 = ci * chunk
            rows = pl.ds(t0, chunk)
            b = p_ref[0, rows, :].astype(F32)
            cx = p_ref[1, rows, :].astype(F32) * p_ref[2, rows, :].astype(F32)
            z = p_ref[3, rows, :].astype(F32)
            if ci == 0:
                prev = jnp.zeros((CONV_HALO, CONV_TC), F32)
            else:
                halo = pl.ds(t0 - CONV_HALO, CONV_HALO)
                prev = p_ref[1, halo, :].astype(F32) * p_ref[2, halo, :].astype(F32)
            conv = w2 * cx + w1 * _shift_down(cx, 1, prev) + w0 * _shift_down(cx, 2, prev)
            o_ref[rows, :] = (z * _sigmoid(z) * b * conv).astype(BF16)

    return _compute_call(
        body, "conv_fwd", grid=(d // CONV_TC,),
        in_specs=[pl.BlockSpec((N_CHIPS, t, CONV_TC), lambda j: (0, 0, j)), pl.BlockSpec((8, CONV_TC), lambda j: (0, j))],
        operands=[proj, cw], out_specs=[pl.BlockSpec((t, CONV_TC), lambda j: (0, j))],
        out_shape=[jax.ShapeDtypeStruct((t, d), BF16)], rider=rider)


def _conv_bwd(proj, dgated, cw):
    _, t, d = proj.shape
    chunk = _conv_chunk(t)
    nchunks = t // chunk

    def body(p_ref, dg_ref, w_ref, dp_ref, dw_ref):
        w = w_ref[...]
        w0, w1, w2 = w[0:1, :], w[1:2, :], w[2:3, :]
        dw0 = jnp.zeros((1, CONV_TC), F32)
        dw1 = jnp.zeros((1, CONV_TC), F32)
        dw2 = jnp.zeros((1, CONV_TC), F32)
        for ci in range(nchunks):
            t0 = ci * chunk
            rows = pl.ds(t0, chunk)
            b = p_ref[0, rows, :].astype(F32)
            c = p_ref[1, rows, :].astype(F32)
            xt = p_ref[2, rows, :].astype(F32)
            z = p_ref[3, rows, :].astype(F32)
            dg = dg_ref[rows, :].astype(F32)
            cx = c * xt
            if ci == 0:
                prev = jnp.zeros((CONV_HALO, CONV_TC), F32)
            else:
                halo = pl.ds(t0 - CONV_HALO, CONV_HALO)
                prev = p_ref[1, halo, :].astype(F32) * p_ref[2, halo, :].astype(F32)
            cx1 = _shift_down(cx, 1, prev)
            cx2 = _shift_down(cx, 2, prev)
            conv = w2 * cx + w1 * cx1 + w0 * cx2
            sig = _sigmoid(z)
            dy = dg * (z * sig)
            dconv = dy * b
            if ci == nchunks - 1:
                nxt = jnp.zeros((CONV_HALO, CONV_TC), F32)
            else:
                halo = pl.ds(t0 + chunk, CONV_HALO)
                zn = p_ref[3, halo, :].astype(F32)
                nxt = dg_ref[halo, :].astype(F32) * (zn * _sigmoid(zn)) * p_ref[0, halo, :].astype(F32)
            dcx = w2 * dconv + w1 * _shift_up(dconv, 1, nxt) + w0 * _shift_up(dconv, 2, nxt)
            dp_ref[0, rows, :] = (dy * conv).astype(BF16)
            dp_ref[1, rows, :] = (dcx * xt).astype(BF16)
            dp_ref[2, rows, :] = (dcx * c).astype(BF16)
            dp_ref[3, rows, :] = (dg * (b * conv) * (sig * (1.0 + z * (1.0 - sig)))).astype(BF16)
            dw0 = dw0 + jnp.sum(dconv * cx2, axis=0, keepdims=True)
            dw1 = dw1 + jnp.sum(dconv * cx1, axis=0, keepdims=True)
            dw2 = dw2 + jnp.sum(dconv * cx, axis=0, keepdims=True)
        taps = lax.broadcasted_iota(jnp.int32, (8, CONV_TC), 0)
        dw_ref[...] = jnp.where(taps == 0, dw0, jnp.where(taps == 1, dw1, jnp.where(taps == 2, dw2, 0.0)))

    return pl.pallas_call(
        body, name="conv_bwd", grid=(d // CONV_TC,),
        in_specs=[pl.BlockSpec((N_CHIPS, t, CONV_TC), lambda j: (0, 0, j)),
                  pl.BlockSpec((t, CONV_TC), lambda j: (0, j)),
                  pl.BlockSpec((8, CONV_TC), lambda j: (0, j))],
        out_specs=[pl.BlockSpec((N_CHIPS, t, CONV_TC), lambda j: (0, 0, j)), pl.BlockSpec((8, CONV_TC), lambda j: (0, j))],
        out_shape=[jax.ShapeDtypeStruct((N_CHIPS, t, d), BF16), jax.ShapeDtypeStruct((8, d), F32)],
        compiler_params=_params(1),
    )(proj, dgated, cw)


SB_DEAD_TAIL = -105.0
SB_COUNT_LANE = HEAD_DIM - 1


def _sb_block(t):
    return min(256, t)


def _split_dot(v, tri):
    hi = v.astype(BF16)
    lo = (v - hi.astype(F32)).astype(BF16)
    return _dot(hi, tri, NN) + _dot(lo, tri, NN)


SB_HEADS_PER_STEP = 4


def _sb_terms(s, diagonal):
    sp = jnp.maximum(s, 0.0) + jnp.log1p(jnp.exp(-jnp.abs(s)))
    if not diagonal:
        return -sp, s - sp, sp, None
    mask = lax.broadcasted_iota(jnp.int32, s.shape, 1) < lax.broadcasted_iota(jnp.int32, s.shape, 0)
    return jnp.where(mask, -sp, 0.0), s - sp, sp, mask


def _masked(mask, v):
    return v if mask is None else jnp.where(mask, v, 0.0)


def _sb_fwd(proj, rider=None):
    _, t, d = proj.shape
    heads = d // HEAD_DIM
    blk = _sb_block(t)
    nblk = t // blk
    scale = 1.0 / math.sqrt(HEAD_DIM)

    hps = SB_HEADS_PER_STEP
    width = hps * HEAD_DIM

    def body(q_ref, k_ref, v_ref, z_ref, gated_ref, o_ref, car_ref, tail_ref, acc_ref):
        i = pl.program_id(1)
        r_i = lax.broadcasted_iota(jnp.int32, (blk, blk), 0)
        c_i = lax.broadcasted_iota(jnp.int32, (blk, blk), 1)
        tri_after = (r_i > c_i).astype(BF16)
        lanes = lax.broadcasted_iota(jnp.int32, (blk, HEAD_DIM), 1)

        tail_ref[...] = jnp.zeros_like(tail_ref)
        acc_ref[...] = jnp.zeros_like(acc_ref)
        car_ref[...] = jnp.zeros_like(car_ref)

        def visit(j, diagonal):
            krows = pl.ds(pl.multiple_of(j * blk, blk), blk)
            hcols = [pl.ds(hh * HEAD_DIM, HEAD_DIM) for hh in range(hps)]
            logits = [_dot(q_ref[:, c], k_ref[krows, c], NT) for c in hcols]
            terms = [_sb_terms(s * scale, diagonal) for s in logits]
            within = [_split_dot(keep, tri_after) for keep, _, _, _ in terms]
            top = None
            for hh, (keep, log_beta, _, mask) in enumerate(terms):
                tail_b = tail_ref[hh]
                w = _masked(mask, jnp.exp(log_beta + tail_b[:, 0:1] + within[hh]))
                acc_ref[hh] += _dot(w.astype(BF16), v_ref[krows, hcols[hh]], NN)
                car_ref[hh] = jnp.where(lanes == j, tail_b, car_ref[hh])
                tail_new = tail_b + jnp.sum(keep, axis=1, keepdims=True)
                tail_ref[hh] = tail_new
                top = jnp.max(tail_new) if top is None else jnp.maximum(top, jnp.max(tail_new))
            return top > SB_DEAD_TAIL

        def more(state):
            jj, live = state
            return jnp.logical_and(jj <= i, live)

        def step(state):
            jj, _ = state
            return jj + 1, visit(i - jj, False)

        visited, _ = lax.while_loop(more, step, (jnp.int32(1), visit(i, True)))
        for hh in range(hps):
            cols = pl.ds(hh * HEAD_DIM, HEAD_DIM)
            car_ref[hh] = jnp.where(lanes == SB_COUNT_LANE, visited.astype(F32), car_ref[hh])
            z = z_ref[:, cols].astype(F32)
            acc = acc_ref[hh]
            o_ref[:, cols] = acc.astype(BF16)
            gated_ref[:, cols] = (z * _sigmoid(z) * acc).astype(BF16)

    qspec = lambda s: pl.BlockSpec((None, blk, width), lambda h, i: (s, i, h))
    kspec = lambda s: pl.BlockSpec((None, t, width), lambda h, i: (s, 0, h))
    ospec = pl.BlockSpec((blk, width), lambda h, i: (i, h))
    return _compute_call(
        body, "sb_fwd", grid=(heads // hps, nblk),
        in_specs=[qspec(0), kspec(1), kspec(2), qspec(3)], operands=[proj, proj, proj, proj],
        out_specs=[ospec, ospec, pl.BlockSpec((hps, blk, HEAD_DIM), lambda h, i: (h, i, 0))],
        out_shape=[jax.ShapeDtypeStruct((t, d), BF16), jax.ShapeDtypeStruct((t, d), BF16),
                   jax.ShapeDtypeStruct((heads, t, HEAD_DIM), F32)],
        scratch=[pltpu.VMEM((hps, blk, HEAD_DIM), F32), pltpu.VMEM((hps, blk, HEAD_DIM), F32)], rider=rider)


def _sb_bwd(proj, dgated, o, car, rider=None):
    _, t, d = proj.shape
    heads = d // HEAD_DIM
    blk = _sb_block(t)
    nblk = t // blk
    scale = 1.0 / math.sqrt(HEAD_DIM)

    hps = SB_HEADS_PER_STEP
    width = hps * HEAD_DIM

    def body(q_ref, k_ref, v_ref, z_ref, dg_ref, o_ref, car_ref, dp_ref, dk_acc, dv_acc, gsum_ref, dq_ref, do_ref):
        step_i = pl.program_id(1)
        i = nblk - 1 - step_i

        @pl.when(step_i == 0)
        def _():
            dk_acc[...] = jnp.zeros_like(dk_acc)
            dv_acc[...] = jnp.zeros_like(dv_acc)

        r_i = lax.broadcasted_iota(jnp.int32, (blk, blk), 0)
        c_i = lax.broadcasted_iota(jnp.int32, (blk, blk), 1)
        tri_after = (r_i > c_i).astype(BF16)
        tri_before = (r_i < c_i).astype(BF16)
        lanes = lax.broadcasted_iota(jnp.int32, (blk, HEAD_DIM), 1)

        gsum_ref[...] = jnp.zeros_like(gsum_ref)
        dq_ref[...] = jnp.zeros_like(dq_ref)
        for hh in range(hps):
            cols = pl.ds(hh * HEAD_DIM, HEAD_DIM)
            z = z_ref[:, cols].astype(F32)
            dg = dg_ref[:, cols].astype(F32)
            sig = _sigmoid(z)
            do_ref[hh] = (dg * (z * sig)).astype(BF16)
            dp_ref[3, :, cols] = (dg * o_ref[:, cols].astype(F32) * (sig * (1.0 + z * (1.0 - sig)))).astype(BF16)

        def visit(j, diagonal):
            krows = pl.ds(pl.multiple_of(j * blk, blk), blk)
            hcols = [pl.ds(hh * HEAD_DIM, HEAD_DIM) for hh in range(hps)]
            logits = [_dot(q_ref[:, c], k_ref[krows, c], NT) for c in hcols]
            dws = [_dot(do_ref[hh], v_ref[krows, c], NT) for hh, c in enumerate(hcols)]
            terms = [_sb_terms(s * scale, diagonal) for s in logits]
            within = [_split_dot(keep, tri_after) for keep, _, _, _ in terms]
            ws, gs = [], []
            for hh, (keep, log_beta, sp, mask) in enumerate(terms):
                tail = jnp.sum(jnp.where(lanes == j, car_ref[hh], 0.0), axis=1, keepdims=True)
                w = _masked(mask, jnp.exp(log_beta + tail + within[hh]))
                ws.append(w.astype(BF16))
                gs.append(w * dws[hh])
            g_within = [_split_dot(g, tri_before) for g in gs]
            for hh, (keep, log_beta, sp, mask) in enumerate(terms):
                c = hcols[hh]
                g_before = gsum_ref[hh]
                g_cum = g_before[:, 0:1] + g_within[hh]
                dl = (_masked(mask, gs[hh] - (gs[hh] + g_cum) * jnp.exp(log_beta)) * scale).astype(BF16)
                dq_ref[hh] += _dot(dl, k_ref[krows, c], NN)
                dk_acc[krows, c] += _dot(dl, q_ref[:, c], TN)
                dv_acc[krows, c] += _dot(ws[hh], do_ref[hh], TN)
                gsum_ref[hh] = g_before + jnp.sum(gs[hh], axis=1, keepdims=True)

        def step(j, carry):
            visit(j, False)
            return carry

        visited = jnp.max(jnp.where(lanes == SB_COUNT_LANE, car_ref[0], 0.0)).astype(jnp.int32)
        lax.fori_loop(i + 1 - visited, i, step, 0)
        visit(i, True)
        own = pl.ds(pl.multiple_of(i * blk, blk), blk)
        for hh in range(hps):
            cols = pl.ds(hh * HEAD_DIM, HEAD_DIM)
            dp_ref[0, :, cols] = dq_ref[hh].astype(BF16)
        dp_ref[1] = dk_acc[own, :].astype(BF16)
        dp_ref[2] = dv_acc[own, :].astype(BF16)

    qspec = lambda s: pl.BlockSpec((None, blk, width), lambda h, i: (s, nblk - 1 - i, h))
    kspec = lambda s: pl.BlockSpec((None, t, width), lambda h, i: (s, 0, h))
    tspec = pl.BlockSpec((blk, width), lambda h, i: (nblk - 1 - i, h))
    return _compute_call(
        body, "sb_bwd", grid=(heads // hps, nblk),
        in_specs=[qspec(0), kspec(1), kspec(2), qspec(3), tspec, tspec,
                  pl.BlockSpec((hps, blk, HEAD_DIM), lambda h, i: (h, nblk - 1 - i, 0))],
        operands=[proj, proj, proj, proj, dgated, o, car],
        out_specs=[pl.BlockSpec((N_CHIPS, blk, width), lambda h, i: (0, nblk - 1 - i, h))],
        out_shape=[jax.ShapeDtypeStruct((N_CHIPS, t, d), BF16)],
        scratch=[pltpu.VMEM((t, width), F32), pltpu.VMEM((t, width), F32),
                 pltpu.VMEM((hps, blk, HEAD_DIM), F32), pltpu.VMEM((hps, blk, HEAD_DIM), F32),
                 pltpu.VMEM((hps, blk, HEAD_DIM), BF16)], rider=rider)


def _pack_weights(w_in, w_out, chip):
    d = w_in.shape[0]
    rb = d // 8
    n_in = d // rb
    n_out = w_out.shape[0] // rb

    def body(chip_ref, wi_ref, wo_ref, o_ref):
        r = pl.program_id(0)

        @pl.when(r < n_in)
        def _():
            o_ref[...] = wi_ref[...].astype(BF16)

        @pl.when(r >= n_in)
        def _():
            o_ref[...] = wo_ref[...].astype(BF16)

    grid_spec = pltpu.PrefetchScalarGridSpec(
        num_scalar_prefetch=1, grid=(n_in + n_out,),
        in_specs=[pl.BlockSpec((rb, d), lambda r, me: (jnp.minimum(r, n_in - 1), 0)),
                  pl.BlockSpec((rb, d), lambda r, me: (jnp.maximum(r - n_in, 0), 0))],
        out_specs=pl.BlockSpec((None, rb, d), lambda r, me: (me[0], r, 0)))
    return pl.pallas_call(
        body, name="pack_weights", grid_spec=grid_spec,
        out_shape=jax.ShapeDtypeStruct((N_CHIPS, d + w_out.shape[0], d), BF16), compiler_params=_params(1),
    )(chip, w_in, w_out)


def _flip(v, bit):
    return 1 - v if bit else v


def _remote(src, dst, send_sem, recv_sem, target):
    return pltpu.make_async_remote_copy(src_ref=src, dst_ref=dst, send_sem=send_sem, recv_sem=recv_sem,
                                        device_id=target, device_id_type=MESH_IDS)


AG_CHUNKS = 8
AG_PLAN_CONV = {
    "mm_proj": dict(chunks=range(0, 5)),
    "conv_fwd": dict(chunks=range(5, 6), landed=range(0, 5)),
    "mm_out": dict(chunks=range(6, 7), landed=range(5, 6)),
    "norm_mid": dict(chunks=range(7, 8), landed=range(6, 7), pass_now=range(7, 8)),
}
AG_PLAN_SB = {
    "mm_proj": dict(chunks=range(0, 4)),
    "sb_fwd": dict(chunks=range(4, 8), landed=range(0, 4)),
    "mm_out": dict(landed=range(4, 8)),
}
HALF_CHUNKS = 8


SWAP_CHUNKS = 2


def _other_chips(x, y):
    return [(_flip(x, k >> 1), _flip(y, k & 1)) for k in (1, 2, 3)]


def _gather_sems(n):
    return [pltpu.SemaphoreType.DMA((3, n * AG_CHUNKS)) for _ in range(4)]


def _gather_pieces(g, chunks):
    hr = g[0].shape[1] // 2
    cr = hr // AG_CHUNKS
    return hr, [(l * AG_CHUNKS + q, g[l], q * cr, cr) for l in range(len(g)) for q in chunks]


def _pass_on(g, fsend, frecv, chunks):
    x, y, c = lax.axis_index("x"), lax.axis_index("y"), lax.axis_index("c")
    hr, pieces = _gather_pieces(g, chunks)
    for k, (px, py) in enumerate(_other_chips(x, y)):
        for i, ref, r0, cr in pieces:
            landed = ref.at[2 * px + py, pl.ds(c * hr + r0, cr)]
            _remote(landed, landed, fsend.at[k, i], frecv.at[k, i], (x, y, 1 - c)).start()


def _gather_start(g, send, recv, fsend, frecv, chunks=range(AG_CHUNKS), landed=()):
    x, y, c = lax.axis_index("x"), lax.axis_index("y"), lax.axis_index("c")
    hr, pieces = _gather_pieces(g, chunks)
    for k, (px, py) in enumerate(_other_chips(x, y)):
        for i, ref, r0, cr in pieces:
            piece = ref.at[2 * x + y, pl.ds(c * hr + r0, cr)]
            _remote(piece, piece, send.at[k, i], recv.at[k, i], (px, py, c)).start()
    _pass_on(g, fsend, frecv, landed)


def _gather_finish(g, send, recv, fsend, frecv, chunks=range(AG_CHUNKS), landed=(), pass_now=None):
    x, y, c = lax.axis_index("x"), lax.axis_index("y"), lax.axis_index("c")
    sibling = (x, y, 1 - c)
    pass_now = chunks if pass_now is None else pass_now
    hr, pieces = _gather_pieces(g, chunks)
    chips = _other_chips(x, y)
    for k, (px, py) in enumerate(chips):
        for i, ref, r0, cr in pieces:
            arrived = ref.at[2 * px + py, pl.ds(c * hr + r0, cr)]
            _remote(arrived, arrived, send.at[k, i], recv.at[k, i], (px, py, c)).wait_recv()
    _pass_on(g, fsend, frecv, pass_now)
    _, passed = _gather_pieces(g, list(landed) + list(pass_now))
    for k, (px, py) in enumerate(chips):
        for i, ref, r0, cr in passed:
            theirs = ref.at[2 * px + py, pl.ds((1 - c) * hr + r0, cr)]
            _remote(theirs, theirs, fsend.at[k, i], frecv.at[k, i], sibling).wait_recv()
    for k, (px, py) in enumerate(chips):
        for i, ref, r0, cr in pieces:
            mine = ref.at[2 * x + y, pl.ds(c * hr + r0, cr)]
            _remote(mine, mine, send.at[k, i], recv.at[k, i], (px, py, c)).wait_send()
        for i, ref, r0, cr in passed:
            mine = ref.at[2 * px + py, pl.ds(c * hr + r0, cr)]
            _remote(mine, mine, fsend.at[k, i], frecv.at[k, i], sibling).wait_send()


def _all_gather_weights(packs, cw):
    n = len(packs)

    def body(*refs):
        cw_ref = refs[n]
        g = refs[n + 1:2 * n + 1]
        cwg = refs[2 * n + 1]
        send, recv, fsend, frecv, csend, crecv, lsem = refs[2 * n + 2:]
        x, y, c = lax.axis_index("x"), lax.axis_index("y"), lax.axis_index("c")
        me = 2 * x + y
        local = pltpu.make_async_copy(cw_ref, cwg.at[me], lsem)
        local.start()
        taps = [_remote(cw_ref, cwg.at[me], csend.at[k], crecv.at[k], (px, py, c))
                for k, (px, py) in enumerate(_other_chips(x, y))]
        for cp in taps:
            cp.start()
        _gather_start(g, send, recv, fsend, frecv)
        _gather_finish(g, send, recv, fsend, frecv)
        for k, (px, py) in enumerate(_other_chips(x, y)):
            _remote(cw_ref, cwg.at[2 * px + py], csend.at[k], crecv.at[k], (px, py, c)).wait_recv()
        for cp in taps:
            cp.wait_send()
        local.wait()

    out_shape = [jax.ShapeDtypeStruct(p.shape, BF16) for p in packs]
    out_shape.append(jax.ShapeDtypeStruct((N_CHIPS,) + cw.shape, cw.dtype))
    return pl.pallas_call(
        body, name="all_gather_weights", in_specs=[HBM_SPEC] * (n + 1), out_specs=[HBM_SPEC] * (n + 1),
        out_shape=out_shape, input_output_aliases={l: l for l in range(n)},
        scratch_shapes=_gather_sems(n) + [pltpu.SemaphoreType.DMA((3,)), pltpu.SemaphoreType.DMA((3,)),
                                          pltpu.SemaphoreType.DMA],
    )(*packs, cw)


def _gather_rider(packs, chunks=(), landed=(), pass_now=()):
    return _Rider(
        operands=packs, out_shapes=[jax.ShapeDtypeStruct(p.shape, BF16) for p in packs],
        aliases={l: l for l in range(len(packs))}, sems=_gather_sems(len(packs)),
        start=lambda ins, outs, sems: _gather_start(outs, *sems, chunks, landed),
        finish=lambda ins, outs, sems: _gather_finish(outs, *sems, chunks, landed, pass_now))


def _join_riders(a, b):
    na, oa, sa = len(a.operands), len(a.out_shapes), len(a.sems)
    aliases = dict(a.aliases)
    aliases.update({na + i: oa + o for i, o in b.aliases.items()})
    return _Rider(
        operands=list(a.operands) + list(b.operands), out_shapes=list(a.out_shapes) + list(b.out_shapes),
        aliases=aliases, sems=list(a.sems) + list(b.sems),
        start=lambda ins, outs, sems: (a.start(ins[:na], outs[:oa], sems[:sa]),
                                       b.start(ins[na:], outs[oa:], sems[sa:])),
        finish=lambda ins, outs, sems: (a.finish(ins[:na], outs[:oa], sems[:sa]),
                                        b.finish(ins[na:], outs[oa:], sems[sa:])))


def _comm_call(name, rider):
    n_in, n_out = len(rider.operands), len(rider.out_shapes)

    def body(*refs):
        ins, outs, sems = refs[:n_in], refs[n_in:n_in + n_out], refs[n_in + n_out:]
        rider.start(ins, outs, sems)
        rider.finish(ins, outs, sems)

    return pl.pallas_call(
        body, name=name, in_specs=[HBM_SPEC] * n_in, out_specs=[HBM_SPEC] * n_out, out_shape=list(rider.out_shapes),
        scratch_shapes=list(rider.sems), input_output_aliases=dict(rider.aliases),
    )(*rider.operands)


def _small_rider(small):
    def peers():
        x, y, c = lax.axis_index("x"), lax.axis_index("y"), lax.axis_index("c")
        return 4 * x + 2 * y + c, [(_flip(x, r >> 2), _flip(y, (r >> 1) & 1), _flip(c, r & 1)) for r in range(1, N_DEV)]

    def start(ins, outs, sems):
        me, others = peers()
        pltpu.make_async_copy(ins[0], outs[0].at[me], sems[2]).start()
        for r, peer in enumerate(others):
            _remote(ins[0], outs[0].at[me], sems[0].at[r], sems[1].at[r], peer).start()

    def finish(ins, outs, sems):
        me, others = peers()
        for r, (tx, ty, tc) in enumerate(others):
            _remote(ins[0], outs[0].at[4 * tx + 2 * ty + tc], sems[0].at[r], sems[1].at[r], (tx, ty, tc)).wait_recv()
        for r, peer in enumerate(others):
            _remote(ins[0], outs[0].at[me], sems[0].at[r], sems[1].at[r], peer).wait_send()
        pltpu.make_async_copy(ins[0], outs[0].at[me], sems[2]).wait()

    return _Rider(
        operands=[small], out_shapes=[jax.ShapeDtypeStruct((N_DEV,) + small.shape, small.dtype)], aliases={},
        sems=[pltpu.SemaphoreType.DMA((N_DEV - 1,)), pltpu.SemaphoreType.DMA((N_DEV - 1,)), pltpu.SemaphoreType.DMA],
        start=start, finish=finish)


SMALL_ROWS = 16


def _pack_small(ln_parts, conv_parts, loss_rows):
    d = ln_parts[0].shape[1]
    n_ln, n_conv = len(ln_parts), len(conv_parts)

    def body(*refs):
        o_ref = refs[-1]
        rows = lax.broadcasted_iota(jnp.int32, (SMALL_ROWS, d), 0)
        acc = jnp.zeros((SMALL_ROWS, d), F32)
        for i in range(n_ln):
            acc = jnp.where(rows == i, refs[i][0:1, :], acc)
        for j in range(n_conv):
            for k in range(3):
                acc = jnp.where(rows == n_ln + 3 * j + k, refs[n_ln + j][k:k + 1, :], acc)
        acc = jnp.where(rows == n_ln + 3 * n_conv, refs[n_ln + n_conv][0:1, :], acc)
        o_ref[...] = acc

    n = n_ln + n_conv + 1
    return pl.pallas_call(
        body, name="pack_small", out_shape=jax.ShapeDtypeStruct((SMALL_ROWS, d), F32),
        in_specs=[pl.BlockSpec(memory_space=pltpu.VMEM)] * n, out_specs=pl.BlockSpec(memory_space=pltpu.VMEM),
    )(*ln_parts, *conv_parts, loss_rows)


def _send_sums_start(s_ref, got, send, recv, part=0, parts=1, span=1):
    x, y, c = lax.axis_index("x"), lax.axis_index("y"), lax.axis_index("c")
    nrows = s_ref.shape[1] // parts
    rows = pl.ds(part * nrows, span * nrows)
    for k, (px, py) in enumerate(_other_chips(x, y)):
        _remote(s_ref.at[2 * px + py, rows], got.at[k, rows], send.at[k], recv.at[k], (px, py, c)).start()


def _send_sums_finish(s_ref, got, send, recv, part=0, parts=1, span=1):
    x, y, c = lax.axis_index("x"), lax.axis_index("y"), lax.axis_index("c")
    nrows = s_ref.shape[1] // parts
    rows = pl.ds(part * nrows, span * nrows)
    for k, (px, py) in enumerate(_other_chips(x, y)):
        _remote(got.at[k, rows], got.at[k, rows], send.at[k], recv.at[k], (px, py, c)).wait_recv()
    for k, (px, py) in enumerate(_other_chips(x, y)):
        _remote(s_ref.at[2 * px + py, rows], got.at[k, rows], send.at[k], recv.at[k], (px, py, c)).wait_send()


def _send_sums_rider(sums, got=None, part=0, parts=1, span=1):
    _, hr, d = sums.shape
    return _Rider(
        operands=[sums] if got is None else [sums, got],
        out_shapes=[jax.ShapeDtypeStruct((N_CHIPS - 1, hr, d), BF16)], aliases={} if got is None else {1: 0},
        sems=[pltpu.SemaphoreType.DMA((3,)), pltpu.SemaphoreType.DMA((3,))],
        start=lambda ins, outs, sems: _send_sums_start(ins[0], outs[0], *sems, part, parts, span),
        finish=lambda ins, outs, sems: _send_sums_finish(ins[0], outs[0], *sems, part, parts, span))


def _swap_pieces(gp_ref, x_ref, c):
    hr = x_ref.shape[1]
    cr = hr // SWAP_CHUNKS
    return [(a * SWAP_CHUNKS + q, gp_ref.at[a, pl.ds((1 - c) * hr + q * cr, cr)], x_ref.at[a, pl.ds(q * cr, cr)])
            for a in range(N_CHIPS) for q in range(SWAP_CHUNKS)]


def _swap_rider(gp):
    _, p_rows, d = gp.shape

    def start(ins, outs, sems):
        x, y, c = lax.axis_index("x"), lax.axis_index("y"), lax.axis_index("c")
        for i, src, dst in _swap_pieces(ins[0], outs[0], c):
            _remote(src, dst, sems[0].at[i], sems[1].at[i], (x, y, 1 - c)).start()

    def finish(ins, outs, sems):
        x, y, c = lax.axis_index("x"), lax.axis_index("y"), lax.axis_index("c")
        pieces = _swap_pieces(ins[0], outs[0], c)
        for i, src, dst in pieces:
            _remote(dst, dst, sems[0].at[i], sems[1].at[i], (x, y, 1 - c)).wait_recv()
        for i, src, dst in pieces:
            _remote(src, dst, sems[0].at[i], sems[1].at[i], (x, y, 1 - c)).wait_send()

    nsem = N_CHIPS * SWAP_CHUNKS
    return _Rider(
        operands=[gp], out_shapes=[jax.ShapeDtypeStruct((N_CHIPS, p_rows // 2, d), BF16)], aliases={},
        sems=[pltpu.SemaphoreType.DMA((nsem,)), pltpu.SemaphoreType.DMA((nsem,))], start=start, finish=finish)


def _presum(gp, theirs, core):
    _, hr, d = theirs.shape
    tr = _row_tile(hr, 640)
    steps = hr // tr

    def body(core_ref, mine_ref, theirs_ref, o_ref):
        o_ref[...] = (mine_ref[...].astype(F32) + theirs_ref[...].astype(F32)).astype(BF16)

    grid_spec = pltpu.PrefetchScalarGridSpec(
        num_scalar_prefetch=1, grid=(N_CHIPS, steps),
        in_specs=[pl.BlockSpec((None, tr, d), lambda a, i, cr: (a, cr[0] * steps + i, 0)),
                  pl.BlockSpec((None, tr, d), lambda a, i, cr: (a, i, 0))],
        out_specs=pl.BlockSpec((None, tr, d), lambda a, i, cr: (a, i, 0)))
    return pl.pallas_call(
        body, name="presum", grid_spec=grid_spec,
        out_shape=jax.ShapeDtypeStruct((N_CHIPS, hr, d), BF16), compiler_params=_params(2),
    )(core, gp, theirs)


def _row_tile(rows, cap=128):
    if rows <= cap:
        return rows
    return next(tr for tr in range(cap, 0, -16) if rows % tr == 0)


def _sum_sources(name, parts):
    nsrc, rows, cols = parts.shape
    tr = _row_tile(rows)

    def body(p_ref, o_ref):
        total = p_ref[0].astype(F32)
        for s in range(1, nsrc):
            total = total + p_ref[s].astype(F32)
        o_ref[...] = total

    return pl.pallas_call(
        body, name=name, grid=(rows // tr,),
        in_specs=[pl.BlockSpec((nsrc, tr, cols), lambda i: (0, i, 0))],
        out_specs=pl.BlockSpec((tr, cols), lambda i: (i, 0)),
        out_shape=jax.ShapeDtypeStruct((rows, cols), F32), compiler_params=_params(1),
    )(parts)


def _sum_grad_half(got, sums, place):
    nsrc, hr, d = got.shape
    tr = _row_tile(hr, 256)

    def body(place_ref, got_ref, own_ref, o_ref):
        total = own_ref[...].astype(F32)
        for s in range(nsrc):
            total = total + got_ref[s].astype(F32)
        o_ref[...] = total

    grid_spec = pltpu.PrefetchScalarGridSpec(
        num_scalar_prefetch=1, grid=(hr // tr,),
        in_specs=[pl.BlockSpec((nsrc, tr, d), lambda i, pc: (0, i, 0)),
                  pl.BlockSpec((None, tr, d), lambda i, pc: (pc[0], i, 0))],
        out_specs=pl.BlockSpec((None, tr, d), lambda i, pc: (pc[1], i, 0)))
    return pl.pallas_call(
        body, name="sum_grad_half", grid_spec=grid_spec,
        out_shape=jax.ShapeDtypeStruct((2, hr, d), F32), compiler_params=_params(1),
    )(place, got, sums)


def _exchange_halves(fulls):
    n = len(fulls)
    _, hr, d = fulls[0].shape
    cr = hr // HALF_CHUNKS
    nc = n * HALF_CHUNKS

    def body(*refs):
        full = refs[n:2 * n]
        send, recv = refs[2 * n:]
        x, y, c = lax.axis_index("x"), lax.axis_index("y"), lax.axis_index("c")
        sibling = (x, y, 1 - c)
        pieces = [(l, q) for l in range(n) for q in range(HALF_CHUNKS)]
        sends = []
        for i, (l, q) in enumerate(pieces):
            piece = full[l].at[c, pl.ds(q * cr, cr)]
            sends.append(_remote(piece, piece, send.at[i], recv.at[i], sibling))
        for cp in sends:
            cp.start()
        for i, (l, q) in enumerate(pieces):
            theirs = full[l].at[1 - c, pl.ds(q * cr, cr)]
            _remote(theirs, theirs, send.at[i], recv.at[i], sibling).wait_recv()
        for cp in sends:
            cp.wait_send()

    return pl.pallas_call(
        body, name="exchange_halves", in_specs=[HBM_SPEC] * n, out_specs=[HBM_SPEC] * n,
        out_shape=[jax.ShapeDtypeStruct(f.shape, F32) for f in fulls], input_output_aliases={l: l for l in range(n)},
        scratch_shapes=[pltpu.SemaphoreType.DMA((nc,)), pltpu.SemaphoreType.DMA((nc,))],
    )(*fulls)


def _adamw(name, w, m, v, g, g_row0=0, rider=None):
    rows, cols = w.shape
    tr = _row_tile(rows, 256)
    off = g_row0 // tr

    def body(w_ref, m_ref, v_ref, g_ref, go_ref, d_ref, mo_ref, vo_ref):
        grad = g_ref[...]
        m_new = ADAM_B1 * m_ref[...] + (1.0 - ADAM_B1) * grad
        v_new = ADAM_B2 * v_ref[...] + (1.0 - ADAM_B2) * (grad * grad)
        m_hat = m_new / (1.0 - ADAM_B1 ** ADAM_STEP)
        v_hat = v_new / (1.0 - ADAM_B2 ** ADAM_STEP)
        go_ref[...] = grad
        d_ref[...] = -ADAM_LR * (m_hat / (jnp.sqrt(v_hat) + ADAM_EPS) + ADAM_WD * w_ref[...])
        mo_ref[...] = m_new
        vo_ref[...] = v_new

    blk = pl.BlockSpec((tr, cols), lambda i: (i, 0))
    return _compute_call(
        body, name, grid=(rows // tr,),
        in_specs=[blk, blk, blk, pl.BlockSpec((tr, cols), lambda i: (i + off, 0))], operands=[w, m, v, g],
        out_specs=[blk, blk, blk, blk], out_shape=[jax.ShapeDtypeStruct((rows, cols), F32)] * 4, rider=rider)


def _pad_rows8(a):
    return jnp.concatenate([a, jnp.zeros((8 - a.shape[0],) + a.shape[1:], a.dtype)], axis=0)


def kernel(x, ln_pre_0, conv_w_in_0, conv_w_0, conv_w_out_0, ln_post_0, ln_pre_1, sb_w_in_1, sb_w_out_1, ln_post_1, ln_pre_2, conv_w_in_2, conv_w_2, conv_w_out_2, ln_post_2, ln_pre_3, sb_w_in_3, sb_w_out_3, ln_post_3, loss_target, m_ln_pre_0, m_conv_w_in_0, m_conv_w_0, m_conv_w_out_0, m_ln_post_0, m_ln_pre_1, m_sb_w_in_1, m_sb_w_out_1, m_ln_post_1, m_ln_pre_2, m_conv_w_in_2, m_conv_w_2, m_conv_w_out_2, m_ln_post_2, m_ln_pre_3, m_sb_w_in_3, m_sb_w_out_3, m_ln_post_3, v_ln_pre_0, v_conv_w_in_0, v_conv_w_0, v_conv_w_out_0, v_ln_post_0, v_ln_pre_1, v_sb_w_in_1, v_sb_w_out_1, v_ln_post_1, v_ln_pre_2, v_conv_w_in_2, v_conv_w_2, v_conv_w_out_2, v_ln_post_2, v_ln_pre_3, v_sb_w_in_3, v_sb_w_out_3, v_ln_post_3):
    t, d = x.shape[1], x.shape[2]
    dq = d // N_CHIPS
    xs = x.reshape(t, d)
    target = loss_target.reshape(t, d)
    w_in = [conv_w_in_0, sb_w_in_1, conv_w_in_2, sb_w_in_3]
    w_out = [conv_w_out_0, sb_w_out_1, conv_w_out_2, sb_w_out_3]
    m_in = [m_conv_w_in_0, m_sb_w_in_1, m_conv_w_in_2, m_sb_w_in_3]
    m_out = [m_conv_w_out_0, m_sb_w_out_1, m_conv_w_out_2, m_sb_w_out_3]
    v_in = [v_conv_w_in_0, v_sb_w_in_1, v_conv_w_in_2, v_sb_w_in_3]
    v_out = [v_conv_w_out_0, v_sb_w_out_1, v_conv_w_out_2, v_sb_w_out_3]
    ln_pre = [ln_pre_0, ln_pre_1, ln_pre_2, ln_pre_3]
    ln_post = [ln_post_0, ln_post_1, ln_post_2, ln_post_3]
    conv_w = [conv_w_0, conv_w_2]
    m_conv = [m_conv_w_0, m_conv_w_2]
    v_conv = [v_conv_w_0, v_conv_w_2]
    chip = 2 * lax.axis_index("x") + lax.axis_index("y")
    chip_arr = jnp.reshape(chip, (1,)).astype(jnp.int32)
    place = jnp.stack([chip, lax.axis_index("c")]).astype(jnp.int32)
    core_arr = jnp.reshape(lax.axis_index("c"), (1,)).astype(jnp.int32)

    packs = [_pack_weights(w_in[l], w_out[l], chip_arr) for l in range(N_LAYERS)]
    cw_local = jnp.concatenate([_pad_rows8(conv_w[0]), _pad_rows8(conv_w[1])], axis=0)
    gathered = list(packs)
    gathered[0], cw_all = _all_gather_weights(packs[:1], cw_local)
    cw_full = jnp.transpose(cw_all, (1, 0, 2)).reshape(16, d)
    conv_taps = {0: cw_full[0:8], 2: cw_full[8:16]}

    h_in, us, projs, gateds, ms, sb_saved = [], [], [], [], [], {}
    h = xs
    u = _norm_first(xs, ln_pre[0])
    for l in range(N_LAYERS):
        h_in.append(h)
        us.append(u)
        nxt = l + 1
        plan = {} if nxt == N_LAYERS else (AG_PLAN_CONV if l % 2 == 0 else AG_PLAN_SB)

        def rider_for(name):
            return _gather_rider(gathered[nxt:nxt + 1], **plan[name]) if name in plan else None

        def take(results, name):
            if name not in plan:
                return results
            gathered[nxt] = results[-1]
            return results[:-1]

        proj, = take(_mm_proj(u, gathered[l], rider=rider_for("mm_proj")), "mm_proj")
        if l % 2 == 0:
            gated, = take(_conv_fwd(proj, conv_taps[l], rider=rider_for("conv_fwd")), "conv_fwd")
        else:
            gated, o, car = take(_sb_fwd(proj, rider=rider_for("sb_fwd")), "sb_fwd")
            sb_saved[l] = (o, car)
        m, = take(_mm_out(gated, gathered[l], rider=rider_for("mm_out")), "mm_out")
        projs.append(proj)
        gateds.append(gated)
        ms.append(m)
        if l < N_LAYERS - 1:
            h, u = take(_norm_mid(h, m, ln_post[l], ln_pre[nxt], rider=rider_for("norm_mid")), "norm_mid")
    dh, dm, dg_post_last, loss_part = _norm_last(h, ms[-1], ln_post[-1], target)

    dg_pre = [None] * N_LAYERS
    dg_post = [None] * N_LAYERS
    dg_post[N_LAYERS - 1] = dg_post_last
    dconv = {}
    sums = [None] * N_LAYERS
    got = [None] * N_LAYERS
    for l in reversed(range(N_LAYERS)):
        above = l + 1 if l + 1 < N_LAYERS else None
        dgated = _mm_dgated(dm, gathered[l])
        gp = _mm_dwout(gateds[l], dm)
        if l % 2 == 0:
            dproj, dconv[l] = _conv_bwd(projs[l], dgated, conv_taps[l])
        elif above is not None:
            o, car = sb_saved[l]
            dproj, got[above] = _sb_bwd(projs[l], dgated, o, car, rider=_send_sums_rider(sums[above]))
        else:
            o, car = sb_saved[l]
            dproj, = _sb_bwd(projs[l], dgated, o, car)
        if l == 0:
            gp, got[1] = _mm_dwin(us[0], dproj, gp, rider=_send_sums_rider(sums[1], None, 0, 4, 3))
            theirs, got[1] = _comm_call(
                "swap_last", _join_riders(_swap_rider(gp), _send_sums_rider(sums[1], got[1], 3, 4, 1)))
            sums[0] = _presum(gp, theirs, core_arr)
            du, got[0] = _mm_du(dproj, gathered[0], rider=_send_sums_rider(sums[0], None, 0, 4, 3))
        elif l % 2 == 0:
            gp, got[above] = _mm_dwin(us[l], dproj, gp, rider=_send_sums_rider(sums[above], None, 0, 2))
            du, theirs, got[above] = _mm_du(
                dproj, gathered[l], rider=_join_riders(_swap_rider(gp), _send_sums_rider(sums[above], got[above], 1, 2)))
            sums[l] = _presum(gp, theirs, core_arr)
        else:
            gp, = _mm_dwin(us[l], dproj, gp)
            du, theirs = _mm_du(dproj, gathered[l], rider=_swap_rider(gp))
            sums[l] = _presum(gp, theirs, core_arr)
        if l > 0:
            dh, dm, dg_pre[l], dg_post[l - 1] = _norm_bwd_mid(dh, du, h_in[l], ln_pre[l], ms[l - 1], ln_post[l - 1])
    grad_x, dg_pre[0] = _norm_bwd_first(dh, du, h_in[0], ln_pre[0])

    loss_rows = jnp.pad(loss_part, ((0, 0), (0, d - loss_part.shape[1])))
    small = _pack_small(dg_pre + dg_post, [dconv[0], dconv[2]], loss_rows)
    got[0], small_all = _comm_call(
        "exchange_last", _join_riders(_send_sums_rider(sums[0], got[0], 3, 4, 1), _small_rider(small)))
    small_sum = _sum_sources("sum_small", small_all)

    halves = [_sum_grad_half(got[l], sums[l], place) for l in range(N_LAYERS)]
    fulls = [f.reshape(d + dq, d) for f in _exchange_halves(halves)]
    res_in = [_adamw("adamw_w_in", w_in[l], m_in[l], v_in[l], fulls[l], 0) for l in range(N_LAYERS)]
    res_out = [_adamw("adamw_w_out", w_out[l], m_out[l], v_out[l], fulls[l], d) for l in range(N_LAYERS)]
    ln_all = ln_pre + ln_post
    ln_m = [m_ln_pre_0, m_ln_pre_1, m_ln_pre_2, m_ln_pre_3, m_ln_post_0, m_ln_post_1, m_ln_post_2, m_ln_post_3]
    ln_v = [v_ln_pre_0, v_ln_pre_1, v_ln_pre_2, v_ln_pre_3, v_ln_post_0, v_ln_post_1, v_ln_post_2, v_ln_post_3]
    res_ln = _adamw("adamw_ln", jnp.stack(ln_all), jnp.stack(ln_m), jnp.stack(ln_v), small_sum[0:2 * N_LAYERS])
    conv_g = [_pad_rows8(lax.dynamic_slice(small_sum, (2 * N_LAYERS + 3 * i, chip * dq), (3, dq))) for i in range(2)]
    stack8 = lambda a, b: jnp.concatenate([_pad_rows8(a), _pad_rows8(b)], axis=0)
    res_conv = _adamw("adamw_conv", stack8(*conv_w), stack8(*m_conv), stack8(*v_conv), jnp.concatenate(conv_g, axis=0))

    def leaf(kind, l, which):
        if kind == "ln_pre":
            return res_ln[which][l]
        if kind == "ln_post":
            return res_ln[which][N_LAYERS + l]
        if kind == "w_in":
            return res_in[l][which]
        if kind == "w_out":
            return res_out[l][which]
        return res_conv[which][8 * (l // 2):8 * (l // 2) + 3]

    order = []
    for l in range(N_LAYERS):
        order.append(("ln_pre", l))
        order.append(("w_in", l))
        if l % 2 == 0:
            order.append(("conv", l))
        order.append(("w_out", l))
        order.append(("ln_post", l))
    loss = small_sum[2 * N_LAYERS + 3 * 2, 0]
    outs = [loss, grad_x.reshape(1, t, d)]
    for which in range(4):
        outs.extend(leaf(kind, l, which) for kind, l in order)
    return tuple(outs)
```

```python
import functools
import math
from typing import Any, Callable, Mapping, NamedTuple, Sequence

import jax
import jax.numpy as jnp
from jax import lax
from jax.experimental import pallas as pl
from jax.experimental.pallas import tpu as pltpu

F32 = jnp.float32
BF16 = jnp.bfloat16

N_CHIPS = 4
N_DEV = 8
N_LAYERS = 4
HEAD_DIM = 128
RMS_EPS = 1e-6
ADAM_LR = 0.001
ADAM_B1 = 0.9
ADAM_B2 = 0.999
ADAM_EPS = 1e-08
ADAM_WD = 0.01
ADAM_STEP = 10

VMEM_LIMIT = 56 * 1024 * 1024
MESH_IDS = pl.DeviceIdType.MESH
HBM_SPEC = pl.BlockSpec(memory_space=pltpu.HBM)

NN = (((1,), (0,)), ((), ()))
NT = (((1,), (1,)), ((), ()))
TN = (((0,), (0,)), ((), ()))


def _params(n_axes):
    return pltpu.CompilerParams(dimension_semantics=("arbitrary",) * n_axes, vmem_limit_bytes=VMEM_LIMIT)


def _dot(a, b, dims):
    return lax.dot_general(a, b, dims, preferred_element_type=F32)


def _sigmoid(z):
    return 1.0 / (1.0 + jnp.exp(-z))


class _Rider(NamedTuple):
    operands: Sequence[Any]
    out_shapes: Sequence[Any]
    aliases: Mapping[int, int]
    sems: Sequence[Any]
    start: Callable
    finish: Callable


def _compute_call(body, name, *, grid, in_specs, operands, out_specs, out_shape, scratch=(), aliases=None, rider=None):
    in_specs, operands = list(in_specs), list(operands)
    out_specs, out_shape, scratch = list(out_specs), list(out_shape), list(scratch)
    aliases = dict(aliases or {})
    n_in, n_out, n_scratch = len(operands), len(out_shape), len(scratch)
    hosted = body
    if rider is not None:
        r_in, r_out = len(rider.operands), len(rider.out_shapes)
        aliases.update({n_in + i: n_out + o for i, o in rider.aliases.items()})

        def hosted(*refs):
            ins, refs = refs[:n_in], refs[n_in:]
            rider_ins, refs = refs[:r_in], refs[r_in:]
            outs, refs = refs[:n_out], refs[n_out:]
            rider_outs, refs = refs[:r_out], refs[r_out:]
            own_scratch, rider_sems = refs[:n_scratch], refs[n_scratch:]
            ids = [pl.program_id(axis) for axis in range(len(grid))]
            first = functools.reduce(jnp.logical_and, [i == 0 for i in ids])
            last = functools.reduce(jnp.logical_and, [i == g - 1 for i, g in zip(ids, grid)])

            @pl.when(first)
            def _():
                rider.start(rider_ins, rider_outs, rider_sems)

            body(*ins, *outs, *own_scratch)

            @pl.when(last)
            def _():
                rider.finish(rider_ins, rider_outs, rider_sems)

        in_specs += [HBM_SPEC] * r_in
        operands += list(rider.operands)
        out_specs += [HBM_SPEC] * r_out
        out_shape += list(rider.out_shapes)
        scratch += list(rider.sems)
    return pl.pallas_call(
        hosted, name=name, grid=grid, in_specs=in_specs, out_specs=out_specs, out_shape=out_shape,
        scratch_shapes=scratch, input_output_aliases=aliases, compiler_params=_params(len(grid)),
    )(*operands)


def _matmul(name, a, b, *, grid, a_spec, b_spec, o_spec, out_shape, dims, reduce_axis=None, acc_shape=None,
            alias_out=None, rider=None):
    out_dtype = out_shape.dtype
    direct = reduce_axis is not None and out_dtype == F32
    n_red = grid[reduce_axis] if reduce_axis is not None else 1

    def body(*refs):
        if alias_out is not None:
            refs = refs[1:]
        a_ref, b_ref, o_ref = refs[:3]
        if reduce_axis is None:
            o_ref[...] = _dot(a_ref[...], b_ref[...], dims).astype(out_dtype)
            return
        acc_ref = o_ref if direct else refs[3]
        k = pl.program_id(reduce_axis)

        @pl.when(k == 0)
        def _():
            acc_ref[...] = jnp.zeros_like(acc_ref)

        acc_ref[...] += _dot(a_ref[...], b_ref[...], dims)

        if not direct:
            @pl.when(k == n_red - 1)
            def _():
                o_ref[...] = acc_ref[...].astype(out_dtype)

    scratch = []
    if reduce_axis is not None and not direct:
        scratch = [pltpu.VMEM(acc_shape, F32)]
    in_specs = [a_spec, b_spec]
    operands = [a, b]
    aliases = {}
    if alias_out is not None:
        in_specs = [HBM_SPEC] + in_specs
        operands = [alias_out] + operands
        aliases = {0: 0}
    return _compute_call(body, name, grid=grid, in_specs=in_specs, operands=operands, out_specs=[o_spec],
                         out_shape=[out_shape], scratch=scratch, aliases=aliases, rider=rider)


def _mm_proj(u, g, rider=None):
    t, d = u.shape
    tm = min(512, t)
    return _matmul(
        "mm_proj", u, g, grid=(N_CHIPS, t // tm),
        a_spec=pl.BlockSpec((tm, d), lambda s, m: (m, 0)),
        b_spec=pl.BlockSpec((None, d, d), lambda s, m: (s, 0, 0)),
        o_spec=pl.BlockSpec((None, tm, d), lambda s, m: (s, m, 0)),
        out_shape=jax.ShapeDtypeStruct((N_CHIPS, t, d), BF16), dims=NN, rider=rider)


def _mm_out(gated, g, rider=None):
    t, d = gated.shape
    dq = d // N_CHIPS
    tm = min(512, t)

    def body(a_ref, b_ref, o_ref):
        acc = _dot(a_ref[:, 0:dq], b_ref[0], NN)
        for s in range(1, N_CHIPS):
            acc = acc + _dot(a_ref[:, s * dq:(s + 1) * dq], b_ref[s], NN)
        o_ref[...] = acc

    return _compute_call(
        body, "mm_out", grid=(t // tm,),
        in_specs=[pl.BlockSpec((tm, d), lambda m: (m, 0)), pl.BlockSpec((N_CHIPS, dq, d), lambda m: (0, N_CHIPS, 0))],
        operands=[gated, g], out_specs=[pl.BlockSpec((tm, d), lambda m: (m, 0))],
        out_shape=[jax.ShapeDtypeStruct((t, d), F32)], rider=rider)


def _mm_dgated(dm, g, rider=None):
    t, d = dm.shape
    dq = d // N_CHIPS
    tm = min(512, t)

    def body(a_ref, b_ref, o_ref):
        a = a_ref[...]
        for s in range(N_CHIPS):
            o_ref[:, s * dq:(s + 1) * dq] = _dot(a, b_ref[s], NT).astype(BF16)

    return _compute_call(
        body, "mm_dgated", grid=(t // tm,),
        in_specs=[pl.BlockSpec((tm, d), lambda m: (m, 0)), pl.BlockSpec((N_CHIPS, dq, d), lambda m: (0, N_CHIPS, 0))],
        operands=[dm, g], out_specs=[pl.BlockSpec((tm, d), lambda m: (m, 0))],
        out_shape=[jax.ShapeDtypeStruct((t, d), BF16)], rider=rider)


def _mm_dwout(gated, dm):
    t, d = gated.shape
    dq = d // N_CHIPS
    tk = min(1024, t)
    return _matmul(
        "mm_dwout", gated, dm, grid=(N_CHIPS, t // tk),
        a_spec=pl.BlockSpec((tk, dq), lambda s, k: (k, s)),
        b_spec=pl.BlockSpec((tk, d), lambda s, k: (k, 0)),
        o_spec=pl.BlockSpec((None, dq, d), lambda s, k: (s, N_CHIPS, 0)),
        out_shape=jax.ShapeDtypeStruct((N_CHIPS, d + dq, d), BF16), dims=TN, reduce_axis=1, acc_shape=(dq, d))[0]


def _mm_du(dproj, g, rider=None):
    _, t, d = dproj.shape
    tm = min(512, t)
    return _matmul(
        "mm_du", dproj, g, grid=(t // tm, N_CHIPS),
        a_spec=pl.BlockSpec((None, tm, d), lambda m, s: (s, m, 0)),
        b_spec=pl.BlockSpec((None, d, d), lambda m, s: (s, 0, 0)),
        o_spec=pl.BlockSpec((tm, d), lambda m, s: (m, 0)),
        out_shape=jax.ShapeDtypeStruct((t, d), F32), dims=NT, reduce_axis=1, rider=rider)


def _mm_dwin(u, dproj, gp, rider=None):
    t, d = u.shape
    tmo = min(1024, d)
    tk = min(1024, t)
    return _matmul(
        "mm_dwin", u, dproj, grid=(N_CHIPS, d // tmo, t // tk),
        a_spec=pl.BlockSpec((tk, tmo), lambda s, mo, k: (k, mo)),
        b_spec=pl.BlockSpec((None, tk, d), lambda s, mo, k: (s, k, 0)),
        o_spec=pl.BlockSpec((None, tmo, d), lambda s, mo, k: (s, mo, 0)),
        out_shape=jax.ShapeDtypeStruct(gp.shape, BF16), dims=TN, reduce_axis=2, acc_shape=(tmo, d), alias_out=gp,
        rider=rider)


def _rms(v):
    r = lax.rsqrt(jnp.mean(v * v, axis=-1, keepdims=True) + RMS_EPS)
    return v * r, r


def _rms_bwd(dout, n, r, gain):
    dn = dout * gain
    return r * (dn - n * jnp.mean(dn * n, axis=-1, keepdims=True))


def _fold8(v):
    return jnp.sum(v.reshape(v.shape[0] // 8, 8, v.shape[1]), axis=0)


def _row0(total):
    rows = lax.broadcasted_iota(jnp.int32, total.shape, 0)
    return jnp.where(rows == 0, jnp.sum(total, axis=0, keepdims=True), 0.0)


def _norm_tile(t):
    return min(256, t)


def _norm_first(x, g_pre):
    t, d = x.shape
    tr = _norm_tile(t)

    def body(x_ref, g_ref, u_ref):
        n, _ = _rms(x_ref[...])
        u_ref[...] = (n * g_ref[...]).astype(BF16)

    row = pl.BlockSpec((tr, d), lambda i: (i, 0))
    vec = pl.BlockSpec((1, d), lambda i: (0, 0))
    return pl.pallas_call(
        body, name="norm_first", grid=(t // tr,), in_specs=[row, vec], out_specs=row,
        out_shape=jax.ShapeDtypeStruct((t, d), BF16), compiler_params=_params(1),
    )(x, g_pre.reshape(1, d))


def _norm_mid(h, m, g_post, g_pre_next, rider=None):
    t, d = h.shape
    tr = _norm_tile(t)

    def body(h_ref, m_ref, gp_ref, gn_ref, hn_ref, u_ref):
        n, _ = _rms(m_ref[...])
        hn = h_ref[...] + n * gp_ref[...]
        hn_ref[...] = hn
        n2, _ = _rms(hn)
        u_ref[...] = (n2 * gn_ref[...]).astype(BF16)

    row = pl.BlockSpec((tr, d), lambda i: (i, 0))
    vec = pl.BlockSpec((1, d), lambda i: (0, 0))
    return _compute_call(
        body, "norm_mid", grid=(t // tr,), in_specs=[row, row, vec, vec],
        operands=[h, m, g_post.reshape(1, d), g_pre_next.reshape(1, d)], out_specs=[row, row],
        out_shape=[jax.ShapeDtypeStruct((t, d), F32), jax.ShapeDtypeStruct((t, d), BF16)], rider=rider)


def _norm_last(h, m, g_post, target):
    t, d = h.shape
    tr = _norm_tile(t)
    nsteps = t // tr

    def body(h_ref, m_ref, gp_ref, tg_ref, dy_ref, dm_ref, dgp_ref, loss_ref, acc_g, acc_l):
        i = pl.program_id(0)

        @pl.when(i == 0)
        def _():
            acc_g[...] = jnp.zeros_like(acc_g)
            acc_l[...] = jnp.zeros_like(acc_l)

        gain = gp_ref[...]
        n, r = _rms(m_ref[...])
        err = h_ref[...] + n * gain - tg_ref[...]
        dy = err / d
        dy_ref[...] = dy
        dm_ref[...] = _rms_bwd(dy, n, r, gain).astype(BF16)
        acc_g[...] += _fold8(dy * n)
        acc_l[...] += _fold8(err * err)

        @pl.when(i == nsteps - 1)
        def _():
            dgp_ref[...] = _row0(acc_g[...])
            loss_ref[...] = jnp.zeros((8, 128), F32) + (0.5 / d) * jnp.sum(acc_l[...])

    row = pl.BlockSpec((tr, d), lambda i: (i, 0))
    vec = pl.BlockSpec((1, d), lambda i: (0, 0))
    acc = pl.BlockSpec((8, d), lambda i: (0, 0))
    return pl.pallas_call(
        body, name="norm_last", grid=(nsteps,), in_specs=[row, row, vec, row],
        out_specs=[row, row, acc, pl.BlockSpec((8, 128), lambda i: (0, 0))],
        out_shape=[jax.ShapeDtypeStruct((t, d), F32), jax.ShapeDtypeStruct((t, d), BF16),
                   jax.ShapeDtypeStruct((8, d), F32), jax.ShapeDtypeStruct((8, 128), F32)],
        scratch_shapes=[pltpu.VMEM((8, d), F32), pltpu.VMEM((8, d), F32)],
        compiler_params=_params(1),
    )(h, m, g_post.reshape(1, d), target)


def _norm_bwd_mid(dh, du, h_in, g_pre, m_prev, g_post_prev):
    t, d = dh.shape
    tr = _norm_tile(t)
    nsteps = t // tr

    def body(dh_ref, du_ref, h_ref, gpre_ref, m_ref, gpost_ref, dhn_ref, dm_ref, dgpre_ref, dgpost_ref, acc_a, acc_b):
        i = pl.program_id(0)

        @pl.when(i == 0)
        def _():
            acc_a[...] = jnp.zeros_like(acc_a)
            acc_b[...] = jnp.zeros_like(acc_b)

        du_t = du_ref[...]
        n, r = _rms(h_ref[...])
        dhn = dh_ref[...] + _rms_bwd(du_t, n, r, gpre_ref[...])
        dhn_ref[...] = dhn
        acc_a[...] += _fold8(du_t * n)
        n2, r2 = _rms(m_ref[...])
        dm_ref[...] = _rms_bwd(dhn, n2, r2, gpost_ref[...]).astype(BF16)
        acc_b[...] += _fold8(dhn * n2)

        @pl.when(i == nsteps - 1)
        def _():
            dgpre_ref[...] = _row0(acc_a[...])
            dgpost_ref[...] = _row0(acc_b[...])

    row = pl.BlockSpec((tr, d), lambda i: (i, 0))
    vec = pl.BlockSpec((1, d), lambda i: (0, 0))
    acc = pl.BlockSpec((8, d), lambda i: (0, 0))
    return pl.pallas_call(
        body, name="norm_bwd_mid", grid=(nsteps,), in_specs=[row, row, row, vec, row, vec],
        out_specs=[row, row, acc, acc],
        out_shape=[jax.ShapeDtypeStruct((t, d), F32), jax.ShapeDtypeStruct((t, d), BF16),
                   jax.ShapeDtypeStruct((8, d), F32), jax.ShapeDtypeStruct((8, d), F32)],
        scratch_shapes=[pltpu.VMEM((8, d), F32), pltpu.VMEM((8, d), F32)],
        compiler_params=_params(1),
    )(dh, du, h_in, g_pre.reshape(1, d), m_prev, g_post_prev.reshape(1, d))


def _norm_bwd_first(dh, du, x, g_pre, rider=None):
    t, d = dh.shape
    tr = _norm_tile(t)
    nsteps = t // tr

    def body(dh_ref, du_ref, x_ref, gpre_ref, dx_ref, dgpre_ref, acc_a):
        i = pl.program_id(0)

        @pl.when(i == 0)
        def _():
            acc_a[...] = jnp.zeros_like(acc_a)

        du_t = du_ref[...]
        n, r = _rms(x_ref[...])
        dx_ref[...] = dh_ref[...] + _rms_bwd(du_t, n, r, gpre_ref[...])
        acc_a[...] += _fold8(du_t * n)

        @pl.when(i == nsteps - 1)
        def _():
            dgpre_ref[...] = _row0(acc_a[...])

    row = pl.BlockSpec((tr, d), lambda i: (i, 0))
    vec = pl.BlockSpec((1, d), lambda i: (0, 0))
    acc = pl.BlockSpec((8, d), lambda i: (0, 0))
    return _compute_call(
        body, "norm_bwd_first", grid=(nsteps,), in_specs=[row, row, row, vec],
        operands=[dh, du, x, g_pre.reshape(1, d)], out_specs=[row, acc],
        out_shape=[jax.ShapeDtypeStruct((t, d), F32), jax.ShapeDtypeStruct((8, d), F32)],
        scratch=[pltpu.VMEM((8, d), F32)], rider=rider)


CONV_TC = 128
CONV_HALO = 16


def _conv_chunk(t):
    return min(512, t)


def _shift_down(v, steps, fill):
    rows = lax.broadcasted_iota(jnp.int32, v.shape, 0)
    out = pltpu.roll(v, steps, axis=0)
    for k in range(steps):
        out = jnp.where(rows == k, fill[CONV_HALO - steps + k:CONV_HALO - steps + k + 1, :], out)
    return out


def _shift_up(v, steps, fill):
    nrows = v.shape[0]
    rows = lax.broadcasted_iota(jnp.int32, v.shape, 0)
    out = pltpu.roll(v, nrows - steps, axis=0)
    for k in range(steps):
        out = jnp.where(rows == nrows - steps + k, fill[k:k + 1, :], out)
    return out


def _conv_fwd(proj, cw, rider=None):
    _, t, d = proj.shape
    chunk = _conv_chunk(t)

    def body(p_ref, w_ref, o_ref):
        w = w_ref[...]
        w0, w1, w2 = w[0:1, :], w[1:2, :], w[2:3, :]
        for ci in range(t // chunk):
            t0 = ci * chunk
            rows = pl.ds(t0, chunk)
            b = p_ref[0, rows, :].astype(F32)
            cx = p_ref[1, rows, :].astype(F32) * p_ref[2, rows, :].astype(F32)
            z = p_ref[3, rows, :].astype(F32)
            if ci == 0:
                prev = jnp.zeros((CONV_HALO, CONV_TC), F32)
            else:
                halo = pl.ds(t0 - CONV_HALO, CONV_HALO)
                prev = p_ref[1, halo, :].astype(F32) * p_ref[2, halo, :].astype(F32)
            conv = w2 * cx + w1 * _shift_down(cx, 1, prev) + w0 * _shift_down(cx, 2, prev)
            o_ref[rows, :] = (z * _sigmoid(z) * b * conv).astype(BF16)

    return _compute_call(
        body, "conv_fwd", grid=(d // CONV_TC,),
        in_specs=[pl.BlockSpec((N_CHIPS, t, CONV_TC), lambda j: (0, 0, j)), pl.BlockSpec((8, CONV_TC), lambda j: (0, j))],
        operands=[proj, cw], out_specs=[pl.BlockSpec((t, CONV_TC), lambda j: (0, j))],
        out_shape=[jax.ShapeDtypeStruct((t, d), BF16)], rider=rider)


def _conv_bwd(proj, dgated, cw):
    _, t, d = proj.shape
    chunk = _conv_chunk(t)
    nchunks = t // chunk

    def body(p_ref, dg_ref, w_ref, dp_ref, dw_ref):
        w = w_ref[...]
        w0, w1, w2 = w[0:1, :], w[1:2, :], w[2:3, :]
        dw0 = jnp.zeros((1, CONV_TC), F32)
        dw1 = jnp.zeros((1, CONV_TC), F32)
        dw2 = jnp.zeros((1, CONV_TC), F32)
        for ci in range(nchunks):
            t0 = ci * chunk
            rows = pl.ds(t0, chunk)
            b = p_ref[0, rows, :].astype(F32)
            c = p_ref[1, rows, :].astype(F32)
            xt = p_ref[2, rows, :].astype(F32)
            z = p_ref[3, rows, :].astype(F32)
            dg = dg_ref[rows, :].astype(F32)
            cx = c * xt
            if ci == 0:
                prev = jnp.zeros((CONV_HALO, CONV_TC), F32)
            else:
                halo = pl.ds(t0 - CONV_HALO, CONV_HALO)
                prev = p_ref[1, halo, :].astype(F32) * p_ref[2, halo, :].astype(F32)
            cx1 = _shift_down(cx, 1, prev)
            cx2 = _shift_down(cx, 2, prev)
            conv = w2 * cx + w1 * cx1 + w0 * cx2
            sig = _sigmoid(z)
            dy = dg * (z * sig)
            dconv = dy * b
            if ci == nchunks - 1:
                nxt = jnp.zeros((CONV_HALO, CONV_TC), F32)
            else:
                halo = pl.ds(t0 + chunk, CONV_HALO)
                zn = p_ref[3, halo, :].astype(F32)
                nxt = dg_ref[halo, :].astype(F32) * (zn * _sigmoid(zn)) * p_ref[0, halo, :].astype(F32)
            dcx = w2 * dconv + w1 * _shift_up(dconv, 1, nxt) + w0 * _shift_up(dconv, 2, nxt)
            dp_ref[0, rows, :] = (dy * conv).astype(BF16)
            dp_ref[1, rows, :] = (dcx * xt).astype(BF16)
            dp_ref[2, rows, :] = (dcx * c).astype(BF16)
            dp_ref[3, rows, :] = (dg * (b * conv) * (sig * (1.0 + z * (1.0 - sig)))).astype(BF16)
            dw0 = dw0 + jnp.sum(dconv * cx2, axis=0, keepdims=True)
            dw1 = dw1 + jnp.sum(dconv * cx1, axis=0, keepdims=True)
            dw2 = dw2 + jnp.sum(dconv * cx, axis=0, keepdims=True)
        taps = lax.broadcasted_iota(jnp.int32, (8, CONV_TC), 0)
        dw_ref[...] = jnp.where(taps == 0, dw0, jnp.where(taps == 1, dw1, jnp.where(taps == 2, dw2, 0.0)))

    return pl.pallas_call(
        body, name="conv_bwd", grid=(d // CONV_TC,),
        in_specs=[pl.BlockSpec((N_CHIPS, t, CONV_TC), lambda j: (0, 0, j)),
                  pl.BlockSpec((t, CONV_TC), lambda j: (0, j)),
                  pl.BlockSpec((8, CONV_TC), lambda j: (0, j))],
        out_specs=[pl.BlockSpec((N_CHIPS, t, CONV_TC), lambda j: (0, 0, j)), pl.BlockSpec((8, CONV_TC), lambda j: (0, j))],
        out_shape=[jax.ShapeDtypeStruct((N_CHIPS, t, d), BF16), jax.ShapeDtypeStruct((8, d), F32)],
        compiler_params=_params(1),
    )(proj, dgated, cw)


SB_DEAD_TAIL = -105.0
SB_COUNT_LANE = HEAD_DIM - 1


def _sb_block(t):
    return min(256, t)


def _split_dot(v, tri):
    hi = v.astype(BF16)
    lo = (v - hi.astype(F32)).astype(BF16)
    return _dot(hi, tri, NN) + _dot(lo, tri, NN)


SB_HEADS_PER_STEP = 4


def _sb_terms(s, diagonal):
    sp = jnp.maximum(s, 0.0) + jnp.log(1.0 + jnp.exp(-jnp.abs(s)))
    if not diagonal:
        return -sp, s - sp, sp, None
    mask = lax.broadcasted_iota(jnp.int32, s.shape, 1) < lax.broadcasted_iota(jnp.int32, s.shape, 0)
    return jnp.where(mask, -sp, 0.0), s - sp, sp, mask


def _masked(mask, v):
    return v if mask is None else jnp.where(mask, v, 0.0)


def _sb_fwd(proj, rider=None):
    _, t, d = proj.shape
    heads = d // HEAD_DIM
    blk = _sb_block(t)
    nblk = t // blk
    scale = 1.0 / math.sqrt(HEAD_DIM)

    hps = SB_HEADS_PER_STEP
    width = hps * HEAD_DIM

    def body(q_ref, k_ref, v_ref, z_ref, gated_ref, o_ref, car_ref, tail_ref, acc_ref):
        i = pl.program_id(1)
        r_i = lax.broadcasted_iota(jnp.int32, (blk, blk), 0)
        c_i = lax.broadcasted_iota(jnp.int32, (blk, blk), 1)
        tri_after = (r_i > c_i).astype(BF16)
        lanes = lax.broadcasted_iota(jnp.int32, (blk, HEAD_DIM), 1)

        tail_ref[...] = jnp.zeros_like(tail_ref)
        acc_ref[...] = jnp.zeros_like(acc_ref)
        car_ref[...] = jnp.zeros_like(car_ref)

        def visit(j, diagonal):
            krows = pl.ds(pl.multiple_of(j * blk, blk), blk)
            hcols = [pl.ds(hh * HEAD_DIM, HEAD_DIM) for hh in range(hps)]
            logits = [_dot(q_ref[:, c], k_ref[krows, c], NT) for c in hcols]
            terms = [_sb_terms(s * scale, diagonal) for s in logits]
            within = [_split_dot(keep, tri_after) for keep, _, _, _ in terms]
            top = None
            for hh, (keep, log_beta, _, mask) in enumerate(terms):
                tail_b = tail_ref[hh]
                w = _masked(mask, jnp.exp(log_beta + tail_b[:, 0:1] + within[hh]))
                acc_ref[hh] += _dot(w.astype(BF16), v_ref[krows, hcols[hh]], NN)
                car_ref[hh] = jnp.where(lanes == j, tail_b, car_ref[hh])
                tail_new = tail_b + jnp.sum(keep, axis=1, keepdims=True)
                tail_ref[hh] = tail_new
                top = jnp.max(tail_new) if top is None else jnp.maximum(top, jnp.max(tail_new))
            return top > SB_DEAD_TAIL

        def more(state):
            jj, live = state
            return jnp.logical_and(jj <= i, live)

        def step(state):
            jj, _ = state
            return jj + 1, visit(i - jj, False)

        visited, _ = lax.while_loop(more, step, (jnp.int32(1), visit(i, True)))
        for hh in range(hps):
            cols = pl.ds(hh * HEAD_DIM, HEAD_DIM)
            car_ref[hh] = jnp.where(lanes == SB_COUNT_LANE, visited.astype(F32), car_ref[hh])
            z = z_ref[:, cols].astype(F32)
            acc = acc_ref[hh]
            o_ref[:, cols] = acc.astype(BF16)
            gated_ref[:, cols] = (z * _sigmoid(z) * acc).astype(BF16)

    qspec = lambda s: pl.BlockSpec((None, blk, width), lambda h, i: (s, i, h))
    kspec = lambda s: pl.BlockSpec((None, t, width), lambda h, i: (s, 0, h))
    ospec = pl.BlockSpec((blk, width), lambda h, i: (i, h))
    return _compute_call(
        body, "sb_fwd", grid=(heads // hps, nblk),
        in_specs=[qspec(0), kspec(1), kspec(2), qspec(3)], operands=[proj, proj, proj, proj],
        out_specs=[ospec, ospec, pl.BlockSpec((hps, blk, HEAD_DIM), lambda h, i: (h, i, 0))],
        out_shape=[jax.ShapeDtypeStruct((t, d), BF16), jax.ShapeDtypeStruct((t, d), BF16),
                   jax.ShapeDtypeStruct((heads, t, HEAD_DIM), F32)],
        scratch=[pltpu.VMEM((hps, blk, HEAD_DIM), F32), pltpu.VMEM((hps, blk, HEAD_DIM), F32)], rider=rider)


def _sb_bwd(proj, dgated, o, car, rider=None):
    _, t, d = proj.shape
    heads = d // HEAD_DIM
    blk = _sb_block(t)
    nblk = t // blk
    scale = 1.0 / math.sqrt(HEAD_DIM)

    hps = SB_HEADS_PER_STEP
    width = hps * HEAD_DIM

    def body(q_ref, k_ref, v_ref, z_ref, dg_ref, o_ref, car_ref, dp_ref, dk_acc, dv_acc, gsum_ref, dq_ref, do_ref):
        step_i = pl.program_id(1)
        i = nblk - 1 - step_i

        @pl.when(step_i == 0)
        def _():
            dk_acc[...] = jnp.zeros_like(dk_acc)
            dv_acc[...] = jnp.zeros_like(dv_acc)

        r_i = lax.broadcasted_iota(jnp.int32, (blk, blk), 0)
        c_i = lax.broadcasted_iota(jnp.int32, (blk, blk), 1)
        tri_after = (r_i > c_i).astype(BF16)
        tri_before = (r_i < c_i).astype(BF16)
        lanes = lax.broadcasted_iota(jnp.int32, (blk, HEAD_DIM), 1)

        gsum_ref[...] = jnp.zeros_like(gsum_ref)
        dq_ref[...] = jnp.zeros_like(dq_ref)
        for hh in range(hps):
            cols = pl.ds(hh * HEAD_DIM, HEAD_DIM)
            z = z_ref[:, cols].astype(F32)
            dg = dg_ref[:, cols].astype(F32)
            sig = _sigmoid(z)
            do_ref[hh] = (dg * (z * sig)).astype(BF16)
            dp_ref[3, :, cols] = (dg * o_ref[:, cols].astype(F32) * (sig * (1.0 + z * (1.0 - sig)))).astype(BF16)

        def visit(j, diagonal):
            krows = pl.ds(pl.multiple_of(j * blk, blk), blk)
            hcols = [pl.ds(hh * HEAD_DIM, HEAD_DIM) for hh in range(hps)]
            logits = [_dot(q_ref[:, c], k_ref[krows, c], NT) for c in hcols]
            dws = [_dot(do_ref[hh], v_ref[krows, c], NT) for hh, c in enumerate(hcols)]
            terms = [_sb_terms(s * scale, diagonal) for s in logits]
            within = [_split_dot(keep, tri_after) for keep, _, _, _ in terms]
            ws, gs = [], []
            for hh, (keep, log_beta, sp, mask) in enumerate(terms):
                tail = jnp.sum(jnp.where(lanes == j, car_ref[hh], 0.0), axis=1, keepdims=True)
                w = _masked(mask, jnp.exp(log_beta + tail + within[hh]))
                ws.append(w.astype(BF16))
                gs.append(w * dws[hh])
            g_within = [_split_dot(g, tri_before) for g in gs]
            for hh, (keep, log_beta, sp, mask) in enumerate(terms):
                c = hcols[hh]
                g_before = gsum_ref[hh]
                g_cum = g_before[:, 0:1] + g_within[hh]
                dl = (_masked(mask, gs[hh] - (gs[hh] + g_cum) * jnp.exp(log_beta)) * scale).astype(BF16)
                dq_ref[hh] += _dot(dl, k_ref[krows, c], NN)
                dk_acc[krows, c] += _dot(dl, q_ref[:, c], TN)
                dv_acc[krows, c] += _dot(ws[hh], do_ref[hh], TN)
                gsum_ref[hh] = g_before + jnp.sum(gs[hh], axis=1, keepdims=True)

        def step(j, carry):
            visit(j, False)
            return carry

        visited = jnp.max(jnp.where(lanes == SB_COUNT_LANE, car_ref[0], 0.0)).astype(jnp.int32)
        lax.fori_loop(i + 1 - visited, i, step, 0)
        visit(i, True)
        own = pl.ds(pl.multiple_of(i * blk, blk), blk)
        for hh in range(hps):
            cols = pl.ds(hh * HEAD_DIM, HEAD_DIM)
            dp_ref[0, :, cols] = dq_ref[hh].astype(BF16)
        dp_ref[1] = dk_acc[own, :].astype(BF16)
        dp_ref[2] = dv_acc[own, :].astype(BF16)

    qspec = lambda s: pl.BlockSpec((None, blk, width), lambda h, i: (s, nblk - 1 - i, h))
    kspec = lambda s: pl.BlockSpec((None, t, width), lambda h, i: (s, 0, h))
    tspec = pl.BlockSpec((blk, width), lambda h, i: (nblk - 1 - i, h))
    return _compute_call(
        body, "sb_bwd", grid=(heads // hps, nblk),
        in_specs=[qspec(0), kspec(1), kspec(2), qspec(3), tspec, tspec,
                  pl.BlockSpec((hps, blk, HEAD_DIM), lambda h, i: (h, nblk - 1 - i, 0))],
        operands=[proj, proj, proj, proj, dgated, o, car],
        out_specs=[pl.BlockSpec((N_CHIPS, blk, width), lambda h, i: (0, nblk - 1 - i, h))],
        out_shape=[jax.ShapeDtypeStruct((N_CHIPS, t, d), BF16)],
        scratch=[pltpu.VMEM((t, width), F32), pltpu.VMEM((t, width), F32),
                 pltpu.VMEM((hps, blk, HEAD_DIM), F32), pltpu.VMEM((hps, blk, HEAD_DIM), F32),
                 pltpu.VMEM((hps, blk, HEAD_DIM), BF16)], rider=rider)


def _pack_weights(w_in, w_out, chip):
    d = w_in.shape[0]
    rb = d // 8
    n_in = d // rb
    n_out = w_out.shape[0] // rb

    def body(chip_ref, wi_ref, wo_ref, o_ref):
        r = pl.program_id(0)

        @pl.when(r < n_in)
        def _():
            o_ref[...] = wi_ref[...].astype(BF16)

        @pl.when(r >= n_in)
        def _():
            o_ref[...] = wo_ref[...].astype(BF16)

    grid_spec = pltpu.PrefetchScalarGridSpec(
        num_scalar_prefetch=1, grid=(n_in + n_out,),
        in_specs=[pl.BlockSpec((rb, d), lambda r, me: (jnp.minimum(r, n_in - 1), 0)),
                  pl.BlockSpec((rb, d), lambda r, me: (jnp.maximum(r - n_in, 0), 0))],
        out_specs=pl.BlockSpec((None, rb, d), lambda r, me: (me[0], r, 0)))
    return pl.pallas_call(
        body, name="pack_weights", grid_spec=grid_spec,
        out_shape=jax.ShapeDtypeStruct((N_CHIPS, d + w_out.shape[0], d), BF16), compiler_params=_params(1),
    )(chip, w_in, w_out)


def _flip(v, bit):
    return 1 - v if bit else v


def _remote(src, dst, send_sem, recv_sem, target):
    return pltpu.make_async_remote_copy(src_ref=src, dst_ref=dst, send_sem=send_sem, recv_sem=recv_sem,
                                        device_id=target, device_id_type=MESH_IDS)


AG_CHUNKS = 8
AG_PLAN_CONV = {
    "mm_proj": dict(chunks=range(0, 5)),
    "conv_fwd": dict(chunks=range(5, 6), landed=range(0, 5)),
    "mm_out": dict(chunks=range(6, 7), landed=range(5, 6)),
    "norm_mid": dict(chunks=range(7, 8), landed=range(6, 7), pass_now=range(7, 8)),
}
AG_PLAN_SB = {
    "mm_proj": dict(chunks=range(0, 4)),
    "sb_fwd": dict(chunks=range(4, 8), landed=range(0, 4)),
    "mm_out": dict(landed=range(4, 8)),
}
HALF_CHUNKS = 8


SWAP_CHUNKS = 2


def _other_chips(x, y):
    return [(_flip(x, k >> 1), _flip(y, k & 1)) for k in (1, 2, 3)]


def _gather_sems(n):
    return [pltpu.SemaphoreType.DMA((3, n * AG_CHUNKS)) for _ in range(4)]


def _gather_pieces(g, chunks):
    hr = g[0].shape[1] // 2
    cr = hr // AG_CHUNKS
    return hr, [(l * AG_CHUNKS + q, g[l], q * cr, cr) for l in range(len(g)) for q in chunks]


def _pass_on(g, fsend, frecv, chunks):
    x, y, c = lax.axis_index("x"), lax.axis_index("y"), lax.axis_index("c")
    hr, pieces = _gather_pieces(g, chunks)
    for k, (px, py) in enumerate(_other_chips(x, y)):
        for i, ref, r0, cr in pieces:
            landed = ref.at[2 * px + py, pl.ds(c * hr + r0, cr)]
            _remote(landed, landed, fsend.at[k, i], frecv.at[k, i], (x, y, 1 - c)).start()


def _gather_start(g, send, recv, fsend, frecv, chunks=range(AG_CHUNKS), landed=()):
    x, y, c = lax.axis_index("x"), lax.axis_index("y"), lax.axis_index("c")
    hr, pieces = _gather_pieces(g, chunks)
    for k, (px, py) in enumerate(_other_chips(x, y)):
        for i, ref, r0, cr in pieces:
            piece = ref.at[2 * x + y, pl.ds(c * hr + r0, cr)]
            _remote(piece, piece, send.at[k, i], recv.at[k, i], (px, py, c)).start()
    _pass_on(g, fsend, frecv, landed)


def _gather_finish(g, send, recv, fsend, frecv, chunks=range(AG_CHUNKS), landed=(), pass_now=None):
    x, y, c = lax.axis_index("x"), lax.axis_index("y"), lax.axis_index("c")
    sibling = (x, y, 1 - c)
    pass_now = chunks if pass_now is None else pass_now
    hr, pieces = _gather_pieces(g, chunks)
    chips = _other_chips(x, y)
    for k, (px, py) in enumerate(chips):
        for i, ref, r0, cr in pieces:
            arrived = ref.at[2 * px + py, pl.ds(c * hr + r0, cr)]
            _remote(arrived, arrived, send.at[k, i], recv.at[k, i], (px, py, c)).wait_recv()
    _pass_on(g, fsend, frecv, pass_now)
    _, passed = _gather_pieces(g, list(landed) + list(pass_now))
    for k, (px, py) in enumerate(chips):
        for i, ref, r0, cr in passed:
            theirs = ref.at[2 * px + py, pl.ds((1 - c) * hr + r0, cr)]
            _remote(theirs, theirs, fsend.at[k, i], frecv.at[k, i], sibling).wait_recv()
    for k, (px, py) in enumerate(chips):
        for i, ref, r0, cr in pieces:
            mine = ref.at[2 * x + y, pl.ds(c * hr + r0, cr)]
            _remote(mine, mine, send.at[k, i], recv.at[k, i], (px, py, c)).wait_send()
        for i, ref, r0, cr in passed:
            mine = ref.at[2 * px + py, pl.ds(c * hr + r0, cr)]
            _remote(mine, mine, fsend.at[k, i], frecv.at[k, i], sibling).wait_send()


def _all_gather_weights(packs, cw):
    n = len(packs)

    def body(*refs):
        cw_ref = refs[n]
        g = refs[n + 1:2 * n + 1]
        cwg = refs[2 * n + 1]
        send, recv, fsend, frecv, csend, crecv, lsem = refs[2 * n + 2:]
        x, y, c = lax.axis_index("x"), lax.axis_index("y"), lax.axis_index("c")
        me = 2 * x + y
        local = pltpu.make_async_copy(cw_ref, cwg.at[me], lsem)
        local.start()
        taps = [_remote(cw_ref, cwg.at[me], csend.at[k], crecv.at[k], (px, py, c))
                for k, (px, py) in enumerate(_other_chips(x, y))]
        for cp in taps:
            cp.start()
        _gather_start(g, send, recv, fsend, frecv)
        _gather_finish(g, send, recv, fsend, frecv)
        for k, (px, py) in enumerate(_other_chips(x, y)):
            _remote(cw_ref, cwg.at[2 * px + py], csend.at[k], crecv.at[k], (px, py, c)).wait_recv()
        for cp in taps:
            cp.wait_send()
        local.wait()

    out_shape = [jax.ShapeDtypeStruct(p.shape, BF16) for p in packs]
    out_shape.append(jax.ShapeDtypeStruct((N_CHIPS,) + cw.shape, cw.dtype))
    return pl.pallas_call(
        body, name="all_gather_weights", in_specs=[HBM_SPEC] * (n + 1), out_specs=[HBM_SPEC] * (n + 1),
        out_shape=out_shape, input_output_aliases={l: l for l in range(n)},
        scratch_shapes=_gather_sems(n) + [pltpu.SemaphoreType.DMA((3,)), pltpu.SemaphoreType.DMA((3,)),
                                          pltpu.SemaphoreType.DMA],
    )(*packs, cw)


def _gather_rider(packs, chunks=(), landed=(), pass_now=()):
    return _Rider(
        operands=packs, out_shapes=[jax.ShapeDtypeStruct(p.shape, BF16) for p in packs],
        aliases={l: l for l in range(len(packs))}, sems=_gather_sems(len(packs)),
        start=lambda ins, outs, sems: _gather_start(outs, *sems, chunks, landed),
        finish=lambda ins, outs, sems: _gather_finish(outs, *sems, chunks, landed, pass_now))


def _join_riders(a, b):
    na, oa, sa = len(a.operands), len(a.out_shapes), len(a.sems)
    aliases = dict(a.aliases)
    aliases.update({na + i: oa + o for i, o in b.aliases.items()})
    return _Rider(
        operands=list(a.operands) + list(b.operands), out_shapes=list(a.out_shapes) + list(b.out_shapes),
        aliases=aliases, sems=list(a.sems) + list(b.sems),
        start=lambda ins, outs, sems: (a.start(ins[:na], outs[:oa], sems[:sa]),
                                       b.start(ins[na:], outs[oa:], sems[sa:])),
        finish=lambda ins, outs, sems: (a.finish(ins[:na], outs[:oa], sems[:sa]),
                                        b.finish(ins[na:], outs[oa:], sems[sa:])))


def _comm_call(name, rider):
    n_in, n_out = len(rider.operands), len(rider.out_shapes)

    def body(*refs):
        ins, outs, sems = refs[:n_in], refs[n_in:n_in + n_out], refs[n_in + n_out:]
        rider.start(ins, outs, sems)
        rider.finish(ins, outs, sems)

    return pl.pallas_call(
        body, name=name, in_specs=[HBM_SPEC] * n_in, out_specs=[HBM_SPEC] * n_out, out_shape=list(rider.out_shapes),
        scratch_shapes=list(rider.sems), input_output_aliases=dict(rider.aliases),
    )(*rider.operands)


def _small_rider(small):
    def peers():
        x, y, c = lax.axis_index("x"), lax.axis_index("y"), lax.axis_index("c")
        return 4 * x + 2 * y + c, [(_flip(x, r >> 2), _flip(y, (r >> 1) & 1), _flip(c, r & 1)) for r in range(1, N_DEV)]

    def start(ins, outs, sems):
        me, others = peers()
        pltpu.make_async_copy(ins[0], outs[0].at[me], sems[2]).start()
        for r, peer in enumerate(others):
            _remote(ins[0], outs[0].at[me], sems[0].at[r], sems[1].at[r], peer).start()

    def finish(ins, outs, sems):
        me, others = peers()
        for r, (tx, ty, tc) in enumerate(others):
            _remote(ins[0], outs[0].at[4 * tx + 2 * ty + tc], sems[0].at[r], sems[1].at[r], (tx, ty, tc)).wait_recv()
        for r, peer in enumerate(others):
            _remote(ins[0], outs[0].at[me], sems[0].at[r], sems[1].at[r], peer).wait_send()
        pltpu.make_async_copy(ins[0], outs[0].at[me], sems[2]).wait()

    return _Rider(
        operands=[small], out_shapes=[jax.ShapeDtypeStruct((N_DEV,) + small.shape, small.dtype)], aliases={},
        sems=[pltpu.SemaphoreType.DMA((N_DEV - 1,)), pltpu.SemaphoreType.DMA((N_DEV - 1,)), pltpu.SemaphoreType.DMA],
        start=start, finish=finish)


SMALL_ROWS = 16


def _pack_small(ln_parts, conv_parts, loss_rows):
    d = ln_parts[0].shape[1]
    n_ln, n_conv = len(ln_parts), len(conv_parts)

    def body(*refs):
        o_ref = refs[-1]
        rows = lax.broadcasted_iota(jnp.int32, (SMALL_ROWS, d), 0)
        acc = jnp.zeros((SMALL_ROWS, d), F32)
        for i in range(n_ln):
            acc = jnp.where(rows == i, refs[i][0:1, :], acc)
        for j in range(n_conv):
            for k in range(3):
                acc = jnp.where(rows == n_ln + 3 * j + k, refs[n_ln + j][k:k + 1, :], acc)
        acc = jnp.where(rows == n_ln + 3 * n_conv, refs[n_ln + n_conv][0:1, :], acc)
        o_ref[...] = acc

    n = n_ln + n_conv + 1
    return pl.pallas_call(
        body, name="pack_small", out_shape=jax.ShapeDtypeStruct((SMALL_ROWS, d), F32),
        in_specs=[pl.BlockSpec(memory_space=pltpu.VMEM)] * n, out_specs=pl.BlockSpec(memory_space=pltpu.VMEM),
    )(*ln_parts, *conv_parts, loss_rows)


def _send_sums_start(s_ref, got, send, recv, part=0, parts=1, span=1):
    x, y, c = lax.axis_index("x"), lax.axis_index("y"), lax.axis_index("c")
    nrows = s_ref.shape[1] // parts
    rows = pl.ds(part * nrows, span * nrows)
    for k, (px, py) in enumerate(_other_chips(x, y)):
        _remote(s_ref.at[2 * px + py, rows], got.at[k, rows], send.at[k], recv.at[k], (px, py, c)).start()


def _send_sums_finish(s_ref, got, send, recv, part=0, parts=1, span=1):
    x, y, c = lax.axis_index("x"), lax.axis_index("y"), lax.axis_index("c")
    nrows = s_ref.shape[1] // parts
    rows = pl.ds(part * nrows, span * nrows)
    for k, (px, py) in enumerate(_other_chips(x, y)):
        _remote(got.at[k, rows], got.at[k, rows], send.at[k], recv.at[k], (px, py, c)).wait_recv()
    for k, (px, py) in enumerate(_other_chips(x, y)):
        _remote(s_ref.at[2 * px + py, rows], got.at[k, rows], send.at[k], recv.at[k], (px, py, c)).wait_send()


def _send_sums_rider(sums, got=None, part=0, parts=1, span=1):
    _, hr, d = sums.shape
    return _Rider(
        operands=[sums] if got is None else [sums, got],
        out_shapes=[jax.ShapeDtypeStruct((N_CHIPS - 1, hr, d), BF16)], aliases={} if got is None else {1: 0},
        sems=[pltpu.SemaphoreType.DMA((3,)), pltpu.SemaphoreType.DMA((3,))],
        start=lambda ins, outs, sems: _send_sums_start(ins[0], outs[0], *sems, part, parts, span),
        finish=lambda ins, outs, sems: _send_sums_finish(ins[0], outs[0], *sems, part, parts, span))


def _swap_pieces(gp_ref, x_ref, c):
    hr = x_ref.shape[1]
    cr = hr // SWAP_CHUNKS
    return [(a * SWAP_CHUNKS + q, gp_ref.at[a, pl.ds((1 - c) * hr + q * cr, cr)], x_ref.at[a, pl.ds(q * cr, cr)])
            for a in range(N_CHIPS) for q in range(SWAP_CHUNKS)]


def _swap_rider(gp):
    _, p_rows, d = gp.shape

    def start(ins, outs, sems):
        x, y, c = lax.axis_index("x"), lax.axis_index("y"), lax.axis_index("c")
        for i, src, dst in _swap_pieces(ins[0], outs[0], c):
            _remote(src, dst, sems[0].at[i], sems[1].at[i], (x, y, 1 - c)).start()

    def finish(ins, outs, sems):
        x, y, c = lax.axis_index("x"), lax.axis_index("y"), lax.axis_index("c")
        pieces = _swap_pieces(ins[0], outs[0], c)
        for i, src, dst in pieces:
            _remote(dst, dst, sems[0].at[i], sems[1].at[i], (x, y, 1 - c)).wait_recv()
        for i, src, dst in pieces:
            _remote(src, dst, sems[0].at[i], sems[1].at[i], (x, y, 1 - c)).wait_send()

    nsem = N_CHIPS * SWAP_CHUNKS
    return _Rider(
        operands=[gp], out_shapes=[jax.ShapeDtypeStruct((N_CHIPS, p_rows // 2, d), BF16)], aliases={},
        sems=[pltpu.SemaphoreType.DMA((nsem,)), pltpu.SemaphoreType.DMA((nsem,))], start=start, finish=finish)


def _presum(gp, theirs, core):
    _, hr, d = theirs.shape
    tr = _row_tile(hr, 640)
    steps = hr // tr

    def body(core_ref, mine_ref, theirs_ref, o_ref):
        o_ref[...] = (mine_ref[...].astype(F32) + theirs_ref[...].astype(F32)).astype(BF16)

    grid_spec = pltpu.PrefetchScalarGridSpec(
        num_scalar_prefetch=1, grid=(N_CHIPS, steps),
        in_specs=[pl.BlockSpec((None, tr, d), lambda a, i, cr: (a, cr[0] * steps + i, 0)),
                  pl.BlockSpec((None, tr, d), lambda a, i, cr: (a, i, 0))],
        out_specs=pl.BlockSpec((None, tr, d), lambda a, i, cr: (a, i, 0)))
    return pl.pallas_call(
        body, name="presum", grid_spec=grid_spec,
        out_shape=jax.ShapeDtypeStruct((N_CHIPS, hr, d), BF16), compiler_params=_params(2),
    )(core, gp, theirs)


def _row_tile(rows, cap=128):
    if rows <= cap:
        return rows
    return next(tr for tr in range(cap, 0, -16) if rows % tr == 0)


def _sum_sources(name, parts):
    nsrc, rows, cols = parts.shape
    tr = _row_tile(rows)

    def body(p_ref, o_ref):
        total = p_ref[0].astype(F32)
        for s in range(1, nsrc):
            total = total + p_ref[s].astype(F32)
        o_ref[...] = total

    return pl.pallas_call(
        body, name=name, grid=(rows // tr,),
        in_specs=[pl.BlockSpec((nsrc, tr, cols), lambda i: (0, i, 0))],
        out_specs=pl.BlockSpec((tr, cols), lambda i: (i, 0)),
        out_shape=jax.ShapeDtypeStruct((rows, cols), F32), compiler_params=_params(1),
    )(parts)


def _sum_grad_half(got, sums, place):
    nsrc, hr, d = got.shape
    tr = _row_tile(hr, 256)

    def body(place_ref, got_ref, own_ref, o_ref):
        total = own_ref[...].astype(F32)
        for s in range(nsrc):
            total = total + got_ref[s].astype(F32)
        o_ref[...] = total

    grid_spec = pltpu.PrefetchScalarGridSpec(
        num_scalar_prefetch=1, grid=(hr // tr,),
        in_specs=[pl.BlockSpec((nsrc, tr, d), lambda i, pc: (0, i, 0)),
                  pl.BlockSpec((None, tr, d), lambda i, pc: (pc[0], i, 0))],
        out_specs=pl.BlockSpec((None, tr, d), lambda i, pc: (pc[1], i, 0)))
    return pl.pallas_call(
        body, name="sum_grad_half", grid_spec=grid_spec,
        out_shape=jax.ShapeDtypeStruct((2, hr, d), F32), compiler_params=_params(1),
    )(place, got, sums)


def _halves_rider(full):
    _, hr, d = full.shape
    cr = hr // HALF_CHUNKS

    def pieces(ref, half):
        return [ref.at[half, pl.ds(q * cr, cr)] for q in range(HALF_CHUNKS)]

    def start(ins, outs, sems):
        x, y, c = lax.axis_index("x"), lax.axis_index("y"), lax.axis_index("c")
        for i, piece in enumerate(pieces(outs[0], c)):
            _remote(piece, piece, sems[0].at[i], sems[1].at[i], (x, y, 1 - c)).start()

    def finish(ins, outs, sems):
        x, y, c = lax.axis_index("x"), lax.axis_index("y"), lax.axis_index("c")
        for i, piece in enumerate(pieces(outs[0], 1 - c)):
            _remote(piece, piece, sems[0].at[i], sems[1].at[i], (x, y, 1 - c)).wait_recv()
        for i, piece in enumerate(pieces(outs[0], c)):
            _remote(piece, piece, sems[0].at[i], sems[1].at[i], (x, y, 1 - c)).wait_send()

    return _Rider(
        operands=[full], out_shapes=[jax.ShapeDtypeStruct(full.shape, F32)], aliases={0: 0},
        sems=[pltpu.SemaphoreType.DMA((HALF_CHUNKS,)), pltpu.SemaphoreType.DMA((HALF_CHUNKS,))],
        start=start, finish=finish)


def _adamw(name, w, m, v, g, g_row0=0, rider=None):
    rows, cols = w.shape
    tr = _row_tile(rows, 256)
    off = g_row0 // tr

    def body(w_ref, m_ref, v_ref, g_ref, go_ref, d_ref, mo_ref, vo_ref):
        grad = g_ref[...]
        m_new = ADAM_B1 * m_ref[...] + (1.0 - ADAM_B1) * grad
        v_new = ADAM_B2 * v_ref[...] + (1.0 - ADAM_B2) * (grad * grad)
        m_hat = m_new / (1.0 - ADAM_B1 ** ADAM_STEP)
        v_hat = v_new / (1.0 - ADAM_B2 ** ADAM_STEP)
        go_ref[...] = grad
        d_ref[...] = -ADAM_LR * (m_hat / (jnp.sqrt(v_hat) + ADAM_EPS) + ADAM_WD * w_ref[...])
        mo_ref[...] = m_new
        vo_ref[...] = v_new

    blk = pl.BlockSpec((tr, cols), lambda i: (i, 0))
    return _compute_call(
        body, name, grid=(rows // tr,),
        in_specs=[blk, blk, blk, pl.BlockSpec((tr, cols), lambda i: (i + off, 0))], operands=[w, m, v, g],
        out_specs=[blk, blk, blk, blk], out_shape=[jax.ShapeDtypeStruct((rows, cols), F32)] * 4, rider=rider)


def _pad_rows8(a):
    return jnp.concatenate([a, jnp.zeros((8 - a.shape[0],) + a.shape[1:], a.dtype)], axis=0)


def kernel(x, ln_pre_0, conv_w_in_0, conv_w_0, conv_w_out_0, ln_post_0, ln_pre_1, sb_w_in_1, sb_w_out_1, ln_post_1, ln_pre_2, conv_w_in_2, conv_w_2, conv_w_out_2, ln_post_2, ln_pre_3, sb_w_in_3, sb_w_out_3, ln_post_3, loss_target, m_ln_pre_0, m_conv_w_in_0, m_conv_w_0, m_conv_w_out_0, m_ln_post_0, m_ln_pre_1, m_sb_w_in_1, m_sb_w_out_1, m_ln_post_1, m_ln_pre_2, m_conv_w_in_2, m_conv_w_2, m_conv_w_out_2, m_ln_post_2, m_ln_pre_3, m_sb_w_in_3, m_sb_w_out_3, m_ln_post_3, v_ln_pre_0, v_conv_w_in_0, v_conv_w_0, v_conv_w_out_0, v_ln_post_0, v_ln_pre_1, v_sb_w_in_1, v_sb_w_out_1, v_ln_post_1, v_ln_pre_2, v_conv_w_in_2, v_conv_w_2, v_conv_w_out_2, v_ln_post_2, v_ln_pre_3, v_sb_w_in_3, v_sb_w_out_3, v_ln_post_3):
    t, d = x.shape[1], x.shape[2]
    dq = d // N_CHIPS
    xs = x.reshape(t, d)
    target = loss_target.reshape(t, d)
    w_in = [conv_w_in_0, sb_w_in_1, conv_w_in_2, sb_w_in_3]
    w_out = [conv_w_out_0, sb_w_out_1, conv_w_out_2, sb_w_out_3]
    m_in = [m_conv_w_in_0, m_sb_w_in_1, m_conv_w_in_2, m_sb_w_in_3]
    m_out = [m_conv_w_out_0, m_sb_w_out_1, m_conv_w_out_2, m_sb_w_out_3]
    v_in = [v_conv_w_in_0, v_sb_w_in_1, v_conv_w_in_2, v_sb_w_in_3]
    v_out = [v_conv_w_out_0, v_sb_w_out_1, v_conv_w_out_2, v_sb_w_out_3]
    ln_pre = [ln_pre_0, ln_pre_1, ln_pre_2, ln_pre_3]
    ln_post = [ln_post_0, ln_post_1, ln_post_2, ln_post_3]
    conv_w = [conv_w_0, conv_w_2]
    m_conv = [m_conv_w_0, m_conv_w_2]
    v_conv = [v_conv_w_0, v_conv_w_2]
    chip = 2 * lax.axis_index("x") + lax.axis_index("y")
    chip_arr = jnp.reshape(chip, (1,)).astype(jnp.int32)
    place = jnp.stack([chip, lax.axis_index("c")]).astype(jnp.int32)
    core_arr = jnp.reshape(lax.axis_index("c"), (1,)).astype(jnp.int32)

    packs = [_pack_weights(w_in[l], w_out[l], chip_arr) for l in range(N_LAYERS)]
    cw_local = jnp.concatenate([_pad_rows8(conv_w[0]), _pad_rows8(conv_w[1])], axis=0)
    gathered = list(packs)
    gathered[0], cw_all = _all_gather_weights(packs[:1], cw_local)
    cw_full = jnp.transpose(cw_all, (1, 0, 2)).reshape(16, d)
    conv_taps = {0: cw_full[0:8], 2: cw_full[8:16]}

    h_in, us, projs, gateds, ms, sb_saved = [], [], [], [], [], {}
    h = xs
    u = _norm_first(xs, ln_pre[0])
    for l in range(N_LAYERS):
        h_in.append(h)
        us.append(u)
        nxt = l + 1
        plan = {} if nxt == N_LAYERS else (AG_PLAN_CONV if l % 2 == 0 else AG_PLAN_SB)

        def rider_for(name):
            return _gather_rider(gathered[nxt:nxt + 1], **plan[name]) if name in plan else None

        def take(results, name):
            if name not in plan:
                return results
            gathered[nxt] = results[-1]
            return results[:-1]

        proj, = take(_mm_proj(u, gathered[l], rider=rider_for("mm_proj")), "mm_proj")
        if l % 2 == 0:
            gated, = take(_conv_fwd(proj, conv_taps[l], rider=rider_for("conv_fwd")), "conv_fwd")
        else:
            gated, o, car = take(_sb_fwd(proj, rider=rider_for("sb_fwd")), "sb_fwd")
            sb_saved[l] = (o, car)
        m, = take(_mm_out(gated, gathered[l], rider=rider_for("mm_out")), "mm_out")
        projs.append(proj)
        gateds.append(gated)
        ms.append(m)
        if l < N_LAYERS - 1:
            h, u = take(_norm_mid(h, m, ln_post[l], ln_pre[nxt], rider=rider_for("norm_mid")), "norm_mid")
    dh, dm, dg_post_last, loss_part = _norm_last(h, ms[-1], ln_post[-1], target)

    dg_pre = [None] * N_LAYERS
    dg_post = [None] * N_LAYERS
    dg_post[N_LAYERS - 1] = dg_post_last
    dconv = {}
    sums = [None] * N_LAYERS
    got = [None] * N_LAYERS
    fulls = [None] * N_LAYERS

    def summed(layer):
        return _halves_rider(_sum_grad_half(got[layer], sums[layer], place))

    for l in reversed(range(N_LAYERS)):
        above = l + 1 if l + 1 < N_LAYERS else None
        if l == 1:
            dgated, fulls[3] = _mm_dgated(dm, gathered[l], rider=summed(3))
        else:
            dgated, = _mm_dgated(dm, gathered[l])
        gp = _mm_dwout(gateds[l], dm)
        if l % 2 == 0:
            dproj, dconv[l] = _conv_bwd(projs[l], dgated, conv_taps[l])
        elif above is not None:
            o, car = sb_saved[l]
            dproj, got[above] = _sb_bwd(projs[l], dgated, o, car, rider=_send_sums_rider(sums[above]))
        else:
            o, car = sb_saved[l]
            dproj, = _sb_bwd(projs[l], dgated, o, car)
        if l == 0:
            gp, got[1] = _mm_dwin(us[0], dproj, gp, rider=_send_sums_rider(sums[1], None, 0, 4, 3))
            theirs, got[1] = _comm_call(
                "swap_last", _join_riders(_swap_rider(gp), _send_sums_rider(sums[1], got[1], 3, 4, 1)))
            sums[0] = _presum(gp, theirs, core_arr)
            du, got[0], fulls[1] = _mm_du(
                dproj, gathered[0], rider=_join_riders(_send_sums_rider(sums[0], None, 0, 4, 3), summed(1)))
        elif l % 2 == 0:
            gp, got[above] = _mm_dwin(us[l], dproj, gp, rider=_send_sums_rider(sums[above], None, 0, 2))
            du, theirs, got[above] = _mm_du(
                dproj, gathered[l], rider=_join_riders(_swap_rider(gp), _send_sums_rider(sums[above], got[above], 1, 2)))
            sums[l] = _presum(gp, theirs, core_arr)
        elif above is not None:
            gp, fulls[above] = _mm_dwin(us[l], dproj, gp, rider=summed(above))
            du, theirs = _mm_du(dproj, gathered[l], rider=_swap_rider(gp))
            sums[l] = _presum(gp, theirs, core_arr)
        else:
            gp, = _mm_dwin(us[l], dproj, gp)
            du, theirs = _mm_du(dproj, gathered[l], rider=_swap_rider(gp))
            sums[l] = _presum(gp, theirs, core_arr)
        if l > 0:
            dh, dm, dg_pre[l], dg_post[l - 1] = _norm_bwd_mid(dh, du, h_in[l], ln_pre[l], ms[l - 1], ln_post[l - 1])
    grad_x, dg_pre[0] = _norm_bwd_first(dh, du, h_in[0], ln_pre[0])

    loss_rows = jnp.pad(loss_part, ((0, 0), (0, d - loss_part.shape[1])))
    small = _pack_small(dg_pre + dg_post, [dconv[0], dconv[2]], loss_rows)
    got[0], small_all = _comm_call(
        "exchange_last", _join_riders(_send_sums_rider(sums[0], got[0], 3, 4, 1), _small_rider(small)))
    small_sum = _sum_sources("sum_small", small_all)

    fulls[0], = _comm_call("exchange_halves", summed(0))
    fulls = [f.reshape(d + dq, d) for f in fulls]
    res_in = [_adamw("adamw_w_in", w_in[l], m_in[l], v_in[l], fulls[l], 0) for l in range(N_LAYERS)]
    res_out = [_adamw("adamw_w_out", w_out[l], m_out[l], v_out[l], fulls[l], d) for l in range(N_LAYERS)]
    ln_all = ln_pre + ln_post
    ln_m = [m_ln_pre_0, m_ln_pre_1, m_ln_pre_2, m_ln_pre_3, m_ln_post_0, m_ln_post_1, m_ln_post_2, m_ln_post_3]
    ln_v = [v_ln_pre_0, v_ln_pre_1, v_ln_pre_2, v_ln_pre_3, v_ln_post_0, v_ln_post_1, v_ln_post_2, v_ln_post_3]
    res_ln = _adamw("adamw_ln", jnp.stack(ln_all), jnp.stack(ln_m), jnp.stack(ln_v), small_sum[0:2 * N_LAYERS])
    conv_g = [_pad_rows8(lax.dynamic_slice(small_sum, (2 * N_LAYERS + 3 * i, chip * dq), (3, dq))) for i in range(2)]
    stack8 = lambda a, b: jnp.concatenate([_pad_rows8(a), _pad_rows8(b)], axis=0)
    res_conv = _adamw("adamw_conv", stack8(*conv_w), stack8(*m_conv), stack8(*v_conv), jnp.concatenate(conv_g, axis=0))

    def leaf(kind, l, which):
        if kind == "ln_pre":
            return res_ln[which][l]
        if kind == "ln_post":
            return res_ln[which][N_LAYERS + l]
        if kind == "w_in":
            return res_in[l][which]
        if kind == "w_out":
            return res_out[l][which]
        return res_conv[which][8 * (l // 2):8 * (l // 2) + 3]

    order = []
    for l in range(N_LAYERS):
        order.append(("ln_pre", l))
        order.append(("w_in", l))
        if l % 2 == 0:
            order.append(("conv", l))
        order.append(("w_out", l))
        order.append(("ln_post", l))
    loss = small_sum[2 * N_LAYERS + 3 * 2, 0]
    outs = [loss, grad_x.reshape(1, t, d)]
    for which in range(4):
        outs.extend(leaf(kind, l, which) for kind, l in order)
    return tuple(outs)
```

```python
import functools
import math
from typing import Any, Callable, Mapping, NamedTuple, Sequence

import jax
import jax.numpy as jnp
from jax import lax
from jax.experimental import pallas as pl
from jax.experimental.pallas import tpu as pltpu

F32 = jnp.float32
BF16 = jnp.bfloat16

N_CHIPS = 4
N_DEV = 8
N_LAYERS = 4
HEAD_DIM = 128
RMS_EPS = 1e-6
ADAM_LR = 0.001
ADAM_B1 = 0.9
ADAM_B2 = 0.999
ADAM_EPS = 1e-08
ADAM_WD = 0.01
ADAM_STEP = 10

VMEM_LIMIT = 56 * 1024 * 1024
MESH_IDS = pl.DeviceIdType.MESH
HBM_SPEC = pl.BlockSpec(memory_space=pltpu.HBM)

NN = (((1,), (0,)), ((), ()))
NT = (((1,), (1,)), ((), ()))
TN = (((0,), (0,)), ((), ()))


def _params(n_axes):
    return pltpu.CompilerParams(dimension_semantics=("arbitrary",) * n_axes, vmem_limit_bytes=VMEM_LIMIT)


def _dot(a, b, dims):
    return lax.dot_general(a, b, dims, preferred_element_type=F32)


def _sigmoid(z):
    return 1.0 / (1.0 + jnp.exp(-z))


class _Rider(NamedTuple):
    operands: Sequence[Any]
    out_shapes: Sequence[Any]
    aliases: Mapping[int, int]
    sems: Sequence[Any]
    start: Callable
    finish: Callable


def _compute_call(body, name, *, grid, in_specs, operands, out_specs, out_shape, scratch=(), aliases=None, rider=None):
    in_specs, operands = list(in_specs), list(operands)
    out_specs, out_shape, scratch = list(out_specs), list(out_shape), list(scratch)
    aliases = dict(aliases or {})
    n_in, n_out, n_scratch = len(operands), len(out_shape), len(scratch)
    hosted = body
    if rider is not None:
        r_in, r_out = len(rider.operands), len(rider.out_shapes)
        aliases.update({n_in + i: n_out + o for i, o in rider.aliases.items()})

        def hosted(*refs):
            ins, refs = refs[:n_in], refs[n_in:]
            rider_ins, refs = refs[:r_in], refs[r_in:]
            outs, refs = refs[:n_out], refs[n_out:]
            rider_outs, refs = refs[:r_out], refs[r_out:]
            own_scratch, rider_sems = refs[:n_scratch], refs[n_scratch:]
            ids = [pl.program_id(axis) for axis in range(len(grid))]
            first = functools.reduce(jnp.logical_and, [i == 0 for i in ids])
            last = functools.reduce(jnp.logical_and, [i == g - 1 for i, g in zip(ids, grid)])

            @pl.when(first)
            def _():
                rider.start(rider_ins, rider_outs, rider_sems)

            body(*ins, *outs, *own_scratch)

            @pl.when(last)
            def _():
                rider.finish(rider_ins, rider_outs, rider_sems)

        in_specs += [HBM_SPEC] * r_in
        operands += list(rider.operands)
        out_specs += [HBM_SPEC] * r_out
        out_shape += list(rider.out_shapes)
        scratch += list(rider.sems)
    return pl.pallas_call(
        hosted, name=name, grid=grid, in_specs=in_specs, out_specs=out_specs, out_shape=out_shape,
        scratch_shapes=scratch, input_output_aliases=aliases, compiler_params=_params(len(grid)),
    )(*operands)


def _matmul(name, a, b, *, grid, a_spec, b_spec, o_spec, out_shape, dims, reduce_axis=None, acc_shape=None,
            alias_out=None, rider=None):
    out_dtype = out_shape.dtype
    direct = reduce_axis is not None and out_dtype == F32
    n_red = grid[reduce_axis] if reduce_axis is not None else 1

    def body(*refs):
        if alias_out is not None:
            refs = refs[1:]
        a_ref, b_ref, o_ref = refs[:3]
        if reduce_axis is None:
            o_ref[...] = _dot(a_ref[...], b_ref[...], dims).astype(out_dtype)
            return
        acc_ref = o_ref if direct else refs[3]
        k = pl.program_id(reduce_axis)

        @pl.when(k == 0)
        def _():
            acc_ref[...] = jnp.zeros_like(acc_ref)

        acc_ref[...] += _dot(a_ref[...], b_ref[...], dims)

        if not direct:
            @pl.when(k == n_red - 1)
            def _():
                o_ref[...] = acc_ref[...].astype(out_dtype)

    scratch = []
    if reduce_axis is not None and not direct:
        scratch = [pltpu.VMEM(acc_shape, F32)]
    in_specs = [a_spec, b_spec]
    operands = [a, b]
    aliases = {}
    if alias_out is not None:
        in_specs = [HBM_SPEC] + in_specs
        operands = [alias_out] + operands
        aliases = {0: 0}
    return _compute_call(body, name, grid=grid, in_specs=in_specs, operands=operands, out_specs=[o_spec],
                         out_shape=[out_shape], scratch=scratch, aliases=aliases, rider=rider)


def _mm_proj(u, g, rider=None):
    t, d = u.shape
    tm = min(512, t)
    return _matmul(
        "mm_proj", u, g, grid=(N_CHIPS, t // tm),
        a_spec=pl.BlockSpec((tm, d), lambda s, m: (m, 0)),
        b_spec=pl.BlockSpec((None, d, d), lambda s, m: (s, 0, 0)),
        o_spec=pl.BlockSpec((None, tm, d), lambda s, m: (s, m, 0)),
        out_shape=jax.ShapeDtypeStruct((N_CHIPS, t, d), BF16), dims=NN, rider=rider)


def _mm_out(gated, g, rider=None):
    t, d = gated.shape
    dq = d // N_CHIPS
    tm = min(512, t)

    def body(a_ref, b_ref, o_ref):
        acc = _dot(a_ref[:, 0:dq], b_ref[0], NN)
        for s in range(1, N_CHIPS):
            acc = acc + _dot(a_ref[:, s * dq:(s + 1) * dq], b_ref[s], NN)
        o_ref[...] = acc

    return _compute_call(
        body, "mm_out", grid=(t // tm,),
        in_specs=[pl.BlockSpec((tm, d), lambda m: (m, 0)), pl.BlockSpec((N_CHIPS, dq, d), lambda m: (0, N_CHIPS, 0))],
        operands=[gated, g], out_specs=[pl.BlockSpec((tm, d), lambda m: (m, 0))],
        out_shape=[jax.ShapeDtypeStruct((t, d), F32)], rider=rider)


def _mm_dgated(dm, g, rider=None):
    t, d = dm.shape
    dq = d // N_CHIPS
    tm = min(512, t)

    def body(a_ref, b_ref, o_ref):
        a = a_ref[...]
        for s in range(N_CHIPS):
            o_ref[:, s * dq:(s + 1) * dq] = _dot(a, b_ref[s], NT).astype(BF16)

    return _compute_call(
        body, "mm_dgated", grid=(t // tm,),
        in_specs=[pl.BlockSpec((tm, d), lambda m: (m, 0)), pl.BlockSpec((N_CHIPS, dq, d), lambda m: (0, N_CHIPS, 0))],
        operands=[dm, g], out_specs=[pl.BlockSpec((tm, d), lambda m: (m, 0))],
        out_shape=[jax.ShapeDtypeStruct((t, d), BF16)], rider=rider)


def _mm_dwout(gated, dm):
    t, d = gated.shape
    dq = d // N_CHIPS
    tk = min(1024, t)
    return _matmul(
        "mm_dwout", gated, dm, grid=(N_CHIPS, t // tk),
        a_spec=pl.BlockSpec((tk, dq), lambda s, k: (k, s)),
        b_spec=pl.BlockSpec((tk, d), lambda s, k: (k, 0)),
        o_spec=pl.BlockSpec((None, dq, d), lambda s, k: (s, N_CHIPS, 0)),
        out_shape=jax.ShapeDtypeStruct((N_CHIPS, d + dq, d), BF16), dims=TN, reduce_axis=1, acc_shape=(dq, d))[0]


def _mm_du(dproj, g, rider=None):
    _, t, d = dproj.shape
    tm = min(512, t)
    return _matmul(
        "mm_du", dproj, g, grid=(t // tm, N_CHIPS),
        a_spec=pl.BlockSpec((None, tm, d), lambda m, s: (s, m, 0)),
        b_spec=pl.BlockSpec((None, d, d), lambda m, s: (s, 0, 0)),
        o_spec=pl.BlockSpec((tm, d), lambda m, s: (m, 0)),
        out_shape=jax.ShapeDtypeStruct((t, d), F32), dims=NT, reduce_axis=1, rider=rider)


def _mm_dwin(u, dproj, gp, rider=None):
    t, d = u.shape
    tmo = min(1024, d)
    tk = min(1024, t)
    return _matmul(
        "mm_dwin", u, dproj, grid=(N_CHIPS, d // tmo, t // tk),
        a_spec=pl.BlockSpec((tk, tmo), lambda s, mo, k: (k, mo)),
        b_spec=pl.BlockSpec((None, tk, d), lambda s, mo, k: (s, k, 0)),
        o_spec=pl.BlockSpec((None, tmo, d), lambda s, mo, k: (s, mo, 0)),
        out_shape=jax.ShapeDtypeStruct(gp.shape, BF16), dims=TN, reduce_axis=2, acc_shape=(tmo, d), alias_out=gp,
        rider=rider)


def _rms(v):
    r = lax.rsqrt(jnp.mean(v * v, axis=-1, keepdims=True) + RMS_EPS)
    return v * r, r


def _rms_bwd(dout, n, r, gain):
    dn = dout * gain
    return r * (dn - n * jnp.mean(dn * n, axis=-1, keepdims=True))


def _fold8(v):
    return jnp.sum(v.reshape(v.shape[0] // 8, 8, v.shape[1]), axis=0)


def _row0(total):
    rows = lax.broadcasted_iota(jnp.int32, total.shape, 0)
    return jnp.where(rows == 0, jnp.sum(total, axis=0, keepdims=True), 0.0)


def _norm_tile(t):
    return min(256, t)


def _norm_first(x, g_pre, rider=None):
    t, d = x.shape
    tr = _norm_tile(t)

    def body(x_ref, g_ref, u_ref):
        n, _ = _rms(x_ref[...])
        u_ref[...] = (n * g_ref[...]).astype(BF16)

    row = pl.BlockSpec((tr, d), lambda i: (i, 0))
    vec = pl.BlockSpec((1, d), lambda i: (0, 0))
    return _compute_call(
        body, "norm_first", grid=(t // tr,), in_specs=[row, vec], operands=[x, g_pre.reshape(1, d)],
        out_specs=[row], out_shape=[jax.ShapeDtypeStruct((t, d), BF16)], rider=rider)


def _norm_mid(h, m, g_post, g_pre_next, rider=None):
    t, d = h.shape
    tr = _norm_tile(t)

    def body(h_ref, m_ref, gp_ref, gn_ref, hn_ref, u_ref):
        n, _ = _rms(m_ref[...])
        hn = h_ref[...] + n * gp_ref[...]
        hn_ref[...] = hn
        n2, _ = _rms(hn)
        u_ref[...] = (n2 * gn_ref[...]).astype(BF16)

    row = pl.BlockSpec((tr, d), lambda i: (i, 0))
    vec = pl.BlockSpec((1, d), lambda i: (0, 0))
    return _compute_call(
        body, "norm_mid", grid=(t // tr,), in_specs=[row, row, vec, vec],
        operands=[h, m, g_post.reshape(1, d), g_pre_next.reshape(1, d)], out_specs=[row, row],
        out_shape=[jax.ShapeDtypeStruct((t, d), F32), jax.ShapeDtypeStruct((t, d), BF16)], rider=rider)


def _norm_last(h, m, g_post, target):
    t, d = h.shape
    tr = _norm_tile(t)
    nsteps = t // tr

    def body(h_ref, m_ref, gp_ref, tg_ref, dy_ref, dm_ref, dgp_ref, loss_ref, acc_g, acc_l):
        i = pl.program_id(0)

        @pl.when(i == 0)
        def _():
            acc_g[...] = jnp.zeros_like(acc_g)
            acc_l[...] = jnp.zeros_like(acc_l)

        gain = gp_ref[...]
        n, r = _rms(m_ref[...])
        err = h_ref[...] + n * gain - tg_ref[...]
        dy = err / d
        dy_ref[...] = dy
        dm_ref[...] = _rms_bwd(dy, n, r, gain).astype(BF16)
        acc_g[...] += _fold8(dy * n)
        acc_l[...] += _fold8(err * err)

        @pl.when(i == nsteps - 1)
        def _():
            dgp_ref[...] = _row0(acc_g[...])
            loss_ref[...] = jnp.zeros((8, 128), F32) + (0.5 / d) * jnp.sum(acc_l[...])

    row = pl.BlockSpec((tr, d), lambda i: (i, 0))
    vec = pl.BlockSpec((1, d), lambda i: (0, 0))
    acc = pl.BlockSpec((8, d), lambda i: (0, 0))
    return pl.pallas_call(
        body, name="norm_last", grid=(nsteps,), in_specs=[row, row, vec, row],
        out_specs=[row, row, acc, pl.BlockSpec((8, 128), lambda i: (0, 0))],
        out_shape=[jax.ShapeDtypeStruct((t, d), F32), jax.ShapeDtypeStruct((t, d), BF16),
                   jax.ShapeDtypeStruct((8, d), F32), jax.ShapeDtypeStruct((8, 128), F32)],
        scratch_shapes=[pltpu.VMEM((8, d), F32), pltpu.VMEM((8, d), F32)],
        compiler_params=_params(1),
    )(h, m, g_post.reshape(1, d), target)


def _norm_bwd_mid(dh, du, h_in, g_pre, m_prev, g_post_prev):
    t, d = dh.shape
    tr = _norm_tile(t)
    nsteps = t // tr

    def body(dh_ref, du_ref, h_ref, gpre_ref, m_ref, gpost_ref, dhn_ref, dm_ref, dgpre_ref, dgpost_ref, acc_a, acc_b):
        i = pl.program_id(0)

        @pl.when(i == 0)
        def _():
            acc_a[...] = jnp.zeros_like(acc_a)
            acc_b[...] = jnp.zeros_like(acc_b)

        du_t = du_ref[...]
        n, r = _rms(h_ref[...])
        dhn = dh_ref[...] + _rms_bwd(du_t, n, r, gpre_ref[...])
        dhn_ref[...] = dhn
        acc_a[...] += _fold8(du_t * n)
        n2, r2 = _rms(m_ref[...])
        dm_ref[...] = _rms_bwd(dhn, n2, r2, gpost_ref[...]).astype(BF16)
        acc_b[...] += _fold8(dhn * n2)

        @pl.when(i == nsteps - 1)
        def _():
            dgpre_ref[...] = _row0(acc_a[...])
            dgpost_ref[...] = _row0(acc_b[...])

    row = pl.BlockSpec((tr, d), lambda i: (i, 0))
    vec = pl.BlockSpec((1, d), lambda i: (0, 0))
    acc = pl.BlockSpec((8, d), lambda i: (0, 0))
    return pl.pallas_call(
        body, name="norm_bwd_mid", grid=(nsteps,), in_specs=[row, row, row, vec, row, vec],
        out_specs=[row, row, acc, acc],
        out_shape=[jax.ShapeDtypeStruct((t, d), F32), jax.ShapeDtypeStruct((t, d), BF16),
                   jax.ShapeDtypeStruct((8, d), F32), jax.ShapeDtypeStruct((8, d), F32)],
        scratch_shapes=[pltpu.VMEM((8, d), F32), pltpu.VMEM((8, d), F32)],
        compiler_params=_params(1),
    )(dh, du, h_in, g_pre.reshape(1, d), m_prev, g_post_prev.reshape(1, d))


def _norm_bwd_first(dh, du, x, g_pre):
    t, d = dh.shape
    tr = _norm_tile(t)
    nsteps = t // tr

    def body(dh_ref, du_ref, x_ref, gpre_ref, dx_ref, dgpre_ref, acc_a):
        i = pl.program_id(0)

        @pl.when(i == 0)
        def _():
            acc_a[...] = jnp.zeros_like(acc_a)

        du_t = du_ref[...]
        n, r = _rms(x_ref[...])
        dx_ref[...] = dh_ref[...] + _rms_bwd(du_t, n, r, gpre_ref[...])
        acc_a[...] += _fold8(du_t * n)

        @pl.when(i == nsteps - 1)
        def _():
            dgpre_ref[...] = _row0(acc_a[...])

    row = pl.BlockSpec((tr, d), lambda i: (i, 0))
    vec = pl.BlockSpec((1, d), lambda i: (0, 0))
    acc = pl.BlockSpec((8, d), lambda i: (0, 0))
    return _compute_call(
        body, "norm_bwd_first", grid=(nsteps,), in_specs=[row, row, row, vec],
        operands=[dh, du, x, g_pre.reshape(1, d)], out_specs=[row, acc],
        out_shape=[jax.ShapeDtypeStruct((t, d), F32), jax.ShapeDtypeStruct((8, d), F32)],
        scratch=[pltpu.VMEM((8, d), F32)])


CONV_TC = 128
CONV_HALO = 16


def _conv_chunk(t):
    return min(512, t)


def _shift_down(v, steps, fill):
    rows = lax.broadcasted_iota(jnp.int32, v.shape, 0)
    out = pltpu.roll(v, steps, axis=0)
    for k in range(steps):
        out = jnp.where(rows == k, fill[CONV_HALO - steps + k:CONV_HALO - steps + k + 1, :], out)
    return out


def _shift_up(v, steps, fill):
    nrows = v.shape[0]
    rows = lax.broadcasted_iota(jnp.int32, v.shape, 0)
    out = pltpu.roll(v, nrows - steps, axis=0)
    for k in range(steps):
        out = jnp.where(rows == nrows - steps + k, fill[k:k + 1, :], out)
    return out


def _conv_fwd(proj, cw, rider=None):
    _, t, d = proj.shape
    chunk = _conv_chunk(t)

    def body(p_ref, w_ref, o_ref):
        w = w_ref[...]
        w0, w1, w2 = w[0:1, :], w[1:2, :], w[2:3, :]
        for ci in range(t // chunk):
            t0 = ci * chunk
            rows = pl.ds(t0, chunk)
            b = p_ref[0, rows, :].astype(F32)
            cx = p_ref[1, rows, :].astype(F32) * p_ref[2, rows, :].astype(F32)
            z = p_ref[3, rows, :].astype(F32)
            if ci == 0:
                prev = jnp.zeros((CONV_HALO, CONV_TC), F32)
            else:
                halo = pl.ds(t0 - CONV_HALO, CONV_HALO)
                prev = p_ref[1, halo, :].astype(F32) * p_ref[2, halo, :].astype(F32)
            conv = w2 * cx + w1 * _shift_down(cx, 1, prev) + w0 * _shift_down(cx, 2, prev)
            o_ref[rows, :] = (z * _sigmoid(z) * b * conv).astype(BF16)

    return _compute_call(
        body, "conv_fwd", grid=(d // CONV_TC,),
        in_specs=[pl.BlockSpec((N_CHIPS, t, CONV_TC), lambda j: (0, 0, j)), pl.BlockSpec((8, CONV_TC), lambda j: (0, j))],
        operands=[proj, cw], out_specs=[pl.BlockSpec((t, CONV_TC), lambda j: (0, j))],
        out_shape=[jax.ShapeDtypeStruct((t, d), BF16)], rider=rider)


def _conv_bwd(proj, dgated, cw):
    _, t, d = proj.shape
    chunk = _conv_chunk(t)
    nchunks = t // chunk

    def body(p_ref, dg_ref, w_ref, dp_ref, dw_ref):
        w = w_ref[...]
        w0, w1, w2 = w[0:1, :], w[1:2, :], w[2:3, :]
        dw0 = jnp.zeros((1, CONV_TC), F32)
        dw1 = jnp.zeros((1, CONV_TC), F32)
        dw2 = jnp.zeros((1, CONV_TC), F32)
        for ci in range(nchunks):
            t0 = ci * chunk
            rows = pl.ds(t0, chunk)
            b = p_ref[0, rows, :].astype(F32)
            c = p_ref[1, rows, :].astype(F32)
            xt = p_ref[2, rows, :].astype(F32)
            z = p_ref[3, rows, :].astype(F32)
            dg = dg_ref[rows, :].astype(F32)
            cx = c * xt
            if ci == 0:
                prev = jnp.zeros((CONV_HALO, CONV_TC), F32)
            else:
                halo = pl.ds(t0 - CONV_HALO, CONV_HALO)
                prev = p_ref[1, halo, :].astype(F32) * p_ref[2, halo, :].astype(F32)
            cx1 = _shift_down(cx, 1, prev)
            cx2 = _shift_down(cx, 2, prev)
            conv = w2 * cx + w1 * cx1 + w0 * cx2
            sig = _sigmoid(z)
            dy = dg * (z * sig)
            dconv = dy * b
            if ci == nchunks - 1:
                nxt = jnp.zeros((CONV_HALO, CONV_TC), F32)
            else:
                halo = pl.ds(t0 + chunk, CONV_HALO)
                zn = p_ref[3, halo, :].astype(F32)
                nxt = dg_ref[halo, :].astype(F32) * (zn * _sigmoid(zn)) * p_ref[0, halo, :].astype(F32)
            dcx = w2 * dconv + w1 * _shift_up(dconv, 1, nxt) + w0 * _shift_up(dconv, 2, nxt)
            dp_ref[0, rows, :] = (dy * conv).astype(BF16)
            dp_ref[1, rows, :] = (dcx * xt).astype(BF16)
            dp_ref[2, rows, :] = (dcx * c).astype(BF16)
            dp_ref[3, rows, :] = (dg * (b * conv) * (sig * (1.0 + z * (1.0 - sig)))).astype(BF16)
            dw0 = dw0 + jnp.sum(dconv * cx2, axis=0, keepdims=True)
            dw1 = dw1 + jnp.sum(dconv * cx1, axis=0, keepdims=True)
            dw2 = dw2 + jnp.sum(dconv * cx, axis=0, keepdims=True)
        taps = lax.broadcasted_iota(jnp.int32, (8, CONV_TC), 0)
        dw_ref[...] = jnp.where(taps == 0, dw0, jnp.where(taps == 1, dw1, jnp.where(taps == 2, dw2, 0.0)))

    return pl.pallas_call(
        body, name="conv_bwd", grid=(d // CONV_TC,),
        in_specs=[pl.BlockSpec((N_CHIPS, t, CONV_TC), lambda j: (0, 0, j)),
                  pl.BlockSpec((t, CONV_TC), lambda j: (0, j)),
                  pl.BlockSpec((8, CONV_TC), lambda j: (0, j))],
        out_specs=[pl.BlockSpec((N_CHIPS, t, CONV_TC), lambda j: (0, 0, j)), pl.BlockSpec((8, CONV_TC), lambda j: (0, j))],
        out_shape=[jax.ShapeDtypeStruct((N_CHIPS, t, d), BF16), jax.ShapeDtypeStruct((8, d), F32)],
        compiler_params=_params(1),
    )(proj, dgated, cw)


SB_DEAD_TAIL = -105.0
SB_COUNT_LANE = HEAD_DIM - 1


def _sb_block(t):
    return min(256, t)


def _split_dot(v, tri):
    hi = v.astype(BF16)
    lo = (v - hi.astype(F32)).astype(BF16)
    return _dot(hi, tri, NN) + _dot(lo, tri, NN)


SB_HEADS_PER_STEP = 4


def _sb_terms(s, diagonal):
    sp = jnp.maximum(s, 0.0) + jnp.log(1.0 + jnp.exp(-jnp.abs(s)))
    if not diagonal:
        return -sp, s - sp, sp, None
    mask = lax.broadcasted_iota(jnp.int32, s.shape, 1) < lax.broadcasted_iota(jnp.int32, s.shape, 0)
    return jnp.where(mask, -sp, 0.0), s - sp, sp, mask


def _masked(mask, v):
    return v if mask is None else jnp.where(mask, v, 0.0)


def _sb_fwd(proj, rider=None):
    _, t, d = proj.shape
    heads = d // HEAD_DIM
    blk = _sb_block(t)
    nblk = t // blk
    scale = 1.0 / math.sqrt(HEAD_DIM)

    hps = SB_HEADS_PER_STEP
    width = hps * HEAD_DIM

    def body(q_ref, k_ref, v_ref, z_ref, gated_ref, o_ref, car_ref, tail_ref, acc_ref):
        i = pl.program_id(1)
        r_i = lax.broadcasted_iota(jnp.int32, (blk, blk), 0)
        c_i = lax.broadcasted_iota(jnp.int32, (blk, blk), 1)
        tri_after = (r_i > c_i).astype(BF16)
        lanes = lax.broadcasted_iota(jnp.int32, (blk, HEAD_DIM), 1)

        tail_ref[...] = jnp.zeros_like(tail_ref)
        acc_ref[...] = jnp.zeros_like(acc_ref)
        car_ref[...] = jnp.zeros_like(car_ref)

        def visit(j, diagonal):
            krows = pl.ds(pl.multiple_of(j * blk, blk), blk)
            hcols = [pl.ds(hh * HEAD_DIM, HEAD_DIM) for hh in range(hps)]
            logits = [_dot(q_ref[:, c], k_ref[krows, c], NT) for c in hcols]
            terms = [_sb_terms(s * scale, diagonal) for s in logits]
            within = [_split_dot(keep, tri_after) for keep, _, _, _ in terms]
            top = None
            for hh, (keep, log_beta, _, mask) in enumerate(terms):
                tail_b = tail_ref[hh]
                w = _masked(mask, jnp.exp(log_beta + tail_b[:, 0:1] + within[hh]))
                acc_ref[hh] += _dot(w.astype(BF16), v_ref[krows, hcols[hh]], NN)
                car_ref[hh] = jnp.where(lanes == j, tail_b, car_ref[hh])
                tail_new = tail_b + jnp.sum(keep, axis=1, keepdims=True)
                tail_ref[hh] = tail_new
                top = jnp.max(tail_new) if top is None else jnp.maximum(top, jnp.max(tail_new))
            return top > SB_DEAD_TAIL

        def more(state):
            jj, live = state
            return jnp.logical_and(jj <= i, live)

        def step(state):
            jj, _ = state
            return jj + 1, visit(i - jj, False)

        visited, _ = lax.while_loop(more, step, (jnp.int32(1), visit(i, True)))
        for hh in range(hps):
            cols = pl.ds(hh * HEAD_DIM, HEAD_DIM)
            car_ref[hh] = jnp.where(lanes == SB_COUNT_LANE, visited.astype(F32), car_ref[hh])
            z = z_ref[:, cols].astype(F32)
            acc = acc_ref[hh]
            o_ref[:, cols] = acc.astype(BF16)
            gated_ref[:, cols] = (z * _sigmoid(z) * acc).astype(BF16)

    qspec = lambda s: pl.BlockSpec((None, blk, width), lambda h, i: (s, i, h))
    kspec = lambda s: pl.BlockSpec((None, t, width), lambda h, i: (s, 0, h))
    ospec = pl.BlockSpec((blk, width), lambda h, i: (i, h))
    return _compute_call(
        body, "sb_fwd", grid=(heads // hps, nblk),
        in_specs=[qspec(0), kspec(1), kspec(2), qspec(3)], operands=[proj, proj, proj, proj],
        out_specs=[ospec, ospec, pl.BlockSpec((hps, blk, HEAD_DIM), lambda h, i: (h, i, 0))],
        out_shape=[jax.ShapeDtypeStruct((t, d), BF16), jax.ShapeDtypeStruct((t, d), BF16),
                   jax.ShapeDtypeStruct((heads, t, HEAD_DIM), F32)],
        scratch=[pltpu.VMEM((hps, blk, HEAD_DIM), F32), pltpu.VMEM((hps, blk, HEAD_DIM), F32)], rider=rider)


def _sb_bwd(proj, dgated, o, car, rider=None):
    _, t, d = proj.shape
    heads = d // HEAD_DIM
    blk = _sb_block(t)
    nblk = t // blk
    scale = 1.0 / math.sqrt(HEAD_DIM)

    hps = SB_HEADS_PER_STEP
    width = hps * HEAD_DIM

    def body(q_ref, k_ref, v_ref, z_ref, dg_ref, o_ref, car_ref, dp_ref, dk_acc, dv_acc, gsum_ref, dq_ref, do_ref):
        step_i = pl.program_id(1)
        i = nblk - 1 - step_i

        @pl.when(step_i == 0)
        def _():
            dk_acc[...] = jnp.zeros_like(dk_acc)
            dv_acc[...] = jnp.zeros_like(dv_acc)

        r_i = lax.broadcasted_iota(jnp.int32, (blk, blk), 0)
        c_i = lax.broadcasted_iota(jnp.int32, (blk, blk), 1)
        tri_after = (r_i > c_i).astype(BF16)
        tri_before = (r_i < c_i).astype(BF16)
        lanes = lax.broadcasted_iota(jnp.int32, (blk, HEAD_DIM), 1)

        gsum_ref[...] = jnp.zeros_like(gsum_ref)
        dq_ref[...] = jnp.zeros_like(dq_ref)
        for hh in range(hps):
            cols = pl.ds(hh * HEAD_DIM, HEAD_DIM)
            z = z_ref[:, cols].astype(F32)
            dg = dg_ref[:, cols].astype(F32)
            sig = _sigmoid(z)
            do_ref[hh] = (dg * (z * sig)).astype(BF16)
            dp_ref[3, :, cols] = (dg * o_ref[:, cols].astype(F32) * (sig * (1.0 + z * (1.0 - sig)))).astype(BF16)

        def visit(j, diagonal):
            krows = pl.ds(pl.multiple_of(j * blk, blk), blk)
            hcols = [pl.ds(hh * HEAD_DIM, HEAD_DIM) for hh in range(hps)]
            logits = [_dot(q_ref[:, c], k_ref[krows, c], NT) for c in hcols]
            dws = [_dot(do_ref[hh], v_ref[krows, c], NT) for hh, c in enumerate(hcols)]
            terms = [_sb_terms(s * scale, diagonal) for s in logits]
            within = [_split_dot(keep, tri_after) for keep, _, _, _ in terms]
            ws, gs = [], []
            for hh, (keep, log_beta, sp, mask) in enumerate(terms):
                tail = jnp.sum(jnp.where(lanes == j, car_ref[hh], 0.0), axis=1, keepdims=True)
                w = _masked(mask, jnp.exp(log_beta + tail + within[hh]))
                ws.append(w.astype(BF16))
                gs.append(w * dws[hh])
            g_within = [_split_dot(g, tri_before) for g in gs]
            for hh, (keep, log_beta, sp, mask) in enumerate(terms):
                c = hcols[hh]
                g_before = gsum_ref[hh]
                g_cum = g_before[:, 0:1] + g_within[hh]
                dl = (_masked(mask, gs[hh] - (gs[hh] + g_cum) * jnp.exp(log_beta)) * scale).astype(BF16)
                dq_ref[hh] += _dot(dl, k_ref[krows, c], NN)
                dk_acc[krows, c] += _dot(dl, q_ref[:, c], TN)
                dv_acc[krows, c] += _dot(ws[hh], do_ref[hh], TN)
                gsum_ref[hh] = g_before + jnp.sum(gs[hh], axis=1, keepdims=True)

        def step(j, carry):
            visit(j, False)
            return carry

        visited = jnp.max(jnp.where(lanes == SB_COUNT_LANE, car_ref[0], 0.0)).astype(jnp.int32)
        lax.fori_loop(i + 1 - visited, i, step, 0)
        visit(i, True)
        own = pl.ds(pl.multiple_of(i * blk, blk), blk)
        for hh in range(hps):
            cols = pl.ds(hh * HEAD_DIM, HEAD_DIM)
            dp_ref[0, :, cols] = dq_ref[hh].astype(BF16)
        dp_ref[1] = dk_acc[own, :].astype(BF16)
        dp_ref[2] = dv_acc[own, :].astype(BF16)

    qspec = lambda s: pl.BlockSpec((None, blk, width), lambda h, i: (s, nblk - 1 - i, h))
    kspec = lambda s: pl.BlockSpec((None, t, width), lambda h, i: (s, 0, h))
    tspec = pl.BlockSpec((blk, width), lambda h, i: (nblk - 1 - i, h))
    return _compute_call(
        body, "sb_bwd", grid=(heads // hps, nblk),
        in_specs=[qspec(0), kspec(1), kspec(2), qspec(3), tspec, tspec,
                  pl.BlockSpec((hps, blk, HEAD_DIM), lambda h, i: (h, nblk - 1 - i, 0))],
        operands=[proj, proj, proj, proj, dgated, o, car],
        out_specs=[pl.BlockSpec((N_CHIPS, blk, width), lambda h, i: (0, nblk - 1 - i, h))],
        out_shape=[jax.ShapeDtypeStruct((N_CHIPS, t, d), BF16)],
        scratch=[pltpu.VMEM((t, width), F32), pltpu.VMEM((t, width), F32),
                 pltpu.VMEM((hps, blk, HEAD_DIM), F32), pltpu.VMEM((hps, blk, HEAD_DIM), F32),
                 pltpu.VMEM((hps, blk, HEAD_DIM), BF16)], rider=rider)


def _pack_weights(w_in, w_out, chip):
    d = w_in.shape[0]
    rb = d // 8
    n_in = d // rb
    n_out = w_out.shape[0] // rb

    def body(chip_ref, wi_ref, wo_ref, o_ref):
        r = pl.program_id(0)

        @pl.when(r < n_in)
        def _():
            o_ref[...] = wi_ref[...].astype(BF16)

        @pl.when(r >= n_in)
        def _():
            o_ref[...] = wo_ref[...].astype(BF16)

    grid_spec = pltpu.PrefetchScalarGridSpec(
        num_scalar_prefetch=1, grid=(n_in + n_out,),
        in_specs=[pl.BlockSpec((rb, d), lambda r, me: (jnp.minimum(r, n_in - 1), 0)),
                  pl.BlockSpec((rb, d), lambda r, me: (jnp.maximum(r - n_in, 0), 0))],
        out_specs=pl.BlockSpec((None, rb, d), lambda r, me: (me[0], r, 0)))
    return pl.pallas_call(
        body, name="pack_weights", grid_spec=grid_spec,
        out_shape=jax.ShapeDtypeStruct((N_CHIPS, d + w_out.shape[0], d), BF16), compiler_params=_params(1),
    )(chip, w_in, w_out)


def _flip(v, bit):
    return 1 - v if bit else v


def _remote(src, dst, send_sem, recv_sem, target):
    return pltpu.make_async_remote_copy(src_ref=src, dst_ref=dst, send_sem=send_sem, recv_sem=recv_sem,
                                        device_id=target, device_id_type=MESH_IDS)


AG_CHUNKS = 8
AG_PLAN_CONV = {
    "mm_proj": dict(chunks=range(0, 5)),
    "conv_fwd": dict(chunks=range(5, 6), landed=range(0, 5)),
    "mm_out": dict(chunks=range(6, 7), landed=range(5, 6)),
    "norm_mid": dict(chunks=range(7, 8), landed=range(6, 7), pass_now=range(7, 8)),
}
AG_PLAN_SB = {
    "mm_proj": dict(chunks=range(0, 4)),
    "sb_fwd": dict(chunks=range(4, 8), landed=range(0, 4)),
    "mm_out": dict(landed=range(4, 8)),
}
HALF_CHUNKS = 8


SWAP_CHUNKS = 4


def _other_chips(x, y):
    return [(_flip(x, k >> 1), _flip(y, k & 1)) for k in (1, 2, 3)]


def _gather_sems(n):
    return [pltpu.SemaphoreType.DMA((3, n * AG_CHUNKS)) for _ in range(4)]


def _gather_pieces(g, chunks):
    hr = g[0].shape[1] // 2
    cr = hr // AG_CHUNKS
    return hr, [(l * AG_CHUNKS + q, g[l], q * cr, cr) for l in range(len(g)) for q in chunks]


def _pass_on(g, fsend, frecv, chunks):
    x, y, c = lax.axis_index("x"), lax.axis_index("y"), lax.axis_index("c")
    hr, pieces = _gather_pieces(g, chunks)
    for k, (px, py) in enumerate(_other_chips(x, y)):
        for i, ref, r0, cr in pieces:
            landed = ref.at[2 * px + py, pl.ds(c * hr + r0, cr)]
            _remote(landed, landed, fsend.at[k, i], frecv.at[k, i], (x, y, 1 - c)).start()


def _gather_start(g, send, recv, fsend, frecv, chunks=range(AG_CHUNKS), landed=()):
    x, y, c = lax.axis_index("x"), lax.axis_index("y"), lax.axis_index("c")
    hr, pieces = _gather_pieces(g, chunks)
    for k, (px, py) in enumerate(_other_chips(x, y)):
        for i, ref, r0, cr in pieces:
            piece = ref.at[2 * x + y, pl.ds(c * hr + r0, cr)]
            _remote(piece, piece, send.at[k, i], recv.at[k, i], (px, py, c)).start()
    _pass_on(g, fsend, frecv, landed)


def _gather_finish(g, send, recv, fsend, frecv, chunks=range(AG_CHUNKS), landed=(), pass_now=None):
    x, y, c = lax.axis_index("x"), lax.axis_index("y"), lax.axis_index("c")
    sibling = (x, y, 1 - c)
    pass_now = chunks if pass_now is None else pass_now
    hr, pieces = _gather_pieces(g, chunks)
    chips = _other_chips(x, y)
    for k, (px, py) in enumerate(chips):
        for i, ref, r0, cr in pieces:
            arrived = ref.at[2 * px + py, pl.ds(c * hr + r0, cr)]
            _remote(arrived, arrived, send.at[k, i], recv.at[k, i], (px, py, c)).wait_recv()
    _pass_on(g, fsend, frecv, pass_now)
    _, passed = _gather_pieces(g, list(landed) + list(pass_now))
    for k, (px, py) in enumerate(chips):
        for i, ref, r0, cr in passed:
            theirs = ref.at[2 * px + py, pl.ds((1 - c) * hr + r0, cr)]
            _remote(theirs, theirs, fsend.at[k, i], frecv.at[k, i], sibling).wait_recv()
    for k, (px, py) in enumerate(chips):
        for i, ref, r0, cr in pieces:
            mine = ref.at[2 * x + y, pl.ds(c * hr + r0, cr)]
            _remote(mine, mine, send.at[k, i], recv.at[k, i], (px, py, c)).wait_send()
        for i, ref, r0, cr in passed:
            mine = ref.at[2 * px + py, pl.ds(c * hr + r0, cr)]
            _remote(mine, mine, fsend.at[k, i], frecv.at[k, i], sibling).wait_send()


def _taps_rider(cw):
    def copies(ins, outs, sems):
        x, y, c = lax.axis_index("x"), lax.axis_index("y"), lax.axis_index("c")
        return [(_remote(ins[0], outs[0].at[2 * x + y], sems[0].at[k], sems[1].at[k], (px, py, c)),
                 _remote(ins[0], outs[0].at[2 * px + py], sems[0].at[k], sems[1].at[k], (px, py, c)))
                for k, (px, py) in enumerate(_other_chips(x, y))]

    def start(ins, outs, sems):
        pltpu.make_async_copy(ins[0], outs[0].at[2 * lax.axis_index("x") + lax.axis_index("y")], sems[2]).start()
        for mine, _ in copies(ins, outs, sems):
            mine.start()

    def finish(ins, outs, sems):
        for _, theirs in copies(ins, outs, sems):
            theirs.wait_recv()
        for mine, _ in copies(ins, outs, sems):
            mine.wait_send()
        pltpu.make_async_copy(ins[0], outs[0].at[2 * lax.axis_index("x") + lax.axis_index("y")], sems[2]).wait()

    return _Rider(
        operands=[cw], out_shapes=[jax.ShapeDtypeStruct((N_CHIPS,) + cw.shape, cw.dtype)], aliases={},
        sems=[pltpu.SemaphoreType.DMA((3,)), pltpu.SemaphoreType.DMA((3,)), pltpu.SemaphoreType.DMA],
        start=start, finish=finish)


def _gather_rider(packs, chunks=(), landed=(), pass_now=()):
    return _Rider(
        operands=packs, out_shapes=[jax.ShapeDtypeStruct(p.shape, BF16) for p in packs],
        aliases={l: l for l in range(len(packs))}, sems=_gather_sems(len(packs)),
        start=lambda ins, outs, sems: _gather_start(outs, *sems, chunks, landed),
        finish=lambda ins, outs, sems: _gather_finish(outs, *sems, chunks, landed, pass_now))


def _join_riders(a, b):
    na, oa, sa = len(a.operands), len(a.out_shapes), len(a.sems)
    aliases = dict(a.aliases)
    aliases.update({na + i: oa + o for i, o in b.aliases.items()})
    return _Rider(
        operands=list(a.operands) + list(b.operands), out_shapes=list(a.out_shapes) + list(b.out_shapes),
        aliases=aliases, sems=list(a.sems) + list(b.sems),
        start=lambda ins, outs, sems: (a.start(ins[:na], outs[:oa], sems[:sa]),
                                       b.start(ins[na:], outs[oa:], sems[sa:])),
        finish=lambda ins, outs, sems: (a.finish(ins[:na], outs[:oa], sems[:sa]),
                                        b.finish(ins[na:], outs[oa:], sems[sa:])))


def _comm_call(name, rider):
    n_in, n_out = len(rider.operands), len(rider.out_shapes)

    def body(*refs):
        ins, outs, sems = refs[:n_in], refs[n_in:n_in + n_out], refs[n_in + n_out:]
        rider.start(ins, outs, sems)
        rider.finish(ins, outs, sems)

    return pl.pallas_call(
        body, name=name, in_specs=[HBM_SPEC] * n_in, out_specs=[HBM_SPEC] * n_out, out_shape=list(rider.out_shapes),
        scratch_shapes=list(rider.sems), input_output_aliases=dict(rider.aliases),
    )(*rider.operands)


def _small_rider(small):
    def peers():
        x, y, c = lax.axis_index("x"), lax.axis_index("y"), lax.axis_index("c")
        return 4 * x + 2 * y + c, [(_flip(x, r >> 2), _flip(y, (r >> 1) & 1), _flip(c, r & 1)) for r in range(1, N_DEV)]

    def start(ins, outs, sems):
        me, others = peers()
        pltpu.make_async_copy(ins[0], outs[0].at[me], sems[2]).start()
        for r, peer in enumerate(others):
            _remote(ins[0], outs[0].at[me], sems[0].at[r], sems[1].at[r], peer).start()

    def finish(ins, outs, sems):
        me, others = peers()
        for r, (tx, ty, tc) in enumerate(others):
            _remote(ins[0], outs[0].at[4 * tx + 2 * ty + tc], sems[0].at[r], sems[1].at[r], (tx, ty, tc)).wait_recv()
        for r, peer in enumerate(others):
            _remote(ins[0], outs[0].at[me], sems[0].at[r], sems[1].at[r], peer).wait_send()
        pltpu.make_async_copy(ins[0], outs[0].at[me], sems[2]).wait()

    return _Rider(
        operands=[small], out_shapes=[jax.ShapeDtypeStruct((N_DEV,) + small.shape, small.dtype)], aliases={},
        sems=[pltpu.SemaphoreType.DMA((N_DEV - 1,)), pltpu.SemaphoreType.DMA((N_DEV - 1,)), pltpu.SemaphoreType.DMA],
        start=start, finish=finish)


SMALL_ROWS = 16


def _pack_small(ln_parts, conv_parts, loss_rows):
    d = ln_parts[0].shape[1]
    n_ln, n_conv = len(ln_parts), len(conv_parts)

    def body(*refs):
        o_ref = refs[-1]
        rows = lax.broadcasted_iota(jnp.int32, (SMALL_ROWS, d), 0)
        acc = jnp.zeros((SMALL_ROWS, d), F32)
        for i in range(n_ln):
            acc = jnp.where(rows == i, refs[i][0:1, :], acc)
        for j in range(n_conv):
            for k in range(3):
                acc = jnp.where(rows == n_ln + 3 * j + k, refs[n_ln + j][k:k + 1, :], acc)
        acc = jnp.where(rows == n_ln + 3 * n_conv, refs[n_ln + n_conv][0:1, :], acc)
        o_ref[...] = acc

    n = n_ln + n_conv + 1
    return pl.pallas_call(
        body, name="pack_small", out_shape=jax.ShapeDtypeStruct((SMALL_ROWS, d), F32),
        in_specs=[pl.BlockSpec(memory_space=pltpu.VMEM)] * n, out_specs=pl.BlockSpec(memory_space=pltpu.VMEM),
    )(*ln_parts, *conv_parts, loss_rows)


def _send_sums_start(s_ref, got, send, recv, part=0, parts=1, span=1):
    x, y, c = lax.axis_index("x"), lax.axis_index("y"), lax.axis_index("c")
    nrows = s_ref.shape[1] // parts
    rows = pl.ds(part * nrows, span * nrows)
    for k, (px, py) in enumerate(_other_chips(x, y)):
        _remote(s_ref.at[2 * px + py, rows], got.at[k, rows], send.at[k], recv.at[k], (px, py, c)).start()


def _send_sums_finish(s_ref, got, send, recv, part=0, parts=1, span=1):
    x, y, c = lax.axis_index("x"), lax.axis_index("y"), lax.axis_index("c")
    nrows = s_ref.shape[1] // parts
    rows = pl.ds(part * nrows, span * nrows)
    for k, (px, py) in enumerate(_other_chips(x, y)):
        _remote(got.at[k, rows], got.at[k, rows], send.at[k], recv.at[k], (px, py, c)).wait_recv()
    for k, (px, py) in enumerate(_other_chips(x, y)):
        _remote(s_ref.at[2 * px + py, rows], got.at[k, rows], send.at[k], recv.at[k], (px, py, c)).wait_send()


def _send_sums_rider(sums, got=None, part=0, parts=1, span=1):
    _, hr, d = sums.shape
    return _Rider(
        operands=[sums] if got is None else [sums, got],
        out_shapes=[jax.ShapeDtypeStruct((N_CHIPS - 1, hr, d), BF16)], aliases={} if got is None else {1: 0},
        sems=[pltpu.SemaphoreType.DMA((3,)), pltpu.SemaphoreType.DMA((3,))],
        start=lambda ins, outs, sems: _send_sums_start(ins[0], outs[0], *sems, part, parts, span),
        finish=lambda ins, outs, sems: _send_sums_finish(ins[0], outs[0], *sems, part, parts, span))


def _swap_pieces(gp_ref, x_ref, c):
    hr = x_ref.shape[1]
    cr = hr // SWAP_CHUNKS
    return [(a * SWAP_CHUNKS + q, gp_ref.at[a, pl.ds((1 - c) * hr + q * cr, cr)], x_ref.at[a, pl.ds(q * cr, cr)])
            for a in range(N_CHIPS) for q in range(SWAP_CHUNKS)]


def _swap_rider(gp):
    _, p_rows, d = gp.shape

    def start(ins, outs, sems):
        x, y, c = lax.axis_index("x"), lax.axis_index("y"), lax.axis_index("c")
        for i, src, dst in _swap_pieces(ins[0], outs[0], c):
            _remote(src, dst, sems[0].at[i], sems[1].at[i], (x, y, 1 - c)).start()

    def finish(ins, outs, sems):
        x, y, c = lax.axis_index("x"), lax.axis_index("y"), lax.axis_index("c")
        pieces = _swap_pieces(ins[0], outs[0], c)
        for i, src, dst in pieces:
            _remote(dst, dst, sems[0].at[i], sems[1].at[i], (x, y, 1 - c)).wait_recv()
        for i, src, dst in pieces:
            _remote(src, dst, sems[0].at[i], sems[1].at[i], (x, y, 1 - c)).wait_send()

    nsem = N_CHIPS * SWAP_CHUNKS
    return _Rider(
        operands=[gp], out_shapes=[jax.ShapeDtypeStruct((N_CHIPS, p_rows // 2, d), BF16)], aliases={},
        sems=[pltpu.SemaphoreType.DMA((nsem,)), pltpu.SemaphoreType.DMA((nsem,))], start=start, finish=finish)


def _presum(gp, theirs, core):
    _, hr, d = theirs.shape
    tr = _row_tile(hr, 640)
    steps = hr // tr

    def body(core_ref, mine_ref, theirs_ref, o_ref):
        o_ref[...] = (mine_ref[...].astype(F32) + theirs_ref[...].astype(F32)).astype(BF16)

    grid_spec = pltpu.PrefetchScalarGridSpec(
        num_scalar_prefetch=1, grid=(N_CHIPS, steps),
        in_specs=[pl.BlockSpec((None, tr, d), lambda a, i, cr: (a, cr[0] * steps + i, 0)),
                  pl.BlockSpec((None, tr, d), lambda a, i, cr: (a, i, 0))],
        out_specs=pl.BlockSpec((None, tr, d), lambda a, i, cr: (a, i, 0)))
    return pl.pallas_call(
        body, name="presum", grid_spec=grid_spec,
        out_shape=jax.ShapeDtypeStruct((N_CHIPS, hr, d), BF16), compiler_params=_params(2),
    )(core, gp, theirs)


def _row_tile(rows, cap=128):
    if rows <= cap:
        return rows
    return next(tr for tr in range(cap, 0, -16) if rows % tr == 0)


def _sum_sources(name, parts):
    nsrc, rows, cols = parts.shape
    tr = _row_tile(rows)

    def body(p_ref, o_ref):
        total = p_ref[0].astype(F32)
        for s in range(1, nsrc):
            total = total + p_ref[s].astype(F32)
        o_ref[...] = total

    return pl.pallas_call(
        body, name=name, grid=(rows // tr,),
        in_specs=[pl.BlockSpec((nsrc, tr, cols), lambda i: (0, i, 0))],
        out_specs=pl.BlockSpec((tr, cols), lambda i: (i, 0)),
        out_shape=jax.ShapeDtypeStruct((rows, cols), F32), compiler_params=_params(1),
    )(parts)


def _sum_grad_half(got, sums, place):
    nsrc, hr, d = got.shape
    tr = _row_tile(hr, 256)

    def body(place_ref, got_ref, own_ref, o_ref):
        total = own_ref[...].astype(F32)
        for s in range(nsrc):
            total = total + got_ref[s].astype(F32)
        o_ref[...] = total

    grid_spec = pltpu.PrefetchScalarGridSpec(
        num_scalar_prefetch=1, grid=(hr // tr,),
        in_specs=[pl.BlockSpec((nsrc, tr, d), lambda i, pc: (0, i, 0)),
                  pl.BlockSpec((None, tr, d), lambda i, pc: (pc[0], i, 0))],
        out_specs=pl.BlockSpec((None, tr, d), lambda i, pc: (pc[1], i, 0)))
    return pl.pallas_call(
        body, name="sum_grad_half", grid_spec=grid_spec,
        out_shape=jax.ShapeDtypeStruct((2, hr, d), F32), compiler_params=_params(1),
    )(place, got, sums)


def _halves_rider(full):
    _, hr, d = full.shape
    cr = hr // HALF_CHUNKS

    def pieces(ref, half):
        return [ref.at[half, pl.ds(q * cr, cr)] for q in range(HALF_CHUNKS)]

    def start(ins, outs, sems):
        x, y, c = lax.axis_index("x"), lax.axis_index("y"), lax.axis_index("c")
        for i, piece in enumerate(pieces(outs[0], c)):
            _remote(piece, piece, sems[0].at[i], sems[1].at[i], (x, y, 1 - c)).start()

    def finish(ins, outs, sems):
        x, y, c = lax.axis_index("x"), lax.axis_index("y"), lax.axis_index("c")
        for i, piece in enumerate(pieces(outs[0], 1 - c)):
            _remote(piece, piece, sems[0].at[i], sems[1].at[i], (x, y, 1 - c)).wait_recv()
        for i, piece in enumerate(pieces(outs[0], c)):
            _remote(piece, piece, sems[0].at[i], sems[1].at[i], (x, y, 1 - c)).wait_send()

    return _Rider(
        operands=[full], out_shapes=[jax.ShapeDtypeStruct(full.shape, F32)], aliases={0: 0},
        sems=[pltpu.SemaphoreType.DMA((HALF_CHUNKS,)), pltpu.SemaphoreType.DMA((HALF_CHUNKS,))],
        start=start, finish=finish)


def _adamw(name, w, m, v, g, g_row0=0):
    rows, cols = w.shape
    tr = _row_tile(rows, 256)
    off = g_row0 // tr

    def body(w_ref, m_ref, v_ref, g_ref, go_ref, d_ref, mo_ref, vo_ref):
        grad = g_ref[...]
        m_new = ADAM_B1 * m_ref[...] + (1.0 - ADAM_B1) * grad
        v_new = ADAM_B2 * v_ref[...] + (1.0 - ADAM_B2) * (grad * grad)
        m_hat = m_new / (1.0 - ADAM_B1 ** ADAM_STEP)
        v_hat = v_new / (1.0 - ADAM_B2 ** ADAM_STEP)
        go_ref[...] = grad
        d_ref[...] = -ADAM_LR * (m_hat / (jnp.sqrt(v_hat) + ADAM_EPS) + ADAM_WD * w_ref[...])
        mo_ref[...] = m_new
        vo_ref[...] = v_new

    blk = pl.BlockSpec((tr, cols), lambda i: (i, 0))
    return _compute_call(
        body, name, grid=(rows // tr,),
        in_specs=[blk, blk, blk, pl.BlockSpec((tr, cols), lambda i: (i + off, 0))], operands=[w, m, v, g],
        out_specs=[blk, blk, blk, blk], out_shape=[jax.ShapeDtypeStruct((rows, cols), F32)] * 4)


def _pad_rows8(a):
    return jnp.concatenate([a, jnp.zeros((8 - a.shape[0],) + a.shape[1:], a.dtype)], axis=0)


def kernel(x, ln_pre_0, conv_w_in_0, conv_w_0, conv_w_out_0, ln_post_0, ln_pre_1, sb_w_in_1, sb_w_out_1, ln_post_1, ln_pre_2, conv_w_in_2, conv_w_2, conv_w_out_2, ln_post_2, ln_pre_3, sb_w_in_3, sb_w_out_3, ln_post_3, loss_target, m_ln_pre_0, m_conv_w_in_0, m_conv_w_0, m_conv_w_out_0, m_ln_post_0, m_ln_pre_1, m_sb_w_in_1, m_sb_w_out_1, m_ln_post_1, m_ln_pre_2, m_conv_w_in_2, m_conv_w_2, m_conv_w_out_2, m_ln_post_2, m_ln_pre_3, m_sb_w_in_3, m_sb_w_out_3, m_ln_post_3, v_ln_pre_0, v_conv_w_in_0, v_conv_w_0, v_conv_w_out_0, v_ln_post_0, v_ln_pre_1, v_sb_w_in_1, v_sb_w_out_1, v_ln_post_1, v_ln_pre_2, v_conv_w_in_2, v_conv_w_2, v_conv_w_out_2, v_ln_post_2, v_ln_pre_3, v_sb_w_in_3, v_sb_w_out_3, v_ln_post_3):
    t, d = x.shape[1], x.shape[2]
    dq = d // N_CHIPS
    xs = x.reshape(t, d)
    target = loss_target.reshape(t, d)
    w_in = [conv_w_in_0, sb_w_in_1, conv_w_in_2, sb_w_in_3]
    w_out = [conv_w_out_0, sb_w_out_1, conv_w_out_2, sb_w_out_3]
    m_in = [m_conv_w_in_0, m_sb_w_in_1, m_conv_w_in_2, m_sb_w_in_3]
    m_out = [m_conv_w_out_0, m_sb_w_out_1, m_conv_w_out_2, m_sb_w_out_3]
    v_in = [v_conv_w_in_0, v_sb_w_in_1, v_conv_w_in_2, v_sb_w_in_3]
    v_out = [v_conv_w_out_0, v_sb_w_out_1, v_conv_w_out_2, v_sb_w_out_3]
    ln_pre = [ln_pre_0, ln_pre_1, ln_pre_2, ln_pre_3]
    ln_post = [ln_post_0, ln_post_1, ln_post_2, ln_post_3]
    conv_w = [conv_w_0, conv_w_2]
    m_conv = [m_conv_w_0, m_conv_w_2]
    v_conv = [v_conv_w_0, v_conv_w_2]
    chip = 2 * lax.axis_index("x") + lax.axis_index("y")
    chip_arr = jnp.reshape(chip, (1,)).astype(jnp.int32)
    place = jnp.stack([chip, lax.axis_index("c")]).astype(jnp.int32)
    core_arr = jnp.reshape(lax.axis_index("c"), (1,)).astype(jnp.int32)

    packs = [_pack_weights(w_in[l], w_out[l], chip_arr) for l in range(N_LAYERS)]
    cw_local = jnp.concatenate([_pad_rows8(conv_w[0]), _pad_rows8(conv_w[1])], axis=0)
    gathered = list(packs)
    every = range(AG_CHUNKS)
    u, gathered[0], cw_all = _norm_first(
        xs, ln_pre[0], rider=_join_riders(_gather_rider(packs[:1], chunks=every, pass_now=every), _taps_rider(cw_local)))
    cw_full = jnp.transpose(cw_all, (1, 0, 2)).reshape(16, d)
    conv_taps = {0: cw_full[0:8], 2: cw_full[8:16]}

    h_in, us, projs, gateds, ms, sb_saved = [], [], [], [], [], {}
    h = xs
    for l in range(N_LAYERS):
        h_in.append(h)
        us.append(u)
        nxt = l + 1
        plan = {} if nxt == N_LAYERS else (AG_PLAN_CONV if l % 2 == 0 else AG_PLAN_SB)

        def rider_for(name):
            return _gather_rider(gathered[nxt:nxt + 1], **plan[name]) if name in plan else None

        def take(results, name):
            if name not in plan:
                return results
            gathered[nxt] = results[-1]
            return results[:-1]

        proj, = take(_mm_proj(u, gathered[l], rider=rider_for("mm_proj")), "mm_proj")
        if l % 2 == 0:
            gated, = take(_conv_fwd(proj, conv_taps[l], rider=rider_for("conv_fwd")), "conv_fwd")
        else:
            gated, o, car = take(_sb_fwd(proj, rider=rider_for("sb_fwd")), "sb_fwd")
            sb_saved[l] = (o, car)
        m, = take(_mm_out(gated, gathered[l], rider=rider_for("mm_out")), "mm_out")
        projs.append(proj)
        gateds.append(gated)
        ms.append(m)
        if l < N_LAYERS - 1:
            h, u = take(_norm_mid(h, m, ln_post[l], ln_pre[nxt], rider=rider_for("norm_mid")), "norm_mid")
    dh, dm, dg_post_last, loss_part = _norm_last(h, ms[-1], ln_post[-1], target)

    dg_pre = [None] * N_LAYERS
    dg_post = [None] * N_LAYERS
    dg_post[N_LAYERS - 1] = dg_post_last
    dconv = {}
    sums = [None] * N_LAYERS
    got = [None] * N_LAYERS
    fulls = [None] * N_LAYERS

    def summed(layer):
        return _halves_rider(_sum_grad_half(got[layer], sums[layer], place))

    for l in reversed(range(N_LAYERS)):
        above = l + 1 if l + 1 < N_LAYERS else None
        if l == 1:
            dgated, fulls[3] = _mm_dgated(dm, gathered[l], rider=summed(3))
        else:
            dgated, = _mm_dgated(dm, gathered[l])
        gp = _mm_dwout(gateds[l], dm)
        if l % 2 == 0:
            dproj, dconv[l] = _conv_bwd(projs[l], dgated, conv_taps[l])
        elif above is not None:
            o, car = sb_saved[l]
            dproj, got[above] = _sb_bwd(projs[l], dgated, o, car, rider=_send_sums_rider(sums[above]))
        else:
            o, car = sb_saved[l]
            dproj, = _sb_bwd(projs[l], dgated, o, car)
        if l == 0:
            gp, got[1] = _mm_dwin(us[0], dproj, gp, rider=_send_sums_rider(sums[1], None, 0, 4, 3))
            theirs, got[1] = _comm_call(
                "swap_last", _join_riders(_swap_rider(gp), _send_sums_rider(sums[1], got[1], 3, 4, 1)))
            sums[0] = _presum(gp, theirs, core_arr)
            du, got[0], fulls[1] = _mm_du(
                dproj, gathered[0], rider=_join_riders(_send_sums_rider(sums[0], None, 0, 4, 3), summed(1)))
        elif l % 2 == 0:
            gp, got[above] = _mm_dwin(us[l], dproj, gp, rider=_send_sums_rider(sums[above], None, 0, 2))
            du, theirs, got[above] = _mm_du(
                dproj, gathered[l], rider=_join_riders(_swap_rider(gp), _send_sums_rider(sums[above], got[above], 1, 2)))
            sums[l] = _presum(gp, theirs, core_arr)
        elif above is not None:
            gp, fulls[above] = _mm_dwin(us[l], dproj, gp, rider=summed(above))
            du, theirs = _mm_du(dproj, gathered[l], rider=_swap_rider(gp))
            sums[l] = _presum(gp, theirs, core_arr)
        else:
            gp, = _mm_dwin(us[l], dproj, gp)
            du, theirs = _mm_du(dproj, gathered[l], rider=_swap_rider(gp))
            sums[l] = _presum(gp, theirs, core_arr)
        if l > 0:
            dh, dm, dg_pre[l], dg_post[l - 1] = _norm_bwd_mid(dh, du, h_in[l], ln_pre[l], ms[l - 1], ln_post[l - 1])
    grad_x, dg_pre[0] = _norm_bwd_first(dh, du, h_in[0], ln_pre[0])

    loss_rows = jnp.pad(loss_part, ((0, 0), (0, d - loss_part.shape[1])))
    small = _pack_small(dg_pre + dg_post, [dconv[0], dconv[2]], loss_rows)
    got[0], small_all = _comm_call(
        "exchange_last", _join_riders(_send_sums_rider(sums[0], got[0], 3, 4, 1), _small_rider(small)))
    small_sum = _sum_sources("sum_small", small_all)

    fulls[0], = _comm_call("exchange_halves", summed(0))
    fulls = [f.reshape(d + dq, d) for f in fulls]
    res_in = [_adamw("adamw_w_in", w_in[l], m_in[l], v_in[l], fulls[l], 0) for l in range(N_LAYERS)]
    res_out = [_adamw("adamw_w_out", w_out[l], m_out[l], v_out[l], fulls[l], d) for l in range(N_LAYERS)]
    ln_all = ln_pre + ln_post
    ln_m = [m_ln_pre_0, m_ln_pre_1, m_ln_pre_2, m_ln_pre_3, m_ln_post_0, m_ln_post_1, m_ln_post_2, m_ln_post_3]
    ln_v = [v_ln_pre_0, v_ln_pre_1, v_ln_pre_2, v_ln_pre_3, v_ln_post_0, v_ln_post_1, v_ln_post_2, v_ln_post_3]
    res_ln = _adamw("adamw_ln", jnp.stack(ln_all), jnp.stack(ln_m), jnp.stack(ln_v), small_sum[0:2 * N_LAYERS])
    conv_g = [_pad_rows8(lax.dynamic_slice(small_sum, (2 * N_LAYERS + 3 * i, chip * dq), (3, dq))) for i in range(2)]
    stack8 = lambda a, b: jnp.concatenate([_pad_rows8(a), _pad_rows8(b)], axis=0)
    res_conv = _adamw("adamw_conv", stack8(*conv_w), stack8(*m_conv), stack8(*v_conv), jnp.concatenate(conv_g, axis=0))

    def leaf(kind, l, which):
        if kind == "ln_pre":
            return res_ln[which][l]
        if kind == "ln_post":
            return res_ln[which][N_LAYERS + l]
        if kind == "w_in":
            return res_in[l][which]
        if kind == "w_out":
            return res_out[l][which]
        return res_conv[which][8 * (l // 2):8 * (l // 2) + 3]

    order = []
    for l in range(N_LAYERS):
        order.append(("ln_pre", l))
        order.append(("w_in", l))
        if l % 2 == 0:
            order.append(("conv", l))
        order.append(("w_out", l))
        order.append(("ln_post", l))
    loss = small_sum[2 * N_LAYERS + 3 * 2, 0]
    outs = [loss, grad_x.reshape(1, t, d)]
    for which in range(4):
        outs.extend(leaf(kind, l, which) for kind, l in order)
    return tuple(outs)
```

```python
import functools
import math
from typing import Any, Callable, Mapping, NamedTuple, Sequence

import jax
import jax.numpy as jnp
from jax import lax
from jax.experimental import pallas as pl
from jax.experimental.pallas import tpu as pltpu

F32 = jnp.float32
BF16 = jnp.bfloat16

N_CHIPS = 4
N_DEV = 8
N_LAYERS = 4
HEAD_DIM = 128
RMS_EPS = 1e-6
ADAM_LR = 0.001
ADAM_B1 = 0.9
ADAM_B2 = 0.999
ADAM_EPS = 1e-08
ADAM_WD = 0.01
ADAM_STEP = 10

VMEM_LIMIT = 56 * 1024 * 1024
MESH_IDS = pl.DeviceIdType.MESH
HBM_SPEC = pl.BlockSpec(memory_space=pltpu.HBM)

NN = (((1,), (0,)), ((), ()))
NT = (((1,), (1,)), ((), ()))
TN = (((0,), (0,)), ((), ()))


def _params(n_axes):
    return pltpu.CompilerParams(dimension_semantics=("arbitrary",) * n_axes, vmem_limit_bytes=VMEM_LIMIT)


def _dot(a, b, dims):
    return lax.dot_general(a, b, dims, preferred_element_type=F32)


def _sigmoid(z):
    return 1.0 / (1.0 + jnp.exp(-z))


class _Rider(NamedTuple):
    operands: Sequence[Any]
    out_shapes: Sequence[Any]
    aliases: Mapping[int, int]
    sems: Sequence[Any]
    start: Callable
    finish: Callable


def _compute_call(body, name, *, grid, in_specs, operands, out_specs, out_shape, scratch=(), aliases=None, rider=None):
    in_specs, operands = list(in_specs), list(operands)
    out_specs, out_shape, scratch = list(out_specs), list(out_shape), list(scratch)
    aliases = dict(aliases or {})
    n_in, n_out, n_scratch = len(operands), len(out_shape), len(scratch)
    hosted = body
    if rider is not None:
        r_in, r_out = len(rider.operands), len(rider.out_shapes)
        aliases.update({n_in + i: n_out + o for i, o in rider.aliases.items()})

        def hosted(*refs):
            ins, refs = refs[:n_in], refs[n_in:]
            rider_ins, refs = refs[:r_in], refs[r_in:]
            outs, refs = refs[:n_out], refs[n_out:]
            rider_outs, refs = refs[:r_out], refs[r_out:]
            own_scratch, rider_sems = refs[:n_scratch], refs[n_scratch:]
            ids = [pl.program_id(axis) for axis in range(len(grid))]
            first = functools.reduce(jnp.logical_and, [i == 0 for i in ids])
            last = functools.reduce(jnp.logical_and, [i == g - 1 for i, g in zip(ids, grid)])

            @pl.when(first)
            def _():
                rider.start(rider_ins, rider_outs, rider_sems)

            body(*ins, *outs, *own_scratch)

            @pl.when(last)
            def _():
                rider.finish(rider_ins, rider_outs, rider_sems)

        in_specs += [HBM_SPEC] * r_in
        operands += list(rider.operands)
        out_specs += [HBM_SPEC] * r_out
        out_shape += list(rider.out_shapes)
        scratch += list(rider.sems)
    return pl.pallas_call(
        hosted, name=name, grid=grid, in_specs=in_specs, out_specs=out_specs, out_shape=out_shape,
        scratch_shapes=scratch, input_output_aliases=aliases, compiler_params=_params(len(grid)),
    )(*operands)


def _matmul(name, a, b, *, grid, a_spec, b_spec, o_spec, out_shape, dims, reduce_axis=None, acc_shape=None,
            alias_out=None, rider=None):
    out_dtype = out_shape.dtype
    direct = reduce_axis is not None and out_dtype == F32
    n_red = grid[reduce_axis] if reduce_axis is not None else 1

    def body(*refs):
        if alias_out is not None:
            refs = refs[1:]
        a_ref, b_ref, o_ref = refs[:3]
        if reduce_axis is None:
            o_ref[...] = _dot(a_ref[...], b_ref[...], dims).astype(out_dtype)
            return
        acc_ref = o_ref if direct else refs[3]
        k = pl.program_id(reduce_axis)

        @pl.when(k == 0)
        def _():
            acc_ref[...] = jnp.zeros_like(acc_ref)

        acc_ref[...] += _dot(a_ref[...], b_ref[...], dims)

        if not direct:
            @pl.when(k == n_red - 1)
            def _():
                o_ref[...] = acc_ref[...].astype(out_dtype)

    scratch = []
    if reduce_axis is not None and not direct:
        scratch = [pltpu.VMEM(acc_shape, F32)]
    in_specs = [a_spec, b_spec]
    operands = [a, b]
    aliases = {}
    if alias_out is not None:
        in_specs = [HBM_SPEC] + in_specs
        operands = [alias_out] + operands
        aliases = {0: 0}
    return _compute_call(body, name, grid=grid, in_specs=in_specs, operands=operands, out_specs=[o_spec],
                         out_shape=[out_shape], scratch=scratch, aliases=aliases, rider=rider)


def _mm_proj(u, g, rider=None):
    t, d = u.shape
    tm = min(512, t)
    return _matmul(
        "mm_proj", u, g, grid=(N_CHIPS, t // tm),
        a_spec=pl.BlockSpec((tm, d), lambda s, m: (m, 0)),
        b_spec=pl.BlockSpec((None, d, d), lambda s, m: (s, 0, 0)),
        o_spec=pl.BlockSpec((None, tm, d), lambda s, m: (s, m, 0)),
        out_shape=jax.ShapeDtypeStruct((N_CHIPS, t, d), BF16), dims=NN, rider=rider)


def _mm_out(gated, g, rider=None):
    t, d = gated.shape
    dq = d // N_CHIPS
    tm = min(512, t)

    def body(a_ref, b_ref, o_ref):
        acc = _dot(a_ref[:, 0:dq], b_ref[0], NN)
        for s in range(1, N_CHIPS):
            acc = acc + _dot(a_ref[:, s * dq:(s + 1) * dq], b_ref[s], NN)
        o_ref[...] = acc

    return _compute_call(
        body, "mm_out", grid=(t // tm,),
        in_specs=[pl.BlockSpec((tm, d), lambda m: (m, 0)), pl.BlockSpec((N_CHIPS, dq, d), lambda m: (0, N_CHIPS, 0))],
        operands=[gated, g], out_specs=[pl.BlockSpec((tm, d), lambda m: (m, 0))],
        out_shape=[jax.ShapeDtypeStruct((t, d), F32)], rider=rider)


def _mm_dgated(dm, g, rider=None):
    t, d = dm.shape
    dq = d // N_CHIPS
    tm = min(512, t)

    def body(a_ref, b_ref, o_ref):
        a = a_ref[...]
        for s in range(N_CHIPS):
            o_ref[:, s * dq:(s + 1) * dq] = _dot(a, b_ref[s], NT).astype(BF16)

    return _compute_call(
        body, "mm_dgated", grid=(t // tm,),
        in_specs=[pl.BlockSpec((tm, d), lambda m: (m, 0)), pl.BlockSpec((N_CHIPS, dq, d), lambda m: (0, N_CHIPS, 0))],
        operands=[dm, g], out_specs=[pl.BlockSpec((tm, d), lambda m: (m, 0))],
        out_shape=[jax.ShapeDtypeStruct((t, d), BF16)], rider=rider)


def _mm_dwout(gated, dm, rider=None):
    t, d = gated.shape
    dq = d // N_CHIPS
    tk = min(1024, t)
    return _matmul(
        "mm_dwout", gated, dm, grid=(N_CHIPS, t // tk),
        a_spec=pl.BlockSpec((tk, dq), lambda s, k: (k, s)),
        b_spec=pl.BlockSpec((tk, d), lambda s, k: (k, 0)),
        o_spec=pl.BlockSpec((None, dq, d), lambda s, k: (s, N_CHIPS, 0)),
        out_shape=jax.ShapeDtypeStruct((N_CHIPS, d + dq, d), BF16), dims=TN, reduce_axis=1, acc_shape=(dq, d),
        rider=rider)


def _mm_du(dproj, g, rider=None):
    _, t, d = dproj.shape
    tm = min(512, t)
    return _matmul(
        "mm_du", dproj, g, grid=(t // tm, N_CHIPS),
        a_spec=pl.BlockSpec((None, tm, d), lambda m, s: (s, m, 0)),
        b_spec=pl.BlockSpec((None, d, d), lambda m, s: (s, 0, 0)),
        o_spec=pl.BlockSpec((tm, d), lambda m, s: (m, 0)),
        out_shape=jax.ShapeDtypeStruct((t, d), F32), dims=NT, reduce_axis=1, rider=rider)


def _mm_dwin(u, dproj, gp, rider=None):
    t, d = u.shape
    tmo = min(1024, d)
    tk = min(1024, t)
    return _matmul(
        "mm_dwin", u, dproj, grid=(N_CHIPS, d // tmo, t // tk),
        a_spec=pl.BlockSpec((tk, tmo), lambda s, mo, k: (k, mo)),
        b_spec=pl.BlockSpec((None, tk, d), lambda s, mo, k: (s, k, 0)),
        o_spec=pl.BlockSpec((None, tmo, d), lambda s, mo, k: (s, mo, 0)),
        out_shape=jax.ShapeDtypeStruct(gp.shape, BF16), dims=TN, reduce_axis=2, acc_shape=(tmo, d), alias_out=gp,
        rider=rider)


def _rms(v):
    r = lax.rsqrt(jnp.mean(v * v, axis=-1, keepdims=True) + RMS_EPS)
    return v * r, r


def _rms_bwd(dout, n, r, gain):
    dn = dout * gain
    return r * (dn - n * jnp.mean(dn * n, axis=-1, keepdims=True))


def _fold8(v):
    return jnp.sum(v.reshape(v.shape[0] // 8, 8, v.shape[1]), axis=0)


def _row0(total):
    rows = lax.broadcasted_iota(jnp.int32, total.shape, 0)
    return jnp.where(rows == 0, jnp.sum(total, axis=0, keepdims=True), 0.0)


def _norm_tile(t):
    return min(256, t)


def _norm_first(x, g_pre, rider=None):
    t, d = x.shape
    tr = _norm_tile(t)

    def body(x_ref, g_ref, u_ref):
        n, _ = _rms(x_ref[...])
        u_ref[...] = (n * g_ref[...]).astype(BF16)

    row = pl.BlockSpec((tr, d), lambda i: (i, 0))
    vec = pl.BlockSpec((1, d), lambda i: (0, 0))
    return _compute_call(
        body, "norm_first", grid=(t // tr,), in_specs=[row, vec], operands=[x, g_pre.reshape(1, d)],
        out_specs=[row], out_shape=[jax.ShapeDtypeStruct((t, d), BF16)], rider=rider)


def _norm_mid(h, m, g_post, g_pre_next, rider=None):
    t, d = h.shape
    tr = _norm_tile(t)

    def body(h_ref, m_ref, gp_ref, gn_ref, hn_ref, u_ref):
        n, _ = _rms(m_ref[...])
        hn = h_ref[...] + n * gp_ref[...]
        hn_ref[...] = hn
        n2, _ = _rms(hn)
        u_ref[...] = (n2 * gn_ref[...]).astype(BF16)

    row = pl.BlockSpec((tr, d), lambda i: (i, 0))
    vec = pl.BlockSpec((1, d), lambda i: (0, 0))
    return _compute_call(
        body, "norm_mid", grid=(t // tr,), in_specs=[row, row, vec, vec],
        operands=[h, m, g_post.reshape(1, d), g_pre_next.reshape(1, d)], out_specs=[row, row],
        out_shape=[jax.ShapeDtypeStruct((t, d), F32), jax.ShapeDtypeStruct((t, d), BF16)], rider=rider)


def _norm_last(h, m, g_post, target):
    t, d = h.shape
    tr = _norm_tile(t)
    nsteps = t // tr

    def body(h_ref, m_ref, gp_ref, tg_ref, dy_ref, dm_ref, dgp_ref, loss_ref, acc_g, acc_l):
        i = pl.program_id(0)

        @pl.when(i == 0)
        def _():
            acc_g[...] = jnp.zeros_like(acc_g)
            acc_l[...] = jnp.zeros_like(acc_l)

        gain = gp_ref[...]
        n, r = _rms(m_ref[...])
        err = h_ref[...] + n * gain - tg_ref[...]
        dy = err / d
        dy_ref[...] = dy
        dm_ref[...] = _rms_bwd(dy, n, r, gain).astype(BF16)
        acc_g[...] += _fold8(dy * n)
        acc_l[...] += _fold8(err * err)

        @pl.when(i == nsteps - 1)
        def _():
            dgp_ref[...] = _row0(acc_g[...])
            loss_ref[...] = jnp.zeros((8, 128), F32) + (0.5 / d) * jnp.sum(acc_l[...])

    row = pl.BlockSpec((tr, d), lambda i: (i, 0))
    vec = pl.BlockSpec((1, d), lambda i: (0, 0))
    acc = pl.BlockSpec((8, d), lambda i: (0, 0))
    return pl.pallas_call(
        body, name="norm_last", grid=(nsteps,), in_specs=[row, row, vec, row],
        out_specs=[row, row, acc, pl.BlockSpec((8, 128), lambda i: (0, 0))],
        out_shape=[jax.ShapeDtypeStruct((t, d), F32), jax.ShapeDtypeStruct((t, d), BF16),
                   jax.ShapeDtypeStruct((8, d), F32), jax.ShapeDtypeStruct((8, 128), F32)],
        scratch_shapes=[pltpu.VMEM((8, d), F32), pltpu.VMEM((8, d), F32)],
        compiler_params=_params(1),
    )(h, m, g_post.reshape(1, d), target)


def _norm_bwd_mid(dh, du, h_in, g_pre, m_prev, g_post_prev):
    t, d = dh.shape
    tr = _norm_tile(t)
    nsteps = t // tr

    def body(dh_ref, du_ref, h_ref, gpre_ref, m_ref, gpost_ref, dhn_ref, dm_ref, dgpre_ref, dgpost_ref, acc_a, acc_b):
        i = pl.program_id(0)

        @pl.when(i == 0)
        def _():
            acc_a[...] = jnp.zeros_like(acc_a)
            acc_b[...] = jnp.zeros_like(acc_b)

        du_t = du_ref[...]
        n, r = _rms(h_ref[...])
        dhn = dh_ref[...] + _rms_bwd(du_t, n, r, gpre_ref[...])
        dhn_ref[...] = dhn
        acc_a[...] += _fold8(du_t * n)
        n2, r2 = _rms(m_ref[...])
        dm_ref[...] = _rms_bwd(dhn, n2, r2, gpost_ref[...]).astype(BF16)
        acc_b[...] += _fold8(dhn * n2)

        @pl.when(i == nsteps - 1)
        def _():
            dgpre_ref[...] = _row0(acc_a[...])
            dgpost_ref[...] = _row0(acc_b[...])

    row = pl.BlockSpec((tr, d), lambda i: (i, 0))
    vec = pl.BlockSpec((1, d), lambda i: (0, 0))
    acc = pl.BlockSpec((8, d), lambda i: (0, 0))
    return pl.pallas_call(
        body, name="norm_bwd_mid", grid=(nsteps,), in_specs=[row, row, row, vec, row, vec],
        out_specs=[row, row, acc, acc],
        out_shape=[jax.ShapeDtypeStruct((t, d), F32), jax.ShapeDtypeStruct((t, d), BF16),
                   jax.ShapeDtypeStruct((8, d), F32), jax.ShapeDtypeStruct((8, d), F32)],
        scratch_shapes=[pltpu.VMEM((8, d), F32), pltpu.VMEM((8, d), F32)],
        compiler_params=_params(1),
    )(dh, du, h_in, g_pre.reshape(1, d), m_prev, g_post_prev.reshape(1, d))


def _norm_bwd_first(dh, du, x, g_pre):
    t, d = dh.shape
    tr = _norm_tile(t)
    nsteps = t // tr

    def body(dh_ref, du_ref, x_ref, gpre_ref, dx_ref, dgpre_ref, acc_a):
        i = pl.program_id(0)

        @pl.when(i == 0)
        def _():
            acc_a[...] = jnp.zeros_like(acc_a)

        du_t = du_ref[...]
        n, r = _rms(x_ref[...])
        dx_ref[...] = dh_ref[...] + _rms_bwd(du_t, n, r, gpre_ref[...])
        acc_a[...] += _fold8(du_t * n)

        @pl.when(i == nsteps - 1)
        def _():
            dgpre_ref[...] = _row0(acc_a[...])

    row = pl.BlockSpec((tr, d), lambda i: (i, 0))
    vec = pl.BlockSpec((1, d), lambda i: (0, 0))
    acc = pl.BlockSpec((8, d), lambda i: (0, 0))
    return _compute_call(
        body, "norm_bwd_first", grid=(nsteps,), in_specs=[row, row, row, vec],
        operands=[dh, du, x, g_pre.reshape(1, d)], out_specs=[row, acc],
        out_shape=[jax.ShapeDtypeStruct((t, d), F32), jax.ShapeDtypeStruct((8, d), F32)],
        scratch=[pltpu.VMEM((8, d), F32)])


CONV_TC = 128
CONV_HALO = 16


def _conv_chunk(t):
    return min(512, t)


def _shift_down(v, steps, fill):
    rows = lax.broadcasted_iota(jnp.int32, v.shape, 0)
    out = pltpu.roll(v, steps, axis=0)
    for k in range(steps):
        out = jnp.where(rows == k, fill[CONV_HALO - steps + k:CONV_HALO - steps + k + 1, :], out)
    return out


def _shift_up(v, steps, fill):
    nrows = v.shape[0]
    rows = lax.broadcasted_iota(jnp.int32, v.shape, 0)
    out = pltpu.roll(v, nrows - steps, axis=0)
    for k in range(steps):
        out = jnp.where(rows == nrows - steps + k, fill[k:k + 1, :], out)
    return out


def _conv_fwd(proj, cw, rider=None):
    _, t, d = proj.shape
    chunk = _conv_chunk(t)

    def body(p_ref, w_ref, o_ref):
        w = w_ref[...]
        w0, w1, w2 = w[0:1, :], w[1:2, :], w[2:3, :]
        for ci in range(t // chunk):
            t0 = ci * chunk
            rows = pl.ds(t0, chunk)
            b = p_ref[0, rows, :].astype(F32)
            cx = p_ref[1, rows, :].astype(F32) * p_ref[2, rows, :].astype(F32)
            z = p_ref[3, rows, :].astype(F32)
            if ci == 0:
                prev = jnp.zeros((CONV_HALO, CONV_TC), F32)
            else:
                halo = pl.ds(t0 - CONV_HALO, CONV_HALO)
                prev = p_ref[1, halo, :].astype(F32) * p_ref[2, halo, :].astype(F32)
            conv = w2 * cx + w1 * _shift_down(cx, 1, prev) + w0 * _shift_down(cx, 2, prev)
            o_ref[rows, :] = (z * _sigmoid(z) * b * conv).astype(BF16)

    return _compute_call(
        body, "conv_fwd", grid=(d // CONV_TC,),
        in_specs=[pl.BlockSpec((N_CHIPS, t, CONV_TC), lambda j: (0, 0, j)), pl.BlockSpec((8, CONV_TC), lambda j: (0, j))],
        operands=[proj, cw], out_specs=[pl.BlockSpec((t, CONV_TC), lambda j: (0, j))],
        out_shape=[jax.ShapeDtypeStruct((t, d), BF16)], rider=rider)


def _conv_bwd(proj, dgated, cw):
    _, t, d = proj.shape
    chunk = _conv_chunk(t)
    nchunks = t // chunk

    def body(p_ref, dg_ref, w_ref, dp_ref, dw_ref):
        w = w_ref[...]
        w0, w1, w2 = w[0:1, :], w[1:2, :], w[2:3, :]
        dw0 = jnp.zeros((1, CONV_TC), F32)
        dw1 = jnp.zeros((1, CONV_TC), F32)
        dw2 = jnp.zeros((1, CONV_TC), F32)
        for ci in range(nchunks):
            t0 = ci * chunk
            rows = pl.ds(t0, chunk)
            b = p_ref[0, rows, :].astype(F32)
            c = p_ref[1, rows, :].astype(F32)
            xt = p_ref[2, rows, :].astype(F32)
            z = p_ref[3, rows, :].astype(F32)
            dg = dg_ref[rows, :].astype(F32)
            cx = c * xt
            if ci == 0:
                prev = jnp.zeros((CONV_HALO, CONV_TC), F32)
            else:
                halo = pl.ds(t0 - CONV_HALO, CONV_HALO)
                prev = p_ref[1, halo, :].astype(F32) * p_ref[2, halo, :].astype(F32)
            cx1 = _shift_down(cx, 1, prev)
            cx2 = _shift_down(cx, 2, prev)
            conv = w2 * cx + w1 * cx1 + w0 * cx2
            sig = _sigmoid(z)
            dy = dg * (z * sig)
            dconv = dy * b
            if ci == nchunks - 1:
                nxt = jnp.zeros((CONV_HALO, CONV_TC), F32)
            else:
                halo = pl.ds(t0 + chunk, CONV_HALO)
                zn = p_ref[3, halo, :].astype(F32)
                nxt = dg_ref[halo, :].astype(F32) * (zn * _sigmoid(zn)) * p_ref[0, halo, :].astype(F32)
            dcx = w2 * dconv + w1 * _shift_up(dconv, 1, nxt) + w0 * _shift_up(dconv, 2, nxt)
            dp_ref[0, rows, :] = (dy * conv).astype(BF16)
            dp_ref[1, rows, :] = (dcx * xt).astype(BF16)
            dp_ref[2, rows, :] = (dcx * c).astype(BF16)
            dp_ref[3, rows, :] = (dg * (b * conv) * (sig * (1.0 + z * (1.0 - sig)))).astype(BF16)
            dw0 = dw0 + jnp.sum(dconv * cx2, axis=0, keepdims=True)
            dw1 = dw1 + jnp.sum(dconv * cx1, axis=0, keepdims=True)
            dw2 = dw2 + jnp.sum(dconv * cx, axis=0, keepdims=True)
        taps = lax.broadcasted_iota(jnp.int32, (8, CONV_TC), 0)
        dw_ref[...] = jnp.where(taps == 0, dw0, jnp.where(taps == 1, dw1, jnp.where(taps == 2, dw2, 0.0)))

    return pl.pallas_call(
        body, name="conv_bwd", grid=(d // CONV_TC,),
        in_specs=[pl.BlockSpec((N_CHIPS, t, CONV_TC), lambda j: (0, 0, j)),
                  pl.BlockSpec((t, CONV_TC), lambda j: (0, j)),
                  pl.BlockSpec((8, CONV_TC), lambda j: (0, j))],
        out_specs=[pl.BlockSpec((N_CHIPS, t, CONV_TC), lambda j: (0, 0, j)), pl.BlockSpec((8, CONV_TC), lambda j: (0, j))],
        out_shape=[jax.ShapeDtypeStruct((N_CHIPS, t, d), BF16), jax.ShapeDtypeStruct((8, d), F32)],
        compiler_params=_params(1),
    )(proj, dgated, cw)


SB_DEAD_TAIL = -105.0
SB_COUNT_LANE = HEAD_DIM - 1


def _sb_block(t):
    return min(256, t)


def _split_dot(v, tri):
    hi = v.astype(BF16)
    lo = (v - hi.astype(F32)).astype(BF16)
    return _dot(hi, tri, NN) + _dot(lo, tri, NN)


SB_HEADS_PER_STEP = 4


def _sb_terms(s, diagonal):
    sp = jnp.maximum(s, 0.0) + jnp.log(1.0 + jnp.exp(-jnp.abs(s)))
    if not diagonal:
        return -sp, s - sp, sp, None
    mask = lax.broadcasted_iota(jnp.int32, s.shape, 1) < lax.broadcasted_iota(jnp.int32, s.shape, 0)
    return jnp.where(mask, -sp, 0.0), s - sp, sp, mask


def _masked(mask, v):
    return v if mask is None else jnp.where(mask, v, 0.0)


def _sb_fwd(proj, rider=None):
    _, t, d = proj.shape
    heads = d // HEAD_DIM
    blk = _sb_block(t)
    nblk = t // blk
    scale = 1.0 / math.sqrt(HEAD_DIM)

    hps = SB_HEADS_PER_STEP
    width = hps * HEAD_DIM

    def body(q_ref, k_ref, v_ref, z_ref, gated_ref, o_ref, car_ref, tail_ref, acc_ref):
        i = pl.program_id(1)
        r_i = lax.broadcasted_iota(jnp.int32, (blk, blk), 0)
        c_i = lax.broadcasted_iota(jnp.int32, (blk, blk), 1)
        tri_after = (r_i > c_i).astype(BF16)
        lanes = lax.broadcasted_iota(jnp.int32, (blk, HEAD_DIM), 1)

        tail_ref[...] = jnp.zeros_like(tail_ref)
        acc_ref[...] = jnp.zeros_like(acc_ref)
        car_ref[...] = jnp.zeros_like(car_ref)

        def visit(j, diagonal):
            krows = pl.ds(pl.multiple_of(j * blk, blk), blk)
            hcols = [pl.ds(hh * HEAD_DIM, HEAD_DIM) for hh in range(hps)]
            logits = [_dot(q_ref[:, c], k_ref[krows, c], NT) for c in hcols]
            terms = [_sb_terms(s * scale, diagonal) for s in logits]
            within = [_split_dot(keep, tri_after) for keep, _, _, _ in terms]
            top = None
            for hh, (keep, log_beta, _, mask) in enumerate(terms):
                tail_b = tail_ref[hh]
                w = _masked(mask, jnp.exp(log_beta + tail_b[:, 0:1] + within[hh]))
                acc_ref[hh] += _dot(w.astype(BF16), v_ref[krows, hcols[hh]], NN)
                car_ref[hh] = jnp.where(lanes == j, tail_b, car_ref[hh])
                tail_new = tail_b + jnp.sum(keep, axis=1, keepdims=True)
                tail_ref[hh] = tail_new
                top = jnp.max(tail_new) if top is None else jnp.maximum(top, jnp.max(tail_new))
            return top > SB_DEAD_TAIL

        def more(state):
            jj, live = state
            return jnp.logical_and(jj <= i, live)

        def step(state):
            jj, _ = state
            return jj + 1, visit(i - jj, False)

        visited, _ = lax.while_loop(more, step, (jnp.int32(1), visit(i, True)))
        for hh in range(hps):
            cols = pl.ds(hh * HEAD_DIM, HEAD_DIM)
            car_ref[hh] = jnp.where(lanes == SB_COUNT_LANE, visited.astype(F32), car_ref[hh])
            z = z_ref[:, cols].astype(F32)
            acc = acc_ref[hh]
            o_ref[:, cols] = acc.astype(BF16)
            gated_ref[:, cols] = (z * _sigmoid(z) * acc).astype(BF16)

    qspec = lambda s: pl.BlockSpec((None, blk, width), lambda h, i: (s, i, h))
    kspec = lambda s: pl.BlockSpec((None, t, width), lambda h, i: (s, 0, h))
    ospec = pl.BlockSpec((blk, width), lambda h, i: (i, h))
    return _compute_call(
        body, "sb_fwd", grid=(heads // hps, nblk),
        in_specs=[qspec(0), kspec(1), kspec(2), qspec(3)], operands=[proj, proj, proj, proj],
        out_specs=[ospec, ospec, pl.BlockSpec((hps, blk, HEAD_DIM), lambda h, i: (h, i, 0))],
        out_shape=[jax.ShapeDtypeStruct((t, d), BF16), jax.ShapeDtypeStruct((t, d), BF16),
                   jax.ShapeDtypeStruct((heads, t, HEAD_DIM), F32)],
        scratch=[pltpu.VMEM((hps, blk, HEAD_DIM), F32), pltpu.VMEM((hps, blk, HEAD_DIM), F32)], rider=rider)


def _sb_bwd(proj, dgated, o, car, rider=None):
    _, t, d = proj.shape
    heads = d // HEAD_DIM
    blk = _sb_block(t)
    nblk = t // blk
    scale = 1.0 / math.sqrt(HEAD_DIM)

    hps = SB_HEADS_PER_STEP
    width = hps * HEAD_DIM

    def body(q_ref, k_ref, v_ref, z_ref, dg_ref, o_ref, car_ref, dp_ref, dk_acc, dv_acc, gsum_ref, dq_ref, do_ref):
        step_i = pl.program_id(1)
        i = nblk - 1 - step_i

        @pl.when(step_i == 0)
        def _():
            dk_acc[...] = jnp.zeros_like(dk_acc)
            dv_acc[...] = jnp.zeros_like(dv_acc)

        r_i = lax.broadcasted_iota(jnp.int32, (blk, blk), 0)
        c_i = lax.broadcasted_iota(jnp.int32, (blk, blk), 1)
        tri_after = (r_i > c_i).astype(BF16)
        tri_before = (r_i < c_i).astype(BF16)
        lanes = lax.broadcasted_iota(jnp.int32, (blk, HEAD_DIM), 1)

        gsum_ref[...] = jnp.zeros_like(gsum_ref)
        dq_ref[...] = jnp.zeros_like(dq_ref)
        for hh in range(hps):
            cols = pl.ds(hh * HEAD_DIM, HEAD_DIM)
            z = z_ref[:, cols].astype(F32)
            dg = dg_ref[:, cols].astype(F32)
            sig = _sigmoid(z)
            do_ref[hh] = (dg * (z * sig)).astype(BF16)
            dp_ref[3, :, cols] = (dg * o_ref[:, cols].astype(F32) * (sig * (1.0 + z * (1.0 - sig)))).astype(BF16)

        def visit(j, diagonal):
            krows = pl.ds(pl.multiple_of(j * blk, blk), blk)
            hcols = [pl.ds(hh * HEAD_DIM, HEAD_DIM) for hh in range(hps)]
            logits = [_dot(q_ref[:, c], k_ref[krows, c], NT) for c in hcols]
            dws = [_dot(do_ref[hh], v_ref[krows, c], NT) for hh, c in enumerate(hcols)]
            terms = [_sb_terms(s * scale, diagonal) for s in logits]
            within = [_split_dot(keep, tri_after) for keep, _, _, _ in terms]
            ws, gs = [], []
            for hh, (keep, log_beta, sp, mask) in enumerate(terms):
                tail = jnp.sum(jnp.where(lanes == j, car_ref[hh], 0.0), axis=1, keepdims=True)
                w = _masked(mask, jnp.exp(log_beta + tail + within[hh]))
                ws.append(w.astype(BF16))
                gs.append(w * dws[hh])
            g_within = [_split_dot(g, tri_before) for g in gs]
            for hh, (keep, log_beta, sp, mask) in enumerate(terms):
                c = hcols[hh]
                g_before = gsum_ref[hh]
                g_cum = g_before[:, 0:1] + g_within[hh]
                dl = (_masked(mask, gs[hh] - (gs[hh] + g_cum) * jnp.exp(log_beta)) * scale).astype(BF16)
                dq_ref[hh] += _dot(dl, k_ref[krows, c], NN)
                dk_acc[krows, c] += _dot(dl, q_ref[:, c], TN)
                dv_acc[krows, c] += _dot(ws[hh], do_ref[hh], TN)
                gsum_ref[hh] = g_before + jnp.sum(gs[hh], axis=1, keepdims=True)

        def step(j, carry):
            visit(j, False)
            return carry

        visited = jnp.max(jnp.where(lanes == SB_COUNT_LANE, car_ref[0], 0.0)).astype(jnp.int32)
        lax.fori_loop(i + 1 - visited, i, step, 0)
        visit(i, True)
        own = pl.ds(pl.multiple_of(i * blk, blk), blk)
        for hh in range(hps):
            cols = pl.ds(hh * HEAD_DIM, HEAD_DIM)
            dp_ref[0, :, cols] = dq_ref[hh].astype(BF16)
        dp_ref[1] = dk_acc[own, :].astype(BF16)
        dp_ref[2] = dv_acc[own, :].astype(BF16)

    qspec = lambda s: pl.BlockSpec((None, blk, width), lambda h, i: (s, nblk - 1 - i, h))
    kspec = lambda s: pl.BlockSpec((None, t, width), lambda h, i: (s, 0, h))
    tspec = pl.BlockSpec((blk, width), lambda h, i: (nblk - 1 - i, h))
    return _compute_call(
        body, "sb_bwd", grid=(heads // hps, nblk),
        in_specs=[qspec(0), kspec(1), kspec(2), qspec(3), tspec, tspec,
                  pl.BlockSpec((hps, blk, HEAD_DIM), lambda h, i: (h, nblk - 1 - i, 0))],
        operands=[proj, proj, proj, proj, dgated, o, car],
        out_specs=[pl.BlockSpec((N_CHIPS, blk, width), lambda h, i: (0, nblk - 1 - i, h))],
        out_shape=[jax.ShapeDtypeStruct((N_CHIPS, t, d), BF16)],
        scratch=[pltpu.VMEM((t, width), F32), pltpu.VMEM((t, width), F32),
                 pltpu.VMEM((hps, blk, HEAD_DIM), F32), pltpu.VMEM((hps, blk, HEAD_DIM), F32),
                 pltpu.VMEM((hps, blk, HEAD_DIM), BF16)], rider=rider)


def _pack_weights(w_in, w_out, chip):
    d = w_in.shape[0]
    rb = d // 8
    n_in = d // rb
    n_out = w_out.shape[0] // rb

    def body(chip_ref, wi_ref, wo_ref, o_ref):
        r = pl.program_id(0)

        @pl.when(r < n_in)
        def _():
            o_ref[...] = wi_ref[...].astype(BF16)

        @pl.when(r >= n_in)
        def _():
            o_ref[...] = wo_ref[...].astype(BF16)

    grid_spec = pltpu.PrefetchScalarGridSpec(
        num_scalar_prefetch=1, grid=(n_in + n_out,),
        in_specs=[pl.BlockSpec((rb, d), lambda r, me: (jnp.minimum(r, n_in - 1), 0)),
                  pl.BlockSpec((rb, d), lambda r, me: (jnp.maximum(r - n_in, 0), 0))],
        out_specs=pl.BlockSpec((None, rb, d), lambda r, me: (me[0], r, 0)))
    return pl.pallas_call(
        body, name="pack_weights", grid_spec=grid_spec,
        out_shape=jax.ShapeDtypeStruct((N_CHIPS, d + w_out.shape[0], d), BF16), compiler_params=_params(1),
    )(chip, w_in, w_out)


def _flip(v, bit):
    return 1 - v if bit else v


def _remote(src, dst, send_sem, recv_sem, target):
    return pltpu.make_async_remote_copy(src_ref=src, dst_ref=dst, send_sem=send_sem, recv_sem=recv_sem,
                                        device_id=target, device_id_type=MESH_IDS)


AG_CHUNKS = 8
AG_PLAN_CONV = {
    "mm_proj": dict(chunks=range(0, 5)),
    "conv_fwd": dict(chunks=range(5, 6), landed=range(0, 5)),
    "mm_out": dict(chunks=range(6, 7), landed=range(5, 6)),
    "norm_mid": dict(chunks=range(7, 8), landed=range(6, 7), pass_now=range(7, 8)),
}
AG_PLAN_SB = {
    "mm_proj": dict(chunks=range(0, 4)),
    "sb_fwd": dict(chunks=range(4, 8), landed=range(0, 4)),
    "mm_out": dict(landed=range(4, 8)),
}
HALF_CHUNKS = 8


SWAP_CHUNKS = 4


def _other_chips(x, y):
    return [(_flip(x, k >> 1), _flip(y, k & 1)) for k in (1, 2, 3)]


def _gather_sems(n):
    return [pltpu.SemaphoreType.DMA((3, n * AG_CHUNKS)) for _ in range(4)]


def _gather_pieces(g, chunks):
    hr = g[0].shape[1] // 2
    cr = hr // AG_CHUNKS
    return hr, [(l * AG_CHUNKS + q, g[l], q * cr, cr) for l in range(len(g)) for q in chunks]


def _pass_on(g, fsend, frecv, chunks):
    x, y, c = lax.axis_index("x"), lax.axis_index("y"), lax.axis_index("c")
    hr, pieces = _gather_pieces(g, chunks)
    for k, (px, py) in enumerate(_other_chips(x, y)):
        for i, ref, r0, cr in pieces:
            landed = ref.at[2 * px + py, pl.ds(c * hr + r0, cr)]
            _remote(landed, landed, fsend.at[k, i], frecv.at[k, i], (x, y, 1 - c)).start()


def _gather_start(g, send, recv, fsend, frecv, chunks=range(AG_CHUNKS), landed=()):
    x, y, c = lax.axis_index("x"), lax.axis_index("y"), lax.axis_index("c")
    hr, pieces = _gather_pieces(g, chunks)
    for k, (px, py) in enumerate(_other_chips(x, y)):
        for i, ref, r0, cr in pieces:
            piece = ref.at[2 * x + y, pl.ds(c * hr + r0, cr)]
            _remote(piece, piece, send.at[k, i], recv.at[k, i], (px, py, c)).start()
    _pass_on(g, fsend, frecv, landed)


def _gather_finish(g, send, recv, fsend, frecv, chunks=range(AG_CHUNKS), landed=(), pass_now=None):
    x, y, c = lax.axis_index("x"), lax.axis_index("y"), lax.axis_index("c")
    sibling = (x, y, 1 - c)
    pass_now = chunks if pass_now is None else pass_now
    hr, pieces = _gather_pieces(g, chunks)
    chips = _other_chips(x, y)
    for k, (px, py) in enumerate(chips):
        for i, ref, r0, cr in pieces:
            arrived = ref.at[2 * px + py, pl.ds(c * hr + r0, cr)]
            _remote(arrived, arrived, send.at[k, i], recv.at[k, i], (px, py, c)).wait_recv()
    _pass_on(g, fsend, frecv, pass_now)
    _, passed = _gather_pieces(g, list(landed) + list(pass_now))
    for k, (px, py) in enumerate(chips):
        for i, ref, r0, cr in passed:
            theirs = ref.at[2 * px + py, pl.ds((1 - c) * hr + r0, cr)]
            _remote(theirs, theirs, fsend.at[k, i], frecv.at[k, i], sibling).wait_recv()
    for k, (px, py) in enumerate(chips):
        for i, ref, r0, cr in pieces:
            mine = ref.at[2 * x + y, pl.ds(c * hr + r0, cr)]
            _remote(mine, mine, send.at[k, i], recv.at[k, i], (px, py, c)).wait_send()
        for i, ref, r0, cr in passed:
            mine = ref.at[2 * px + py, pl.ds(c * hr + r0, cr)]
            _remote(mine, mine, fsend.at[k, i], frecv.at[k, i], sibling).wait_send()


def _taps_rider(cw):
    def copies(ins, outs, sems):
        x, y, c = lax.axis_index("x"), lax.axis_index("y"), lax.axis_index("c")
        return [(_remote(ins[0], outs[0].at[2 * x + y], sems[0].at[k], sems[1].at[k], (px, py, c)),
                 _remote(ins[0], outs[0].at[2 * px + py], sems[0].at[k], sems[1].at[k], (px, py, c)))
                for k, (px, py) in enumerate(_other_chips(x, y))]

    def start(ins, outs, sems):
        pltpu.make_async_copy(ins[0], outs[0].at[2 * lax.axis_index("x") + lax.axis_index("y")], sems[2]).start()
        for mine, _ in copies(ins, outs, sems):
            mine.start()

    def finish(ins, outs, sems):
        for _, theirs in copies(ins, outs, sems):
            theirs.wait_recv()
        for mine, _ in copies(ins, outs, sems):
            mine.wait_send()
        pltpu.make_async_copy(ins[0], outs[0].at[2 * lax.axis_index("x") + lax.axis_index("y")], sems[2]).wait()

    return _Rider(
        operands=[cw], out_shapes=[jax.ShapeDtypeStruct((N_CHIPS,) + cw.shape, cw.dtype)], aliases={},
        sems=[pltpu.SemaphoreType.DMA((3,)), pltpu.SemaphoreType.DMA((3,)), pltpu.SemaphoreType.DMA],
        start=start, finish=finish)


def _gather_rider(packs, chunks=(), landed=(), pass_now=()):
    return _Rider(
        operands=packs, out_shapes=[jax.ShapeDtypeStruct(p.shape, BF16) for p in packs],
        aliases={l: l for l in range(len(packs))}, sems=_gather_sems(len(packs)),
        start=lambda ins, outs, sems: _gather_start(outs, *sems, chunks, landed),
        finish=lambda ins, outs, sems: _gather_finish(outs, *sems, chunks, landed, pass_now))


def _join_riders(a, b):
    na, oa, sa = len(a.operands), len(a.out_shapes), len(a.sems)
    aliases = dict(a.aliases)
    aliases.update({na + i: oa + o for i, o in b.aliases.items()})
    return _Rider(
        operands=list(a.operands) + list(b.operands), out_shapes=list(a.out_shapes) + list(b.out_shapes),
        aliases=aliases, sems=list(a.sems) + list(b.sems),
        start=lambda ins, outs, sems: (a.start(ins[:na], outs[:oa], sems[:sa]),
                                       b.start(ins[na:], outs[oa:], sems[sa:])),
        finish=lambda ins, outs, sems: (a.finish(ins[:na], outs[:oa], sems[:sa]),
                                        b.finish(ins[na:], outs[oa:], sems[sa:])))


def _comm_call(name, rider):
    n_in, n_out = len(rider.operands), len(rider.out_shapes)

    def body(*refs):
        ins, outs, sems = refs[:n_in], refs[n_in:n_in + n_out], refs[n_in + n_out:]
        rider.start(ins, outs, sems)
        rider.finish(ins, outs, sems)

    return pl.pallas_call(
        body, name=name, in_specs=[HBM_SPEC] * n_in, out_specs=[HBM_SPEC] * n_out, out_shape=list(rider.out_shapes),
        scratch_shapes=list(rider.sems), input_output_aliases=dict(rider.aliases),
    )(*rider.operands)


def _small_rider(small):
    def peers():
        x, y, c = lax.axis_index("x"), lax.axis_index("y"), lax.axis_index("c")
        return 4 * x + 2 * y + c, [(_flip(x, r >> 2), _flip(y, (r >> 1) & 1), _flip(c, r & 1)) for r in range(1, N_DEV)]

    def start(ins, outs, sems):
        me, others = peers()
        pltpu.make_async_copy(ins[0], outs[0].at[me], sems[2]).start()
        for r, peer in enumerate(others):
            _remote(ins[0], outs[0].at[me], sems[0].at[r], sems[1].at[r], peer).start()

    def finish(ins, outs, sems):
        me, others = peers()
        for r, (tx, ty, tc) in enumerate(others):
            _remote(ins[0], outs[0].at[4 * tx + 2 * ty + tc], sems[0].at[r], sems[1].at[r], (tx, ty, tc)).wait_recv()
        for r, peer in enumerate(others):
            _remote(ins[0], outs[0].at[me], sems[0].at[r], sems[1].at[r], peer).wait_send()
        pltpu.make_async_copy(ins[0], outs[0].at[me], sems[2]).wait()

    return _Rider(
        operands=[small], out_shapes=[jax.ShapeDtypeStruct((N_DEV,) + small.shape, small.dtype)], aliases={},
        sems=[pltpu.SemaphoreType.DMA((N_DEV - 1,)), pltpu.SemaphoreType.DMA((N_DEV - 1,)), pltpu.SemaphoreType.DMA],
        start=start, finish=finish)


SMALL_ROWS = 16


def _pack_small(ln_parts, conv_parts, loss_rows):
    d = ln_parts[0].shape[1]
    n_ln, n_conv = len(ln_parts), len(conv_parts)

    def body(*refs):
        o_ref = refs[-1]
        rows = lax.broadcasted_iota(jnp.int32, (SMALL_ROWS, d), 0)
        acc = jnp.zeros((SMALL_ROWS, d), F32)
        for i in range(n_ln):
            acc = jnp.where(rows == i, refs[i][0:1, :], acc)
        for j in range(n_conv):
            for k in range(3):
                acc = jnp.where(rows == n_ln + 3 * j + k, refs[n_ln + j][k:k + 1, :], acc)
        acc = jnp.where(rows == n_ln + 3 * n_conv, refs[n_ln + n_conv][0:1, :], acc)
        o_ref[...] = acc

    n = n_ln + n_conv + 1
    return pl.pallas_call(
        body, name="pack_small", out_shape=jax.ShapeDtypeStruct((SMALL_ROWS, d), F32),
        in_specs=[pl.BlockSpec(memory_space=pltpu.VMEM)] * n, out_specs=pl.BlockSpec(memory_space=pltpu.VMEM),
    )(*ln_parts, *conv_parts, loss_rows)


def _send_sums_start(s_ref, got, send, recv, part=0, parts=1, span=1):
    x, y, c = lax.axis_index("x"), lax.axis_index("y"), lax.axis_index("c")
    nrows = s_ref.shape[1] // parts
    rows = pl.ds(part * nrows, span * nrows)
    for k, (px, py) in enumerate(_other_chips(x, y)):
        _remote(s_ref.at[2 * px + py, rows], got.at[k, rows], send.at[k], recv.at[k], (px, py, c)).start()


def _send_sums_finish(s_ref, got, send, recv, part=0, parts=1, span=1):
    x, y, c = lax.axis_index("x"), lax.axis_index("y"), lax.axis_index("c")
    nrows = s_ref.shape[1] // parts
    rows = pl.ds(part * nrows, span * nrows)
    for k, (px, py) in enumerate(_other_chips(x, y)):
        _remote(got.at[k, rows], got.at[k, rows], send.at[k], recv.at[k], (px, py, c)).wait_recv()
    for k, (px, py) in enumerate(_other_chips(x, y)):
        _remote(s_ref.at[2 * px + py, rows], got.at[k, rows], send.at[k], recv.at[k], (px, py, c)).wait_send()


def _send_sums_rider(sums, got=None, part=0, parts=1, span=1):
    _, hr, d = sums.shape
    return _Rider(
        operands=[sums] if got is None else [sums, got],
        out_shapes=[jax.ShapeDtypeStruct((N_CHIPS - 1, hr, d), BF16)], aliases={} if got is None else {1: 0},
        sems=[pltpu.SemaphoreType.DMA((3,)), pltpu.SemaphoreType.DMA((3,))],
        start=lambda ins, outs, sems: _send_sums_start(ins[0], outs[0], *sems, part, parts, span),
        finish=lambda ins, outs, sems: _send_sums_finish(ins[0], outs[0], *sems, part, parts, span))


def _swap_pieces(gp_ref, x_ref, c):
    hr = x_ref.shape[1]
    cr = hr // SWAP_CHUNKS
    return [(a * SWAP_CHUNKS + q, gp_ref.at[a, pl.ds((1 - c) * hr + q * cr, cr)], x_ref.at[a, pl.ds(q * cr, cr)])
            for a in range(N_CHIPS) for q in range(SWAP_CHUNKS)]


def _swap_rider(gp):
    _, p_rows, d = gp.shape

    def start(ins, outs, sems):
        x, y, c = lax.axis_index("x"), lax.axis_index("y"), lax.axis_index("c")
        for i, src, dst in _swap_pieces(ins[0], outs[0], c):
            _remote(src, dst, sems[0].at[i], sems[1].at[i], (x, y, 1 - c)).start()

    def finish(ins, outs, sems):
        x, y, c = lax.axis_index("x"), lax.axis_index("y"), lax.axis_index("c")
        pieces = _swap_pieces(ins[0], outs[0], c)
        for i, src, dst in pieces:
            _remote(dst, dst, sems[0].at[i], sems[1].at[i], (x, y, 1 - c)).wait_recv()
        for i, src, dst in pieces:
            _remote(src, dst, sems[0].at[i], sems[1].at[i], (x, y, 1 - c)).wait_send()

    nsem = N_CHIPS * SWAP_CHUNKS
    return _Rider(
        operands=[gp], out_shapes=[jax.ShapeDtypeStruct((N_CHIPS, p_rows // 2, d), BF16)], aliases={},
        sems=[pltpu.SemaphoreType.DMA((nsem,)), pltpu.SemaphoreType.DMA((nsem,))], start=start, finish=finish)


def _presum(gp, theirs, core):
    _, hr, d = theirs.shape
    tr = _row_tile(hr, 640)
    steps = hr // tr

    def body(core_ref, mine_ref, theirs_ref, o_ref):
        o_ref[...] = (mine_ref[...].astype(F32) + theirs_ref[...].astype(F32)).astype(BF16)

    grid_spec = pltpu.PrefetchScalarGridSpec(
        num_scalar_prefetch=1, grid=(N_CHIPS, steps),
        in_specs=[pl.BlockSpec((None, tr, d), lambda a, i, cr: (a, cr[0] * steps + i, 0)),
                  pl.BlockSpec((None, tr, d), lambda a, i, cr: (a, i, 0))],
        out_specs=pl.BlockSpec((None, tr, d), lambda a, i, cr: (a, i, 0)))
    return pl.pallas_call(
        body, name="presum", grid_spec=grid_spec,
        out_shape=jax.ShapeDtypeStruct((N_CHIPS, hr, d), BF16), compiler_params=_params(2),
    )(core, gp, theirs)


def _row_tile(rows, cap=128):
    if rows <= cap:
        return rows
    return next(tr for tr in range(cap, 0, -16) if rows % tr == 0)


def _sum_sources(name, parts):
    nsrc, rows, cols = parts.shape
    tr = _row_tile(rows)

    def body(p_ref, o_ref):
        total = p_ref[0].astype(F32)
        for s in range(1, nsrc):
            total = total + p_ref[s].astype(F32)
        o_ref[...] = total

    return pl.pallas_call(
        body, name=name, grid=(rows // tr,),
        in_specs=[pl.BlockSpec((nsrc, tr, cols), lambda i: (0, i, 0))],
        out_specs=pl.BlockSpec((tr, cols), lambda i: (i, 0)),
        out_shape=jax.ShapeDtypeStruct((rows, cols), F32), compiler_params=_params(1),
    )(parts)


def _sum_grad_half(got, sums, place):
    nsrc, hr, d = got.shape
    tr = _row_tile(hr, 256)

    def body(place_ref, got_ref, own_ref, o_ref):
        total = own_ref[...].astype(F32)
        for s in range(nsrc):
            total = total + got_ref[s].astype(F32)
        o_ref[...] = total

    grid_spec = pltpu.PrefetchScalarGridSpec(
        num_scalar_prefetch=1, grid=(hr // tr,),
        in_specs=[pl.BlockSpec((nsrc, tr, d), lambda i, pc: (0, i, 0)),
                  pl.BlockSpec((None, tr, d), lambda i, pc: (pc[0], i, 0))],
        out_specs=pl.BlockSpec((None, tr, d), lambda i, pc: (pc[1], i, 0)))
    return pl.pallas_call(
        body, name="sum_grad_half", grid_spec=grid_spec,
        out_shape=jax.ShapeDtypeStruct((2, hr, d), F32), compiler_params=_params(1),
    )(place, got, sums)


def _halves_rider(full):
    _, hr, d = full.shape
    cr = hr // HALF_CHUNKS

    def pieces(ref, half):
        return [ref.at[half, pl.ds(q * cr, cr)] for q in range(HALF_CHUNKS)]

    def start(ins, outs, sems):
        x, y, c = lax.axis_index("x"), lax.axis_index("y"), lax.axis_index("c")
        for i, piece in enumerate(pieces(outs[0], c)):
            _remote(piece, piece, sems[0].at[i], sems[1].at[i], (x, y, 1 - c)).start()

    def finish(ins, outs, sems):
        x, y, c = lax.axis_index("x"), lax.axis_index("y"), lax.axis_index("c")
        for i, piece in enumerate(pieces(outs[0], 1 - c)):
            _remote(piece, piece, sems[0].at[i], sems[1].at[i], (x, y, 1 - c)).wait_recv()
        for i, piece in enumerate(pieces(outs[0], c)):
            _remote(piece, piece, sems[0].at[i], sems[1].at[i], (x, y, 1 - c)).wait_send()

    return _Rider(
        operands=[full], out_shapes=[jax.ShapeDtypeStruct(full.shape, F32)], aliases={0: 0},
        sems=[pltpu.SemaphoreType.DMA((HALF_CHUNKS,)), pltpu.SemaphoreType.DMA((HALF_CHUNKS,))],
        start=start, finish=finish)


def _adamw(name, w, m, v, g, g_row0=0):
    rows, cols = w.shape
    tr = _row_tile(rows, 256)
    off = g_row0 // tr

    def body(w_ref, m_ref, v_ref, g_ref, go_ref, d_ref, mo_ref, vo_ref):
        grad = g_ref[...]
        m_new = ADAM_B1 * m_ref[...] + (1.0 - ADAM_B1) * grad
        v_new = ADAM_B2 * v_ref[...] + (1.0 - ADAM_B2) * (grad * grad)
        m_hat = m_new / (1.0 - ADAM_B1 ** ADAM_STEP)
        v_hat = v_new / (1.0 - ADAM_B2 ** ADAM_STEP)
        go_ref[...] = grad
        d_ref[...] = -ADAM_LR * (m_hat / (jnp.sqrt(v_hat) + ADAM_EPS) + ADAM_WD * w_ref[...])
        mo_ref[...] = m_new
        vo_ref[...] = v_new

    blk = pl.BlockSpec((tr, cols), lambda i: (i, 0))
    return _compute_call(
        body, name, grid=(rows // tr,),
        in_specs=[blk, blk, blk, pl.BlockSpec((tr, cols), lambda i: (i + off, 0))], operands=[w, m, v, g],
        out_specs=[blk, blk, blk, blk], out_shape=[jax.ShapeDtypeStruct((rows, cols), F32)] * 4)


def _pad_rows8(a):
    return jnp.concatenate([a, jnp.zeros((8 - a.shape[0],) + a.shape[1:], a.dtype)], axis=0)


def kernel(x, ln_pre_0, conv_w_in_0, conv_w_0, conv_w_out_0, ln_post_0, ln_pre_1, sb_w_in_1, sb_w_out_1, ln_post_1, ln_pre_2, conv_w_in_2, conv_w_2, conv_w_out_2, ln_post_2, ln_pre_3, sb_w_in_3, sb_w_out_3, ln_post_3, loss_target, m_ln_pre_0, m_conv_w_in_0, m_conv_w_0, m_conv_w_out_0, m_ln_post_0, m_ln_pre_1, m_sb_w_in_1, m_sb_w_out_1, m_ln_post_1, m_ln_pre_2, m_conv_w_in_2, m_conv_w_2, m_conv_w_out_2, m_ln_post_2, m_ln_pre_3, m_sb_w_in_3, m_sb_w_out_3, m_ln_post_3, v_ln_pre_0, v_conv_w_in_0, v_conv_w_0, v_conv_w_out_0, v_ln_post_0, v_ln_pre_1, v_sb_w_in_1, v_sb_w_out_1, v_ln_post_1, v_ln_pre_2, v_conv_w_in_2, v_conv_w_2, v_conv_w_out_2, v_ln_post_2, v_ln_pre_3, v_sb_w_in_3, v_sb_w_out_3, v_ln_post_3):
    t, d = x.shape[1], x.shape[2]
    dq = d // N_CHIPS
    xs = x.reshape(t, d)
    target = loss_target.reshape(t, d)
    w_in = [conv_w_in_0, sb_w_in_1, conv_w_in_2, sb_w_in_3]
    w_out = [conv_w_out_0, sb_w_out_1, conv_w_out_2, sb_w_out_3]
    m_in = [m_conv_w_in_0, m_sb_w_in_1, m_conv_w_in_2, m_sb_w_in_3]
    m_out = [m_conv_w_out_0, m_sb_w_out_1, m_conv_w_out_2, m_sb_w_out_3]
    v_in = [v_conv_w_in_0, v_sb_w_in_1, v_conv_w_in_2, v_sb_w_in_3]
    v_out = [v_conv_w_out_0, v_sb_w_out_1, v_conv_w_out_2, v_sb_w_out_3]
    ln_pre = [ln_pre_0, ln_pre_1, ln_pre_2, ln_pre_3]
    ln_post = [ln_post_0, ln_post_1, ln_post_2, ln_post_3]
    conv_w = [conv_w_0, conv_w_2]
    m_conv = [m_conv_w_0, m_conv_w_2]
    v_conv = [v_conv_w_0, v_conv_w_2]
    chip = 2 * lax.axis_index("x") + lax.axis_index("y")
    chip_arr = jnp.reshape(chip, (1,)).astype(jnp.int32)
    place = jnp.stack([chip, lax.axis_index("c")]).astype(jnp.int32)
    core_arr = jnp.reshape(lax.axis_index("c"), (1,)).astype(jnp.int32)

    packs = [_pack_weights(w_in[l], w_out[l], chip_arr) for l in range(N_LAYERS)]
    cw_local = jnp.concatenate([_pad_rows8(conv_w[0]), _pad_rows8(conv_w[1])], axis=0)
    gathered = list(packs)
    every = range(AG_CHUNKS)
    u, gathered[0], cw_all = _norm_first(
        xs, ln_pre[0], rider=_join_riders(_gather_rider(packs[:1], chunks=every, pass_now=every), _taps_rider(cw_local)))
    cw_full = jnp.transpose(cw_all, (1, 0, 2)).reshape(16, d)
    conv_taps = {0: cw_full[0:8], 2: cw_full[8:16]}

    h_in, us, projs, gateds, ms, sb_saved = [], [], [], [], [], {}
    h = xs
    for l in range(N_LAYERS):
        h_in.append(h)
        us.append(u)
        nxt = l + 1
        plan = {} if nxt == N_LAYERS else (AG_PLAN_CONV if l % 2 == 0 else AG_PLAN_SB)

        def rider_for(name):
            return _gather_rider(gathered[nxt:nxt + 1], **plan[name]) if name in plan else None

        def take(results, name):
            if name not in plan:
                return results
            gathered[nxt] = results[-1]
            return results[:-1]

        proj, = take(_mm_proj(u, gathered[l], rider=rider_for("mm_proj")), "mm_proj")
        if l % 2 == 0:
            gated, = take(_conv_fwd(proj, conv_taps[l], rider=rider_for("conv_fwd")), "conv_fwd")
        else:
            gated, o, car = take(_sb_fwd(proj, rider=rider_for("sb_fwd")), "sb_fwd")
            sb_saved[l] = (o, car)
        m, = take(_mm_out(gated, gathered[l], rider=rider_for("mm_out")), "mm_out")
        projs.append(proj)
        gateds.append(gated)
        ms.append(m)
        if l < N_LAYERS - 1:
            h, u = take(_norm_mid(h, m, ln_post[l], ln_pre[nxt], rider=rider_for("norm_mid")), "norm_mid")
    dh, dm, dg_post_last, loss_part = _norm_last(h, ms[-1], ln_post[-1], target)

    dg_pre = [None] * N_LAYERS
    dg_post = [None] * N_LAYERS
    dg_post[N_LAYERS - 1] = dg_post_last
    dconv = {}
    sums = [None] * N_LAYERS
    got = [None] * N_LAYERS
    fulls = [None] * N_LAYERS

    def summed(layer):
        return _halves_rider(_sum_grad_half(got[layer], sums[layer], place))

    for l in reversed(range(N_LAYERS)):
        above = l + 1 if l + 1 < N_LAYERS else None
        if l == 1:
            dgated, fulls[3] = _mm_dgated(dm, gathered[l], rider=summed(3))
        else:
            dgated, = _mm_dgated(dm, gathered[l])
        if l == 0:
            gp, got[1] = _mm_dwout(gateds[l], dm, rider=_send_sums_rider(sums[1], None, 0, 4, 1))
        else:
            gp, = _mm_dwout(gateds[l], dm)
        if l % 2 == 0:
            dproj, dconv[l] = _conv_bwd(projs[l], dgated, conv_taps[l])
        elif above is not None:
            o, car = sb_saved[l]
            dproj, got[above] = _sb_bwd(projs[l], dgated, o, car, rider=_send_sums_rider(sums[above]))
        else:
            o, car = sb_saved[l]
            dproj, = _sb_bwd(projs[l], dgated, o, car)
        if l == 0:
            gp, got[1] = _mm_dwin(us[0], dproj, gp, rider=_send_sums_rider(sums[1], got[1], 1, 4, 3))
            theirs, = _comm_call("swap_last", _swap_rider(gp))
            sums[0] = _presum(gp, theirs, core_arr)
            du, got[0], fulls[1] = _mm_du(
                dproj, gathered[0], rider=_join_riders(_send_sums_rider(sums[0], None, 0, 4, 3), summed(1)))
        elif l % 2 == 0:
            gp, got[above] = _mm_dwin(us[l], dproj, gp, rider=_send_sums_rider(sums[above], None, 0, 2))
            du, theirs, got[above] = _mm_du(
                dproj, gathered[l], rider=_join_riders(_swap_rider(gp), _send_sums_rider(sums[above], got[above], 1, 2)))
            sums[l] = _presum(gp, theirs, core_arr)
        elif above is not None:
            gp, fulls[above] = _mm_dwin(us[l], dproj, gp, rider=summed(above))
            du, theirs = _mm_du(dproj, gathered[l], rider=_swap_rider(gp))
            sums[l] = _presum(gp, theirs, core_arr)
        else:
            gp, = _mm_dwin(us[l], dproj, gp)
            du, theirs = _mm_du(dproj, gathered[l], rider=_swap_rider(gp))
            sums[l] = _presum(gp, theirs, core_arr)
        if l > 0:
            dh, dm, dg_pre[l], dg_post[l - 1] = _norm_bwd_mid(dh, du, h_in[l], ln_pre[l], ms[l - 1], ln_post[l - 1])
    grad_x, dg_pre[0] = _norm_bwd_first(dh, du, h_in[0], ln_pre[0])

    loss_rows = jnp.pad(loss_part, ((0, 0), (0, d - loss_part.shape[1])))
    small = _pack_small(dg_pre + dg_post, [dconv[0], dconv[2]], loss_rows)
    got[0], small_all = _comm_call(
        "exchange_last", _join_riders(_send_sums_rider(sums[0], got[0], 3, 4, 1), _small_rider(small)))
    small_sum = _sum_sources("sum_small", small_all)

    fulls[0], = _comm_call("exchange_halves", summed(0))
    fulls = [f.reshape(d + dq, d) for f in fulls]
    res_in = [_adamw("adamw_w_in", w_in[l], m_in[l], v_in[l], fulls[l], 0) for l in range(N_LAYERS)]
    res_out = [_adamw("adamw_w_out", w_out[l], m_out[l], v_out[l], fulls[l], d) for l in range(N_LAYERS)]
    ln_all = ln_pre + ln_post
    ln_m = [m_ln_pre_0, m_ln_pre_1, m_ln_pre_2, m_ln_pre_3, m_ln_post_0, m_ln_post_1, m_ln_post_2, m_ln_post_3]
    ln_v = [v_ln_pre_0, v_ln_pre_1, v_ln_pre_2, v_ln_pre_3, v_ln_post_0, v_ln_post_1, v_ln_post_2, v_ln_post_3]
    res_ln = _adamw("adamw_ln", jnp.stack(ln_all), jnp.stack(ln_m), jnp.stack(ln_v), small_sum[0:2 * N_LAYERS])
    conv_g = [_pad_rows8(lax.dynamic_slice(small_sum, (2 * N_LAYERS + 3 * i, chip * dq), (3, dq))) for i in range(2)]
    stack8 = lambda a, b: jnp.concatenate([_pad_rows8(a), _pad_rows8(b)], axis=0)
    res_conv = _adamw("adamw_conv", stack8(*conv_w), stack8(*m_conv), stack8(*v_conv), jnp.concatenate(conv_g, axis=0))

    def leaf(kind, l, which):
        if kind == "ln_pre":
            return res_ln[which][l]
        if kind == "ln_post":
            return res_ln[which][N_LAYERS + l]
        if kind == "w_in":
            return res_in[l][which]
        if kind == "w_out":
            return res_out[l][which]
        return res_conv[which][8 * (l // 2):8 * (l // 2) + 3]

    order = []
    for l in range(N_LAYERS):
        order.append(("ln_pre", l))
        order.append(("w_in", l))
        if l % 2 == 0:
            order.append(("conv", l))
        order.append(("w_out", l))
        order.append(("ln_post", l))
    loss = small_sum[2 * N_LAYERS + 3 * 2, 0]
    outs = [loss, grad_x.reshape(1, t, d)]
    for which in range(4):
        outs.extend(leaf(kind, l, which) for kind, l in order)
    return tuple(outs)
```

```python
import functools
import math
from typing import Any, Callable, Mapping, NamedTuple, Sequence

import jax
import jax.numpy as jnp
from jax import lax
from jax.experimental import pallas as pl
from jax.experimental.pallas import tpu as pltpu

F32 = jnp.float32
BF16 = jnp.bfloat16

N_CHIPS = 4
N_DEV = 8
N_LAYERS = 4
HEAD_DIM = 128
RMS_EPS = 1e-6
ADAM_LR = 0.001
ADAM_B1 = 0.9
ADAM_B2 = 0.999
ADAM_EPS = 1e-08
ADAM_WD = 0.01
ADAM_STEP = 10

VMEM_LIMIT = 56 * 1024 * 1024
MESH_IDS = pl.DeviceIdType.MESH
HBM_SPEC = pl.BlockSpec(memory_space=pltpu.HBM)

NN = (((1,), (0,)), ((), ()))
NT = (((1,), (1,)), ((), ()))
TN = (((0,), (0,)), ((), ()))


def _params(n_axes):
    return pltpu.CompilerParams(dimension_semantics=("arbitrary",) * n_axes, vmem_limit_bytes=VMEM_LIMIT)


def _dot(a, b, dims):
    return lax.dot_general(a, b, dims, preferred_element_type=F32)


def _sigmoid(z):
    return 1.0 / (1.0 + jnp.exp(-z))


class _Rider(NamedTuple):
    operands: Sequence[Any]
    out_shapes: Sequence[Any]
    aliases: Mapping[int, int]
    sems: Sequence[Any]
    start: Callable
    finish: Callable


def _compute_call(body, name, *, grid, in_specs, operands, out_specs, out_shape, scratch=(), aliases=None, rider=None):
    in_specs, operands = list(in_specs), list(operands)
    out_specs, out_shape, scratch = list(out_specs), list(out_shape), list(scratch)
    aliases = dict(aliases or {})
    n_in, n_out, n_scratch = len(operands), len(out_shape), len(scratch)
    hosted = body
    if rider is not None:
        r_in, r_out = len(rider.operands), len(rider.out_shapes)
        aliases.update({n_in + i: n_out + o for i, o in rider.aliases.items()})

        def hosted(*refs):
            ins, refs = refs[:n_in], refs[n_in:]
            rider_ins, refs = refs[:r_in], refs[r_in:]
            outs, refs = refs[:n_out], refs[n_out:]
            rider_outs, refs = refs[:r_out], refs[r_out:]
            own_scratch, rider_sems = refs[:n_scratch], refs[n_scratch:]
            ids = [pl.program_id(axis) for axis in range(len(grid))]
            first = functools.reduce(jnp.logical_and, [i == 0 for i in ids])
            last = functools.reduce(jnp.logical_and, [i == g - 1 for i, g in zip(ids, grid)])

            @pl.when(first)
            def _():
                rider.start(rider_ins, rider_outs, rider_sems)

            body(*ins, *outs, *own_scratch)

            @pl.when(last)
            def _():
                rider.finish(rider_ins, rider_outs, rider_sems)

        in_specs += [HBM_SPEC] * r_in
        operands += list(rider.operands)
        out_specs += [HBM_SPEC] * r_out
        out_shape += list(rider.out_shapes)
        scratch += list(rider.sems)
    return pl.pallas_call(
        hosted, name=name, grid=grid, in_specs=in_specs, out_specs=out_specs, out_shape=out_shape,
        scratch_shapes=scratch, input_output_aliases=aliases, compiler_params=_params(len(grid)),
    )(*operands)


def _matmul(name, a, b, *, grid, a_spec, b_spec, o_spec, out_shape, dims, reduce_axis=None, acc_shape=None,
            alias_out=None, rider=None):
    out_dtype = out_shape.dtype
    direct = reduce_axis is not None and out_dtype == F32
    n_red = grid[reduce_axis] if reduce_axis is not None else 1

    def body(*refs):
        if alias_out is not None:
            refs = refs[1:]
        a_ref, b_ref, o_ref = refs[:3]
        if reduce_axis is None:
            o_ref[...] = _dot(a_ref[...], b_ref[...], dims).astype(out_dtype)
            return
        acc_ref = o_ref if direct else refs[3]
        k = pl.program_id(reduce_axis)

        @pl.when(k == 0)
        def _():
            acc_ref[...] = jnp.zeros_like(acc_ref)

        acc_ref[...] += _dot(a_ref[...], b_ref[...], dims)

        if not direct:
            @pl.when(k == n_red - 1)
            def _():
                o_ref[...] = acc_ref[...].astype(out_dtype)

    scratch = []
    if reduce_axis is not None and not direct:
        scratch = [pltpu.VMEM(acc_shape, F32)]
    in_specs = [a_spec, b_spec]
    operands = [a, b]
    aliases = {}
    if alias_out is not None:
        in_specs = [HBM_SPEC] + in_specs
        operands = [alias_out] + operands
        aliases = {0: 0}
    return _compute_call(body, name, grid=grid, in_specs=in_specs, operands=operands, out_specs=[o_spec],
                         out_shape=[out_shape], scratch=scratch, aliases=aliases, rider=rider)


def _mm_proj(u, g, rider=None):
    t, d = u.shape
    tm = min(512, t)
    return _matmul(
        "mm_proj", u, g, grid=(N_CHIPS, t // tm),
        a_spec=pl.BlockSpec((tm, d), lambda s, m: (m, 0)),
        b_spec=pl.BlockSpec((None, d, d), lambda s, m: (s, 0, 0)),
        o_spec=pl.BlockSpec((None, tm, d), lambda s, m: (s, m, 0)),
        out_shape=jax.ShapeDtypeStruct((N_CHIPS, t, d), BF16), dims=NN, rider=rider)


def _mm_out(gated, g):
    t, d = gated.shape
    dq = d // N_CHIPS
    tm = min(512, t)

    def body(a_ref, b_ref, o_ref):
        acc = _dot(a_ref[:, 0:dq], b_ref[0], NN)
        for s in range(1, N_CHIPS):
            acc = acc + _dot(a_ref[:, s * dq:(s + 1) * dq], b_ref[s], NN)
        o_ref[...] = acc

    return _compute_call(
        body, "mm_out", grid=(t // tm,),
        in_specs=[pl.BlockSpec((tm, d), lambda m: (m, 0)), pl.BlockSpec((N_CHIPS, dq, d), lambda m: (0, N_CHIPS, 0))],
        operands=[gated, g], out_specs=[pl.BlockSpec((tm, d), lambda m: (m, 0))],
        out_shape=[jax.ShapeDtypeStruct((t, d), F32)])


def _mm_dgated(dm, g, rider=None):
    t, d = dm.shape
    dq = d // N_CHIPS
    tm = min(512, t)

    def body(a_ref, b_ref, o_ref):
        a = a_ref[...]
        for s in range(N_CHIPS):
            o_ref[:, s * dq:(s + 1) * dq] = _dot(a, b_ref[s], NT).astype(BF16)

    return _compute_call(
        body, "mm_dgated", grid=(t // tm,),
        in_specs=[pl.BlockSpec((tm, d), lambda m: (m, 0)), pl.BlockSpec((N_CHIPS, dq, d), lambda m: (0, N_CHIPS, 0))],
        operands=[dm, g], out_specs=[pl.BlockSpec((tm, d), lambda m: (m, 0))],
        out_shape=[jax.ShapeDtypeStruct((t, d), BF16)], rider=rider)


def _mm_dwout(gated, dm, rider=None):
    t, d = gated.shape
    dq = d // N_CHIPS
    tk = min(1024, t)
    return _matmul(
        "mm_dwout", gated, dm, grid=(N_CHIPS, t // tk),
        a_spec=pl.BlockSpec((tk, dq), lambda s, k: (k, s)),
        b_spec=pl.BlockSpec((tk, d), lambda s, k: (k, 0)),
        o_spec=pl.BlockSpec((None, dq, d), lambda s, k: (s, N_CHIPS, 0)),
        out_shape=jax.ShapeDtypeStruct((N_CHIPS, d + dq, d), BF16), dims=TN, reduce_axis=1, acc_shape=(dq, d),
        rider=rider)


def _mm_du(dproj, g, rider=None):
    _, t, d = dproj.shape
    tm = min(512, t)
    return _matmul(
        "mm_du", dproj, g, grid=(t // tm, N_CHIPS),
        a_spec=pl.BlockSpec((None, tm, d), lambda m, s: (s, m, 0)),
        b_spec=pl.BlockSpec((None, d, d), lambda m, s: (s, 0, 0)),
        o_spec=pl.BlockSpec((tm, d), lambda m, s: (m, 0)),
        out_shape=jax.ShapeDtypeStruct((t, d), F32), dims=NT, reduce_axis=1, rider=rider)


def _mm_dwin(u, dproj, gp, rider=None):
    t, d = u.shape
    tmo = min(1024, d)
    tk = min(1024, t)
    return _matmul(
        "mm_dwin", u, dproj, grid=(N_CHIPS, d // tmo, t // tk),
        a_spec=pl.BlockSpec((tk, tmo), lambda s, mo, k: (k, mo)),
        b_spec=pl.BlockSpec((None, tk, d), lambda s, mo, k: (s, k, 0)),
        o_spec=pl.BlockSpec((None, tmo, d), lambda s, mo, k: (s, mo, 0)),
        out_shape=jax.ShapeDtypeStruct(gp.shape, BF16), dims=TN, reduce_axis=2, acc_shape=(tmo, d), alias_out=gp,
        rider=rider)


def _rms(v):
    r = lax.rsqrt(jnp.mean(v * v, axis=-1, keepdims=True) + RMS_EPS)
    return v * r, r


def _rms_bwd(dout, n, r, gain):
    dn = dout * gain
    return r * (dn - n * jnp.mean(dn * n, axis=-1, keepdims=True))


def _fold8(v):
    return jnp.sum(v.reshape(v.shape[0] // 8, 8, v.shape[1]), axis=0)


def _row0(total):
    rows = lax.broadcasted_iota(jnp.int32, total.shape, 0)
    return jnp.where(rows == 0, jnp.sum(total, axis=0, keepdims=True), 0.0)


def _norm_tile(t):
    return min(256, t)


def _norm_first(x, g_pre, rider=None):
    t, d = x.shape
    tr = _norm_tile(t)

    def body(x_ref, g_ref, u_ref):
        n, _ = _rms(x_ref[...])
        u_ref[...] = (n * g_ref[...]).astype(BF16)

    row = pl.BlockSpec((tr, d), lambda i: (i, 0))
    vec = pl.BlockSpec((1, d), lambda i: (0, 0))
    return _compute_call(
        body, "norm_first", grid=(t // tr,), in_specs=[row, vec], operands=[x, g_pre.reshape(1, d)],
        out_specs=[row], out_shape=[jax.ShapeDtypeStruct((t, d), BF16)], rider=rider)


def _mm_out_norm(gated, g, h, g_post, g_pre_next, rider=None):
    t, d = h.shape
    dq = d // N_CHIPS
    tm = _norm_tile(t)

    def body(a_ref, b_ref, h_ref, gp_ref, gn_ref, m_ref, hn_ref, u_ref):
        acc = _dot(a_ref[:, 0:dq], b_ref[0], NN)
        for s in range(1, N_CHIPS):
            acc = acc + _dot(a_ref[:, s * dq:(s + 1) * dq], b_ref[s], NN)
        m_ref[...] = acc
        n, _ = _rms(acc)
        hn = h_ref[...] + n * gp_ref[...]
        hn_ref[...] = hn
        n2, _ = _rms(hn)
        u_ref[...] = (n2 * gn_ref[...]).astype(BF16)

    row = pl.BlockSpec((tm, d), lambda i: (i, 0))
    vec = pl.BlockSpec((1, d), lambda i: (0, 0))
    return _compute_call(
        body, "mm_out_norm", grid=(t // tm,),
        in_specs=[row, pl.BlockSpec((N_CHIPS, dq, d), lambda i: (0, N_CHIPS, 0)), row, vec, vec],
        operands=[gated, g, h, g_post.reshape(1, d), g_pre_next.reshape(1, d)], out_specs=[row, row, row],
        out_shape=[jax.ShapeDtypeStruct((t, d), F32), jax.ShapeDtypeStruct((t, d), F32), jax.ShapeDtypeStruct((t, d), BF16)],
        rider=rider)


def _norm_last(h, m, g_post, target):
    t, d = h.shape
    tr = _norm_tile(t)
    nsteps = t // tr

    def body(h_ref, m_ref, gp_ref, tg_ref, dy_ref, dm_ref, dgp_ref, loss_ref, acc_g, acc_l):
        i = pl.program_id(0)

        @pl.when(i == 0)
        def _():
            acc_g[...] = jnp.zeros_like(acc_g)
            acc_l[...] = jnp.zeros_like(acc_l)

        gain = gp_ref[...]
        n, r = _rms(m_ref[...])
        err = h_ref[...] + n * gain - tg_ref[...]
        dy = err / d
        dy_ref[...] = dy
        dm_ref[...] = _rms_bwd(dy, n, r, gain).astype(BF16)
        acc_g[...] += _fold8(dy * n)
        acc_l[...] += _fold8(err * err)

        @pl.when(i == nsteps - 1)
        def _():
            dgp_ref[...] = _row0(acc_g[...])
            loss_ref[...] = jnp.zeros((8, 128), F32) + (0.5 / d) * jnp.sum(acc_l[...])

    row = pl.BlockSpec((tr, d), lambda i: (i, 0))
    vec = pl.BlockSpec((1, d), lambda i: (0, 0))
    acc = pl.BlockSpec((8, d), lambda i: (0, 0))
    return pl.pallas_call(
        body, name="norm_last", grid=(nsteps,), in_specs=[row, row, vec, row],
        out_specs=[row, row, acc, pl.BlockSpec((8, 128), lambda i: (0, 0))],
        out_shape=[jax.ShapeDtypeStruct((t, d), F32), jax.ShapeDtypeStruct((t, d), BF16),
                   jax.ShapeDtypeStruct((8, d), F32), jax.ShapeDtypeStruct((8, 128), F32)],
        scratch_shapes=[pltpu.VMEM((8, d), F32), pltpu.VMEM((8, d), F32)],
        compiler_params=_params(1),
    )(h, m, g_post.reshape(1, d), target)


def _norm_bwd_mid(dh, du, h_in, g_pre, m_prev, g_post_prev):
    t, d = dh.shape
    tr = _norm_tile(t)
    nsteps = t // tr

    def body(dh_ref, du_ref, h_ref, gpre_ref, m_ref, gpost_ref, dhn_ref, dm_ref, dgpre_ref, dgpost_ref, acc_a, acc_b):
        i = pl.program_id(0)

        @pl.when(i == 0)
        def _():
            acc_a[...] = jnp.zeros_like(acc_a)
            acc_b[...] = jnp.zeros_like(acc_b)

        du_t = du_ref[...]
        n, r = _rms(h_ref[...])
        dhn = dh_ref[...] + _rms_bwd(du_t, n, r, gpre_ref[...])
        dhn_ref[...] = dhn
        acc_a[...] += _fold8(du_t * n)
        n2, r2 = _rms(m_ref[...])
        dm_ref[...] = _rms_bwd(dhn, n2, r2, gpost_ref[...]).astype(BF16)
        acc_b[...] += _fold8(dhn * n2)

        @pl.when(i == nsteps - 1)
        def _():
            dgpre_ref[...] = _row0(acc_a[...])
            dgpost_ref[...] = _row0(acc_b[...])

    row = pl.BlockSpec((tr, d), lambda i: (i, 0))
    vec = pl.BlockSpec((1, d), lambda i: (0, 0))
    acc = pl.BlockSpec((8, d), lambda i: (0, 0))
    return pl.pallas_call(
        body, name="norm_bwd_mid", grid=(nsteps,), in_specs=[row, row, row, vec, row, vec],
        out_specs=[row, row, acc, acc],
        out_shape=[jax.ShapeDtypeStruct((t, d), F32), jax.ShapeDtypeStruct((t, d), BF16),
                   jax.ShapeDtypeStruct((8, d), F32), jax.ShapeDtypeStruct((8, d), F32)],
        scratch_shapes=[pltpu.VMEM((8, d), F32), pltpu.VMEM((8, d), F32)],
        compiler_params=_params(1),
    )(dh, du, h_in, g_pre.reshape(1, d), m_prev, g_post_prev.reshape(1, d))


def _norm_bwd_first(dh, du, x, g_pre):
    t, d = dh.shape
    tr = _norm_tile(t)
    nsteps = t // tr

    def body(dh_ref, du_ref, x_ref, gpre_ref, dx_ref, dgpre_ref, acc_a):
        i = pl.program_id(0)

        @pl.when(i == 0)
        def _():
            acc_a[...] = jnp.zeros_like(acc_a)

        du_t = du_ref[...]
        n, r = _rms(x_ref[...])
        dx_ref[...] = dh_ref[...] + _rms_bwd(du_t, n, r, gpre_ref[...])
        acc_a[...] += _fold8(du_t * n)

        @pl.when(i == nsteps - 1)
        def _():
            dgpre_ref[...] = _row0(acc_a[...])

    row = pl.BlockSpec((tr, d), lambda i: (i, 0))
    vec = pl.BlockSpec((1, d), lambda i: (0, 0))
    acc = pl.BlockSpec((8, d), lambda i: (0, 0))
    return _compute_call(
        body, "norm_bwd_first", grid=(nsteps,), in_specs=[row, row, row, vec],
        operands=[dh, du, x, g_pre.reshape(1, d)], out_specs=[row, acc],
        out_shape=[jax.ShapeDtypeStruct((t, d), F32), jax.ShapeDtypeStruct((8, d), F32)],
        scratch=[pltpu.VMEM((8, d), F32)])


CONV_TC = 128
CONV_HALO = 16


def _conv_chunk(t):
    return min(512, t)


def _shift_down(v, steps, fill):
    rows = lax.broadcasted_iota(jnp.int32, v.shape, 0)
    out = pltpu.roll(v, steps, axis=0)
    for k in range(steps):
        out = jnp.where(rows == k, fill[CONV_HALO - steps + k:CONV_HALO - steps + k + 1, :], out)
    return out


def _shift_up(v, steps, fill):
    nrows = v.shape[0]
    rows = lax.broadcasted_iota(jnp.int32, v.shape, 0)
    out = pltpu.roll(v, nrows - steps, axis=0)
    for k in range(steps):
        out = jnp.where(rows == nrows - steps + k, fill[k:k + 1, :], out)
    return out


def _conv_fwd(proj, cw, rider=None):
    _, t, d = proj.shape
    chunk = _conv_chunk(t)

    def body(p_ref, w_ref, o_ref):
        w = w_ref[...]
        w0, w1, w2 = w[0:1, :], w[1:2, :], w[2:3, :]
        for ci in range(t // chunk):
            t0 = ci * chunk
            rows = pl.ds(t0, chunk)
            b = p_ref[0, rows, :].astype(F32)
            cx = p_ref[1, rows, :].astype(F32) * p_ref[2, rows, :].astype(F32)
            z = p_ref[3, rows, :].astype(F32)
            if ci == 0:
                prev = jnp.zeros((CONV_HALO, CONV_TC), F32)
            else:
                halo = pl.ds(t0 - CONV_HALO, CONV_HALO)
                prev = p_ref[1, halo, :].astype(F32) * p_ref[2, halo, :].astype(F32)
            conv = w2 * cx + w1 * _shift_down(cx, 1, prev) + w0 * _shift_down(cx, 2, prev)
            o_ref[rows, :] = (z * _sigmoid(z) * b * conv).astype(BF16)

    return _compute_call(
        body, "conv_fwd", grid=(d // CONV_TC,),
        in_specs=[pl.BlockSpec((N_CHIPS, t, CONV_TC), lambda j: (0, 0, j)), pl.BlockSpec((8, CONV_TC), lambda j: (0, j))],
        operands=[proj, cw], out_specs=[pl.BlockSpec((t, CONV_TC), lambda j: (0, j))],
        out_shape=[jax.ShapeDtypeStruct((t, d), BF16)], rider=rider)


def _conv_bwd(proj, dgated, cw):
    _, t, d = proj.shape
    chunk = _conv_chunk(t)
    nchunks = t // chunk

    def body(p_ref, dg_ref, w_ref, dp_ref, dw_ref):
        w = w_ref[...]
        w0, w1, w2 = w[0:1, :], w[1:2, :], w[2:3, :]
        dw0 = jnp.zeros((1, CONV_TC), F32)
        dw1 = jnp.zeros((1, CONV_TC), F32)
        dw2 = jnp.zeros((1, CONV_TC), F32)
        for ci in range(nchunks):
            t0 = ci * chunk
            rows = pl.ds(t0, chunk)
            b = p_ref[0, rows, :].astype(F32)
            c = p_ref[1, rows, :].astype(F32)
            xt = p_ref[2, rows, :].astype(F32)
            z = p_ref[3, rows, :].astype(F32)
            dg = dg_ref[rows, :].astype(F32)
            cx = c * xt
            if ci == 0:
                prev = jnp.zeros((CONV_HALO, CONV_TC), F32)
            else:
                halo = pl.ds(t0 - CONV_HALO, CONV_HALO)
                prev = p_ref[1, halo, :].astype(F32) * p_ref[2, halo, :].astype(F32)
            cx1 = _shift_down(cx, 1, prev)
            cx2 = _shift_down(cx, 2, prev)
            conv = w2 * cx + w1 * cx1 + w0 * cx2
            sig = _sigmoid(z)
            dy = dg * (z * sig)
            dconv = dy * b
            if ci == nchunks - 1:
                nxt = jnp.zeros((CONV_HALO, CONV_TC), F32)
            else:
                halo = pl.ds(t0 + chunk, CONV_HALO)
                zn = p_ref[3, halo, :].astype(F32)
                nxt = dg_ref[halo, :].astype(F32) * (zn * _sigmoid(zn)) * p_ref[0, halo, :].astype(F32)
            dcx = w2 * dconv + w1 * _shift_up(dconv, 1, nxt) + w0 * _shift_up(dconv, 2, nxt)
            dp_ref[0, rows, :] = (dy * conv).astype(BF16)
            dp_ref[1, rows, :] = (dcx * xt).astype(BF16)
            dp_ref[2, rows, :] = (dcx * c).astype(BF16)
            dp_ref[3, rows, :] = (dg * (b * conv) * (sig * (1.0 + z * (1.0 - sig)))).astype(BF16)
            dw0 = dw0 + jnp.sum(dconv * cx2, axis=0, keepdims=True)
            dw1 = dw1 + jnp.sum(dconv * cx1, axis=0, keepdims=True)
            dw2 = dw2 + jnp.sum(dconv * cx, axis=0, keepdims=True)
        taps = lax.broadcasted_iota(jnp.int32, (8, CONV_TC), 0)
        dw_ref[...] = jnp.where(taps == 0, dw0, jnp.where(taps == 1, dw1, jnp.where(taps == 2, dw2, 0.0)))

    return pl.pallas_call(
        body, name="conv_bwd", grid=(d // CONV_TC,),
        in_specs=[pl.BlockSpec((N_CHIPS, t, CONV_TC), lambda j: (0, 0, j)),
                  pl.BlockSpec((t, CONV_TC), lambda j: (0, j)),
                  pl.BlockSpec((8, CONV_TC), lambda j: (0, j))],
        out_specs=[pl.BlockSpec((N_CHIPS, t, CONV_TC), lambda j: (0, 0, j)), pl.BlockSpec((8, CONV_TC), lambda j: (0, j))],
        out_shape=[jax.ShapeDtypeStruct((N_CHIPS, t, d), BF16), jax.ShapeDtypeStruct((8, d), F32)],
        compiler_params=_params(1),
    )(proj, dgated, cw)


SB_DEAD_TAIL = -105.0
SB_COUNT_LANE = HEAD_DIM - 1


def _sb_block(t):
    return min(256, t)


def _split_dot(v, tri):
    hi = v.astype(BF16)
    lo = (v - hi.astype(F32)).astype(BF16)
    return _dot(hi, tri, NN) + _dot(lo, tri, NN)


SB_HEADS_PER_STEP = 4


def _sb_terms(s, diagonal):
    sp = jnp.maximum(s, 0.0) + jnp.log(1.0 + jnp.exp(-jnp.abs(s)))
    if not diagonal:
        return -sp, s - sp, sp, None
    mask = lax.broadcasted_iota(jnp.int32, s.shape, 1) < lax.broadcasted_iota(jnp.int32, s.shape, 0)
    return jnp.where(mask, -sp, 0.0), s - sp, sp, mask


def _masked(mask, v):
    return v if mask is None else jnp.where(mask, v, 0.0)


def _sb_fwd(proj, rider=None):
    _, t, d = proj.shape
    heads = d // HEAD_DIM
    blk = _sb_block(t)
    nblk = t // blk
    scale = 1.0 / math.sqrt(HEAD_DIM)

    hps = SB_HEADS_PER_STEP
    width = hps * HEAD_DIM

    def body(q_ref, k_ref, v_ref, z_ref, gated_ref, o_ref, car_ref, tail_ref, acc_ref):
        i = pl.program_id(1)
        r_i = lax.broadcasted_iota(jnp.int32, (blk, blk), 0)
        c_i = lax.broadcasted_iota(jnp.int32, (blk, blk), 1)
        tri_after = (r_i > c_i).astype(BF16)
        lanes = lax.broadcasted_iota(jnp.int32, (blk, HEAD_DIM), 1)

        tail_ref[...] = jnp.zeros_like(tail_ref)
        acc_ref[...] = jnp.zeros_like(acc_ref)
        car_ref[...] = jnp.zeros_like(car_ref)

        def visit(j, diagonal):
            krows = pl.ds(pl.multiple_of(j * blk, blk), blk)
            hcols = [pl.ds(hh * HEAD_DIM, HEAD_DIM) for hh in range(hps)]
            logits = [_dot(q_ref[:, c], k_ref[krows, c], NT) for c in hcols]
            terms = [_sb_terms(s * scale, diagonal) for s in logits]
            within = [_split_dot(keep, tri_after) for keep, _, _, _ in terms]
            top = None
            for hh, (keep, log_beta, _, mask) in enumerate(terms):
                tail_b = tail_ref[hh]
                w = _masked(mask, jnp.exp(log_beta + tail_b[:, 0:1] + within[hh]))
                acc_ref[hh] += _dot(w.astype(BF16), v_ref[krows, hcols[hh]], NN)
                car_ref[hh] = jnp.where(lanes == j, tail_b, car_ref[hh])
                tail_new = tail_b + jnp.sum(keep, axis=1, keepdims=True)
                tail_ref[hh] = tail_new
                top = jnp.max(tail_new) if top is None else jnp.maximum(top, jnp.max(tail_new))
            return top > SB_DEAD_TAIL

        def more(state):
            jj, live = state
            return jnp.logical_and(jj <= i, live)

        def step(state):
            jj, _ = state
            return jj + 1, visit(i - jj, False)

        visited, _ = lax.while_loop(more, step, (jnp.int32(1), visit(i, True)))
        for hh in range(hps):
            cols = pl.ds(hh * HEAD_DIM, HEAD_DIM)
            car_ref[hh] = jnp.where(lanes == SB_COUNT_LANE, visited.astype(F32), car_ref[hh])
            z = z_ref[:, cols].astype(F32)
            acc = acc_ref[hh]
            o_ref[:, cols] = acc.astype(BF16)
            gated_ref[:, cols] = (z * _sigmoid(z) * acc).astype(BF16)

    qspec = lambda s: pl.BlockSpec((None, blk, width), lambda h, i: (s, i, h))
    kspec = lambda s: pl.BlockSpec((None, t, width), lambda h, i: (s, 0, h))
    ospec = pl.BlockSpec((blk, width), lambda h, i: (i, h))
    return _compute_call(
        body, "sb_fwd", grid=(heads // hps, nblk),
        in_specs=[qspec(0), kspec(1), kspec(2), qspec(3)], operands=[proj, proj, proj, proj],
        out_specs=[ospec, ospec, pl.BlockSpec((hps, blk, HEAD_DIM), lambda h, i: (h, i, 0))],
        out_shape=[jax.ShapeDtypeStruct((t, d), BF16), jax.ShapeDtypeStruct((t, d), BF16),
                   jax.ShapeDtypeStruct((heads, t, HEAD_DIM), F32)],
        scratch=[pltpu.VMEM((hps, blk, HEAD_DIM), F32), pltpu.VMEM((hps, blk, HEAD_DIM), F32)], rider=rider)


def _sb_bwd(proj, dgated, o, car, rider=None):
    _, t, d = proj.shape
    heads = d // HEAD_DIM
    blk = _sb_block(t)
    nblk = t // blk
    scale = 1.0 / math.sqrt(HEAD_DIM)

    hps = SB_HEADS_PER_STEP
    width = hps * HEAD_DIM

    def body(q_ref, k_ref, v_ref, z_ref, dg_ref, o_ref, car_ref, dp_ref, dk_acc, dv_acc, gsum_ref, dq_ref, do_ref):
        step_i = pl.program_id(1)
        i = nblk - 1 - step_i

        @pl.when(step_i == 0)
        def _():
            dk_acc[...] = jnp.zeros_like(dk_acc)
            dv_acc[...] = jnp.zeros_like(dv_acc)

        r_i = lax.broadcasted_iota(jnp.int32, (blk, blk), 0)
        c_i = lax.broadcasted_iota(jnp.int32, (blk, blk), 1)
        tri_after = (r_i > c_i).astype(BF16)
        tri_before = (r_i < c_i).astype(BF16)
        lanes = lax.broadcasted_iota(jnp.int32, (blk, HEAD_DIM), 1)

        gsum_ref[...] = jnp.zeros_like(gsum_ref)
        dq_ref[...] = jnp.zeros_like(dq_ref)
        for hh in range(hps):
            cols = pl.ds(hh * HEAD_DIM, HEAD_DIM)
            z = z_ref[:, cols].astype(F32)
            dg = dg_ref[:, cols].astype(F32)
            sig = _sigmoid(z)
            do_ref[hh] = (dg * (z * sig)).astype(BF16)
            dp_ref[3, :, cols] = (dg * o_ref[:, cols].astype(F32) * (sig * (1.0 + z * (1.0 - sig)))).astype(BF16)

        def visit(j, diagonal):
            krows = pl.ds(pl.multiple_of(j * blk, blk), blk)
            hcols = [pl.ds(hh * HEAD_DIM, HEAD_DIM) for hh in range(hps)]
            logits = [_dot(q_ref[:, c], k_ref[krows, c], NT) for c in hcols]
            dws = [_dot(do_ref[hh], v_ref[krows, c], NT) for hh, c in enumerate(hcols)]
            terms = [_sb_terms(s * scale, diagonal) for s in logits]
            within = [_split_dot(keep, tri_after) for keep, _, _, _ in terms]
            ws, gs = [], []
            for hh, (keep, log_beta, sp, mask) in enumerate(terms):
                tail = jnp.sum(jnp.where(lanes == j, car_ref[hh], 0.0), axis=1, keepdims=True)
                w = _masked(mask, jnp.exp(log_beta + tail + within[hh]))
                ws.append(w.astype(BF16))
                gs.append(w * dws[hh])
            g_within = [_split_dot(g, tri_before) for g in gs]
            for hh, (keep, log_beta, sp, mask) in enumerate(terms):
                c = hcols[hh]
                g_before = gsum_ref[hh]
                g_cum = g_before[:, 0:1] + g_within[hh]
                dl = (_masked(mask, gs[hh] - (gs[hh] + g_cum) * jnp.exp(log_beta)) * scale).astype(BF16)
                dq_ref[hh] += _dot(dl, k_ref[krows, c], NN)
                dk_acc[krows, c] += _dot(dl, q_ref[:, c], TN)
                dv_acc[krows, c] += _dot(ws[hh], do_ref[hh], TN)
                gsum_ref[hh] = g_before + jnp.sum(gs[hh], axis=1, keepdims=True)

        def step(j, carry):
            visit(j, False)
            return carry

        visited = jnp.max(jnp.where(lanes == SB_COUNT_LANE, car_ref[0], 0.0)).astype(jnp.int32)
        lax.fori_loop(i + 1 - visited, i, step, 0)
        visit(i, True)
        own = pl.ds(pl.multiple_of(i * blk, blk), blk)
        for hh in range(hps):
            cols = pl.ds(hh * HEAD_DIM, HEAD_DIM)
            dp_ref[0, :, cols] = dq_ref[hh].astype(BF16)
        dp_ref[1] = dk_acc[own, :].astype(BF16)
        dp_ref[2] = dv_acc[own, :].astype(BF16)

    qspec = lambda s: pl.BlockSpec((None, blk, width), lambda h, i: (s, nblk - 1 - i, h))
    kspec = lambda s: pl.BlockSpec((None, t, width), lambda h, i: (s, 0, h))
    tspec = pl.BlockSpec((blk, width), lambda h, i: (nblk - 1 - i, h))
    return _compute_call(
        body, "sb_bwd", grid=(heads // hps, nblk),
        in_specs=[qspec(0), kspec(1), kspec(2), qspec(3), tspec, tspec,
                  pl.BlockSpec((hps, blk, HEAD_DIM), lambda h, i: (h, nblk - 1 - i, 0))],
        operands=[proj, proj, proj, proj, dgated, o, car],
        out_specs=[pl.BlockSpec((N_CHIPS, blk, width), lambda h, i: (0, nblk - 1 - i, h))],
        out_shape=[jax.ShapeDtypeStruct((N_CHIPS, t, d), BF16)],
        scratch=[pltpu.VMEM((t, width), F32), pltpu.VMEM((t, width), F32),
                 pltpu.VMEM((hps, blk, HEAD_DIM), F32), pltpu.VMEM((hps, blk, HEAD_DIM), F32),
                 pltpu.VMEM((hps, blk, HEAD_DIM), BF16)], rider=rider)


def _pack_weights(w_in, w_out, chip):
    d = w_in.shape[0]
    rb = d // 8
    n_in = d // rb
    n_out = w_out.shape[0] // rb

    def body(chip_ref, wi_ref, wo_ref, o_ref):
        r = pl.program_id(0)

        @pl.when(r < n_in)
        def _():
            o_ref[...] = wi_ref[...].astype(BF16)

        @pl.when(r >= n_in)
        def _():
            o_ref[...] = wo_ref[...].astype(BF16)

    grid_spec = pltpu.PrefetchScalarGridSpec(
        num_scalar_prefetch=1, grid=(n_in + n_out,),
        in_specs=[pl.BlockSpec((rb, d), lambda r, me: (jnp.minimum(r, n_in - 1), 0)),
                  pl.BlockSpec((rb, d), lambda r, me: (jnp.maximum(r - n_in, 0), 0))],
        out_specs=pl.BlockSpec((None, rb, d), lambda r, me: (me[0], r, 0)))
    return pl.pallas_call(
        body, name="pack_weights", grid_spec=grid_spec,
        out_shape=jax.ShapeDtypeStruct((N_CHIPS, d + w_out.shape[0], d), BF16), compiler_params=_params(1),
    )(chip, w_in, w_out)


def _flip(v, bit):
    return 1 - v if bit else v


def _remote(src, dst, send_sem, recv_sem, target):
    return pltpu.make_async_remote_copy(src_ref=src, dst_ref=dst, send_sem=send_sem, recv_sem=recv_sem,
                                        device_id=target, device_id_type=MESH_IDS)


AG_CHUNKS = 8
AG_PLAN_CONV = {
    "mm_proj": dict(chunks=range(0, 5)),
    "conv_fwd": dict(chunks=range(5, 6), landed=range(0, 5)),
    "mm_out_norm": dict(chunks=range(6, 8), landed=range(5, 6), pass_now=range(6, 8)),
}
AG_PLAN_SB = {
    "mm_proj": dict(chunks=range(0, 4)),
    "sb_fwd": dict(chunks=range(4, 8), landed=range(0, 4)),
    "mm_out_norm": dict(landed=range(4, 8)),
}
HALF_CHUNKS = 8


SWAP_CHUNKS = 4


def _other_chips(x, y):
    return [(_flip(x, k >> 1), _flip(y, k & 1)) for k in (1, 2, 3)]


def _gather_sems(n):
    return [pltpu.SemaphoreType.DMA((3, n * AG_CHUNKS)) for _ in range(4)]


def _gather_pieces(g, chunks):
    hr = g[0].shape[1] // 2
    cr = hr // AG_CHUNKS
    return hr, [(l * AG_CHUNKS + q, g[l], q * cr, cr) for l in range(len(g)) for q in chunks]


def _pass_on(g, fsend, frecv, chunks):
    x, y, c = lax.axis_index("x"), lax.axis_index("y"), lax.axis_index("c")
    hr, pieces = _gather_pieces(g, chunks)
    for k, (px, py) in enumerate(_other_chips(x, y)):
        for i, ref, r0, cr in pieces:
            landed = ref.at[2 * px + py, pl.ds(c * hr + r0, cr)]
            _remote(landed, landed, fsend.at[k, i], frecv.at[k, i], (x, y, 1 - c)).start()


def _gather_start(g, send, recv, fsend, frecv, chunks=range(AG_CHUNKS), landed=()):
    x, y, c = lax.axis_index("x"), lax.axis_index("y"), lax.axis_index("c")
    hr, pieces = _gather_pieces(g, chunks)
    for k, (px, py) in enumerate(_other_chips(x, y)):
        for i, ref, r0, cr in pieces:
            piece = ref.at[2 * x + y, pl.ds(c * hr + r0, cr)]
            _remote(piece, piece, send.at[k, i], recv.at[k, i], (px, py, c)).start()
    _pass_on(g, fsend, frecv, landed)


def _gather_finish(g, send, recv, fsend, frecv, chunks=range(AG_CHUNKS), landed=(), pass_now=None):
    x, y, c = lax.axis_index("x"), lax.axis_index("y"), lax.axis_index("c")
    sibling = (x, y, 1 - c)
    pass_now = chunks if pass_now is None else pass_now
    hr, pieces = _gather_pieces(g, chunks)
    chips = _other_chips(x, y)
    for k, (px, py) in enumerate(chips):
        for i, ref, r0, cr in pieces:
            arrived = ref.at[2 * px + py, pl.ds(c * hr + r0, cr)]
            _remote(arrived, arrived, send.at[k, i], recv.at[k, i], (px, py, c)).wait_recv()
    _pass_on(g, fsend, frecv, pass_now)
    _, passed = _gather_pieces(g, list(landed) + list(pass_now))
    for k, (px, py) in enumerate(chips):
        for i, ref, r0, cr in passed:
            theirs = ref.at[2 * px + py, pl.ds((1 - c) * hr + r0, cr)]
            _remote(theirs, theirs, fsend.at[k, i], frecv.at[k, i], sibling).wait_recv()
    for k, (px, py) in enumerate(chips):
        for i, ref, r0, cr in pieces:
            mine = ref.at[2 * x + y, pl.ds(c * hr + r0, cr)]
            _remote(mine, mine, send.at[k, i], recv.at[k, i], (px, py, c)).wait_send()
        for i, ref, r0, cr in passed:
            mine = ref.at[2 * px + py, pl.ds(c * hr + r0, cr)]
            _remote(mine, mine, fsend.at[k, i], frecv.at[k, i], sibling).wait_send()


def _taps_rider(cw):
    def copies(ins, outs, sems):
        x, y, c = lax.axis_index("x"), lax.axis_index("y"), lax.axis_index("c")
        return [(_remote(ins[0], outs[0].at[2 * x + y], sems[0].at[k], sems[1].at[k], (px, py, c)),
                 _remote(ins[0], outs[0].at[2 * px + py], sems[0].at[k], sems[1].at[k], (px, py, c)))
                for k, (px, py) in enumerate(_other_chips(x, y))]

    def start(ins, outs, sems):
        pltpu.make_async_copy(ins[0], outs[0].at[2 * lax.axis_index("x") + lax.axis_index("y")], sems[2]).start()
        for mine, _ in copies(ins, outs, sems):
            mine.start()

    def finish(ins, outs, sems):
        for _, theirs in copies(ins, outs, sems):
            theirs.wait_recv()
        for mine, _ in copies(ins, outs, sems):
            mine.wait_send()
        pltpu.make_async_copy(ins[0], outs[0].at[2 * lax.axis_index("x") + lax.axis_index("y")], sems[2]).wait()

    return _Rider(
        operands=[cw], out_shapes=[jax.ShapeDtypeStruct((N_CHIPS,) + cw.shape, cw.dtype)], aliases={},
        sems=[pltpu.SemaphoreType.DMA((3,)), pltpu.SemaphoreType.DMA((3,)), pltpu.SemaphoreType.DMA],
        start=start, finish=finish)


def _gather_rider(packs, chunks=(), landed=(), pass_now=()):
    return _Rider(
        operands=packs, out_shapes=[jax.ShapeDtypeStruct(p.shape, BF16) for p in packs],
        aliases={l: l for l in range(len(packs))}, sems=_gather_sems(len(packs)),
        start=lambda ins, outs, sems: _gather_start(outs, *sems, chunks, landed),
        finish=lambda ins, outs, sems: _gather_finish(outs, *sems, chunks, landed, pass_now))


def _join_riders(a, b):
    na, oa, sa = len(a.operands), len(a.out_shapes), len(a.sems)
    aliases = dict(a.aliases)
    aliases.update({na + i: oa + o for i, o in b.aliases.items()})
    return _Rider(
        operands=list(a.operands) + list(b.operands), out_shapes=list(a.out_shapes) + list(b.out_shapes),
        aliases=aliases, sems=list(a.sems) + list(b.sems),
        start=lambda ins, outs, sems: (a.start(ins[:na], outs[:oa], sems[:sa]),
                                       b.start(ins[na:], outs[oa:], sems[sa:])),
        finish=lambda ins, outs, sems: (a.finish(ins[:na], outs[:oa], sems[:sa]),
                                        b.finish(ins[na:], outs[oa:], sems[sa:])))


def _comm_call(name, rider):
    n_in, n_out = len(rider.operands), len(rider.out_shapes)

    def body(*refs):
        ins, outs, sems = refs[:n_in], refs[n_in:n_in + n_out], refs[n_in + n_out:]
        rider.start(ins, outs, sems)
        rider.finish(ins, outs, sems)

    return pl.pallas_call(
        body, name=name, in_specs=[HBM_SPEC] * n_in, out_specs=[HBM_SPEC] * n_out, out_shape=list(rider.out_shapes),
        scratch_shapes=list(rider.sems), input_output_aliases=dict(rider.aliases),
    )(*rider.operands)


def _small_rider(small):
    def peers():
        x, y, c = lax.axis_index("x"), lax.axis_index("y"), lax.axis_index("c")
        return 4 * x + 2 * y + c, [(_flip(x, r >> 2), _flip(y, (r >> 1) & 1), _flip(c, r & 1)) for r in range(1, N_DEV)]

    def start(ins, outs, sems):
        me, others = peers()
        pltpu.make_async_copy(ins[0], outs[0].at[me], sems[2]).start()
        for r, peer in enumerate(others):
            _remote(ins[0], outs[0].at[me], sems[0].at[r], sems[1].at[r], peer).start()

    def finish(ins, outs, sems):
        me, others = peers()
        for r, (tx, ty, tc) in enumerate(others):
            _remote(ins[0], outs[0].at[4 * tx + 2 * ty + tc], sems[0].at[r], sems[1].at[r], (tx, ty, tc)).wait_recv()
        for r, peer in enumerate(others):
            _remote(ins[0], outs[0].at[me], sems[0].at[r], sems[1].at[r], peer).wait_send()
        pltpu.make_async_copy(ins[0], outs[0].at[me], sems[2]).wait()

    return _Rider(
        operands=[small], out_shapes=[jax.ShapeDtypeStruct((N_DEV,) + small.shape, small.dtype)], aliases={},
        sems=[pltpu.SemaphoreType.DMA((N_DEV - 1,)), pltpu.SemaphoreType.DMA((N_DEV - 1,)), pltpu.SemaphoreType.DMA],
        start=start, finish=finish)


SMALL_ROWS = 16


def _pack_small(ln_parts, conv_parts, loss_rows):
    d = ln_parts[0].shape[1]
    n_ln, n_conv = len(ln_parts), len(conv_parts)

    def body(*refs):
        o_ref = refs[-1]
        rows = lax.broadcasted_iota(jnp.int32, (SMALL_ROWS, d), 0)
        acc = jnp.zeros((SMALL_ROWS, d), F32)
        for i in range(n_ln):
            acc = jnp.where(rows == i, refs[i][0:1, :], acc)
        for j in range(n_conv):
            for k in range(3):
                acc = jnp.where(rows == n_ln + 3 * j + k, refs[n_ln + j][k:k + 1, :], acc)
        acc = jnp.where(rows == n_ln + 3 * n_conv, refs[n_ln + n_conv][0:1, :], acc)
        o_ref[...] = acc

    n = n_ln + n_conv + 1
    return pl.pallas_call(
        body, name="pack_small", out_shape=jax.ShapeDtypeStruct((SMALL_ROWS, d), F32),
        in_specs=[pl.BlockSpec(memory_space=pltpu.VMEM)] * n, out_specs=pl.BlockSpec(memory_space=pltpu.VMEM),
    )(*ln_parts, *conv_parts, loss_rows)


def _send_sums_start(s_ref, got, send, recv, part=0, parts=1, span=1):
    x, y, c = lax.axis_index("x"), lax.axis_index("y"), lax.axis_index("c")
    nrows = s_ref.shape[1] // parts
    rows = pl.ds(part * nrows, span * nrows)
    for k, (px, py) in enumerate(_other_chips(x, y)):
        _remote(s_ref.at[2 * px + py, rows], got.at[k, rows], send.at[k], recv.at[k], (px, py, c)).start()


def _send_sums_finish(s_ref, got, send, recv, part=0, parts=1, span=1):
    x, y, c = lax.axis_index("x"), lax.axis_index("y"), lax.axis_index("c")
    nrows = s_ref.shape[1] // parts
    rows = pl.ds(part * nrows, span * nrows)
    for k, (px, py) in enumerate(_other_chips(x, y)):
        _remote(got.at[k, rows], got.at[k, rows], send.at[k], recv.at[k], (px, py, c)).wait_recv()
    for k, (px, py) in enumerate(_other_chips(x, y)):
        _remote(s_ref.at[2 * px + py, rows], got.at[k, rows], send.at[k], recv.at[k], (px, py, c)).wait_send()


def _send_sums_rider(sums, got=None, part=0, parts=1, span=1):
    _, hr, d = sums.shape
    return _Rider(
        operands=[sums] if got is None else [sums, got],
        out_shapes=[jax.ShapeDtypeStruct((N_CHIPS - 1, hr, d), BF16)], aliases={} if got is None else {1: 0},
        sems=[pltpu.SemaphoreType.DMA((3,)), pltpu.SemaphoreType.DMA((3,))],
        start=lambda ins, outs, sems: _send_sums_start(ins[0], outs[0], *sems, part, parts, span),
        finish=lambda ins, outs, sems: _send_sums_finish(ins[0], outs[0], *sems, part, parts, span))


def _swap_pieces(gp_ref, x_ref, c):
    hr = x_ref.shape[1]
    cr = hr // SWAP_CHUNKS
    return [(a * SWAP_CHUNKS + q, gp_ref.at[a, pl.ds((1 - c) * hr + q * cr, cr)], x_ref.at[a, pl.ds(q * cr, cr)])
            for a in range(N_CHIPS) for q in range(SWAP_CHUNKS)]


def _swap_rider(gp):
    _, p_rows, d = gp.shape

    def start(ins, outs, sems):
        x, y, c = lax.axis_index("x"), lax.axis_index("y"), lax.axis_index("c")
        for i, src, dst in _swap_pieces(ins[0], outs[0], c):
            _remote(src, dst, sems[0].at[i], sems[1].at[i], (x, y, 1 - c)).start()

    def finish(ins, outs, sems):
        x, y, c = lax.axis_index("x"), lax.axis_index("y"), lax.axis_index("c")
        pieces = _swap_pieces(ins[0], outs[0], c)
        for i, src, dst in pieces:
            _remote(dst, dst, sems[0].at[i], sems[1].at[i], (x, y, 1 - c)).wait_recv()
        for i, src, dst in pieces:
            _remote(src, dst, sems[0].at[i], sems[1].at[i], (x, y, 1 - c)).wait_send()

    nsem = N_CHIPS * SWAP_CHUNKS
    return _Rider(
        operands=[gp], out_shapes=[jax.ShapeDtypeStruct((N_CHIPS, p_rows // 2, d), BF16)], aliases={},
        sems=[pltpu.SemaphoreType.DMA((nsem,)), pltpu.SemaphoreType.DMA((nsem,))], start=start, finish=finish)


def _presum(gp, theirs, core):
    _, hr, d = theirs.shape
    tr = _row_tile(hr, 640)
    steps = hr // tr

    def body(core_ref, mine_ref, theirs_ref, o_ref):
        o_ref[...] = (mine_ref[...].astype(F32) + theirs_ref[...].astype(F32)).astype(BF16)

    grid_spec = pltpu.PrefetchScalarGridSpec(
        num_scalar_prefetch=1, grid=(N_CHIPS, steps),
        in_specs=[pl.BlockSpec((None, tr, d), lambda a, i, cr: (a, cr[0] * steps + i, 0)),
                  pl.BlockSpec((None, tr, d), lambda a, i, cr: (a, i, 0))],
        out_specs=pl.BlockSpec((None, tr, d), lambda a, i, cr: (a, i, 0)))
    return pl.pallas_call(
        body, name="presum", grid_spec=grid_spec,
        out_shape=jax.ShapeDtypeStruct((N_CHIPS, hr, d), BF16), compiler_params=_params(2),
    )(core, gp, theirs)


def _row_tile(rows, cap=128):
    if rows <= cap:
        return rows
    return next(tr for tr in range(cap, 0, -16) if rows % tr == 0)


def _sum_sources(name, parts):
    nsrc, rows, cols = parts.shape
    tr = _row_tile(rows)

    def body(p_ref, o_ref):
        total = p_ref[0].astype(F32)
        for s in range(1, nsrc):
            total = total + p_ref[s].astype(F32)
        o_ref[...] = total

    return pl.pallas_call(
        body, name=name, grid=(rows // tr,),
        in_specs=[pl.BlockSpec((nsrc, tr, cols), lambda i: (0, i, 0))],
        out_specs=pl.BlockSpec((tr, cols), lambda i: (i, 0)),
        out_shape=jax.ShapeDtypeStruct((rows, cols), F32), compiler_params=_params(1),
    )(parts)


def _sum_grad_half(got, sums, place):
    nsrc, hr, d = got.shape
    tr = _row_tile(hr, 256)

    def body(place_ref, got_ref, own_ref, o_ref):
        total = own_ref[...].astype(F32)
        for s in range(nsrc):
            total = total + got_ref[s].astype(F32)
        o_ref[...] = total

    grid_spec = pltpu.PrefetchScalarGridSpec(
        num_scalar_prefetch=1, grid=(hr // tr,),
        in_specs=[pl.BlockSpec((nsrc, tr, d), lambda i, pc: (0, i, 0)),
                  pl.BlockSpec((None, tr, d), lambda i, pc: (pc[0], i, 0))],
        out_specs=pl.BlockSpec((None, tr, d), lambda i, pc: (pc[1], i, 0)))
    return pl.pallas_call(
        body, name="sum_grad_half", grid_spec=grid_spec,
        out_shape=jax.ShapeDtypeStruct((2, hr, d), F32), compiler_params=_params(1),
    )(place, got, sums)


def _halves_rider(full):
    _, hr, d = full.shape
    cr = hr // HALF_CHUNKS

    def pieces(ref, half):
        return [ref.at[half, pl.ds(q * cr, cr)] for q in range(HALF_CHUNKS)]

    def start(ins, outs, sems):
        x, y, c = lax.axis_index("x"), lax.axis_index("y"), lax.axis_index("c")
        for i, piece in enumerate(pieces(outs[0], c)):
            _remote(piece, piece, sems[0].at[i], sems[1].at[i], (x, y, 1 - c)).start()

    def finish(ins, outs, sems):
        x, y, c = lax.axis_index("x"), lax.axis_index("y"), lax.axis_index("c")
        for i, piece in enumerate(pieces(outs[0], 1 - c)):
            _remote(piece, piece, sems[0].at[i], sems[1].at[i], (x, y, 1 - c)).wait_recv()
        for i, piece in enumerate(pieces(outs[0], c)):
            _remote(piece, piece, sems[0].at[i], sems[1].at[i], (x, y, 1 - c)).wait_send()

    return _Rider(
        operands=[full], out_shapes=[jax.ShapeDtypeStruct(full.shape, F32)], aliases={0: 0},
        sems=[pltpu.SemaphoreType.DMA((HALF_CHUNKS,)), pltpu.SemaphoreType.DMA((HALF_CHUNKS,))],
        start=start, finish=finish)


def _adamw(name, w, m, v, g, g_row0=0):
    rows, cols = w.shape
    tr = _row_tile(rows, 256)
    off = g_row0 // tr

    def body(w_ref, m_ref, v_ref, g_ref, go_ref, d_ref, mo_ref, vo_ref):
        grad = g_ref[...]
        m_new = ADAM_B1 * m_ref[...] + (1.0 - ADAM_B1) * grad
        v_new = ADAM_B2 * v_ref[...] + (1.0 - ADAM_B2) * (grad * grad)
        m_hat = m_new / (1.0 - ADAM_B1 ** ADAM_STEP)
        v_hat = v_new / (1.0 - ADAM_B2 ** ADAM_STEP)
        go_ref[...] = grad
        d_ref[...] = -ADAM_LR * (m_hat / (jnp.sqrt(v_hat) + ADAM_EPS) + ADAM_WD * w_ref[...])
        mo_ref[...] = m_new
        vo_ref[...] = v_new

    blk = pl.BlockSpec((tr, cols), lambda i: (i, 0))
    return _compute_call(
        body, name, grid=(rows // tr,),
        in_specs=[blk, blk, blk, pl.BlockSpec((tr, cols), lambda i: (i + off, 0))], operands=[w, m, v, g],
        out_specs=[blk, blk, blk, blk], out_shape=[jax.ShapeDtypeStruct((rows, cols), F32)] * 4)


def _pad_rows8(a):
    return jnp.concatenate([a, jnp.zeros((8 - a.shape[0],) + a.shape[1:], a.dtype)], axis=0)


def kernel(x, ln_pre_0, conv_w_in_0, conv_w_0, conv_w_out_0, ln_post_0, ln_pre_1, sb_w_in_1, sb_w_out_1, ln_post_1, ln_pre_2, conv_w_in_2, conv_w_2, conv_w_out_2, ln_post_2, ln_pre_3, sb_w_in_3, sb_w_out_3, ln_post_3, loss_target, m_ln_pre_0, m_conv_w_in_0, m_conv_w_0, m_conv_w_out_0, m_ln_post_0, m_ln_pre_1, m_sb_w_in_1, m_sb_w_out_1, m_ln_post_1, m_ln_pre_2, m_conv_w_in_2, m_conv_w_2, m_conv_w_out_2, m_ln_post_2, m_ln_pre_3, m_sb_w_in_3, m_sb_w_out_3, m_ln_post_3, v_ln_pre_0, v_conv_w_in_0, v_conv_w_0, v_conv_w_out_0, v_ln_post_0, v_ln_pre_1, v_sb_w_in_1, v_sb_w_out_1, v_ln_post_1, v_ln_pre_2, v_conv_w_in_2, v_conv_w_2, v_conv_w_out_2, v_ln_post_2, v_ln_pre_3, v_sb_w_in_3, v_sb_w_out_3, v_ln_post_3):
    t, d = x.shape[1], x.shape[2]
    dq = d // N_CHIPS
    xs = x.reshape(t, d)
    target = loss_target.reshape(t, d)
    w_in = [conv_w_in_0, sb_w_in_1, conv_w_in_2, sb_w_in_3]
    w_out = [conv_w_out_0, sb_w_out_1, conv_w_out_2, sb_w_out_3]
    m_in = [m_conv_w_in_0, m_sb_w_in_1, m_conv_w_in_2, m_sb_w_in_3]
    m_out = [m_conv_w_out_0, m_sb_w_out_1, m_conv_w_out_2, m_sb_w_out_3]
    v_in = [v_conv_w_in_0, v_sb_w_in_1, v_conv_w_in_2, v_sb_w_in_3]
    v_out = [v_conv_w_out_0, v_sb_w_out_1, v_conv_w_out_2, v_sb_w_out_3]
    ln_pre = [ln_pre_0, ln_pre_1, ln_pre_2, ln_pre_3]
    ln_post = [ln_post_0, ln_post_1, ln_post_2, ln_post_3]
    conv_w = [conv_w_0, conv_w_2]
    m_conv = [m_conv_w_0, m_conv_w_2]
    v_conv = [v_conv_w_0, v_conv_w_2]
    chip = 2 * lax.axis_index("x") + lax.axis_index("y")
    chip_arr = jnp.reshape(chip, (1,)).astype(jnp.int32)
    place = jnp.stack([chip, lax.axis_index("c")]).astype(jnp.int32)
    core_arr = jnp.reshape(lax.axis_index("c"), (1,)).astype(jnp.int32)

    packs = [_pack_weights(w_in[l], w_out[l], chip_arr) for l in range(N_LAYERS)]
    cw_local = jnp.concatenate([_pad_rows8(conv_w[0]), _pad_rows8(conv_w[1])], axis=0)
    gathered = list(packs)
    every = range(AG_CHUNKS)
    u, gathered[0], cw_all = _norm_first(
        xs, ln_pre[0], rider=_join_riders(_gather_rider(packs[:1], chunks=every, pass_now=every), _taps_rider(cw_local)))
    cw_full = jnp.transpose(cw_all, (1, 0, 2)).reshape(16, d)
    conv_taps = {0: cw_full[0:8], 2: cw_full[8:16]}

    h_in, us, projs, gateds, ms, sb_saved = [], [], [], [], [], {}
    h = xs
    for l in range(N_LAYERS):
        h_in.append(h)
        us.append(u)
        nxt = l + 1
        plan = {} if nxt == N_LAYERS else (AG_PLAN_CONV if l % 2 == 0 else AG_PLAN_SB)

        def rider_for(name):
            return _gather_rider(gathered[nxt:nxt + 1], **plan[name]) if name in plan else None

        def take(results, name):
            if name not in plan:
                return results
            gathered[nxt] = results[-1]
            return results[:-1]

        proj, = take(_mm_proj(u, gathered[l], rider=rider_for("mm_proj")), "mm_proj")
        if l % 2 == 0:
            gated, = take(_conv_fwd(proj, conv_taps[l], rider=rider_for("conv_fwd")), "conv_fwd")
        else:
            gated, o, car = take(_sb_fwd(proj, rider=rider_for("sb_fwd")), "sb_fwd")
            sb_saved[l] = (o, car)
        if l < N_LAYERS - 1:
            m, h, u = take(_mm_out_norm(gated, gathered[l], h, ln_post[l], ln_pre[nxt], rider=rider_for("mm_out_norm")),
                           "mm_out_norm")
        else:
            m, = _mm_out(gated, gathered[l])
        projs.append(proj)
        gateds.append(gated)
        ms.append(m)
    dh, dm, dg_post_last, loss_part = _norm_last(h, ms[-1], ln_post[-1], target)

    dg_pre = [None] * N_LAYERS
    dg_post = [None] * N_LAYERS
    dg_post[N_LAYERS - 1] = dg_post_last
    dconv = {}
    sums = [None] * N_LAYERS
    got = [None] * N_LAYERS
    fulls = [None] * N_LAYERS

    def summed(layer):
        return _halves_rider(_sum_grad_half(got[layer], sums[layer], place))

    for l in reversed(range(N_LAYERS)):
        above = l + 1 if l + 1 < N_LAYERS else None
        if l == 1:
            dgated, fulls[3] = _mm_dgated(dm, gathered[l], rider=summed(3))
        else:
            dgated, = _mm_dgated(dm, gathered[l])
        if l == 0:
            gp, got[1] = _mm_dwout(gateds[l], dm, rider=_send_sums_rider(sums[1], None, 0, 4, 1))
        else:
            gp, = _mm_dwout(gateds[l], dm)
        if l % 2 == 0:
            dproj, dconv[l] = _conv_bwd(projs[l], dgated, conv_taps[l])
        elif above is not None:
            o, car = sb_saved[l]
            dproj, got[above] = _sb_bwd(projs[l], dgated, o, car, rider=_send_sums_rider(sums[above]))
        else:
            o, car = sb_saved[l]
            dproj, = _sb_bwd(projs[l], dgated, o, car)
        if l == 0:
            gp, got[1] = _mm_dwin(us[0], dproj, gp, rider=_send_sums_rider(sums[1], got[1], 1, 4, 3))
            theirs, = _comm_call("swap_last", _swap_rider(gp))
            sums[0] = _presum(gp, theirs, core_arr)
            du, got[0], fulls[1] = _mm_du(
                dproj, gathered[0], rider=_join_riders(_send_sums_rider(sums[0], None, 0, 4, 3), summed(1)))
        elif l % 2 == 0:
            gp, got[above] = _mm_dwin(us[l], dproj, gp, rider=_send_sums_rider(sums[above], None, 0, 2))
            du, theirs, got[above] = _mm_du(
                dproj, gathered[l], rider=_join_riders(_swap_rider(gp), _send_sums_rider(sums[above], got[above], 1, 2)))
            sums[l] = _presum(gp, theirs, core_arr)
        elif above is not None:
            gp, fulls[above] = _mm_dwin(us[l], dproj, gp, rider=summed(above))
            du, theirs = _mm_du(dproj, gathered[l], rider=_swap_rider(gp))
            sums[l] = _presum(gp, theirs, core_arr)
        else:
            gp, = _mm_dwin(us[l], dproj, gp)
            du, theirs = _mm_du(dproj, gathered[l], rider=_swap_rider(gp))
            sums[l] = _presum(gp, theirs, core_arr)
        if l > 0:
            dh, dm, dg_pre[l], dg_post[l - 1] = _norm_bwd_mid(dh, du, h_in[l], ln_pre[l], ms[l - 1], ln_post[l - 1])
    grad_x, dg_pre[0] = _norm_bwd_first(dh, du, h_in[0], ln_pre[0])

    loss_rows = jnp.pad(loss_part, ((0, 0), (0, d - loss_part.shape[1])))
    small = _pack_small(dg_pre + dg_post, [dconv[0], dconv[2]], loss_rows)
    got[0], small_all = _comm_call(
        "exchange_last", _join_riders(_send_sums_rider(sums[0], got[0], 3, 4, 1), _small_rider(small)))
    small_sum = _sum_sources("sum_small", small_all)

    fulls[0], = _comm_call("exchange_halves", summed(0))
    fulls = [f.reshape(d + dq, d) for f in fulls]
    res_in = [_adamw("adamw_w_in", w_in[l], m_in[l], v_in[l], fulls[l], 0) for l in range(N_LAYERS)]
    res_out = [_adamw("adamw_w_out", w_out[l], m_out[l], v_out[l], fulls[l], d) for l in range(N_LAYERS)]
    ln_all = ln_pre + ln_post
    ln_m = [m_ln_pre_0, m_ln_pre_1, m_ln_pre_2, m_ln_pre_3, m_ln_post_0, m_ln_post_1, m_ln_post_2, m_ln_post_3]
    ln_v = [v_ln_pre_0, v_ln_pre_1, v_ln_pre_2, v_ln_pre_3, v_ln_post_0, v_ln_post_1, v_ln_post_2, v_ln_post_3]
    res_ln = _adamw("adamw_ln", jnp.stack(ln_all), jnp.stack(ln_m), jnp.stack(ln_v), small_sum[0:2 * N_LAYERS])
    conv_g = [_pad_rows8(lax.dynamic_slice(small_sum, (2 * N_LAYERS + 3 * i, chip * dq), (3, dq))) for i in range(2)]
    stack8 = lambda a, b: jnp.concatenate([_pad_rows8(a), _pad_rows8(b)], axis=0)
    res_conv = _adamw("adamw_conv", stack8(*conv_w), stack8(*m_conv), stack8(*v_conv), jnp.concatenate(conv_g, axis=0))

    def leaf(kind, l, which):
        if kind == "ln_pre":
            return res_ln[which][l]
        if kind == "ln_post":
            return res_ln[which][N_LAYERS + l]
        if kind == "w_in":
            return res_in[l][which]
        if kind == "w_out":
            return res_out[l][which]
        return res_conv[which][8 * (l // 2):8 * (l // 2) + 3]

    order = []
    for l in range(N_LAYERS):
        order.append(("ln_pre", l))
        order.append(("w_in", l))
        if l % 2 == 0:
            order.append(("conv", l))
        order.append(("w_out", l))
        order.append(("ln_post", l))
    loss = small_sum[2 * N_LAYERS + 3 * 2, 0]
    outs = [loss, grad_x.reshape(1, t, d)]
    for which in range(4):
        outs.extend(leaf(kind, l, which) for kind, l in order)
    return tuple(outs)
```

```python
import functools
import math
from typing import Any, Callable, Mapping, NamedTuple, Sequence

import jax
import jax.numpy as jnp
from jax import lax
from jax.experimental import pallas as pl
from jax.experimental.pallas import tpu as pltpu

F32 = jnp.float32
BF16 = jnp.bfloat16

N_CHIPS = 4
N_DEV = 8
N_LAYERS = 4
HEAD_DIM = 128
RMS_EPS = 1e-6
ADAM_LR = 0.001
ADAM_B1 = 0.9
ADAM_B2 = 0.999
ADAM_EPS = 1e-08
ADAM_WD = 0.01
ADAM_STEP = 10

VMEM_LIMIT = 56 * 1024 * 1024
MESH_IDS = pl.DeviceIdType.MESH
HBM_SPEC = pl.BlockSpec(memory_space=pltpu.HBM)

NN = (((1,), (0,)), ((), ()))
NT = (((1,), (1,)), ((), ()))
TN = (((0,), (0,)), ((), ()))


def _params(n_axes):
    return pltpu.CompilerParams(dimension_semantics=("arbitrary",) * n_axes, vmem_limit_bytes=VMEM_LIMIT)


def _dot(a, b, dims):
    return lax.dot_general(a, b, dims, preferred_element_type=F32)


def _sigmoid(z):
    return 1.0 / (1.0 + jnp.exp(-z))


class _Rider(NamedTuple):
    operands: Sequence[Any]
    out_shapes: Sequence[Any]
    aliases: Mapping[int, int]
    sems: Sequence[Any]
    start: Callable
    finish: Callable


def _compute_call(body, name, *, grid, in_specs, operands, out_specs, out_shape, scratch=(), aliases=None, rider=None):
    in_specs, operands = list(in_specs), list(operands)
    out_specs, out_shape, scratch = list(out_specs), list(out_shape), list(scratch)
    aliases = dict(aliases or {})
    n_in, n_out, n_scratch = len(operands), len(out_shape), len(scratch)
    hosted = body
    if rider is not None:
        r_in, r_out = len(rider.operands), len(rider.out_shapes)
        aliases.update({n_in + i: n_out + o for i, o in rider.aliases.items()})

        def hosted(*refs):
            ins, refs = refs[:n_in], refs[n_in:]
            rider_ins, refs = refs[:r_in], refs[r_in:]
            outs, refs = refs[:n_out], refs[n_out:]
            rider_outs, refs = refs[:r_out], refs[r_out:]
            own_scratch, rider_sems = refs[:n_scratch], refs[n_scratch:]
            ids = [pl.program_id(axis) for axis in range(len(grid))]
            first = functools.reduce(jnp.logical_and, [i == 0 for i in ids])
            last = functools.reduce(jnp.logical_and, [i == g - 1 for i, g in zip(ids, grid)])

            @pl.when(first)
            def _():
                rider.start(rider_ins, rider_outs, rider_sems)

            body(*ins, *outs, *own_scratch)

            @pl.when(last)
            def _():
                rider.finish(rider_ins, rider_outs, rider_sems)

        in_specs += [HBM_SPEC] * r_in
        operands += list(rider.operands)
        out_specs += [HBM_SPEC] * r_out
        out_shape += list(rider.out_shapes)
        scratch += list(rider.sems)
    return pl.pallas_call(
        hosted, name=name, grid=grid, in_specs=in_specs, out_specs=out_specs, out_shape=out_shape,
        scratch_shapes=scratch, input_output_aliases=aliases, compiler_params=_params(len(grid)),
    )(*operands)


def _matmul(name, a, b, *, grid, a_spec, b_spec, o_spec, out_shape, dims, reduce_axis=None, acc_shape=None,
            alias_out=None, rider=None):
    out_dtype = out_shape.dtype
    direct = reduce_axis is not None and out_dtype == F32
    n_red = grid[reduce_axis] if reduce_axis is not None else 1

    def body(*refs):
        if alias_out is not None:
            refs = refs[1:]
        a_ref, b_ref, o_ref = refs[:3]
        if reduce_axis is None:
            o_ref[...] = _dot(a_ref[...], b_ref[...], dims).astype(out_dtype)
            return
        acc_ref = o_ref if direct else refs[3]
        k = pl.program_id(reduce_axis)

        @pl.when(k == 0)
        def _():
            acc_ref[...] = jnp.zeros_like(acc_ref)

        acc_ref[...] += _dot(a_ref[...], b_ref[...], dims)

        if not direct:
            @pl.when(k == n_red - 1)
            def _():
                o_ref[...] = acc_ref[...].astype(out_dtype)

    scratch = []
    if reduce_axis is not None and not direct:
        scratch = [pltpu.VMEM(acc_shape, F32)]
    in_specs = [a_spec, b_spec]
    operands = [a, b]
    aliases = {}
    if alias_out is not None:
        in_specs = [HBM_SPEC] + in_specs
        operands = [alias_out] + operands
        aliases = {0: 0}
    return _compute_call(body, name, grid=grid, in_specs=in_specs, operands=operands, out_specs=[o_spec],
                         out_shape=[out_shape], scratch=scratch, aliases=aliases, rider=rider)


def _mm_proj(u, g, rider=None):
    t, d = u.shape
    tm = min(512, t)
    return _matmul(
        "mm_proj", u, g, grid=(N_CHIPS, t // tm),
        a_spec=pl.BlockSpec((tm, d), lambda s, m: (m, 0)),
        b_spec=pl.BlockSpec((None, d, d), lambda s, m: (s, 0, 0)),
        o_spec=pl.BlockSpec((None, tm, d), lambda s, m: (s, m, 0)),
        out_shape=jax.ShapeDtypeStruct((N_CHIPS, t, d), BF16), dims=NN, rider=rider)


def _mm_dgated(dm, g, rider=None):
    t, d = dm.shape
    dq = d // N_CHIPS
    tm = min(512, t)

    def body(a_ref, b_ref, o_ref):
        a = a_ref[...]
        for s in range(N_CHIPS):
            o_ref[:, s * dq:(s + 1) * dq] = _dot(a, b_ref[s], NT).astype(BF16)

    return _compute_call(
        body, "mm_dgated", grid=(t // tm,),
        in_specs=[pl.BlockSpec((tm, d), lambda m: (m, 0)), pl.BlockSpec((N_CHIPS, dq, d), lambda m: (0, N_CHIPS, 0))],
        operands=[dm, g], out_specs=[pl.BlockSpec((tm, d), lambda m: (m, 0))],
        out_shape=[jax.ShapeDtypeStruct((t, d), BF16)], rider=rider)


def _mm_dwout(gated, dm, rider=None):
    t, d = gated.shape
    dq = d // N_CHIPS
    tk = min(1024, t)
    return _matmul(
        "mm_dwout", gated, dm, grid=(N_CHIPS, t // tk),
        a_spec=pl.BlockSpec((tk, dq), lambda s, k: (k, s)),
        b_spec=pl.BlockSpec((tk, d), lambda s, k: (k, 0)),
        o_spec=pl.BlockSpec((None, dq, d), lambda s, k: (s, N_CHIPS, 0)),
        out_shape=jax.ShapeDtypeStruct((N_CHIPS, d + dq, d), BF16), dims=TN, reduce_axis=1, acc_shape=(dq, d),
        rider=rider)


def _mm_du(dproj, g, rider=None):
    _, t, d = dproj.shape
    tm = min(512, t)
    return _matmul(
        "mm_du", dproj, g, grid=(t // tm, N_CHIPS),
        a_spec=pl.BlockSpec((None, tm, d), lambda m, s: (s, m, 0)),
        b_spec=pl.BlockSpec((None, d, d), lambda m, s: (s, 0, 0)),
        o_spec=pl.BlockSpec((tm, d), lambda m, s: (m, 0)),
        out_shape=jax.ShapeDtypeStruct((t, d), F32), dims=NT, reduce_axis=1, rider=rider)


def _mm_dwin(u, dproj, gp, rider=None):
    t, d = u.shape
    tmo = min(1024, d)
    tk = min(1024, t)
    return _matmul(
        "mm_dwin", u, dproj, grid=(N_CHIPS, d // tmo, t // tk),
        a_spec=pl.BlockSpec((tk, tmo), lambda s, mo, k: (k, mo)),
        b_spec=pl.BlockSpec((None, tk, d), lambda s, mo, k: (s, k, 0)),
        o_spec=pl.BlockSpec((None, tmo, d), lambda s, mo, k: (s, mo, 0)),
        out_shape=jax.ShapeDtypeStruct(gp.shape, BF16), dims=TN, reduce_axis=2, acc_shape=(tmo, d), alias_out=gp,
        rider=rider)


def _rms(v):
    r = lax.rsqrt(jnp.mean(v * v, axis=-1, keepdims=True) + RMS_EPS)
    return v * r, r


def _rms_bwd(dout, n, r, gain):
    dn = dout * gain
    return r * (dn - n * jnp.mean(dn * n, axis=-1, keepdims=True))


def _fold8(v):
    return jnp.sum(v.reshape(v.shape[0] // 8, 8, v.shape[1]), axis=0)


def _row0(total):
    rows = lax.broadcasted_iota(jnp.int32, total.shape, 0)
    return jnp.where(rows == 0, jnp.sum(total, axis=0, keepdims=True), 0.0)


def _norm_tile(t):
    return min(256, t)


def _norm_first(x, g_pre, rider=None):
    t, d = x.shape
    tr = _norm_tile(t)

    def body(x_ref, g_ref, u_ref):
        n, _ = _rms(x_ref[...])
        u_ref[...] = (n * g_ref[...]).astype(BF16)

    row = pl.BlockSpec((tr, d), lambda i: (i, 0))
    vec = pl.BlockSpec((1, d), lambda i: (0, 0))
    return _compute_call(
        body, "norm_first", grid=(t // tr,), in_specs=[row, vec], operands=[x, g_pre.reshape(1, d)],
        out_specs=[row], out_shape=[jax.ShapeDtypeStruct((t, d), BF16)], rider=rider)


def _mm_out_norm(gated, g, h, g_post, g_pre_next, rider=None):
    t, d = h.shape
    dq = d // N_CHIPS
    tm = _norm_tile(t)

    def body(a_ref, b_ref, h_ref, gp_ref, gn_ref, m_ref, hn_ref, u_ref):
        acc = _dot(a_ref[:, 0:dq], b_ref[0], NN)
        for s in range(1, N_CHIPS):
            acc = acc + _dot(a_ref[:, s * dq:(s + 1) * dq], b_ref[s], NN)
        m_ref[...] = acc
        n, _ = _rms(acc)
        hn = h_ref[...] + n * gp_ref[...]
        hn_ref[...] = hn
        n2, _ = _rms(hn)
        u_ref[...] = (n2 * gn_ref[...]).astype(BF16)

    row = pl.BlockSpec((tm, d), lambda i: (i, 0))
    vec = pl.BlockSpec((1, d), lambda i: (0, 0))
    return _compute_call(
        body, "mm_out_norm", grid=(t // tm,),
        in_specs=[row, pl.BlockSpec((N_CHIPS, dq, d), lambda i: (0, N_CHIPS, 0)), row, vec, vec],
        operands=[gated, g, h, g_post.reshape(1, d), g_pre_next.reshape(1, d)], out_specs=[row, row, row],
        out_shape=[jax.ShapeDtypeStruct((t, d), F32), jax.ShapeDtypeStruct((t, d), F32), jax.ShapeDtypeStruct((t, d), BF16)],
        rider=rider)


def _mm_out_last(gated, g, h, g_post, target):
    t, d = h.shape
    dq = d // N_CHIPS
    tr = _norm_tile(t)
    nsteps = t // tr

    def body(a_ref, b_ref, h_ref, gp_ref, tg_ref, dy_ref, dm_ref, dgp_ref, loss_ref, acc_g, acc_l):
        i = pl.program_id(0)

        @pl.when(i == 0)
        def _():
            acc_g[...] = jnp.zeros_like(acc_g)
            acc_l[...] = jnp.zeros_like(acc_l)

        m = _dot(a_ref[:, 0:dq], b_ref[0], NN)
        for s in range(1, N_CHIPS):
            m = m + _dot(a_ref[:, s * dq:(s + 1) * dq], b_ref[s], NN)
        gain = gp_ref[...]
        n, r = _rms(m)
        err = h_ref[...] + n * gain - tg_ref[...]
        dy = err / d
        dy_ref[...] = dy
        dm_ref[...] = _rms_bwd(dy, n, r, gain).astype(BF16)
        acc_g[...] += _fold8(dy * n)
        acc_l[...] += _fold8(err * err)

        @pl.when(i == nsteps - 1)
        def _():
            dgp_ref[...] = _row0(acc_g[...])
            loss_ref[...] = jnp.zeros((8, 128), F32) + (0.5 / d) * jnp.sum(acc_l[...])

    row = pl.BlockSpec((tr, d), lambda i: (i, 0))
    vec = pl.BlockSpec((1, d), lambda i: (0, 0))
    acc = pl.BlockSpec((8, d), lambda i: (0, 0))
    return pl.pallas_call(
        body, name="mm_out_last", grid=(nsteps,),
        in_specs=[row, pl.BlockSpec((N_CHIPS, dq, d), lambda i: (0, N_CHIPS, 0)), row, vec, row],
        out_specs=[row, row, acc, pl.BlockSpec((8, 128), lambda i: (0, 0))],
        out_shape=[jax.ShapeDtypeStruct((t, d), F32), jax.ShapeDtypeStruct((t, d), BF16),
                   jax.ShapeDtypeStruct((8, d), F32), jax.ShapeDtypeStruct((8, 128), F32)],
        scratch_shapes=[pltpu.VMEM((8, d), F32), pltpu.VMEM((8, d), F32)],
        compiler_params=_params(1),
    )(gated, g, h, g_post.reshape(1, d), target)


def _norm_bwd_mid(dh, du, h_in, g_pre, m_prev, g_post_prev):
    t, d = dh.shape
    tr = _norm_tile(t)
    nsteps = t // tr

    def body(dh_ref, du_ref, h_ref, gpre_ref, m_ref, gpost_ref, dhn_ref, dm_ref, dgpre_ref, dgpost_ref, acc_a, acc_b):
        i = pl.program_id(0)

        @pl.when(i == 0)
        def _():
            acc_a[...] = jnp.zeros_like(acc_a)
            acc_b[...] = jnp.zeros_like(acc_b)

        du_t = du_ref[...]
        n, r = _rms(h_ref[...])
        dhn = dh_ref[...] + _rms_bwd(du_t, n, r, gpre_ref[...])
        dhn_ref[...] = dhn
        acc_a[...] += _fold8(du_t * n)
        n2, r2 = _rms(m_ref[...])
        dm_ref[...] = _rms_bwd(dhn, n2, r2, gpost_ref[...]).astype(BF16)
        acc_b[...] += _fold8(dhn * n2)

        @pl.when(i == nsteps - 1)
        def _():
            dgpre_ref[...] = _row0(acc_a[...])
            dgpost_ref[...] = _row0(acc_b[...])

    row = pl.BlockSpec((tr, d), lambda i: (i, 0))
    vec = pl.BlockSpec((1, d), lambda i: (0, 0))
    acc = pl.BlockSpec((8, d), lambda i: (0, 0))
    return pl.pallas_call(
        body, name="norm_bwd_mid", grid=(nsteps,), in_specs=[row, row, row, vec, row, vec],
        out_specs=[row, row, acc, acc],
        out_shape=[jax.ShapeDtypeStruct((t, d), F32), jax.ShapeDtypeStruct((t, d), BF16),
                   jax.ShapeDtypeStruct((8, d), F32), jax.ShapeDtypeStruct((8, d), F32)],
        scratch_shapes=[pltpu.VMEM((8, d), F32), pltpu.VMEM((8, d), F32)],
        compiler_params=_params(1),
    )(dh, du, h_in, g_pre.reshape(1, d), m_prev, g_post_prev.reshape(1, d))


def _norm_bwd_first(dh, du, x, g_pre):
    t, d = dh.shape
    tr = _norm_tile(t)
    nsteps = t // tr

    def body(dh_ref, du_ref, x_ref, gpre_ref, dx_ref, dgpre_ref, acc_a):
        i = pl.program_id(0)

        @pl.when(i == 0)
        def _():
            acc_a[...] = jnp.zeros_like(acc_a)

        du_t = du_ref[...]
        n, r = _rms(x_ref[...])
        dx_ref[...] = dh_ref[...] + _rms_bwd(du_t, n, r, gpre_ref[...])
        acc_a[...] += _fold8(du_t * n)

        @pl.when(i == nsteps - 1)
        def _():
            dgpre_ref[...] = _row0(acc_a[...])

    row = pl.BlockSpec((tr, d), lambda i: (i, 0))
    vec = pl.BlockSpec((1, d), lambda i: (0, 0))
    acc = pl.BlockSpec((8, d), lambda i: (0, 0))
    return _compute_call(
        body, "norm_bwd_first", grid=(nsteps,), in_specs=[row, row, row, vec],
        operands=[dh, du, x, g_pre.reshape(1, d)], out_specs=[row, acc],
        out_shape=[jax.ShapeDtypeStruct((t, d), F32), jax.ShapeDtypeStruct((8, d), F32)],
        scratch=[pltpu.VMEM((8, d), F32)])


CONV_TC = 128
CONV_HALO = 16


def _conv_chunk(t):
    return min(512, t)


def _shift_down(v, steps, fill):
    rows = lax.broadcasted_iota(jnp.int32, v.shape, 0)
    out = pltpu.roll(v, steps, axis=0)
    for k in range(steps):
        out = jnp.where(rows == k, fill[CONV_HALO - steps + k:CONV_HALO - steps + k + 1, :], out)
    return out


def _shift_up(v, steps, fill):
    nrows = v.shape[0]
    rows = lax.broadcasted_iota(jnp.int32, v.shape, 0)
    out = pltpu.roll(v, nrows - steps, axis=0)
    for k in range(steps):
        out = jnp.where(rows == nrows - steps + k, fill[k:k + 1, :], out)
    return out


def _conv_fwd(proj, cw, rider=None):
    _, t, d = proj.shape
    chunk = _conv_chunk(t)

    def body(p_ref, w_ref, o_ref):
        w = w_ref[...]
        w0, w1, w2 = w[0:1, :], w[1:2, :], w[2:3, :]
        for ci in range(t // chunk):
            t0 = ci * chunk
            rows = pl.ds(t0, chunk)
            b = p_ref[0, rows, :].astype(F32)
            cx = p_ref[1, rows, :].astype(F32) * p_ref[2, rows, :].astype(F32)
            z = p_ref[3, rows, :].astype(F32)
            if ci == 0:
                prev = jnp.zeros((CONV_HALO, CONV_TC), F32)
            else:
                halo = pl.ds(t0 - CONV_HALO, CONV_HALO)
                prev = p_ref[1, halo, :].astype(F32) * p_ref[2, halo, :].astype(F32)
            conv = w2 * cx + w1 * _shift_down(cx, 1, prev) + w0 * _shift_down(cx, 2, prev)
            o_ref[rows, :] = (z * _sigmoid(z) * b * conv).astype(BF16)

    return _compute_call(
        body, "conv_fwd", grid=(d // CONV_TC,),
        in_specs=[pl.BlockSpec((N_CHIPS, t, CONV_TC), lambda j: (0, 0, j)), pl.BlockSpec((8, CONV_TC), lambda j: (0, j))],
        operands=[proj, cw], out_specs=[pl.BlockSpec((t, CONV_TC), lambda j: (0, j))],
        out_shape=[jax.ShapeDtypeStruct((t, d), BF16)], rider=rider)


def _conv_bwd(proj, dgated, cw):
    _, t, d = proj.shape
    chunk = _conv_chunk(t)
    nchunks = t // chunk

    def body(p_ref, dg_ref, w_ref, dp_ref, dw_ref):
        w = w_ref[...]
        w0, w1, w2 = w[0:1, :], w[1:2, :], w[2:3, :]
        dw0 = jnp.zeros((1, CONV_TC), F32)
        dw1 = jnp.zeros((1, CONV_TC), F32)
        dw2 = jnp.zeros((1, CONV_TC), F32)
        for ci in range(nchunks):
            t0 = ci * chunk
            rows = pl.ds(t0, chunk)
            b = p_ref[0, rows, :].astype(F32)
            c = p_ref[1, rows, :].astype(F32)
            xt = p_ref[2, rows, :].astype(F32)
            z = p_ref[3, rows, :].astype(F32)
            dg = dg_ref[rows, :].astype(F32)
            cx = c * xt
            if ci == 0:
                prev = jnp.zeros((CONV_HALO, CONV_TC), F32)
            else:
                halo = pl.ds(t0 - CONV_HALO, CONV_HALO)
                prev = p_ref[1, halo, :].astype(F32) * p_ref[2, halo, :].astype(F32)
            cx1 = _shift_down(cx, 1, prev)
            cx2 = _shift_down(cx, 2, prev)
            conv = w2 * cx + w1 * cx1 + w0 * cx2
            sig = _sigmoid(z)
            dy = dg * (z * sig)
            dconv = dy * b
            if ci == nchunks - 1:
                nxt = jnp.zeros((CONV_HALO, CONV_TC), F32)
            else:
                halo = pl.ds(t0 + chunk, CONV_HALO)
                zn = p_ref[3, halo, :].astype(F32)
                nxt = dg_ref[halo, :].astype(F32) * (zn * _sigmoid(zn)) * p_ref[0, halo, :].astype(F32)
            dcx = w2 * dconv + w1 * _shift_up(dconv, 1, nxt) + w0 * _shift_up(dconv, 2, nxt)
            dp_ref[0, rows, :] = (dy * conv).astype(BF16)
            dp_ref[1, rows, :] = (dcx * xt).astype(BF16)
            dp_ref[2, rows, :] = (dcx * c).astype(BF16)
            dp_ref[3, rows, :] = (dg * (b * conv) * (sig * (1.0 + z * (1.0 - sig)))).astype(BF16)
            dw0 = dw0 + jnp.sum(dconv * cx2, axis=0, keepdims=True)
            dw1 = dw1 + jnp.sum(dconv * cx1, axis=0, keepdims=True)
            dw2 = dw2 + jnp.sum(dconv * cx, axis=0, keepdims=True)
        taps = lax.broadcasted_iota(jnp.int32, (8, CONV_TC), 0)
        dw_ref[...] = jnp.where(taps == 0, dw0, jnp.where(taps == 1, dw1, jnp.where(taps == 2, dw2, 0.0)))

    return pl.pallas_call(
        body, name="conv_bwd", grid=(d // CONV_TC,),
        in_specs=[pl.BlockSpec((N_CHIPS, t, CONV_TC), lambda j: (0, 0, j)),
                  pl.BlockSpec((t, CONV_TC), lambda j: (0, j)),
                  pl.BlockSpec((8, CONV_TC), lambda j: (0, j))],
        out_specs=[pl.BlockSpec((N_CHIPS, t, CONV_TC), lambda j: (0, 0, j)), pl.BlockSpec((8, CONV_TC), lambda j: (0, j))],
        out_shape=[jax.ShapeDtypeStruct((N_CHIPS, t, d), BF16), jax.ShapeDtypeStruct((8, d), F32)],
        compiler_params=_params(1),
    )(proj, dgated, cw)


SB_DEAD_TAIL = -105.0
SB_COUNT_LANE = HEAD_DIM - 1


def _sb_block(t):
    return min(256, t)


def _split_dot(v, tri):
    hi = v.astype(BF16)
    lo = (v - hi.astype(F32)).astype(BF16)
    return _dot(hi, tri, NN) + _dot(lo, tri, NN)


SB_HEADS_PER_STEP = 4


def _sb_terms(s, diagonal):
    sp = jnp.maximum(s, 0.0) + jnp.log(1.0 + jnp.exp(-jnp.abs(s)))
    if not diagonal:
        return -sp, s - sp, sp, None
    mask = lax.broadcasted_iota(jnp.int32, s.shape, 1) < lax.broadcasted_iota(jnp.int32, s.shape, 0)
    return jnp.where(mask, -sp, 0.0), s - sp, sp, mask


def _masked(mask, v):
    return v if mask is None else jnp.where(mask, v, 0.0)


def _sb_fwd(proj, rider=None):
    _, t, d = proj.shape
    heads = d // HEAD_DIM
    blk = _sb_block(t)
    nblk = t // blk
    scale = 1.0 / math.sqrt(HEAD_DIM)

    hps = SB_HEADS_PER_STEP
    width = hps * HEAD_DIM

    def body(q_ref, k_ref, v_ref, z_ref, gated_ref, o_ref, car_ref, tail_ref, acc_ref):
        i = pl.program_id(1)
        r_i = lax.broadcasted_iota(jnp.int32, (blk, blk), 0)
        c_i = lax.broadcasted_iota(jnp.int32, (blk, blk), 1)
        tri_after = (r_i > c_i).astype(BF16)
        lanes = lax.broadcasted_iota(jnp.int32, (blk, HEAD_DIM), 1)

        tail_ref[...] = jnp.zeros_like(tail_ref)
        acc_ref[...] = jnp.zeros_like(acc_ref)
        car_ref[...] = jnp.zeros_like(car_ref)

        def visit(j, diagonal):
            krows = pl.ds(pl.multiple_of(j * blk, blk), blk)
            hcols = [pl.ds(hh * HEAD_DIM, HEAD_DIM) for hh in range(hps)]
            logits = [_dot(q_ref[:, c], k_ref[krows, c], NT) for c in hcols]
            terms = [_sb_terms(s * scale, diagonal) for s in logits]
            within = [_split_dot(keep, tri_after) for keep, _, _, _ in terms]
            top = None
            for hh, (keep, log_beta, _, mask) in enumerate(terms):
                tail_b = tail_ref[hh]
                w = _masked(mask, jnp.exp(log_beta + tail_b[:, 0:1] + within[hh]))
                acc_ref[hh] += _dot(w.astype(BF16), v_ref[krows, hcols[hh]], NN)
                car_ref[hh] = jnp.where(lanes == j, tail_b, car_ref[hh])
                tail_new = tail_b + jnp.sum(keep, axis=1, keepdims=True)
                tail_ref[hh] = tail_new
                top = jnp.max(tail_new) if top is None else jnp.maximum(top, jnp.max(tail_new))
            return top > SB_DEAD_TAIL

        def more(state):
            jj, live = state
            return jnp.logical_and(jj <= i, live)

        def step(state):
            jj, _ = state
            return jj + 1, visit(i - jj, False)

        visited, _ = lax.while_loop(more, step, (jnp.int32(1), visit(i, True)))
        for hh in range(hps):
            cols = pl.ds(hh * HEAD_DIM, HEAD_DIM)
            car_ref[hh] = jnp.where(lanes == SB_COUNT_LANE, visited.astype(F32), car_ref[hh])
            z = z_ref[:, cols].astype(F32)
            acc = acc_ref[hh]
            o_ref[:, cols] = acc.astype(BF16)
            gated_ref[:, cols] = (z * _sigmoid(z) * acc).astype(BF16)

    qspec = lambda s: pl.BlockSpec((None, blk, width), lambda h, i: (s, i, h))
    kspec = lambda s: pl.BlockSpec((None, t, width), lambda h, i: (s, 0, h))
    ospec = pl.BlockSpec((blk, width), lambda h, i: (i, h))
    return _compute_call(
        body, "sb_fwd", grid=(heads // hps, nblk),
        in_specs=[qspec(0), kspec(1), kspec(2), qspec(3)], operands=[proj, proj, proj, proj],
        out_specs=[ospec, ospec, pl.BlockSpec((hps, blk, HEAD_DIM), lambda h, i: (h, i, 0))],
        out_shape=[jax.ShapeDtypeStruct((t, d), BF16), jax.ShapeDtypeStruct((t, d), BF16),
                   jax.ShapeDtypeStruct((heads, t, HEAD_DIM), F32)],
        scratch=[pltpu.VMEM((hps, blk, HEAD_DIM), F32), pltpu.VMEM((hps, blk, HEAD_DIM), F32)], rider=rider)


def _sb_bwd(proj, dgated, o, car, rider=None):
    _, t, d = proj.shape
    heads = d // HEAD_DIM
    blk = _sb_block(t)
    nblk = t // blk
    scale = 1.0 / math.sqrt(HEAD_DIM)

    hps = SB_HEADS_PER_STEP
    width = hps * HEAD_DIM

    def body(q_ref, k_ref, v_ref, z_ref, dg_ref, o_ref, car_ref, dp_ref, dk_acc, dv_acc, gsum_ref, dq_ref, do_ref):
        step_i = pl.program_id(1)
        i = nblk - 1 - step_i

        @pl.when(step_i == 0)
        def _():
            dk_acc[...] = jnp.zeros_like(dk_acc)
            dv_acc[...] = jnp.zeros_like(dv_acc)

        r_i = lax.broadcasted_iota(jnp.int32, (blk, blk), 0)
        c_i = lax.broadcasted_iota(jnp.int32, (blk, blk), 1)
        tri_after = (r_i > c_i).astype(BF16)
        tri_before = (r_i < c_i).astype(BF16)
        lanes = lax.broadcasted_iota(jnp.int32, (blk, HEAD_DIM), 1)

        gsum_ref[...] = jnp.zeros_like(gsum_ref)
        dq_ref[...] = jnp.zeros_like(dq_ref)
        for hh in range(hps):
            cols = pl.ds(hh * HEAD_DIM, HEAD_DIM)
            z = z_ref[:, cols].astype(F32)
            dg = dg_ref[:, cols].astype(F32)
            sig = _sigmoid(z)
            do_ref[hh] = (dg * (z * sig)).astype(BF16)
            dp_ref[3, :, cols] = (dg * o_ref[:, cols].astype(F32) * (sig * (1.0 + z * (1.0 - sig)))).astype(BF16)

        def visit(j, diagonal):
            krows = pl.ds(pl.multiple_of(j * blk, blk), blk)
            hcols = [pl.ds(hh * HEAD_DIM, HEAD_DIM) for hh in range(hps)]
            logits = [_dot(q_ref[:, c], k_ref[krows, c], NT) for c in hcols]
            dws = [_dot(do_ref[hh], v_ref[krows, c], NT) for hh, c in enumerate(hcols)]
            terms = [_sb_terms(s * scale, diagonal) for s in logits]
            within = [_split_dot(keep, tri_after) for keep, _, _, _ in terms]
            ws, gs = [], []
            for hh, (keep, log_beta, sp, mask) in enumerate(terms):
                tail = jnp.sum(jnp.where(lanes == j, car_ref[hh], 0.0), axis=1, keepdims=True)
                w = _masked(mask, jnp.exp(log_beta + tail + within[hh]))
                ws.append(w.astype(BF16))
                gs.append(w * dws[hh])
            g_within = [_split_dot(g, tri_before) for g in gs]
            for hh, (keep, log_beta, sp, mask) in enumerate(terms):
                c = hcols[hh]
                g_before = gsum_ref[hh]
                g_cum = g_before[:, 0:1] + g_within[hh]
                dl = (_masked(mask, gs[hh] - (gs[hh] + g_cum) * jnp.exp(log_beta)) * scale).astype(BF16)
                dq_ref[hh] += _dot(dl, k_ref[krows, c], NN)
                dk_acc[krows, c] += _dot(dl, q_ref[:, c], TN)
                dv_acc[krows, c] += _dot(ws[hh], do_ref[hh], TN)
                gsum_ref[hh] = g_before + jnp.sum(gs[hh], axis=1, keepdims=True)

        def step(j, carry):
            visit(j, False)
            return carry

        visited = jnp.max(jnp.where(lanes == SB_COUNT_LANE, car_ref[0], 0.0)).astype(jnp.int32)
        lax.fori_loop(i + 1 - visited, i, step, 0)
        visit(i, True)
        own = pl.ds(pl.multiple_of(i * blk, blk), blk)
        for hh in range(hps):
            cols = pl.ds(hh * HEAD_DIM, HEAD_DIM)
            dp_ref[0, :, cols] = dq_ref[hh].astype(BF16)
        dp_ref[1] = dk_acc[own, :].astype(BF16)
        dp_ref[2] = dv_acc[own, :].astype(BF16)

    qspec = lambda s: pl.BlockSpec((None, blk, width), lambda h, i: (s, nblk - 1 - i, h))
    kspec = lambda s: pl.BlockSpec((None, t, width), lambda h, i: (s, 0, h))
    tspec = pl.BlockSpec((blk, width), lambda h, i: (nblk - 1 - i, h))
    return _compute_call(
        body, "sb_bwd", grid=(heads // hps, nblk),
        in_specs=[qspec(0), kspec(1), kspec(2), qspec(3), tspec, tspec,
                  pl.BlockSpec((hps, blk, HEAD_DIM), lambda h, i: (h, nblk - 1 - i, 0))],
        operands=[proj, proj, proj, proj, dgated, o, car],
        out_specs=[pl.BlockSpec((N_CHIPS, blk, width), lambda h, i: (0, nblk - 1 - i, h))],
        out_shape=[jax.ShapeDtypeStruct((N_CHIPS, t, d), BF16)],
        scratch=[pltpu.VMEM((t, width), F32), pltpu.VMEM((t, width), F32),
                 pltpu.VMEM((hps, blk, HEAD_DIM), F32), pltpu.VMEM((hps, blk, HEAD_DIM), F32),
                 pltpu.VMEM((hps, blk, HEAD_DIM), BF16)], rider=rider)


def _pack_weights(w_in, w_out, chip):
    d = w_in.shape[0]
    rb = d // 8
    n_in = d // rb
    n_out = w_out.shape[0] // rb

    def body(chip_ref, wi_ref, wo_ref, o_ref):
        r = pl.program_id(0)

        @pl.when(r < n_in)
        def _():
            o_ref[...] = wi_ref[...].astype(BF16)

        @pl.when(r >= n_in)
        def _():
            o_ref[...] = wo_ref[...].astype(BF16)

    grid_spec = pltpu.PrefetchScalarGridSpec(
        num_scalar_prefetch=1, grid=(n_in + n_out,),
        in_specs=[pl.BlockSpec((rb, d), lambda r, me: (jnp.minimum(r, n_in - 1), 0)),
                  pl.BlockSpec((rb, d), lambda r, me: (jnp.maximum(r - n_in, 0), 0))],
        out_specs=pl.BlockSpec((None, rb, d), lambda r, me: (me[0], r, 0)))
    return pl.pallas_call(
        body, name="pack_weights", grid_spec=grid_spec,
        out_shape=jax.ShapeDtypeStruct((N_CHIPS, d + w_out.shape[0], d), BF16), compiler_params=_params(1),
    )(chip, w_in, w_out)


def _flip(v, bit):
    return 1 - v if bit else v


def _remote(src, dst, send_sem, recv_sem, target):
    return pltpu.make_async_remote_copy(src_ref=src, dst_ref=dst, send_sem=send_sem, recv_sem=recv_sem,
                                        device_id=target, device_id_type=MESH_IDS)


AG_CHUNKS = 8
AG_PLAN_CONV = {
    "mm_proj": dict(chunks=range(0, 5)),
    "conv_fwd": dict(chunks=range(5, 6), landed=range(0, 5)),
    "mm_out_norm": dict(chunks=range(6, 8), landed=range(5, 6), pass_now=range(6, 8)),
}
AG_PLAN_SB = {
    "mm_proj": dict(chunks=range(0, 4)),
    "sb_fwd": dict(chunks=range(4, 8), landed=range(0, 4)),
    "mm_out_norm": dict(landed=range(4, 8)),
}
HALF_CHUNKS = 8


SWAP_CHUNKS = 4


def _other_chips(x, y):
    return [(_flip(x, k >> 1), _flip(y, k & 1)) for k in (1, 2, 3)]


def _gather_sems(n):
    return [pltpu.SemaphoreType.DMA((3, n * AG_CHUNKS)) for _ in range(4)]


def _gather_pieces(g, chunks):
    hr = g[0].shape[1] // 2
    cr = hr // AG_CHUNKS
    return hr, [(l * AG_CHUNKS + q, g[l], q * cr, cr) for l in range(len(g)) for q in chunks]


def _pass_on(g, fsend, frecv, chunks):
    x, y, c = lax.axis_index("x"), lax.axis_index("y"), lax.axis_index("c")
    hr, pieces = _gather_pieces(g, chunks)
    for k, (px, py) in enumerate(_other_chips(x, y)):
        for i, ref, r0, cr in pieces:
            landed = ref.at[2 * px + py, pl.ds(c * hr + r0, cr)]
            _remote(landed, landed, fsend.at[k, i], frecv.at[k, i], (x, y, 1 - c)).start()


def _gather_start(g, send, recv, fsend, frecv, chunks=range(AG_CHUNKS), landed=()):
    x, y, c = lax.axis_index("x"), lax.axis_index("y"), lax.axis_index("c")
    hr, pieces = _gather_pieces(g, chunks)
    for k, (px, py) in enumerate(_other_chips(x, y)):
        for i, ref, r0, cr in pieces:
            piece = ref.at[2 * x + y, pl.ds(c * hr + r0, cr)]
            _remote(piece, piece, send.at[k, i], recv.at[k, i], (px, py, c)).start()
    _pass_on(g, fsend, frecv, landed)


def _gather_finish(g, send, recv, fsend, frecv, chunks=range(AG_CHUNKS), landed=(), pass_now=None):
    x, y, c = lax.axis_index("x"), lax.axis_index("y"), lax.axis_index("c")
    sibling = (x, y, 1 - c)
    pass_now = chunks if pass_now is None else pass_now
    hr, pieces = _gather_pieces(g, chunks)
    chips = _other_chips(x, y)
    for k, (px, py) in enumerate(chips):
        for i, ref, r0, cr in pieces:
            arrived = ref.at[2 * px + py, pl.ds(c * hr + r0, cr)]
            _remote(arrived, arrived, send.at[k, i], recv.at[k, i], (px, py, c)).wait_recv()
    _pass_on(g, fsend, frecv, pass_now)
    _, passed = _gather_pieces(g, list(landed) + list(pass_now))
    for k, (px, py) in enumerate(chips):
        for i, ref, r0, cr in passed:
            theirs = ref.at[2 * px + py, pl.ds((1 - c) * hr + r0, cr)]
            _remote(theirs, theirs, fsend.at[k, i], frecv.at[k, i], sibling).wait_recv()
    for k, (px, py) in enumerate(chips):
        for i, ref, r0, cr in pieces:
            mine = ref.at[2 * x + y, pl.ds(c * hr + r0, cr)]
            _remote(mine, mine, send.at[k, i], recv.at[k, i], (px, py, c)).wait_send()
        for i, ref, r0, cr in passed:
            mine = ref.at[2 * px + py, pl.ds(c * hr + r0, cr)]
            _remote(mine, mine, fsend.at[k, i], frecv.at[k, i], sibling).wait_send()


def _taps_rider(cw):
    def copies(ins, outs, sems):
        x, y, c = lax.axis_index("x"), lax.axis_index("y"), lax.axis_index("c")
        return [(_remote(ins[0], outs[0].at[2 * x + y], sems[0].at[k], sems[1].at[k], (px, py, c)),
                 _remote(ins[0], outs[0].at[2 * px + py], sems[0].at[k], sems[1].at[k], (px, py, c)))
                for k, (px, py) in enumerate(_other_chips(x, y))]

    def start(ins, outs, sems):
        pltpu.make_async_copy(ins[0], outs[0].at[2 * lax.axis_index("x") + lax.axis_index("y")], sems[2]).start()
        for mine, _ in copies(ins, outs, sems):
            mine.start()

    def finish(ins, outs, sems):
        for _, theirs in copies(ins, outs, sems):
            theirs.wait_recv()
        for mine, _ in copies(ins, outs, sems):
            mine.wait_send()
        pltpu.make_async_copy(ins[0], outs[0].at[2 * lax.axis_index("x") + lax.axis_index("y")], sems[2]).wait()

    return _Rider(
        operands=[cw], out_shapes=[jax.ShapeDtypeStruct((N_CHIPS,) + cw.shape, cw.dtype)], aliases={},
        sems=[pltpu.SemaphoreType.DMA((3,)), pltpu.SemaphoreType.DMA((3,)), pltpu.SemaphoreType.DMA],
        start=start, finish=finish)


def _gather_rider(packs, chunks=(), landed=(), pass_now=()):
    return _Rider(
        operands=packs, out_shapes=[jax.ShapeDtypeStruct(p.shape, BF16) for p in packs],
        aliases={l: l for l in range(len(packs))}, sems=_gather_sems(len(packs)),
        start=lambda ins, outs, sems: _gather_start(outs, *sems, chunks, landed),
        finish=lambda ins, outs, sems: _gather_finish(outs, *sems, chunks, landed, pass_now))


def _join_riders(a, b):
    na, oa, sa = len(a.operands), len(a.out_shapes), len(a.sems)
    aliases = dict(a.aliases)
    aliases.update({na + i: oa + o for i, o in b.aliases.items()})
    return _Rider(
        operands=list(a.operands) + list(b.operands), out_shapes=list(a.out_shapes) + list(b.out_shapes),
        aliases=aliases, sems=list(a.sems) + list(b.sems),
        start=lambda ins, outs, sems: (a.start(ins[:na], outs[:oa], sems[:sa]),
                                       b.start(ins[na:], outs[oa:], sems[sa:])),
        finish=lambda ins, outs, sems: (a.finish(ins[:na], outs[:oa], sems[:sa]),
                                        b.finish(ins[na:], outs[oa:], sems[sa:])))


def _comm_call(name, rider):
    n_in, n_out = len(rider.operands), len(rider.out_shapes)

    def body(*refs):
        ins, outs, sems = refs[:n_in], refs[n_in:n_in + n_out], refs[n_in + n_out:]
        rider.start(ins, outs, sems)
        rider.finish(ins, outs, sems)

    return pl.pallas_call(
        body, name=name, in_specs=[HBM_SPEC] * n_in, out_specs=[HBM_SPEC] * n_out, out_shape=list(rider.out_shapes),
        scratch_shapes=list(rider.sems), input_output_aliases=dict(rider.aliases),
    )(*rider.operands)


def _small_rider(small):
    def peers():
        x, y, c = lax.axis_index("x"), lax.axis_index("y"), lax.axis_index("c")
        return 4 * x + 2 * y + c, [(_flip(x, r >> 2), _flip(y, (r >> 1) & 1), _flip(c, r & 1)) for r in range(1, N_DEV)]

    def start(ins, outs, sems):
        me, others = peers()
        pltpu.make_async_copy(ins[0], outs[0].at[me], sems[2]).start()
        for r, peer in enumerate(others):
            _remote(ins[0], outs[0].at[me], sems[0].at[r], sems[1].at[r], peer).start()

    def finish(ins, outs, sems):
        me, others = peers()
        for r, (tx, ty, tc) in enumerate(others):
            _remote(ins[0], outs[0].at[4 * tx + 2 * ty + tc], sems[0].at[r], sems[1].at[r], (tx, ty, tc)).wait_recv()
        for r, peer in enumerate(others):
            _remote(ins[0], outs[0].at[me], sems[0].at[r], sems[1].at[r], peer).wait_send()
        pltpu.make_async_copy(ins[0], outs[0].at[me], sems[2]).wait()

    return _Rider(
        operands=[small], out_shapes=[jax.ShapeDtypeStruct((N_DEV,) + small.shape, small.dtype)], aliases={},
        sems=[pltpu.SemaphoreType.DMA((N_DEV - 1,)), pltpu.SemaphoreType.DMA((N_DEV - 1,)), pltpu.SemaphoreType.DMA],
        start=start, finish=finish)


SMALL_ROWS = 16


def _pack_small(ln_parts, conv_parts, loss_rows):
    d = ln_parts[0].shape[1]
    n_ln, n_conv = len(ln_parts), len(conv_parts)

    def body(*refs):
        o_ref = refs[-1]
        rows = lax.broadcasted_iota(jnp.int32, (SMALL_ROWS, d), 0)
        acc = jnp.zeros((SMALL_ROWS, d), F32)
        for i in range(n_ln):
            acc = jnp.where(rows == i, refs[i][0:1, :], acc)
        for j in range(n_conv):
            for k in range(3):
                acc = jnp.where(rows == n_ln + 3 * j + k, refs[n_ln + j][k:k + 1, :], acc)
        acc = jnp.where(rows == n_ln + 3 * n_conv, refs[n_ln + n_conv][0:1, :], acc)
        o_ref[...] = acc

    n = n_ln + n_conv + 1
    return pl.pallas_call(
        body, name="pack_small", out_shape=jax.ShapeDtypeStruct((SMALL_ROWS, d), F32),
        in_specs=[pl.BlockSpec(memory_space=pltpu.VMEM)] * n, out_specs=pl.BlockSpec(memory_space=pltpu.VMEM),
    )(*ln_parts, *conv_parts, loss_rows)


def _send_sums_start(s_ref, got, send, recv, part=0, parts=1, span=1):
    x, y, c = lax.axis_index("x"), lax.axis_index("y"), lax.axis_index("c")
    nrows = s_ref.shape[1] // parts
    rows = pl.ds(part * nrows, span * nrows)
    for k, (px, py) in enumerate(_other_chips(x, y)):
        _remote(s_ref.at[2 * px + py, rows], got.at[k, rows], send.at[k], recv.at[k], (px, py, c)).start()


def _send_sums_finish(s_ref, got, send, recv, part=0, parts=1, span=1):
    x, y, c = lax.axis_index("x"), lax.axis_index("y"), lax.axis_index("c")
    nrows = s_ref.shape[1] // parts
    rows = pl.ds(part * nrows, span * nrows)
    for k, (px, py) in enumerate(_other_chips(x, y)):
        _remote(got.at[k, rows], got.at[k, rows], send.at[k], recv.at[k], (px, py, c)).wait_recv()
    for k, (px, py) in enumerate(_other_chips(x, y)):
        _remote(s_ref.at[2 * px + py, rows], got.at[k, rows], send.at[k], recv.at[k], (px, py, c)).wait_send()


def _send_sums_rider(sums, got=None, part=0, parts=1, span=1):
    _, hr, d = sums.shape
    return _Rider(
        operands=[sums] if got is None else [sums, got],
        out_shapes=[jax.ShapeDtypeStruct((N_CHIPS - 1, hr, d), BF16)], aliases={} if got is None else {1: 0},
        sems=[pltpu.SemaphoreType.DMA((3,)), pltpu.SemaphoreType.DMA((3,))],
        start=lambda ins, outs, sems: _send_sums_start(ins[0], outs[0], *sems, part, parts, span),
        finish=lambda ins, outs, sems: _send_sums_finish(ins[0], outs[0], *sems, part, parts, span))


def _swap_pieces(gp_ref, x_ref, c):
    hr = x_ref.shape[1]
    cr = hr // SWAP_CHUNKS
    return [(a * SWAP_CHUNKS + q, gp_ref.at[a, pl.ds((1 - c) * hr + q * cr, cr)], x_ref.at[a, pl.ds(q * cr, cr)])
            for a in range(N_CHIPS) for q in range(SWAP_CHUNKS)]


def _swap_rider(gp):
    _, p_rows, d = gp.shape

    def start(ins, outs, sems):
        x, y, c = lax.axis_index("x"), lax.axis_index("y"), lax.axis_index("c")
        for i, src, dst in _swap_pieces(ins[0], outs[0], c):
            _remote(src, dst, sems[0].at[i], sems[1].at[i], (x, y, 1 - c)).start()

    def finish(ins, outs, sems):
        x, y, c = lax.axis_index("x"), lax.axis_index("y"), lax.axis_index("c")
        pieces = _swap_pieces(ins[0], outs[0], c)
        for i, src, dst in pieces:
            _remote(dst, dst, sems[0].at[i], sems[1].at[i], (x, y, 1 - c)).wait_recv()
        for i, src, dst in pieces:
            _remote(src, dst, sems[0].at[i], sems[1].at[i], (x, y, 1 - c)).wait_send()

    nsem = N_CHIPS * SWAP_CHUNKS
    return _Rider(
        operands=[gp], out_shapes=[jax.ShapeDtypeStruct((N_CHIPS, p_rows // 2, d), BF16)], aliases={},
        sems=[pltpu.SemaphoreType.DMA((nsem,)), pltpu.SemaphoreType.DMA((nsem,))], start=start, finish=finish)


def _presum(gp, theirs, core):
    _, hr, d = theirs.shape
    tr = _row_tile(hr, 640)
    steps = hr // tr

    def body(core_ref, mine_ref, theirs_ref, o_ref):
        o_ref[...] = (mine_ref[...].astype(F32) + theirs_ref[...].astype(F32)).astype(BF16)

    grid_spec = pltpu.PrefetchScalarGridSpec(
        num_scalar_prefetch=1, grid=(N_CHIPS, steps),
        in_specs=[pl.BlockSpec((None, tr, d), lambda a, i, cr: (a, cr[0] * steps + i, 0)),
                  pl.BlockSpec((None, tr, d), lambda a, i, cr: (a, i, 0))],
        out_specs=pl.BlockSpec((None, tr, d), lambda a, i, cr: (a, i, 0)))
    return pl.pallas_call(
        body, name="presum", grid_spec=grid_spec,
        out_shape=jax.ShapeDtypeStruct((N_CHIPS, hr, d), BF16), compiler_params=_params(2),
    )(core, gp, theirs)


def _row_tile(rows, cap=128):
    if rows <= cap:
        return rows
    return next(tr for tr in range(cap, 0, -16) if rows % tr == 0)


def _sum_sources(name, parts):
    nsrc, rows, cols = parts.shape
    tr = _row_tile(rows)

    def body(p_ref, o_ref):
        total = p_ref[0].astype(F32)
        for s in range(1, nsrc):
            total = total + p_ref[s].astype(F32)
        o_ref[...] = total

    return pl.pallas_call(
        body, name=name, grid=(rows // tr,),
        in_specs=[pl.BlockSpec((nsrc, tr, cols), lambda i: (0, i, 0))],
        out_specs=pl.BlockSpec((tr, cols), lambda i: (i, 0)),
        out_shape=jax.ShapeDtypeStruct((rows, cols), F32), compiler_params=_params(1),
    )(parts)


def _sum_grad_half(got, sums, place):
    nsrc, hr, d = got.shape
    tr = _row_tile(hr, 256)

    def body(place_ref, got_ref, own_ref, o_ref):
        total = own_ref[...].astype(F32)
        for s in range(nsrc):
            total = total + got_ref[s].astype(F32)
        o_ref[...] = total

    grid_spec = pltpu.PrefetchScalarGridSpec(
        num_scalar_prefetch=1, grid=(hr // tr,),
        in_specs=[pl.BlockSpec((nsrc, tr, d), lambda i, pc: (0, i, 0)),
                  pl.BlockSpec((None, tr, d), lambda i, pc: (pc[0], i, 0))],
        out_specs=pl.BlockSpec((None, tr, d), lambda i, pc: (pc[1], i, 0)))
    return pl.pallas_call(
        body, name="sum_grad_half", grid_spec=grid_spec,
        out_shape=jax.ShapeDtypeStruct((2, hr, d), F32), compiler_params=_params(1),
    )(place, got, sums)


def _halves_rider(full):
    _, hr, d = full.shape
    cr = hr // HALF_CHUNKS

    def pieces(ref, half):
        return [ref.at[half, pl.ds(q * cr, cr)] for q in range(HALF_CHUNKS)]

    def start(ins, outs, sems):
        x, y, c = lax.axis_index("x"), lax.axis_index("y"), lax.axis_index("c")
        for i, piece in enumerate(pieces(outs[0], c)):
            _remote(piece, piece, sems[0].at[i], sems[1].at[i], (x, y, 1 - c)).start()

    def finish(ins, outs, sems):
        x, y, c = lax.axis_index("x"), lax.axis_index("y"), lax.axis_index("c")
        for i, piece in enumerate(pieces(outs[0], 1 - c)):
            _remote(piece, piece, sems[0].at[i], sems[1].at[i], (x, y, 1 - c)).wait_recv()
        for i, piece in enumerate(pieces(outs[0], c)):
            _remote(piece, piece, sems[0].at[i], sems[1].at[i], (x, y, 1 - c)).wait_send()

    return _Rider(
        operands=[full], out_shapes=[jax.ShapeDtypeStruct(full.shape, F32)], aliases={0: 0},
        sems=[pltpu.SemaphoreType.DMA((HALF_CHUNKS,)), pltpu.SemaphoreType.DMA((HALF_CHUNKS,))],
        start=start, finish=finish)


def _adamw(name, w, m, v, g, g_row0=0):
    rows, cols = w.shape
    tr = _row_tile(rows, 256)
    off = g_row0 // tr

    def body(w_ref, m_ref, v_ref, g_ref, go_ref, d_ref, mo_ref, vo_ref):
        grad = g_ref[...]
        m_new = ADAM_B1 * m_ref[...] + (1.0 - ADAM_B1) * grad
        v_new = ADAM_B2 * v_ref[...] + (1.0 - ADAM_B2) * (grad * grad)
        m_hat = m_new / (1.0 - ADAM_B1 ** ADAM_STEP)
        v_hat = v_new / (1.0 - ADAM_B2 ** ADAM_STEP)
        go_ref[...] = grad
        d_ref[...] = -ADAM_LR * (m_hat / (jnp.sqrt(v_hat) + ADAM_EPS) + ADAM_WD * w_ref[...])
        mo_ref[...] = m_new
        vo_ref[...] = v_new

    blk = pl.BlockSpec((tr, cols), lambda i: (i, 0))
    return _compute_call(
        body, name, grid=(rows // tr,),
        in_specs=[blk, blk, blk, pl.BlockSpec((tr, cols), lambda i: (i + off, 0))], operands=[w, m, v, g],
        out_specs=[blk, blk, blk, blk], out_shape=[jax.ShapeDtypeStruct((rows, cols), F32)] * 4)


def _pad_rows8(a):
    return jnp.concatenate([a, jnp.zeros((8 - a.shape[0],) + a.shape[1:], a.dtype)], axis=0)


def kernel(x, ln_pre_0, conv_w_in_0, conv_w_0, conv_w_out_0, ln_post_0, ln_pre_1, sb_w_in_1, sb_w_out_1, ln_post_1, ln_pre_2, conv_w_in_2, conv_w_2, conv_w_out_2, ln_post_2, ln_pre_3, sb_w_in_3, sb_w_out_3, ln_post_3, loss_target, m_ln_pre_0, m_conv_w_in_0, m_conv_w_0, m_conv_w_out_0, m_ln_post_0, m_ln_pre_1, m_sb_w_in_1, m_sb_w_out_1, m_ln_post_1, m_ln_pre_2, m_conv_w_in_2, m_conv_w_2, m_conv_w_out_2, m_ln_post_2, m_ln_pre_3, m_sb_w_in_3, m_sb_w_out_3, m_ln_post_3, v_ln_pre_0, v_conv_w_in_0, v_conv_w_0, v_conv_w_out_0, v_ln_post_0, v_ln_pre_1, v_sb_w_in_1, v_sb_w_out_1, v_ln_post_1, v_ln_pre_2, v_conv_w_in_2, v_conv_w_2, v_conv_w_out_2, v_ln_post_2, v_ln_pre_3, v_sb_w_in_3, v_sb_w_out_3, v_ln_post_3):
    t, d = x.shape[1], x.shape[2]
    dq = d // N_CHIPS
    xs = x.reshape(t, d)
    target = loss_target.reshape(t, d)
    w_in = [conv_w_in_0, sb_w_in_1, conv_w_in_2, sb_w_in_3]
    w_out = [conv_w_out_0, sb_w_out_1, conv_w_out_2, sb_w_out_3]
    m_in = [m_conv_w_in_0, m_sb_w_in_1, m_conv_w_in_2, m_sb_w_in_3]
    m_out = [m_conv_w_out_0, m_sb_w_out_1, m_conv_w_out_2, m_sb_w_out_3]
    v_in = [v_conv_w_in_0, v_sb_w_in_1, v_conv_w_in_2, v_sb_w_in_3]
    v_out = [v_conv_w_out_0, v_sb_w_out_1, v_conv_w_out_2, v_sb_w_out_3]
    ln_pre = [ln_pre_0, ln_pre_1, ln_pre_2, ln_pre_3]
    ln_post = [ln_post_0, ln_post_1, ln_post_2, ln_post_3]
    conv_w = [conv_w_0, conv_w_2]
    m_conv = [m_conv_w_0, m_conv_w_2]
    v_conv = [v_conv_w_0, v_conv_w_2]
    chip = 2 * lax.axis_index("x") + lax.axis_index("y")
    chip_arr = jnp.reshape(chip, (1,)).astype(jnp.int32)
    place = jnp.stack([chip, lax.axis_index("c")]).astype(jnp.int32)
    core_arr = jnp.reshape(lax.axis_index("c"), (1,)).astype(jnp.int32)

    packs = [_pack_weights(w_in[l], w_out[l], chip_arr) for l in range(N_LAYERS)]
    cw_local = jnp.concatenate([_pad_rows8(conv_w[0]), _pad_rows8(conv_w[1])], axis=0)
    gathered = list(packs)
    every = range(AG_CHUNKS)
    u, gathered[0], cw_all = _norm_first(
        xs, ln_pre[0], rider=_join_riders(_gather_rider(packs[:1], chunks=every, pass_now=every), _taps_rider(cw_local)))
    cw_full = jnp.transpose(cw_all, (1, 0, 2)).reshape(16, d)
    conv_taps = {0: cw_full[0:8], 2: cw_full[8:16]}

    h_in, us, projs, gateds, ms, sb_saved = [], [], [], [], [], {}
    h = xs
    for l in range(N_LAYERS):
        h_in.append(h)
        us.append(u)
        nxt = l + 1
        plan = {} if nxt == N_LAYERS else (AG_PLAN_CONV if l % 2 == 0 else AG_PLAN_SB)

        def rider_for(name):
            return _gather_rider(gathered[nxt:nxt + 1], **plan[name]) if name in plan else None

        def take(results, name):
            if name not in plan:
                return results
            gathered[nxt] = results[-1]
            return results[:-1]

        proj, = take(_mm_proj(u, gathered[l], rider=rider_for("mm_proj")), "mm_proj")
        if l % 2 == 0:
            gated, = take(_conv_fwd(proj, conv_taps[l], rider=rider_for("conv_fwd")), "conv_fwd")
        else:
            gated, o, car = take(_sb_fwd(proj, rider=rider_for("sb_fwd")), "sb_fwd")
            sb_saved[l] = (o, car)
        if l < N_LAYERS - 1:
            m, h, u = take(_mm_out_norm(gated, gathered[l], h, ln_post[l], ln_pre[nxt], rider=rider_for("mm_out_norm")),
                           "mm_out_norm")
            ms.append(m)
        projs.append(proj)
        gateds.append(gated)
    dh, dm, dg_post_last, loss_part = _mm_out_last(gateds[-1], gathered[-1], h, ln_post[-1], target)

    dg_pre = [None] * N_LAYERS
    dg_post = [None] * N_LAYERS
    dg_post[N_LAYERS - 1] = dg_post_last
    dconv = {}
    sums = [None] * N_LAYERS
    got = [None] * N_LAYERS
    fulls = [None] * N_LAYERS

    def summed(layer):
        return _halves_rider(_sum_grad_half(got[layer], sums[layer], place))

    for l in reversed(range(N_LAYERS)):
        above = l + 1 if l + 1 < N_LAYERS else None
        if l == 1:
            dgated, fulls[3] = _mm_dgated(dm, gathered[l], rider=summed(3))
        else:
            dgated, = _mm_dgated(dm, gathered[l])
        if l == 0:
            gp, got[1] = _mm_dwout(gateds[l], dm, rider=_send_sums_rider(sums[1], None, 0, 4, 1))
        else:
            gp, = _mm_dwout(gateds[l], dm)
        if l % 2 == 0:
            dproj, dconv[l] = _conv_bwd(projs[l], dgated, conv_taps[l])
        elif above is not None:
            o, car = sb_saved[l]
            dproj, got[above] = _sb_bwd(projs[l], dgated, o, car, rider=_send_sums_rider(sums[above]))
        else:
            o, car = sb_saved[l]
            dproj, = _sb_bwd(projs[l], dgated, o, car)
        if l == 0:
            gp, got[1] = _mm_dwin(us[0], dproj, gp, rider=_send_sums_rider(sums[1], got[1], 1, 4, 3))
            theirs, = _comm_call("swap_last", _swap_rider(gp))
            sums[0] = _presum(gp, theirs, core_arr)
            du, got[0], fulls[1] = _mm_du(
                dproj, gathered[0], rider=_join_riders(_send_sums_rider(sums[0], None, 0, 4, 3), summed(1)))
        elif l % 2 == 0:
            gp, got[above] = _mm_dwin(us[l], dproj, gp, rider=_send_sums_rider(sums[above], None, 0, 2))
            du, theirs, got[above] = _mm_du(
                dproj, gathered[l], rider=_join_riders(_swap_rider(gp), _send_sums_rider(sums[above], got[above], 1, 2)))
            sums[l] = _presum(gp, theirs, core_arr)
        elif above is not None:
            gp, fulls[above] = _mm_dwin(us[l], dproj, gp, rider=summed(above))
            du, theirs = _mm_du(dproj, gathered[l], rider=_swap_rider(gp))
            sums[l] = _presum(gp, theirs, core_arr)
        else:
            gp, = _mm_dwin(us[l], dproj, gp)
            du, theirs = _mm_du(dproj, gathered[l], rider=_swap_rider(gp))
            sums[l] = _presum(gp, theirs, core_arr)
        if l > 0:
            dh, dm, dg_pre[l], dg_post[l - 1] = _norm_bwd_mid(dh, du, h_in[l], ln_pre[l], ms[l - 1], ln_post[l - 1])
    grad_x, dg_pre[0] = _norm_bwd_first(dh, du, h_in[0], ln_pre[0])

    loss_rows = jnp.pad(loss_part, ((0, 0), (0, d - loss_part.shape[1])))
    small = _pack_small(dg_pre + dg_post, [dconv[0], dconv[2]], loss_rows)
    got[0], small_all = _comm_call(
        "exchange_last", _join_riders(_send_sums_rider(sums[0], got[0], 3, 4, 1), _small_rider(small)))
    small_sum = _sum_sources("sum_small", small_all)

    fulls[0], = _comm_call("exchange_halves", summed(0))
    fulls = [f.reshape(d + dq, d) for f in fulls]
    res_in = [_adamw("adamw_w_in", w_in[l], m_in[l], v_in[l], fulls[l], 0) for l in range(N_LAYERS)]
    res_out = [_adamw("adamw_w_out", w_out[l], m_out[l], v_out[l], fulls[l], d) for l in range(N_LAYERS)]
    ln_all = ln_pre + ln_post
    ln_m = [m_ln_pre_0, m_ln_pre_1, m_ln_pre_2, m_ln_pre_3, m_ln_post_0, m_ln_post_1, m_ln_post_2, m_ln_post_3]
    ln_v = [v_ln_pre_0, v_ln_pre_1, v_ln_pre_2, v_ln_pre_3, v_ln_post_0, v_ln_post_1, v_ln_post_2, v_ln_post_3]
    res_ln = _adamw("adamw_ln", jnp.stack(ln_all), jnp.stack(ln_m), jnp.stack(ln_v), small_sum[0:2 * N_LAYERS])
    conv_g = [_pad_rows8(lax.dynamic_slice(small_sum, (2 * N_LAYERS + 3 * i, chip * dq), (3, dq))) for i in range(2)]
    stack8 = lambda a, b: jnp.concatenate([_pad_rows8(a), _pad_rows8(b)], axis=0)
    res_conv = _adamw("adamw_conv", stack8(*conv_w), stack8(*m_conv), stack8(*v_conv), jnp.concatenate(conv_g, axis=0))

    def leaf(kind, l, which):
        if kind == "ln_pre":
            return res_ln[which][l]
        if kind == "ln_post":
            return res_ln[which][N_LAYERS + l]
        if kind == "w_in":
            return res_in[l][which]
        if kind == "w_out":
            return res_out[l][which]
        return res_conv[which][8 * (l // 2):8 * (l // 2) + 3]

    order = []
    for l in range(N_LAYERS):
        order.append(("ln_pre", l))
        order.append(("w_in", l))
        if l % 2 == 0:
            order.append(("conv", l))
        order.append(("w_out", l))
        order.append(("ln_post", l))
    loss = small_sum[2 * N_LAYERS + 3 * 2, 0]
    outs = [loss, grad_x.reshape(1, t, d)]
    for which in range(4):
        outs.extend(leaf(kind, l, which) for kind, l in order)
    return tuple(outs)
```

```python
import functools
import math
from typing import Any, Callable, Mapping, NamedTuple, Sequence

import jax
import jax.numpy as jnp
from jax import lax
from jax.experimental import pallas as pl
from jax.experimental.pallas import tpu as pltpu

F32 = jnp.float32
BF16 = jnp.bfloat16

N_CHIPS = 4
N_DEV = 8
N_LAYERS = 4
HEAD_DIM = 128
RMS_EPS = 1e-6
ADAM_LR = 0.001
ADAM_B1 = 0.9
ADAM_B2 = 0.999
ADAM_EPS = 1e-08
ADAM_WD = 0.01
ADAM_STEP = 10

VMEM_LIMIT = 56 * 1024 * 1024
MESH_IDS = pl.DeviceIdType.MESH
HBM_SPEC = pl.BlockSpec(memory_space=pltpu.HBM)

NN = (((1,), (0,)), ((), ()))
NT = (((1,), (1,)), ((), ()))
TN = (((0,), (0,)), ((), ()))


def _params(n_axes):
    return pltpu.CompilerParams(dimension_semantics=("arbitrary",) * n_axes, vmem_limit_bytes=VMEM_LIMIT)


def _dot(a, b, dims):
    return lax.dot_general(a, b, dims, preferred_element_type=F32)


def _sigmoid(z):
    return 1.0 / (1.0 + jnp.exp(-z))


class _Rider(NamedTuple):
    operands: Sequence[Any]
    out_shapes: Sequence[Any]
    aliases: Mapping[int, int]
    sems: Sequence[Any]
    start: Callable
    finish: Callable


def _compute_call(body, name, *, grid, in_specs, operands, out_specs, out_shape, scratch=(), aliases=None, rider=None):
    in_specs, operands = list(in_specs), list(operands)
    out_specs, out_shape, scratch = list(out_specs), list(out_shape), list(scratch)
    aliases = dict(aliases or {})
    n_in, n_out, n_scratch = len(operands), len(out_shape), len(scratch)
    hosted = body
    if rider is not None:
        r_in, r_out = len(rider.operands), len(rider.out_shapes)
        aliases.update({n_in + i: n_out + o for i, o in rider.aliases.items()})

        def hosted(*refs):
            ins, refs = refs[:n_in], refs[n_in:]
            rider_ins, refs = refs[:r_in], refs[r_in:]
            outs, refs = refs[:n_out], refs[n_out:]
            rider_outs, refs = refs[:r_out], refs[r_out:]
            own_scratch, rider_sems = refs[:n_scratch], refs[n_scratch:]
            ids = [pl.program_id(axis) for axis in range(len(grid))]
            first = functools.reduce(jnp.logical_and, [i == 0 for i in ids])
            last = functools.reduce(jnp.logical_and, [i == g - 1 for i, g in zip(ids, grid)])

            @pl.when(first)
            def _():
                rider.start(rider_ins, rider_outs, rider_sems)

            body(*ins, *outs, *own_scratch)

            @pl.when(last)
            def _():
                rider.finish(rider_ins, rider_outs, rider_sems)

        in_specs += [HBM_SPEC] * r_in
        operands += list(rider.operands)
        out_specs += [HBM_SPEC] * r_out
        out_shape += list(rider.out_shapes)
        scratch += list(rider.sems)
    return pl.pallas_call(
        hosted, name=name, grid=grid, in_specs=in_specs, out_specs=out_specs, out_shape=out_shape,
        scratch_shapes=scratch, input_output_aliases=aliases, compiler_params=_params(len(grid)),
    )(*operands)


def _matmul(name, a, b, *, grid, a_spec, b_spec, o_spec, out_shape, dims, reduce_axis=None, acc_shape=None,
            alias_out=None, rider=None):
    out_dtype = out_shape.dtype
    direct = reduce_axis is not None and out_dtype == F32
    n_red = grid[reduce_axis] if reduce_axis is not None else 1

    def body(*refs):
        if alias_out is not None:
            refs = refs[1:]
        a_ref, b_ref, o_ref = refs[:3]
        if reduce_axis is None:
            o_ref[...] = _dot(a_ref[...], b_ref[...], dims).astype(out_dtype)
            return
        acc_ref = o_ref if direct else refs[3]
        k = pl.program_id(reduce_axis)

        @pl.when(k == 0)
        def _():
            acc_ref[...] = jnp.zeros_like(acc_ref)

        acc_ref[...] += _dot(a_ref[...], b_ref[...], dims)

        if not direct:
            @pl.when(k == n_red - 1)
            def _():
                o_ref[...] = acc_ref[...].astype(out_dtype)

    scratch = []
    if reduce_axis is not None and not direct:
        scratch = [pltpu.VMEM(acc_shape, F32)]
    in_specs = [a_spec, b_spec]
    operands = [a, b]
    aliases = {}
    if alias_out is not None:
        in_specs = [HBM_SPEC] + in_specs
        operands = [alias_out] + operands
        aliases = {0: 0}
    return _compute_call(body, name, grid=grid, in_specs=in_specs, operands=operands, out_specs=[o_spec],
                         out_shape=[out_shape], scratch=scratch, aliases=aliases, rider=rider)


def _mm_proj(u, g, rider=None):
    t, d = u.shape
    tm = min(1024, t)
    return _matmul(
        "mm_proj", u, g, grid=(N_CHIPS, t // tm),
        a_spec=pl.BlockSpec((tm, d), lambda s, m: (m, 0)),
        b_spec=pl.BlockSpec((None, d, d), lambda s, m: (s, 0, 0)),
        o_spec=pl.BlockSpec((None, tm, d), lambda s, m: (s, m, 0)),
        out_shape=jax.ShapeDtypeStruct((N_CHIPS, t, d), BF16), dims=NN, rider=rider)


def _mm_dgated(dm, g, rider=None):
    t, d = dm.shape
    dq = d // N_CHIPS
    tm = min(512, t)

    def body(a_ref, b_ref, o_ref):
        a = a_ref[...]
        for s in range(N_CHIPS):
            o_ref[:, s * dq:(s + 1) * dq] = _dot(a, b_ref[s], NT).astype(BF16)

    return _compute_call(
        body, "mm_dgated", grid=(t // tm,),
        in_specs=[pl.BlockSpec((tm, d), lambda m: (m, 0)), pl.BlockSpec((N_CHIPS, dq, d), lambda m: (0, N_CHIPS, 0))],
        operands=[dm, g], out_specs=[pl.BlockSpec((tm, d), lambda m: (m, 0))],
        out_shape=[jax.ShapeDtypeStruct((t, d), BF16)], rider=rider)


def _mm_dwout(gated, dm, rider=None):
    t, d = gated.shape
    dq = d // N_CHIPS
    tk = min(1024, t)
    return _matmul(
        "mm_dwout", gated, dm, grid=(N_CHIPS, t // tk),
        a_spec=pl.BlockSpec((tk, dq), lambda s, k: (k, s)),
        b_spec=pl.BlockSpec((tk, d), lambda s, k: (k, 0)),
        o_spec=pl.BlockSpec((None, dq, d), lambda s, k: (s, N_CHIPS, 0)),
        out_shape=jax.ShapeDtypeStruct((N_CHIPS, d + dq, d), BF16), dims=TN, reduce_axis=1, acc_shape=(dq, d),
        rider=rider)


def _mm_du(dproj, g, rider=None):
    _, t, d = dproj.shape
    tm = min(512, t)
    return _matmul(
        "mm_du", dproj, g, grid=(t // tm, N_CHIPS),
        a_spec=pl.BlockSpec((None, tm, d), lambda m, s: (s, m, 0)),
        b_spec=pl.BlockSpec((None, d, d), lambda m, s: (s, 0, 0)),
        o_spec=pl.BlockSpec((tm, d), lambda m, s: (m, 0)),
        out_shape=jax.ShapeDtypeStruct((t, d), F32), dims=NT, reduce_axis=1, rider=rider)


def _mm_dwin(u, dproj, gp, rider=None):
    t, d = u.shape
    tmo = min(1024, d)
    tk = min(1024, t)
    return _matmul(
        "mm_dwin", u, dproj, grid=(N_CHIPS, d // tmo, t // tk),
        a_spec=pl.BlockSpec((tk, tmo), lambda s, mo, k: (k, mo)),
        b_spec=pl.BlockSpec((None, tk, d), lambda s, mo, k: (s, k, 0)),
        o_spec=pl.BlockSpec((None, tmo, d), lambda s, mo, k: (s, mo, 0)),
        out_shape=jax.ShapeDtypeStruct(gp.shape, BF16), dims=TN, reduce_axis=2, acc_shape=(tmo, d), alias_out=gp,
        rider=rider)


def _rms(v):
    r = lax.rsqrt(jnp.mean(v * v, axis=-1, keepdims=True) + RMS_EPS)
    return v * r, r


def _rms_bwd(dout, n, r, gain):
    dn = dout * gain
    return r * (dn - n * jnp.mean(dn * n, axis=-1, keepdims=True))


def _fold8(v):
    return jnp.sum(v.reshape(v.shape[0] // 8, 8, v.shape[1]), axis=0)


def _row0(total):
    rows = lax.broadcasted_iota(jnp.int32, total.shape, 0)
    return jnp.where(rows == 0, jnp.sum(total, axis=0, keepdims=True), 0.0)


def _norm_tile(t):
    return min(256, t)


def _norm_first(x, g_pre, rider=None):
    t, d = x.shape
    tr = _norm_tile(t)

    def body(x_ref, g_ref, u_ref):
        n, _ = _rms(x_ref[...])
        u_ref[...] = (n * g_ref[...]).astype(BF16)

    row = pl.BlockSpec((tr, d), lambda i: (i, 0))
    vec = pl.BlockSpec((1, d), lambda i: (0, 0))
    return _compute_call(
        body, "norm_first", grid=(t // tr,), in_specs=[row, vec], operands=[x, g_pre.reshape(1, d)],
        out_specs=[row], out_shape=[jax.ShapeDtypeStruct((t, d), BF16)], rider=rider)


def _mm_out_norm(gated, g, h, g_post, g_pre_next, rider=None):
    t, d = h.shape
    dq = d // N_CHIPS
    tm = _norm_tile(t)

    def body(a_ref, b_ref, h_ref, gp_ref, gn_ref, m_ref, hn_ref, u_ref):
        acc = _dot(a_ref[:, 0:dq], b_ref[0], NN)
        for s in range(1, N_CHIPS):
            acc = acc + _dot(a_ref[:, s * dq:(s + 1) * dq], b_ref[s], NN)
        m_ref[...] = acc
        n, _ = _rms(acc)
        hn = h_ref[...] + n * gp_ref[...]
        hn_ref[...] = hn
        n2, _ = _rms(hn)
        u_ref[...] = (n2 * gn_ref[...]).astype(BF16)

    row = pl.BlockSpec((tm, d), lambda i: (i, 0))
    vec = pl.BlockSpec((1, d), lambda i: (0, 0))
    return _compute_call(
        body, "mm_out_norm", grid=(t // tm,),
        in_specs=[row, pl.BlockSpec((N_CHIPS, dq, d), lambda i: (0, N_CHIPS, 0)), row, vec, vec],
        operands=[gated, g, h, g_post.reshape(1, d), g_pre_next.reshape(1, d)], out_specs=[row, row, row],
        out_shape=[jax.ShapeDtypeStruct((t, d), F32), jax.ShapeDtypeStruct((t, d), F32), jax.ShapeDtypeStruct((t, d), BF16)],
        rider=rider)


def _mm_out_last(gated, g, h, g_post, target):
    t, d = h.shape
    dq = d // N_CHIPS
    tr = _norm_tile(t)
    nsteps = t // tr

    def body(a_ref, b_ref, h_ref, gp_ref, tg_ref, dy_ref, dm_ref, dgp_ref, loss_ref, acc_g, acc_l):
        i = pl.program_id(0)

        @pl.when(i == 0)
        def _():
            acc_g[...] = jnp.zeros_like(acc_g)
            acc_l[...] = jnp.zeros_like(acc_l)

        m = _dot(a_ref[:, 0:dq], b_ref[0], NN)
        for s in range(1, N_CHIPS):
            m = m + _dot(a_ref[:, s * dq:(s + 1) * dq], b_ref[s], NN)
        gain = gp_ref[...]
        n, r = _rms(m)
        err = h_ref[...] + n * gain - tg_ref[...]
        dy = err / d
        dy_ref[...] = dy
        dm_ref[...] = _rms_bwd(dy, n, r, gain).astype(BF16)
        acc_g[...] += _fold8(dy * n)
        acc_l[...] += _fold8(err * err)

        @pl.when(i == nsteps - 1)
        def _():
            dgp_ref[...] = _row0(acc_g[...])
            loss_ref[...] = jnp.zeros((8, 128), F32) + (0.5 / d) * jnp.sum(acc_l[...])

    row = pl.BlockSpec((tr, d), lambda i: (i, 0))
    vec = pl.BlockSpec((1, d), lambda i: (0, 0))
    acc = pl.BlockSpec((8, d), lambda i: (0, 0))
    return pl.pallas_call(
        body, name="mm_out_last", grid=(nsteps,),
        in_specs=[row, pl.BlockSpec((N_CHIPS, dq, d), lambda i: (0, N_CHIPS, 0)), row, vec, row],
        out_specs=[row, row, acc, pl.BlockSpec((8, 128), lambda i: (0, 0))],
        out_shape=[jax.ShapeDtypeStruct((t, d), F32), jax.ShapeDtypeStruct((t, d), BF16),
                   jax.ShapeDtypeStruct((8, d), F32), jax.ShapeDtypeStruct((8, 128), F32)],
        scratch_shapes=[pltpu.VMEM((8, d), F32), pltpu.VMEM((8, d), F32)],
        compiler_params=_params(1),
    )(gated, g, h, g_post.reshape(1, d), target)


def _norm_bwd_mid(dh, du, h_in, g_pre, m_prev, g_post_prev):
    t, d = dh.shape
    tr = _norm_tile(t)
    nsteps = t // tr

    def body(dh_ref, du_ref, h_ref, gpre_ref, m_ref, gpost_ref, dhn_ref, dm_ref, dgpre_ref, dgpost_ref, acc_a, acc_b):
        i = pl.program_id(0)

        @pl.when(i == 0)
        def _():
            acc_a[...] = jnp.zeros_like(acc_a)
            acc_b[...] = jnp.zeros_like(acc_b)

        du_t = du_ref[...]
        n, r = _rms(h_ref[...])
        dhn = dh_ref[...] + _rms_bwd(du_t, n, r, gpre_ref[...])
        dhn_ref[...] = dhn
        acc_a[...] += _fold8(du_t * n)
        n2, r2 = _rms(m_ref[...])
        dm_ref[...] = _rms_bwd(dhn, n2, r2, gpost_ref[...]).astype(BF16)
        acc_b[...] += _fold8(dhn * n2)

        @pl.when(i == nsteps - 1)
        def _():
            dgpre_ref[...] = _row0(acc_a[...])
            dgpost_ref[...] = _row0(acc_b[...])

    row = pl.BlockSpec((tr, d), lambda i: (i, 0))
    vec = pl.BlockSpec((1, d), lambda i: (0, 0))
    acc = pl.BlockSpec((8, d), lambda i: (0, 0))
    return pl.pallas_call(
        body, name="norm_bwd_mid", grid=(nsteps,), in_specs=[row, row, row, vec, row, vec],
        out_specs=[row, row, acc, acc],
        out_shape=[jax.ShapeDtypeStruct((t, d), F32), jax.ShapeDtypeStruct((t, d), BF16),
                   jax.ShapeDtypeStruct((8, d), F32), jax.ShapeDtypeStruct((8, d), F32)],
        scratch_shapes=[pltpu.VMEM((8, d), F32), pltpu.VMEM((8, d), F32)],
        compiler_params=_params(1),
    )(dh, du, h_in, g_pre.reshape(1, d), m_prev, g_post_prev.reshape(1, d))


def _norm_bwd_first(dh, du, x, g_pre):
    t, d = dh.shape
    tr = _norm_tile(t)
    nsteps = t // tr

    def body(dh_ref, du_ref, x_ref, gpre_ref, dx_ref, dgpre_ref, acc_a):
        i = pl.program_id(0)

        @pl.when(i == 0)
        def _():
            acc_a[...] = jnp.zeros_like(acc_a)

        du_t = du_ref[...]
        n, r = _rms(x_ref[...])
        dx_ref[...] = dh_ref[...] + _rms_bwd(du_t, n, r, gpre_ref[...])
        acc_a[...] += _fold8(du_t * n)

        @pl.when(i == nsteps - 1)
        def _():
            dgpre_ref[...] = _row0(acc_a[...])

    row = pl.BlockSpec((tr, d), lambda i: (i, 0))
    vec = pl.BlockSpec((1, d), lambda i: (0, 0))
    acc = pl.BlockSpec((8, d), lambda i: (0, 0))
    return _compute_call(
        body, "norm_bwd_first", grid=(nsteps,), in_specs=[row, row, row, vec],
        operands=[dh, du, x, g_pre.reshape(1, d)], out_specs=[row, acc],
        out_shape=[jax.ShapeDtypeStruct((t, d), F32), jax.ShapeDtypeStruct((8, d), F32)],
        scratch=[pltpu.VMEM((8, d), F32)])


CONV_TC = 128
CONV_HALO = 16


def _conv_chunk(t):
    return min(512, t)


def _shift_down(v, steps, fill):
    rows = lax.broadcasted_iota(jnp.int32, v.shape, 0)
    out = pltpu.roll(v, steps, axis=0)
    for k in range(steps):
        out = jnp.where(rows == k, fill[CONV_HALO - steps + k:CONV_HALO - steps + k + 1, :], out)
    return out


def _shift_up(v, steps, fill):
    nrows = v.shape[0]
    rows = lax.broadcasted_iota(jnp.int32, v.shape, 0)
    out = pltpu.roll(v, nrows - steps, axis=0)
    for k in range(steps):
        out = jnp.where(rows == nrows - steps + k, fill[k:k + 1, :], out)
    return out


def _conv_fwd(proj, cw, rider=None):
    _, t, d = proj.shape
    chunk = _conv_chunk(t)

    def body(p_ref, w_ref, o_ref):
        w = w_ref[...]
        w0, w1, w2 = w[0:1, :], w[1:2, :], w[2:3, :]
        for ci in range(t // chunk):
            t0 = ci * chunk
            rows = pl.ds(t0, chunk)
            b = p_ref[0, rows, :].astype(F32)
            cx = p_ref[1, rows, :].astype(F32) * p_ref[2, rows, :].astype(F32)
            z = p_ref[3, rows, :].astype(F32)
            if ci == 0:
                prev = jnp.zeros((CONV_HALO, CONV_TC), F32)
            else:
                halo = pl.ds(t0 - CONV_HALO, CONV_HALO)
                prev = p_ref[1, halo, :].astype(F32) * p_ref[2, halo, :].astype(F32)
            conv = w2 * cx + w1 * _shift_down(cx, 1, prev) + w0 * _shift_down(cx, 2, prev)
            o_ref[rows, :] = (z * _sigmoid(z) * b * conv).astype(BF16)

    return _compute_call(
        body, "conv_fwd", grid=(d // CONV_TC,),
        in_specs=[pl.BlockSpec((N_CHIPS, t, CONV_TC), lambda j: (0, 0, j)), pl.BlockSpec((8, CONV_TC), lambda j: (0, j))],
        operands=[proj, cw], out_specs=[pl.BlockSpec((t, CONV_TC), lambda j: (0, j))],
        out_shape=[jax.ShapeDtypeStruct((t, d), BF16)], rider=rider)


def _conv_bwd(proj, dgated, cw):
    _, t, d = proj.shape
    chunk = _conv_chunk(t)
    nchunks = t // chunk

    def body(p_ref, dg_ref, w_ref, dp_ref, dw_ref):
        w = w_ref[...]
        w0, w1, w2 = w[0:1, :], w[1:2, :], w[2:3, :]
        dw0 = jnp.zeros((1, CONV_TC), F32)
        dw1 = jnp.zeros((1, CONV_TC), F32)
        dw2 = jnp.zeros((1, CONV_TC), F32)
        for ci in range(nchunks):
            t0 = ci * chunk
            rows = pl.ds(t0, chunk)
            b = p_ref[0, rows, :].astype(F32)
            c = p_ref[1, rows, :].astype(F32)
            xt = p_ref[2, rows, :].astype(F32)
            z = p_ref[3, rows, :].astype(F32)
            dg = dg_ref[rows, :].astype(F32)
            cx = c * xt
            if ci == 0:
                prev = jnp.zeros((CONV_HALO, CONV_TC), F32)
            else:
                halo = pl.ds(t0 - CONV_HALO, CONV_HALO)
                prev = p_ref[1, halo, :].astype(F32) * p_ref[2, halo, :].astype(F32)
            cx1 = _shift_down(cx, 1, prev)
            cx2 = _shift_down(cx, 2, prev)
            conv = w2 * cx + w1 * cx1 + w0 * cx2
            sig = _sigmoid(z)
            dy = dg * (z * sig)
            dconv = dy * b
            if ci == nchunks - 1:
                nxt = jnp.zeros((CONV_HALO, CONV_TC), F32)
            else:
                halo = pl.ds(t0 + chunk, CONV_HALO)
                zn = p_ref[3, halo, :].astype(F32)
                nxt = dg_ref[halo, :].astype(F32) * (zn * _sigmoid(zn)) * p_ref[0, halo, :].astype(F32)
            dcx = w2 * dconv + w1 * _shift_up(dconv, 1, nxt) + w0 * _shift_up(dconv, 2, nxt)
            dp_ref[0, rows, :] = (dy * conv).astype(BF16)
            dp_ref[1, rows, :] = (dcx * xt).astype(BF16)
            dp_ref[2, rows, :] = (dcx * c).astype(BF16)
            dp_ref[3, rows, :] = (dg * (b * conv) * (sig * (1.0 + z * (1.0 - sig)))).astype(BF16)
            dw0 = dw0 + jnp.sum(dconv * cx2, axis=0, keepdims=True)
            dw1 = dw1 + jnp.sum(dconv * cx1, axis=0, keepdims=True)
            dw2 = dw2 + jnp.sum(dconv * cx, axis=0, keepdims=True)
        taps = lax.broadcasted_iota(jnp.int32, (8, CONV_TC), 0)
        dw_ref[...] = jnp.where(taps == 0, dw0, jnp.where(taps == 1, dw1, jnp.where(taps == 2, dw2, 0.0)))

    return pl.pallas_call(
        body, name="conv_bwd", grid=(d // CONV_TC,),
        in_specs=[pl.BlockSpec((N_CHIPS, t, CONV_TC), lambda j: (0, 0, j)),
                  pl.BlockSpec((t, CONV_TC), lambda j: (0, j)),
                  pl.BlockSpec((8, CONV_TC), lambda j: (0, j))],
        out_specs=[pl.BlockSpec((N_CHIPS, t, CONV_TC), lambda j: (0, 0, j)), pl.BlockSpec((8, CONV_TC), lambda j: (0, j))],
        out_shape=[jax.ShapeDtypeStruct((N_CHIPS, t, d), BF16), jax.ShapeDtypeStruct((8, d), F32)],
        compiler_params=_params(1),
    )(proj, dgated, cw)


SB_DEAD_TAIL = -105.0
SB_COUNT_LANE = HEAD_DIM - 1


def _sb_block(t):
    return min(256, t)


def _split_dot(v, tri):
    hi = v.astype(BF16)
    lo = (v - hi.astype(F32)).astype(BF16)
    return _dot(hi, tri, NN) + _dot(lo, tri, NN)


SB_HEADS_PER_STEP = 4


def _sb_terms(s, diagonal):
    sp = jnp.maximum(s, 0.0) + jnp.log(1.0 + jnp.exp(-jnp.abs(s)))
    if not diagonal:
        return -sp, s - sp, sp, None
    mask = lax.broadcasted_iota(jnp.int32, s.shape, 1) < lax.broadcasted_iota(jnp.int32, s.shape, 0)
    return jnp.where(mask, -sp, 0.0), s - sp, sp, mask


def _masked(mask, v):
    return v if mask is None else jnp.where(mask, v, 0.0)


def _sb_fwd(proj, rider=None):
    _, t, d = proj.shape
    heads = d // HEAD_DIM
    blk = _sb_block(t)
    nblk = t // blk
    scale = 1.0 / math.sqrt(HEAD_DIM)

    hps = SB_HEADS_PER_STEP
    width = hps * HEAD_DIM

    def body(q_ref, k_ref, v_ref, z_ref, gated_ref, o_ref, car_ref, tail_ref, acc_ref):
        i = pl.program_id(1)
        r_i = lax.broadcasted_iota(jnp.int32, (blk, blk), 0)
        c_i = lax.broadcasted_iota(jnp.int32, (blk, blk), 1)
        tri_after = (r_i > c_i).astype(BF16)
        lanes = lax.broadcasted_iota(jnp.int32, (blk, HEAD_DIM), 1)

        tail_ref[...] = jnp.zeros_like(tail_ref)
        acc_ref[...] = jnp.zeros_like(acc_ref)
        car_ref[...] = jnp.zeros_like(car_ref)

        def visit(j, diagonal):
            krows = pl.ds(pl.multiple_of(j * blk, blk), blk)
            hcols = [pl.ds(hh * HEAD_DIM, HEAD_DIM) for hh in range(hps)]
            logits = [_dot(q_ref[:, c], k_ref[krows, c], NT) for c in hcols]
            terms = [_sb_terms(s * scale, diagonal) for s in logits]
            within = [_split_dot(keep, tri_after) for keep, _, _, _ in terms]
            top = None
            for hh, (keep, log_beta, _, mask) in enumerate(terms):
                tail_b = tail_ref[hh]
                w = _masked(mask, jnp.exp(log_beta + tail_b[:, 0:1] + within[hh]))
                acc_ref[hh] += _dot(w.astype(BF16), v_ref[krows, hcols[hh]], NN)
                car_ref[hh] = jnp.where(lanes == j, tail_b, car_ref[hh])
                tail_new = tail_b + jnp.sum(keep, axis=1, keepdims=True)
                tail_ref[hh] = tail_new
                top = jnp.max(tail_new) if top is None else jnp.maximum(top, jnp.max(tail_new))
            return top > SB_DEAD_TAIL

        def more(state):
            jj, live = state
            return jnp.logical_and(jj <= i, live)

        def step(state):
            jj, _ = state
            return jj + 1, visit(i - jj, False)

        visited, _ = lax.while_loop(more, step, (jnp.int32(1), visit(i, True)))
        for hh in range(hps):
            cols = pl.ds(hh * HEAD_DIM, HEAD_DIM)
            car_ref[hh] = jnp.where(lanes == SB_COUNT_LANE, visited.astype(F32), car_ref[hh])
            z = z_ref[:, cols].astype(F32)
            acc = acc_ref[hh]
            o_ref[:, cols] = acc.astype(BF16)
            gated_ref[:, cols] = (z * _sigmoid(z) * acc).astype(BF16)

    qspec = lambda s: pl.BlockSpec((None, blk, width), lambda h, i: (s, i, h))
    kspec = lambda s: pl.BlockSpec((None, t, width), lambda h, i: (s, 0, h))
    ospec = pl.BlockSpec((blk, width), lambda h, i: (i, h))
    return _compute_call(
        body, "sb_fwd", grid=(heads // hps, nblk),
        in_specs=[qspec(0), kspec(1), kspec(2), qspec(3)], operands=[proj, proj, proj, proj],
        out_specs=[ospec, ospec, pl.BlockSpec((hps, blk, HEAD_DIM), lambda h, i: (h, i, 0))],
        out_shape=[jax.ShapeDtypeStruct((t, d), BF16), jax.ShapeDtypeStruct((t, d), BF16),
                   jax.ShapeDtypeStruct((heads, t, HEAD_DIM), F32)],
        scratch=[pltpu.VMEM((hps, blk, HEAD_DIM), F32), pltpu.VMEM((hps, blk, HEAD_DIM), F32)], rider=rider)


def _sb_bwd(proj, dgated, o, car, rider=None):
    _, t, d = proj.shape
    heads = d // HEAD_DIM
    blk = _sb_block(t)
    nblk = t // blk
    scale = 1.0 / math.sqrt(HEAD_DIM)

    hps = SB_HEADS_PER_STEP
    width = hps * HEAD_DIM

    def body(q_ref, k_ref, v_ref, z_ref, dg_ref, o_ref, car_ref, dp_ref, dk_acc, dv_acc, gsum_ref, dq_ref, do_ref):
        step_i = pl.program_id(1)
        i = nblk - 1 - step_i

        @pl.when(step_i == 0)
        def _():
            dk_acc[...] = jnp.zeros_like(dk_acc)
            dv_acc[...] = jnp.zeros_like(dv_acc)

        r_i = lax.broadcasted_iota(jnp.int32, (blk, blk), 0)
        c_i = lax.broadcasted_iota(jnp.int32, (blk, blk), 1)
        tri_after = (r_i > c_i).astype(BF16)
        tri_before = (r_i < c_i).astype(BF16)
        lanes = lax.broadcasted_iota(jnp.int32, (blk, HEAD_DIM), 1)

        gsum_ref[...] = jnp.zeros_like(gsum_ref)
        dq_ref[...] = jnp.zeros_like(dq_ref)
        for hh in range(hps):
            cols = pl.ds(hh * HEAD_DIM, HEAD_DIM)
            z = z_ref[:, cols].astype(F32)
            dg = dg_ref[:, cols].astype(F32)
            sig = _sigmoid(z)
            do_ref[hh] = (dg * (z * sig)).astype(BF16)
            dp_ref[3, :, cols] = (dg * o_ref[:, cols].astype(F32) * (sig * (1.0 + z * (1.0 - sig)))).astype(BF16)

        def visit(j, diagonal):
            krows = pl.ds(pl.multiple_of(j * blk, blk), blk)
            hcols = [pl.ds(hh * HEAD_DIM, HEAD_DIM) for hh in range(hps)]
            logits = [_dot(q_ref[:, c], k_ref[krows, c], NT) for c in hcols]
            dws = [_dot(do_ref[hh], v_ref[krows, c], NT) for hh, c in enumerate(hcols)]
            terms = [_sb_terms(s * scale, diagonal) for s in logits]
            within = [_split_dot(keep, tri_after) for keep, _, _, _ in terms]
            ws, gs = [], []
            for hh, (keep, log_beta, sp, mask) in enumerate(terms):
                tail = jnp.sum(jnp.where(lanes == j, car_ref[hh], 0.0), axis=1, keepdims=True)
                w = _masked(mask, jnp.exp(log_beta + tail + within[hh]))
                ws.append(w.astype(BF16))
                gs.append(w * dws[hh])
            g_within = [_split_dot(g, tri_before) for g in gs]
            for hh, (keep, log_beta, sp, mask) in enumerate(terms):
                c = hcols[hh]
                g_before = gsum_ref[hh]
                g_cum = g_before[:, 0:1] + g_within[hh]
                dl = (_masked(mask, gs[hh] - (gs[hh] + g_cum) * jnp.exp(log_beta)) * scale).astype(BF16)
                dq_ref[hh] += _dot(dl, k_ref[krows, c], NN)
                dk_acc[krows, c] += _dot(dl, q_ref[:, c], TN)
                dv_acc[krows, c] += _dot(ws[hh], do_ref[hh], TN)
                gsum_ref[hh] = g_before + jnp.sum(gs[hh], axis=1, keepdims=True)

        def step(j, carry):
            visit(j, False)
            return carry

        visited = jnp.max(jnp.where(lanes == SB_COUNT_LANE, car_ref[0], 0.0)).astype(jnp.int32)
        lax.fori_loop(i + 1 - visited, i, step, 0)
        visit(i, True)
        own = pl.ds(pl.multiple_of(i * blk, blk), blk)
        for hh in range(hps):
            cols = pl.ds(hh * HEAD_DIM, HEAD_DIM)
            dp_ref[0, :, cols] = dq_ref[hh].astype(BF16)
        dp_ref[1] = dk_acc[own, :].astype(BF16)
        dp_ref[2] = dv_acc[own, :].astype(BF16)

    qspec = lambda s: pl.BlockSpec((None, blk, width), lambda h, i: (s, nblk - 1 - i, h))
    kspec = lambda s: pl.BlockSpec((None, t, width), lambda h, i: (s, 0, h))
    tspec = pl.BlockSpec((blk, width), lambda h, i: (nblk - 1 - i, h))
    return _compute_call(
        body, "sb_bwd", grid=(heads // hps, nblk),
        in_specs=[qspec(0), kspec(1), kspec(2), qspec(3), tspec, tspec,
                  pl.BlockSpec((hps, blk, HEAD_DIM), lambda h, i: (h, nblk - 1 - i, 0))],
        operands=[proj, proj, proj, proj, dgated, o, car],
        out_specs=[pl.BlockSpec((N_CHIPS, blk, width), lambda h, i: (0, nblk - 1 - i, h))],
        out_shape=[jax.ShapeDtypeStruct((N_CHIPS, t, d), BF16)],
        scratch=[pltpu.VMEM((t, width), F32), pltpu.VMEM((t, width), F32),
                 pltpu.VMEM((hps, blk, HEAD_DIM), F32), pltpu.VMEM((hps, blk, HEAD_DIM), F32),
                 pltpu.VMEM((hps, blk, HEAD_DIM), BF16)], rider=rider)


def _pack_weights(w_in, w_out, chip):
    d = w_in.shape[0]
    rb = d // 8
    n_in = d // rb
    n_out = w_out.shape[0] // rb

    def body(chip_ref, wi_ref, wo_ref, o_ref):
        r = pl.program_id(0)

        @pl.when(r < n_in)
        def _():
            o_ref[...] = wi_ref[...].astype(BF16)

        @pl.when(r >= n_in)
        def _():
            o_ref[...] = wo_ref[...].astype(BF16)

    grid_spec = pltpu.PrefetchScalarGridSpec(
        num_scalar_prefetch=1, grid=(n_in + n_out,),
        in_specs=[pl.BlockSpec((rb, d), lambda r, me: (jnp.minimum(r, n_in - 1), 0)),
                  pl.BlockSpec((rb, d), lambda r, me: (jnp.maximum(r - n_in, 0), 0))],
        out_specs=pl.BlockSpec((None, rb, d), lambda r, me: (me[0], r, 0)))
    return pl.pallas_call(
        body, name="pack_weights", grid_spec=grid_spec,
        out_shape=jax.ShapeDtypeStruct((N_CHIPS, d + w_out.shape[0], d), BF16), compiler_params=_params(1),
    )(chip, w_in, w_out)


def _flip(v, bit):
    return 1 - v if bit else v


def _remote(src, dst, send_sem, recv_sem, target):
    return pltpu.make_async_remote_copy(src_ref=src, dst_ref=dst, send_sem=send_sem, recv_sem=recv_sem,
                                        device_id=target, device_id_type=MESH_IDS)


AG_CHUNKS = 8
AG_PLAN_CONV = {
    "mm_proj": dict(chunks=range(0, 5)),
    "conv_fwd": dict(chunks=range(5, 6), landed=range(0, 5)),
    "mm_out_norm": dict(chunks=range(6, 8), landed=range(5, 6), pass_now=range(6, 8)),
}
AG_PLAN_SB = {
    "mm_proj": dict(chunks=range(0, 4)),
    "sb_fwd": dict(chunks=range(4, 8), landed=range(0, 4)),
    "mm_out_norm": dict(landed=range(4, 8)),
}
HALF_CHUNKS = 8


SWAP_CHUNKS = 4


def _other_chips(x, y):
    return [(_flip(x, k >> 1), _flip(y, k & 1)) for k in (1, 2, 3)]


def _gather_sems(n):
    return [pltpu.SemaphoreType.DMA((3, n * AG_CHUNKS)) for _ in range(4)]


def _gather_pieces(g, chunks):
    hr = g[0].shape[1] // 2
    cr = hr // AG_CHUNKS
    return hr, [(l * AG_CHUNKS + q, g[l], q * cr, cr) for l in range(len(g)) for q in chunks]


def _pass_on(g, fsend, frecv, chunks):
    x, y, c = lax.axis_index("x"), lax.axis_index("y"), lax.axis_index("c")
    hr, pieces = _gather_pieces(g, chunks)
    for k, (px, py) in enumerate(_other_chips(x, y)):
        for i, ref, r0, cr in pieces:
            landed = ref.at[2 * px + py, pl.ds(c * hr + r0, cr)]
            _remote(landed, landed, fsend.at[k, i], frecv.at[k, i], (x, y, 1 - c)).start()


def _gather_start(g, send, recv, fsend, frecv, chunks=range(AG_CHUNKS), landed=()):
    x, y, c = lax.axis_index("x"), lax.axis_index("y"), lax.axis_index("c")
    hr, pieces = _gather_pieces(g, chunks)
    for k, (px, py) in enumerate(_other_chips(x, y)):
        for i, ref, r0, cr in pieces:
            piece = ref.at[2 * x + y, pl.ds(c * hr + r0, cr)]
            _remote(piece, piece, send.at[k, i], recv.at[k, i], (px, py, c)).start()
    _pass_on(g, fsend, frecv, landed)


def _gather_finish(g, send, recv, fsend, frecv, chunks=range(AG_CHUNKS), landed=(), pass_now=None):
    x, y, c = lax.axis_index("x"), lax.axis_index("y"), lax.axis_index("c")
    sibling = (x, y, 1 - c)
    pass_now = chunks if pass_now is None else pass_now
    hr, pieces = _gather_pieces(g, chunks)
    chips = _other_chips(x, y)
    for k, (px, py) in enumerate(chips):
        for i, ref, r0, cr in pieces:
            arrived = ref.at[2 * px + py, pl.ds(c * hr + r0, cr)]
            _remote(arrived, arrived, send.at[k, i], recv.at[k, i], (px, py, c)).wait_recv()
    _pass_on(g, fsend, frecv, pass_now)
    _, passed = _gather_pieces(g, list(landed) + list(pass_now))
    for k, (px, py) in enumerate(chips):
        for i, ref, r0, cr in passed:
            theirs = ref.at[2 * px + py, pl.ds((1 - c) * hr + r0, cr)]
            _remote(theirs, theirs, fsend.at[k, i], frecv.at[k, i], sibling).wait_recv()
    for k, (px, py) in enumerate(chips):
        for i, ref, r0, cr in pieces:
            mine = ref.at[2 * x + y, pl.ds(c * hr + r0, cr)]
            _remote(mine, mine, send.at[k, i], recv.at[k, i], (px, py, c)).wait_send()
        for i, ref, r0, cr in passed:
            mine = ref.at[2 * px + py, pl.ds(c * hr + r0, cr)]
            _remote(mine, mine, fsend.at[k, i], frecv.at[k, i], sibling).wait_send()


def _taps_rider(cw):
    def copies(ins, outs, sems):
        x, y, c = lax.axis_index("x"), lax.axis_index("y"), lax.axis_index("c")
        return [(_remote(ins[0], outs[0].at[2 * x + y], sems[0].at[k], sems[1].at[k], (px, py, c)),
                 _remote(ins[0], outs[0].at[2 * px + py], sems[0].at[k], sems[1].at[k], (px, py, c)))
                for k, (px, py) in enumerate(_other_chips(x, y))]

    def start(ins, outs, sems):
        pltpu.make_async_copy(ins[0], outs[0].at[2 * lax.axis_index("x") + lax.axis_index("y")], sems[2]).start()
        for mine, _ in copies(ins, outs, sems):
            mine.start()

    def finish(ins, outs, sems):
        for _, theirs in copies(ins, outs, sems):
            theirs.wait_recv()
        for mine, _ in copies(ins, outs, sems):
            mine.wait_send()
        pltpu.make_async_copy(ins[0], outs[0].at[2 * lax.axis_index("x") + lax.axis_index("y")], sems[2]).wait()

    return _Rider(
        operands=[cw], out_shapes=[jax.ShapeDtypeStruct((N_CHIPS,) + cw.shape, cw.dtype)], aliases={},
        sems=[pltpu.SemaphoreType.DMA((3,)), pltpu.SemaphoreType.DMA((3,)), pltpu.SemaphoreType.DMA],
        start=start, finish=finish)


def _gather_rider(packs, chunks=(), landed=(), pass_now=()):
    return _Rider(
        operands=packs, out_shapes=[jax.ShapeDtypeStruct(p.shape, BF16) for p in packs],
        aliases={l: l for l in range(len(packs))}, sems=_gather_sems(len(packs)),
        start=lambda ins, outs, sems: _gather_start(outs, *sems, chunks, landed),
        finish=lambda ins, outs, sems: _gather_finish(outs, *sems, chunks, landed, pass_now))


def _join_riders(a, b):
    na, oa, sa = len(a.operands), len(a.out_shapes), len(a.sems)
    aliases = dict(a.aliases)
    aliases.update({na + i: oa + o for i, o in b.aliases.items()})
    return _Rider(
        operands=list(a.operands) + list(b.operands), out_shapes=list(a.out_shapes) + list(b.out_shapes),
        aliases=aliases, sems=list(a.sems) + list(b.sems),
        start=lambda ins, outs, sems: (a.start(ins[:na], outs[:oa], sems[:sa]),
                                       b.start(ins[na:], outs[oa:], sems[sa:])),
        finish=lambda ins, outs, sems: (a.finish(ins[:na], outs[:oa], sems[:sa]),
                                        b.finish(ins[na:], outs[oa:], sems[sa:])))


def _comm_call(name, rider):
    n_in, n_out = len(rider.operands), len(rider.out_shapes)

    def body(*refs):
        ins, outs, sems = refs[:n_in], refs[n_in:n_in + n_out], refs[n_in + n_out:]
        rider.start(ins, outs, sems)
        rider.finish(ins, outs, sems)

    return pl.pallas_call(
        body, name=name, in_specs=[HBM_SPEC] * n_in, out_specs=[HBM_SPEC] * n_out, out_shape=list(rider.out_shapes),
        scratch_shapes=list(rider.sems), input_output_aliases=dict(rider.aliases),
    )(*rider.operands)


def _small_rider(small):
    def peers():
        x, y, c = lax.axis_index("x"), lax.axis_index("y"), lax.axis_index("c")
        return 4 * x + 2 * y + c, [(_flip(x, r >> 2), _flip(y, (r >> 1) & 1), _flip(c, r & 1)) for r in range(1, N_DEV)]

    def start(ins, outs, sems):
        me, others = peers()
        pltpu.make_async_copy(ins[0], outs[0].at[me], sems[2]).start()
        for r, peer in enumerate(others):
            _remote(ins[0], outs[0].at[me], sems[0].at[r], sems[1].at[r], peer).start()

    def finish(ins, outs, sems):
        me, others = peers()
        for r, (tx, ty, tc) in enumerate(others):
            _remote(ins[0], outs[0].at[4 * tx + 2 * ty + tc], sems[0].at[r], sems[1].at[r], (tx, ty, tc)).wait_recv()
        for r, peer in enumerate(others):
            _remote(ins[0], outs[0].at[me], sems[0].at[r], sems[1].at[r], peer).wait_send()
        pltpu.make_async_copy(ins[0], outs[0].at[me], sems[2]).wait()

    return _Rider(
        operands=[small], out_shapes=[jax.ShapeDtypeStruct((N_DEV,) + small.shape, small.dtype)], aliases={},
        sems=[pltpu.SemaphoreType.DMA((N_DEV - 1,)), pltpu.SemaphoreType.DMA((N_DEV - 1,)), pltpu.SemaphoreType.DMA],
        start=start, finish=finish)


SMALL_ROWS = 16


def _pack_small(ln_parts, conv_parts, loss_rows):
    d = ln_parts[0].shape[1]
    n_ln, n_conv = len(ln_parts), len(conv_parts)

    def body(*refs):
        o_ref = refs[-1]
        rows = lax.broadcasted_iota(jnp.int32, (SMALL_ROWS, d), 0)
        acc = jnp.zeros((SMALL_ROWS, d), F32)
        for i in range(n_ln):
            acc = jnp.where(rows == i, refs[i][0:1, :], acc)
        for j in range(n_conv):
            for k in range(3):
                acc = jnp.where(rows == n_ln + 3 * j + k, refs[n_ln + j][k:k + 1, :], acc)
        acc = jnp.where(rows == n_ln + 3 * n_conv, refs[n_ln + n_conv][0:1, :], acc)
        o_ref[...] = acc

    n = n_ln + n_conv + 1
    return pl.pallas_call(
        body, name="pack_small", out_shape=jax.ShapeDtypeStruct((SMALL_ROWS, d), F32),
        in_specs=[pl.BlockSpec(memory_space=pltpu.VMEM)] * n, out_specs=pl.BlockSpec(memory_space=pltpu.VMEM),
    )(*ln_parts, *conv_parts, loss_rows)


def _send_sums_start(s_ref, got, send, recv, part=0, parts=1, span=1):
    x, y, c = lax.axis_index("x"), lax.axis_index("y"), lax.axis_index("c")
    nrows = s_ref.shape[1] // parts
    rows = pl.ds(part * nrows, span * nrows)
    for k, (px, py) in enumerate(_other_chips(x, y)):
        _remote(s_ref.at[2 * px + py, rows], got.at[k, rows], send.at[k], recv.at[k], (px, py, c)).start()


def _send_sums_finish(s_ref, got, send, recv, part=0, parts=1, span=1):
    x, y, c = lax.axis_index("x"), lax.axis_index("y"), lax.axis_index("c")
    nrows = s_ref.shape[1] // parts
    rows = pl.ds(part * nrows, span * nrows)
    for k, (px, py) in enumerate(_other_chips(x, y)):
        _remote(got.at[k, rows], got.at[k, rows], send.at[k], recv.at[k], (px, py, c)).wait_recv()
    for k, (px, py) in enumerate(_other_chips(x, y)):
        _remote(s_ref.at[2 * px + py, rows], got.at[k, rows], send.at[k], recv.at[k], (px, py, c)).wait_send()


def _send_sums_rider(sums, got=None, part=0, parts=1, span=1):
    _, hr, d = sums.shape
    return _Rider(
        operands=[sums] if got is None else [sums, got],
        out_shapes=[jax.ShapeDtypeStruct((N_CHIPS - 1, hr, d), BF16)], aliases={} if got is None else {1: 0},
        sems=[pltpu.SemaphoreType.DMA((3,)), pltpu.SemaphoreType.DMA((3,))],
        start=lambda ins, outs, sems: _send_sums_start(ins[0], outs[0], *sems, part, parts, span),
        finish=lambda ins, outs, sems: _send_sums_finish(ins[0], outs[0], *sems, part, parts, span))


def _swap_pieces(gp_ref, x_ref, c):
    hr = x_ref.shape[1]
    cr = hr // SWAP_CHUNKS
    return [(a * SWAP_CHUNKS + q, gp_ref.at[a, pl.ds((1 - c) * hr + q * cr, cr)], x_ref.at[a, pl.ds(q * cr, cr)])
            for a in range(N_CHIPS) for q in range(SWAP_CHUNKS)]


def _swap_rider(gp):
    _, p_rows, d = gp.shape

    def start(ins, outs, sems):
        x, y, c = lax.axis_index("x"), lax.axis_index("y"), lax.axis_index("c")
        for i, src, dst in _swap_pieces(ins[0], outs[0], c):
            _remote(src, dst, sems[0].at[i], sems[1].at[i], (x, y, 1 - c)).start()

    def finish(ins, outs, sems):
        x, y, c = lax.axis_index("x"), lax.axis_index("y"), lax.axis_index("c")
        pieces = _swap_pieces(ins[0], outs[0], c)
        for i, src, dst in pieces:
            _remote(dst, dst, sems[0].at[i], sems[1].at[i], (x, y, 1 - c)).wait_recv()
        for i, src, dst in pieces:
            _remote(src, dst, sems[0].at[i], sems[1].at[i], (x, y, 1 - c)).wait_send()

    nsem = N_CHIPS * SWAP_CHUNKS
    return _Rider(
        operands=[gp], out_shapes=[jax.ShapeDtypeStruct((N_CHIPS, p_rows // 2, d), BF16)], aliases={},
        sems=[pltpu.SemaphoreType.DMA((nsem,)), pltpu.SemaphoreType.DMA((nsem,))], start=start, finish=finish)


def _presum(gp, theirs, core):
    _, hr, d = theirs.shape
    tr = _row_tile(hr, 640)
    steps = hr // tr

    def body(core_ref, mine_ref, theirs_ref, o_ref):
        o_ref[...] = (mine_ref[...].astype(F32) + theirs_ref[...].astype(F32)).astype(BF16)

    grid_spec = pltpu.PrefetchScalarGridSpec(
        num_scalar_prefetch=1, grid=(N_CHIPS, steps),
        in_specs=[pl.BlockSpec((None, tr, d), lambda a, i, cr: (a, cr[0] * steps + i, 0)),
                  pl.BlockSpec((None, tr, d), lambda a, i, cr: (a, i, 0))],
        out_specs=pl.BlockSpec((None, tr, d), lambda a, i, cr: (a, i, 0)))
    return pl.pallas_call(
        body, name="presum", grid_spec=grid_spec,
        out_shape=jax.ShapeDtypeStruct((N_CHIPS, hr, d), BF16), compiler_params=_params(2),
    )(core, gp, theirs)


def _row_tile(rows, cap=128):
    if rows <= cap:
        return rows
    return next(tr for tr in range(cap, 0, -16) if rows % tr == 0)


def _sum_sources(name, parts):
    nsrc, rows, cols = parts.shape
    tr = _row_tile(rows)

    def body(p_ref, o_ref):
        total = p_ref[0].astype(F32)
        for s in range(1, nsrc):
            total = total + p_ref[s].astype(F32)
        o_ref[...] = total

    return pl.pallas_call(
        body, name=name, grid=(rows // tr,),
        in_specs=[pl.BlockSpec((nsrc, tr, cols), lambda i: (0, i, 0))],
        out_specs=pl.BlockSpec((tr, cols), lambda i: (i, 0)),
        out_shape=jax.ShapeDtypeStruct((rows, cols), F32), compiler_params=_params(1),
    )(parts)


def _sum_grad_half(got, sums, place):
    nsrc, hr, d = got.shape
    tr = _row_tile(hr, 256)

    def body(place_ref, got_ref, own_ref, o_ref):
        total = own_ref[...].astype(F32)
        for s in range(nsrc):
            total = total + got_ref[s].astype(F32)
        o_ref[...] = total

    grid_spec = pltpu.PrefetchScalarGridSpec(
        num_scalar_prefetch=1, grid=(hr // tr,),
        in_specs=[pl.BlockSpec((nsrc, tr, d), lambda i, pc: (0, i, 0)),
                  pl.BlockSpec((None, tr, d), lambda i, pc: (pc[0], i, 0))],
        out_specs=pl.BlockSpec((None, tr, d), lambda i, pc: (pc[1], i, 0)))
    return pl.pallas_call(
        body, name="sum_grad_half", grid_spec=grid_spec,
        out_shape=jax.ShapeDtypeStruct((2, hr, d), F32), compiler_params=_params(1),
    )(place, got, sums)


def _halves_rider(full):
    _, hr, d = full.shape
    cr = hr // HALF_CHUNKS

    def pieces(ref, half):
        return [ref.at[half, pl.ds(q * cr, cr)] for q in range(HALF_CHUNKS)]

    def start(ins, outs, sems):
        x, y, c = lax.axis_index("x"), lax.axis_index("y"), lax.axis_index("c")
        for i, piece in enumerate(pieces(outs[0], c)):
            _remote(piece, piece, sems[0].at[i], sems[1].at[i], (x, y, 1 - c)).start()

    def finish(ins, outs, sems):
        x, y, c = lax.axis_index("x"), lax.axis_index("y"), lax.axis_index("c")
        for i, piece in enumerate(pieces(outs[0], 1 - c)):
            _remote(piece, piece, sems[0].at[i], sems[1].at[i], (x, y, 1 - c)).wait_recv()
        for i, piece in enumerate(pieces(outs[0], c)):
            _remote(piece, piece, sems[0].at[i], sems[1].at[i], (x, y, 1 - c)).wait_send()

    return _Rider(
        operands=[full], out_shapes=[jax.ShapeDtypeStruct(full.shape, F32)], aliases={0: 0},
        sems=[pltpu.SemaphoreType.DMA((HALF_CHUNKS,)), pltpu.SemaphoreType.DMA((HALF_CHUNKS,))],
        start=start, finish=finish)


def _adamw(name, w, m, v, g, g_row0=0):
    rows, cols = w.shape
    tr = _row_tile(rows, 256)
    off = g_row0 // tr

    def body(w_ref, m_ref, v_ref, g_ref, go_ref, d_ref, mo_ref, vo_ref):
        grad = g_ref[...]
        m_new = ADAM_B1 * m_ref[...] + (1.0 - ADAM_B1) * grad
        v_new = ADAM_B2 * v_ref[...] + (1.0 - ADAM_B2) * (grad * grad)
        m_hat = m_new / (1.0 - ADAM_B1 ** ADAM_STEP)
        v_hat = v_new / (1.0 - ADAM_B2 ** ADAM_STEP)
        go_ref[...] = grad
        d_ref[...] = -ADAM_LR * (m_hat / (jnp.sqrt(v_hat) + ADAM_EPS) + ADAM_WD * w_ref[...])
        mo_ref[...] = m_new
        vo_ref[...] = v_new

    blk = pl.BlockSpec((tr, cols), lambda i: (i, 0))
    return _compute_call(
        body, name, grid=(rows // tr,),
        in_specs=[blk, blk, blk, pl.BlockSpec((tr, cols), lambda i: (i + off, 0))], operands=[w, m, v, g],
        out_specs=[blk, blk, blk, blk], out_shape=[jax.ShapeDtypeStruct((rows, cols), F32)] * 4)


def _pad_rows8(a):
    return jnp.concatenate([a, jnp.zeros((8 - a.shape[0],) + a.shape[1:], a.dtype)], axis=0)


def kernel(x, ln_pre_0, conv_w_in_0, conv_w_0, conv_w_out_0, ln_post_0, ln_pre_1, sb_w_in_1, sb_w_out_1, ln_post_1, ln_pre_2, conv_w_in_2, conv_w_2, conv_w_out_2, ln_post_2, ln_pre_3, sb_w_in_3, sb_w_out_3, ln_post_3, loss_target, m_ln_pre_0, m_conv_w_in_0, m_conv_w_0, m_conv_w_out_0, m_ln_post_0, m_ln_pre_1, m_sb_w_in_1, m_sb_w_out_1, m_ln_post_1, m_ln_pre_2, m_conv_w_in_2, m_conv_w_2, m_conv_w_out_2, m_ln_post_2, m_ln_pre_3, m_sb_w_in_3, m_sb_w_out_3, m_ln_post_3, v_ln_pre_0, v_conv_w_in_0, v_conv_w_0, v_conv_w_out_0, v_ln_post_0, v_ln_pre_1, v_sb_w_in_1, v_sb_w_out_1, v_ln_post_1, v_ln_pre_2, v_conv_w_in_2, v_conv_w_2, v_conv_w_out_2, v_ln_post_2, v_ln_pre_3, v_sb_w_in_3, v_sb_w_out_3, v_ln_post_3):
    t, d = x.shape[1], x.shape[2]
    dq = d // N_CHIPS
    xs = x.reshape(t, d)
    target = loss_target.reshape(t, d)
    w_in = [conv_w_in_0, sb_w_in_1, conv_w_in_2, sb_w_in_3]
    w_out = [conv_w_out_0, sb_w_out_1, conv_w_out_2, sb_w_out_3]
    m_in = [m_conv_w_in_0, m_sb_w_in_1, m_conv_w_in_2, m_sb_w_in_3]
    m_out = [m_conv_w_out_0, m_sb_w_out_1, m_conv_w_out_2, m_sb_w_out_3]
    v_in = [v_conv_w_in_0, v_sb_w_in_1, v_conv_w_in_2, v_sb_w_in_3]
    v_out = [v_conv_w_out_0, v_sb_w_out_1, v_conv_w_out_2, v_sb_w_out_3]
    ln_pre = [ln_pre_0, ln_pre_1, ln_pre_2, ln_pre_3]
    ln_post = [ln_post_0, ln_post_1, ln_post_2, ln_post_3]
    conv_w = [conv_w_0, conv_w_2]
    m_conv = [m_conv_w_0, m_conv_w_2]
    v_conv = [v_conv_w_0, v_conv_w_2]
    chip = 2 * lax.axis_index("x") + lax.axis_index("y")
    chip_arr = jnp.reshape(chip, (1,)).astype(jnp.int32)
    place = jnp.stack([chip, lax.axis_index("c")]).astype(jnp.int32)
    core_arr = jnp.reshape(lax.axis_index("c"), (1,)).astype(jnp.int32)

    packs = [_pack_weights(w_in[l], w_out[l], chip_arr) for l in range(N_LAYERS)]
    cw_local = jnp.concatenate([_pad_rows8(conv_w[0]), _pad_rows8(conv_w[1])], axis=0)
    gathered = list(packs)
    every = range(AG_CHUNKS)
    u, gathered[0], cw_all = _norm_first(
        xs, ln_pre[0], rider=_join_riders(_gather_rider(packs[:1], chunks=every, pass_now=every), _taps_rider(cw_local)))
    cw_full = jnp.transpose(cw_all, (1, 0, 2)).reshape(16, d)
    conv_taps = {0: cw_full[0:8], 2: cw_full[8:16]}

    h_in, us, projs, gateds, ms, sb_saved = [], [], [], [], [], {}
    h = xs
    for l in range(N_LAYERS):
        h_in.append(h)
        us.append(u)
        nxt = l + 1
        plan = {} if nxt == N_LAYERS else (AG_PLAN_CONV if l % 2 == 0 else AG_PLAN_SB)

        def rider_for(name):
            return _gather_rider(gathered[nxt:nxt + 1], **plan[name]) if name in plan else None

        def take(results, name):
            if name not in plan:
                return results
            gathered[nxt] = results[-1]
            return results[:-1]

        proj, = take(_mm_proj(u, gathered[l], rider=rider_for("mm_proj")), "mm_proj")
        if l % 2 == 0:
            gated, = take(_conv_fwd(proj, conv_taps[l], rider=rider_for("conv_fwd")), "conv_fwd")
        else:
            gated, o, car = take(_sb_fwd(proj, rider=rider_for("sb_fwd")), "sb_fwd")
            sb_saved[l] = (o, car)
        if l < N_LAYERS - 1:
            m, h, u = take(_mm_out_norm(gated, gathered[l], h, ln_post[l], ln_pre[nxt], rider=rider_for("mm_out_norm")),
                           "mm_out_norm")
            ms.append(m)
        projs.append(proj)
        gateds.append(gated)
    dh, dm, dg_post_last, loss_part = _mm_out_last(gateds[-1], gathered[-1], h, ln_post[-1], target)

    dg_pre = [None] * N_LAYERS
    dg_post = [None] * N_LAYERS
    dg_post[N_LAYERS - 1] = dg_post_last
    dconv = {}
    sums = [None] * N_LAYERS
    got = [None] * N_LAYERS
    fulls = [None] * N_LAYERS

    def summed(layer):
        return _halves_rider(_sum_grad_half(got[layer], sums[layer], place))

    for l in reversed(range(N_LAYERS)):
        above = l + 1 if l + 1 < N_LAYERS else None
        if l == 1:
            dgated, fulls[3] = _mm_dgated(dm, gathered[l], rider=summed(3))
        else:
            dgated, = _mm_dgated(dm, gathered[l])
        if l == 0:
            gp, got[1] = _mm_dwout(gateds[l], dm, rider=_send_sums_rider(sums[1], None, 0, 4, 1))
        else:
            gp, = _mm_dwout(gateds[l], dm)
        if l % 2 == 0:
            dproj, dconv[l] = _conv_bwd(projs[l], dgated, conv_taps[l])
        elif above is not None:
            o, car = sb_saved[l]
            dproj, got[above] = _sb_bwd(projs[l], dgated, o, car, rider=_send_sums_rider(sums[above]))
        else:
            o, car = sb_saved[l]
            dproj, = _sb_bwd(projs[l], dgated, o, car)
        if l == 0:
            gp, got[1] = _mm_dwin(us[0], dproj, gp, rider=_send_sums_rider(sums[1], got[1], 1, 4, 3))
            theirs, = _comm_call("swap_last", _swap_rider(gp))
            sums[0] = _presum(gp, theirs, core_arr)
            du, got[0], fulls[1] = _mm_du(
                dproj, gathered[0], rider=_join_riders(_send_sums_rider(sums[0], None, 0, 4, 3), summed(1)))
        elif l % 2 == 0:
            gp, got[above] = _mm_dwin(us[l], dproj, gp, rider=_send_sums_rider(sums[above], None, 0, 2))
            du, theirs, got[above] = _mm_du(
                dproj, gathered[l], rider=_join_riders(_swap_rider(gp), _send_sums_rider(sums[above], got[above], 1, 2)))
            sums[l] = _presum(gp, theirs, core_arr)
        elif above is not None:
            gp, fulls[above] = _mm_dwin(us[l], dproj, gp, rider=summed(above))
            du, theirs = _mm_du(dproj, gathered[l], rider=_swap_rider(gp))
            sums[l] = _presum(gp, theirs, core_arr)
        else:
            gp, = _mm_dwin(us[l], dproj, gp)
            du, theirs = _mm_du(dproj, gathered[l], rider=_swap_rider(gp))
            sums[l] = _presum(gp, theirs, core_arr)
        if l > 0:
            dh, dm, dg_pre[l], dg_post[l - 1] = _norm_bwd_mid(dh, du, h_in[l], ln_pre[l], ms[l - 1], ln_post[l - 1])
    grad_x, dg_pre[0] = _norm_bwd_first(dh, du, h_in[0], ln_pre[0])

    loss_rows = jnp.pad(loss_part, ((0, 0), (0, d - loss_part.shape[1])))
    small = _pack_small(dg_pre + dg_post, [dconv[0], dconv[2]], loss_rows)
    got[0], small_all = _comm_call(
        "exchange_last", _join_riders(_send_sums_rider(sums[0], got[0], 3, 4, 1), _small_rider(small)))
    small_sum = _sum_sources("sum_small", small_all)

    fulls[0], = _comm_call("exchange_halves", summed(0))
    fulls = [f.reshape(d + dq, d) for f in fulls]
    res_in = [_adamw("adamw_w_in", w_in[l], m_in[l], v_in[l], fulls[l], 0) for l in range(N_LAYERS)]
    res_out = [_adamw("adamw_w_out", w_out[l], m_out[l], v_out[l], fulls[l], d) for l in range(N_LAYERS)]
    ln_all = ln_pre + ln_post
    ln_m = [m_ln_pre_0, m_ln_pre_1, m_ln_pre_2, m_ln_pre_3, m_ln_post_0, m_ln_post_1, m_ln_post_2, m_ln_post_3]
    ln_v = [v_ln_pre_0, v_ln_pre_1, v_ln_pre_2, v_ln_pre_3, v_ln_post_0, v_ln_post_1, v_ln_post_2, v_ln_post_3]
    res_ln = _adamw("adamw_ln", jnp.stack(ln_all), jnp.stack(ln_m), jnp.stack(ln_v), small_sum[0:2 * N_LAYERS])
    conv_g = [_pad_rows8(lax.dynamic_slice(small_sum, (2 * N_LAYERS + 3 * i, chip * dq), (3, dq))) for i in range(2)]
    stack8 = lambda a, b: jnp.concatenate([_pad_rows8(a), _pad_rows8(b)], axis=0)
    res_conv = _adamw("adamw_conv", stack8(*conv_w), stack8(*m_conv), stack8(*v_conv), jnp.concatenate(conv_g, axis=0))

    def leaf(kind, l, which):
        if kind == "ln_pre":
            return res_ln[which][l]
        if kind == "ln_post":
            return res_ln[which][N_LAYERS + l]
        if kind == "w_in":
            return res_in[l][which]
        if kind == "w_out":
            return res_out[l][which]
        return res_conv[which][8 * (l // 2):8 * (l // 2) + 3]

    order = []
    for l in range(N_LAYERS):
        order.append(("ln_pre", l))
        order.append(("w_in", l))
        if l % 2 == 0:
            order.append(("conv", l))
        order.append(("w_out", l))
        order.append(("ln_post", l))
    loss = small_sum[2 * N_LAYERS + 3 * 2, 0]
    outs = [loss, grad_x.reshape(1, t, d)]
    for which in range(4):
        outs.extend(leaf(kind, l, which) for kind, l in order)
    return tuple(outs)
```
